```python
import math
import jax, jax.numpy as jnp
from jax import lax
import numpy as np

D_MODEL = 1024
BATCH = 4
SEQ = 4096
DEPTH = 2

GRID_W = 64
CTX_LEN = 256

D_S5 = D_MODEL // 4
D_HG = D_MODEL // 4
D_RW = D_MODEL // 2
D_MIX = D_S5 + D_HG + D_RW

S5_H = 16
S5_G = D_S5 // S5_H
S5_P = 64

HG_HEAD = 64
HG_HEADS = D_HG // HG_HEAD
HG_CHUNK = 16

RW_HEAD = 64
RW_HEADS = D_RW // RW_HEAD
RW_W_LORA = 64
RW_A_LORA = 64
RW_G_LORA = 128
RW_COLS = 3 * D_RW + RW_G_LORA + 2 * RW_W_LORA + 2 * RW_A_LORA
RW_GN_EPS = 64e-5

P_IN = D_S5 + 5 * D_HG + RW_COLS

PEER_HEADS = 8
PEER_NKEYS = 128
PEER_N = PEER_NKEYS * PEER_NKEYS
PEER_QDIM = 256
PEER_TOPK = 16
PEER_BLOCK = 128

LN_EPS = 1e-5
RMS_EPS = 1e-6
DEEPNORM_ALPHA = (2.0 * DEPTH) ** 0.25
DEEPNORM_BETA = (8.0 * DEPTH) ** -0.25

kernel_name = 'hybrid_s5_hgrn2_rwkv7_peer_block'


def _f32(t):
    return t.astype(jnp.float32)


def _layer_norm(x, g, b):
    xf = _f32(x)
    mu = jnp.mean(xf, -1, keepdims=True)
    var = jnp.mean(jnp.square(xf - mu), -1, keepdims=True)
    return ((xf - mu) * lax.rsqrt(var + LN_EPS) * _f32(g) + _f32(b)).astype(x.dtype)


def _shift(t, axis, step):
    n = t.shape[axis]
    pad = [(0, 0)] * t.ndim
    if step > 0:
        pad[axis] = (1, 0)
        return lax.slice_in_dim(jnp.pad(t, pad), 0, n, axis=axis)
    pad[axis] = (0, 1)
    return lax.slice_in_dim(jnp.pad(t, pad), 1, n + 1, axis=axis)


def _qshift_grid(s, rows):
    b, l, ch = s.shape
    s4 = s.reshape(b, rows, GRID_W, ch // 4, 4)
    out = jnp.stack([_shift(s4[..., 0], 2, 1), _shift(s4[..., 1], 2, -1),
                     _shift(s4[..., 2], 1, 1), _shift(s4[..., 3], 1, -1)], axis=-1)
    return out.reshape(b, l, ch)


def _qshift_seq(s):
    b, l, ch = s.shape
    s4 = s.reshape(b, l, ch // 4, 4)
    out = jnp.stack([_shift(s4[..., 0], 1, 1), _shift(s4[..., 1], 1, -1),
                     _shift(s4[..., 2], 1, 1), _shift(s4[..., 3], 1, -1)], axis=-1)
    return out.reshape(b, l, ch)


def _s5_discretise(lam_re, lam_im, log_step, b_re, b_im):
    dt = jnp.exp(log_step)[:, None]
    mag = jnp.exp(lam_re * dt)
    ang = lam_im * dt
    lb_re, lb_im = mag * jnp.cos(ang), mag * jnp.sin(ang)
    den = lam_re * lam_re + lam_im * lam_im
    nr, ni = lb_re - 1.0, lb_im
    co_re = (nr * lam_re + ni * lam_im) / den
    co_im = (ni * lam_re - nr * lam_im) / den
    bb_re = co_re[..., None] * b_re - co_im[..., None] * b_im
    bb_im = co_re[..., None] * b_im + co_im[..., None] * b_re
    return lb_re, lb_im, bb_re, bb_im


def _complex_affine_combine(e1, e2):
    a1r, a1i, b1r, b1i = e1
    a2r, a2i, b2r, b2i = e2
    return (a2r * a1r - a2i * a1i, a2r * a1i + a2i * a1r,
            a2r * b1r - a2i * b1i + b2r, a2r * b1i + a2i * b1r + b2i)


def _s5_scan(u, h0, disc, c_re, c_im):
    lb_re, lb_im, bb_re, bb_im = disc
    bu_re = jnp.einsum('blgh,gph->blgp', u, bb_re)
    bu_im = jnp.einsum('blgh,gph->blgp', u, bb_im)
    a_re = jnp.broadcast_to(lb_re, bu_re.shape)
    a_im = jnp.broadcast_to(lb_im, bu_im.shape)
    p_re, p_im, s_re, s_im = lax.associative_scan(
        _complex_affine_combine, (a_re, a_im, bu_re, bu_im), axis=1)
    h_re, h_im = h0[0][:, None], h0[1][:, None]
    s_re = s_re + p_re * h_re - p_im * h_im
    s_im = s_im + p_re * h_im + p_im * h_re
    y = jnp.einsum('blgp,ghp->blgh', s_re, c_re) - jnp.einsum('blgp,ghp->blgh', s_im, c_im)
    return y, (s_re[:, -1], s_im[:, -1])


def _s5_mixer(uc, ux, lam_re, lam_im, log_step, b_re, b_im, c_re, c_im, d_skip, glu_w, glu_b):
    disc = [_s5_discretise(_f32(lam_re[d]), _f32(lam_im[d]), _f32(log_step[d]),
                           _f32(b_re[d]), _f32(b_im[d])) for d in range(2)]
    c_re, c_im = _f32(c_re), _f32(c_im)

    def run(u, h0s):
        ug = u.reshape(u.shape[0], u.shape[1], S5_G, S5_H)
        y_f, h_f = _s5_scan(ug, h0s[0], disc[0], c_re[0], c_im[0])
        y_b, h_b = _s5_scan(jnp.flip(ug, 1), h0s[1], disc[1], c_re[1], c_im[1])
        y = (y_f + jnp.flip(y_b, 1)).reshape(u.shape) + _f32(d_skip) * u
        y = jax.nn.gelu(y)
        return y * jax.nn.sigmoid(y @ _f32(glu_w) + _f32(glu_b)), (h_f, h_b)

    zero = jnp.zeros((uc.shape[0], S5_G, S5_P), jnp.float32)
    yc, h_ctx = run(_f32(uc), ((zero, zero), (zero, zero)))
    yx, _ = run(_f32(ux), h_ctx)
    return yc.astype(uc.dtype), yx.astype(ux.dtype)


def _gla_chunkwise(q, k, v, log_f, s0):
    b, h, l, _ = q.shape
    dv = v.shape[-1]
    n = l // HG_CHUNK
    blk = lambda t: t.reshape(b, h, n, HG_CHUNK, t.shape[-1])
    q, k, v, log_f = blk(q), blk(k), blk(v), blk(log_f)
    cum = jnp.cumsum(log_f, axis=3)
    lower = jnp.tril(jnp.ones((HG_CHUNK, HG_CHUNK), bool))[:, :, None]
    rel = jnp.exp(jnp.where(lower, cum[..., :, None, :] - cum[..., None, :, :], -jnp.inf))
    scores = jnp.einsum('bhntd,bhnsd,bhntsd->bhnts', q, k, rel)
    o_intra = jnp.einsum('bhnts,bhnsv->bhntv', scores, v)
    last = cum[..., -1:, :]
    chunk_kv = jnp.einsum('bhnsd,bhnsv->bhndv', k * jnp.exp(last - cum), v)
    chunk_decay = jnp.exp(last[..., 0, :])

    def step(state, inp):
        dec, kv = inp
        return dec[..., None] * state + kv, state

    s_fin, s_prev = lax.scan(step, s0, (jnp.moveaxis(chunk_decay, 2, 0), jnp.moveaxis(chunk_kv, 2, 0)))
    o_inter = jnp.einsum('bhntd,bhndv->bhntv', q * jnp.exp(cum), jnp.moveaxis(s_prev, 0, 2))
    return (o_intra + o_inter).reshape(b, h, l, dv), s_fin


def _hgrn2_mixer(pc, px, lb, norm_g):
    def heads(t):
        return t.reshape(t.shape[0], t.shape[1], HG_HEADS, HG_HEAD).transpose(0, 2, 1, 3)

    def fl(t):
        return jnp.flip(t, axis=2)

    def run(p, s0s):
        q, i, f_f, f_b, g = jnp.split(_f32(p), 5, axis=-1)
        q, i = heads(jax.nn.silu(q)), heads(i)
        logf_f = jnp.log(lb[0] + (1.0 - lb[0]) * jax.nn.sigmoid(f_f))
        logf_b = jnp.log(lb[1] + (1.0 - lb[1]) * jax.nn.sigmoid(f_b))
        o_f, s_f = _gla_chunkwise(q, heads(-jnp.expm1(logf_f)), i, heads(logf_f), s0s[0])
        o_b, s_b = _gla_chunkwise(fl(q), fl(heads(-jnp.expm1(logf_b))), fl(i), fl(heads(logf_b)), s0s[1])
        o = (o_f + fl(o_b)).transpose(0, 2, 1, 3)
        o = o * lax.rsqrt(jnp.mean(o * o, -1, keepdims=True) + RMS_EPS)
        o = o.reshape(o.shape[0], o.shape[1], D_HG) * _f32(norm_g) * jax.nn.silu(g)
        return o, (s_f, s_b)

    zero = jnp.zeros((pc.shape[0], HG_HEADS, HG_HEAD, HG_HEAD), jnp.float32)
    yc, s_ctx = run(pc, (zero, zero))
    yx, _ = run(px, s_ctx)
    return yc.astype(pc.dtype), yx.astype(px.dtype)


def _rwkv7_step(state, inp):
    r, w, k, v, kk, a = inp
    sk = jnp.einsum('bhvk,bhk->bhv', state, kk)
    state = (state * w[:, :, None, :] - sk[..., None] * (kk * a)[:, :, None, :]
             + v[..., None] * k[:, :, None, :])
    return state, jnp.einsum('bhvk,bhk->bhv', state, r)


def _rwkv7_scan(r, w, k, v, kk, a, s0, reverse):
    xs = tuple(jnp.moveaxis(t, 1, 0) for t in (r, w, k, v, kk, a))
    s_fin, y = lax.scan(_rwkv7_step, s0, xs, reverse=reverse)
    return jnp.moveaxis(y, 0, 1), s_fin


def _rwkv7_mixer(pc, px, rows, mu, w0, w_up, a0, a_up, g_up, k_k, k_a, r_k, gn_g, gn_b):
    split_at = [D_RW, 2 * D_RW, 3 * D_RW, 3 * D_RW + RW_G_LORA,
                3 * D_RW + RW_G_LORA + RW_W_LORA, 3 * D_RW + RW_G_LORA + 2 * RW_W_LORA,
                3 * D_RW + RW_G_LORA + 2 * RW_W_LORA + RW_A_LORA]

    def heads(t):
        return t.reshape(t.shape[0], t.shape[1], RW_HEADS, RW_HEAD)

    def run(p, shifted, s0s):
        s = p + _f32(mu) * (shifted - p)
        r, k, v, g_lo, wl_f, wl_b, al_f, al_b = jnp.split(s, split_at, axis=-1)
        g = jax.nn.sigmoid(g_lo) @ _f32(g_up)
        kk = heads(k * _f32(k_k))
        kk = kk / jnp.maximum(jnp.sqrt(jnp.sum(kk * kk, -1, keepdims=True)), 1e-12)
        r, v = heads(r), heads(v)
        ys, bonuses, finals = [], [], []
        for d, (wl, al) in enumerate(((wl_f, al_f), (wl_b, al_b))):
            w_log = -jax.nn.softplus(-(_f32(w0[d]) + jnp.tanh(wl) @ _f32(w_up[d]))) - 0.5
            decay = heads(jnp.exp(-jnp.exp(w_log)))
            a = jax.nn.sigmoid(_f32(a0[d]) + al @ _f32(a_up[d]))
            k_d = heads(k * (1.0 + (a - 1.0) * _f32(k_a)))
            y_d, s_d = _rwkv7_scan(r, decay, k_d, v, kk, heads(a), s0s[d], d == 1)
            ys.append(y_d)
            bonuses.append(jnp.sum(r * k_d * _f32(r_k), -1, keepdims=True))
            finals.append(s_d)
        y = ys[0] + ys[1]
        mu_y = jnp.mean(y, -1, keepdims=True)
        var_y = jnp.mean(jnp.square(y - mu_y), -1, keepdims=True)
        y = ((y - mu_y) * lax.rsqrt(var_y + RW_GN_EPS) * _f32(gn_g).reshape(RW_HEADS, RW_HEAD)
             + _f32(gn_b).reshape(RW_HEADS, RW_HEAD))
        y = (y + (bonuses[0] + bonuses[1]) * v).reshape(p.shape[0], p.shape[1], D_RW) * g
        return y, (finals[0], finals[1])

    pc32, px32 = _f32(pc), _f32(px)
    zero = jnp.zeros((pc.shape[0], RW_HEADS, RW_HEAD, RW_HEAD), jnp.float32)
    yc, s_ctx = run(pc32, _qshift_seq(pc32), (zero, zero))
    yx, _ = run(px32, _qshift_grid(px32, rows), s_ctx)
    return yc.astype(pc.dtype), yx.astype(px.dtype)


def _peer_ffn(h, wq, keys, u_tab, v_tab):
    b, l, d = h.shape
    t = b * l
    hf = h.reshape(t, d)
    q = (hf @ wq).reshape(t, PEER_HEADS, 2, PEER_QDIM // 2)
    scores = _f32(jnp.einsum('thpd,hpkd->thpk', q, keys))
    s_top, i_top = lax.top_k(scores, PEER_TOPK)
    cand = s_top[..., 0, :, None] + s_top[..., 1, None, :]
    cand_s, cand_i = lax.top_k(cand.reshape(t, PEER_HEADS, PEER_TOPK * PEER_TOPK), PEER_TOPK)
    i1 = jnp.take_along_axis(i_top[..., 0, :], cand_i // PEER_TOPK, axis=-1)
    i2 = jnp.take_along_axis(i_top[..., 1, :], cand_i % PEER_TOPK, axis=-1)
    expert = (i1 * PEER_NKEYS + i2).reshape(t, PEER_HEADS * PEER_TOPK)
    gate = jax.nn.softmax(cand_s, axis=-1).reshape(t, PEER_HEADS * PEER_TOPK).astype(h.dtype)
    nb = t // PEER_BLOCK

    def block(args):
        hb, eb, gb = args
        z = jnp.einsum('tkd,td->tk', u_tab[eb], hb)
        return jnp.einsum('tk,tkd->td', jax.nn.gelu(z) * gb, v_tab[eb])

    out = lax.map(block, (hf.reshape(nb, PEER_BLOCK, d),
                          expert.reshape(nb, PEER_BLOCK, PEER_HEADS * PEER_TOPK),
                          gate.reshape(nb, PEER_BLOCK, PEER_HEADS * PEER_TOPK)))
    return out.reshape(b, l, d)


def setup_inputs(seed: int = 0) -> dict:
    key = jax.random.key(seed)
    ks = iter(jax.random.split(key, 48))
    f32 = jnp.float32

    def nrm(shape, scale):
        return scale * jax.random.normal(next(ks), shape, f32)

    def unif(shape, lo, hi):
        return jax.random.uniform(next(ks), shape, f32, lo, hi)

    L = DEPTH
    D = D_MODEL
    return {
        'x': nrm((BATCH, SEQ, D), 1.0),
        'c': nrm((BATCH, D), 1.0),
        'ctx': nrm((BATCH, CTX_LEN, D), 1.0),
        'c_ctx': nrm((D,), 1.0),
        'ada_w': nrm((L, D, 6 * D), D ** -0.5),
        'ada_b': nrm((L, 6 * D), 0.02),
        'w_in': nrm((L, D, P_IN), D ** -0.5),
        'w_out': nrm((L, D_MIX, D), DEEPNORM_BETA * D_MIX ** -0.5),
        'ln1_g': 1.0 + nrm((L, D), 0.02),
        'ln1_b': nrm((L, D), 0.02),
        'ln2_g': 1.0 + nrm((L, D), 0.02),
        'ln2_b': nrm((L, D), 0.02),
        's5_lam_re': -0.5 + nrm((L, 2, S5_G, S5_P), 0.01),
        's5_lam_im': jnp.pi * jnp.arange(S5_P, dtype=f32) + nrm((L, 2, S5_G, S5_P), 0.01),
        's5_log_step': unif((L, 2, S5_G), math.log(1e-3), math.log(1e-1)),
        's5_b_re': nrm((L, 2, S5_G, S5_P, S5_H), (2.0 * S5_H) ** -0.5),
        's5_b_im': nrm((L, 2, S5_G, S5_P, S5_H), (2.0 * S5_H) ** -0.5),
        's5_c_re': nrm((L, 2, S5_G, S5_H, S5_P), S5_P ** -0.5),
        's5_c_im': nrm((L, 2, S5_G, S5_H, S5_P), S5_P ** -0.5),
        's5_d': nrm((L, D_S5), 1.0),
        's5_glu_w': nrm((L, D_S5, D_S5), D_S5 ** -0.5),
        's5_glu_b': nrm((L, D_S5), 0.02),
        'hgrn_lb_logits': nrm((2, L, D_HG), 0.1),
        'hgrn_norm_g': 1.0 + nrm((L, D_HG), 0.02),
        'rwkv_mu': unif((L, RW_COLS), 0.0, 1.0),
        'rwkv_w0': unif((L, 2, D_RW), -6.5, -1.5),
        'rwkv_w_up': nrm((L, 2, RW_W_LORA, D_RW), 0.5 * RW_W_LORA ** -0.5),
        'rwkv_a0': nrm((L, 2, D_RW), 0.1),
        'rwkv_a_up': nrm((L, 2, RW_A_LORA, D_RW), 0.5 * RW_A_LORA ** -0.5),
        'rwkv_g_up': nrm((L, RW_G_LORA, D_RW), RW_G_LORA ** -0.5),
        'rwkv_k_k': 0.85 + nrm((L, D_RW), 0.02),
        'rwkv_k_a': 1.0 + nrm((L, D_RW), 0.02),
        'rwkv_r_k': nrm((L, RW_HEADS, RW_HEAD), 0.1),
        'rwkv_gn_g': 1.0 + nrm((L, D_RW), 0.02),
        'rwkv_gn_b': nrm((L, D_RW), 0.02),
        'peer_wq': nrm((L, D, PEER_HEADS * PEER_QDIM), D ** -0.5),
        'peer_keys': nrm((L, PEER_HEADS, 2, PEER_NKEYS, PEER_QDIM // 2), (PEER_QDIM // 2) ** -0.5),
        'peer_u': nrm((L, PEER_N, D), D ** -0.5),
        'peer_v': nrm((L, PEER_N, D), DEEPNORM_BETA),
    }


def reference(x, c, ctx, c_ctx, ada_w, ada_b, w_in, w_out, ln1_g, ln1_b, ln2_g, ln2_b,
              s5_lam_re, s5_lam_im, s5_log_step, s5_b_re, s5_b_im, s5_c_re, s5_c_im,
              s5_d, s5_glu_w, s5_glu_b, hgrn_lb_logits, hgrn_norm_g,
              rwkv_mu, rwkv_w0, rwkv_w_up, rwkv_a0, rwkv_a_up, rwkv_g_up, rwkv_k_k, rwkv_k_a,
              rwkv_r_k, rwkv_gn_g, rwkv_gn_b, peer_wq, peer_keys, peer_u, peer_v):
    rows = x.shape[1] // GRID_W
    lb_cum = jnp.cumsum(jax.nn.softmax(_f32(hgrn_lb_logits), axis=1), axis=1)
    lb_all = lb_cum - lb_cum[:, :1]
    col_split = [D_S5, D_S5 + 5 * D_HG]
    xs, cs = x, ctx
    for l in range(DEPTH):
        mod_x = (jax.nn.silu(c) @ ada_w[l] + ada_b[l])[:, None, :]
        mod_c = (jax.nn.silu(c_ctx) @ ada_w[l] + ada_b[l])[None, None, :]
        sh1x, sc1x, g1x, sh2x, sc2x, g2x = jnp.split(mod_x, 6, axis=-1)
        sh1c, sc1c, g1c, sh2c, sc2c, g2c = jnp.split(mod_c, 6, axis=-1)

        px = (xs * (1.0 + sc1x) + sh1x) @ w_in[l]
        pc = (cs * (1.0 + sc1c) + sh1c) @ w_in[l]
        pxa, pxb, pxc = jnp.split(px, col_split, axis=-1)
        pca, pcb, pcc = jnp.split(pc, col_split, axis=-1)

        ya_c, ya_x = _s5_mixer(pca, pxa, s5_lam_re[l], s5_lam_im[l], s5_log_step[l],
                               s5_b_re[l], s5_b_im[l], s5_c_re[l], s5_c_im[l],
                               s5_d[l], s5_glu_w[l], s5_glu_b[l])
        yb_c, yb_x = _hgrn2_mixer(pcb, pxb, lb_all[:, l], hgrn_norm_g[l])
        yc_c, yc_x = _rwkv7_mixer(pcc, pxc, rows, rwkv_mu[l], rwkv_w0[l], rwkv_w_up[l],
                                  rwkv_a0[l], rwkv_a_up[l], rwkv_g_up[l], rwkv_k_k[l],
                                  rwkv_k_a[l], rwkv_r_k[l], rwkv_gn_g[l], rwkv_gn_b[l])

        mix_x = jnp.concatenate([ya_x, yb_x, yc_x], axis=-1) @ w_out[l]
        xs = _layer_norm(DEEPNORM_ALPHA * xs + g1x * mix_x, ln1_g[l], ln1_b[l])
        ffn_x = _peer_ffn(xs * (1.0 + sc2x) + sh2x, peer_wq[l], peer_keys[l], peer_u[l], peer_v[l])
        xs = _layer_norm(DEEPNORM_ALPHA * xs + g2x * ffn_x, ln2_g[l], ln2_b[l])

        if l < DEPTH - 1:
            mix_c = jnp.concatenate([ya_c, yb_c, yc_c], axis=-1) @ w_out[l]
            cs = _layer_norm(DEEPNORM_ALPHA * cs + g1c * mix_c, ln1_g[l], ln1_b[l])
            ffn_c = _peer_ffn(cs * (1.0 + sc2c) + sh2c, peer_wq[l], peer_keys[l], peer_u[l], peer_v[l])
            cs = _layer_norm(DEEPNORM_ALPHA * cs + g2c * ffn_c, ln2_g[l], ln2_b[l])
    return xs
```

```python
import functools
import math

import jax
import jax.numpy as jnp
from jax import lax
from jax.experimental import pallas as pl
from jax.experimental.pallas import tpu as pltpu

D_MODEL = 1024
BATCH = 4
SEQ = 4096
DEPTH = 2
GRID_W = 64
CTX_LEN = 256
D_S5 = D_MODEL // 4
D_HG = D_MODEL // 4
D_RW = D_MODEL // 2
D_MIX = D_S5 + D_HG + D_RW
S5_H = 16
S5_G = D_S5 // S5_H
S5_P = 64
HG_HEAD = 64
HG_HEADS = D_HG // HG_HEAD
HG_CHUNK = 16
RW_HEAD = 64
RW_HEADS = D_RW // RW_HEAD
RW_W_LORA = 64
RW_A_LORA = 64
RW_G_LORA = 128
RW_COLS = 3 * D_RW + RW_G_LORA + 2 * RW_W_LORA + 2 * RW_A_LORA
RW_GN_EPS = 64e-5
P_IN = D_S5 + 5 * D_HG + RW_COLS
PEER_HEADS = 8
PEER_NKEYS = 128
PEER_N = PEER_NKEYS * PEER_NKEYS
PEER_QDIM = 256
PEER_TOPK = 16
PEER_BLOCK = 128
LN_EPS = 1e-5
RMS_EPS = 1e-6
DEEPNORM_ALPHA = (2.0 * DEPTH) ** 0.25
DEEPNORM_BETA = (8.0 * DEPTH) ** -0.25


def _mm_kernel(x_ref, w_ref, o_ref):
    o_ref[...] = jnp.dot(x_ref[...].astype(jnp.bfloat16), w_ref[...],
                         preferred_element_type=jnp.float32)


def _pick_tile(n, cands):
    for c in cands:
        if n % c == 0:
            return c
    return n


def _matmul(x, w):
    m, k = x.shape
    n = w.shape[1]
    tm = _pick_tile(m, (512, 256, 128, 8))
    tn = _pick_tile(n, (1152, 1024, 512, 256, 128))
    return pl.pallas_call(
        _mm_kernel,
        grid=(m // tm, n // tn),
        in_specs=[pl.BlockSpec((tm, k), lambda i, j: (i, 0)),
                  pl.BlockSpec((k, tn), lambda i, j: (0, j))],
        out_specs=pl.BlockSpec((tm, tn), lambda i, j: (i, j)),
        out_shape=jax.ShapeDtypeStruct((m, n), jnp.float32),
        compiler_params=pltpu.CompilerParams(
            dimension_semantics=("parallel", "parallel"),
            vmem_limit_bytes=48 * 1024 * 1024),
    )(x, w.astype(jnp.bfloat16))


def _mm3(x, w):
    b, l, k = x.shape
    return _matmul(x.reshape(b * l, k), w).reshape(b, l, w.shape[1])


def _f32(t):
    return t.astype(jnp.float32)


def _layer_norm(x, g, b):
    xf = _f32(x)
    mu = jnp.mean(xf, -1, keepdims=True)
    var = jnp.mean(jnp.square(xf - mu), -1, keepdims=True)
    return ((xf - mu) * lax.rsqrt(var + LN_EPS) * _f32(g) + _f32(b)).astype(x.dtype)


def _shift(t, axis, step):
    n = t.shape[axis]
    pad = [(0, 0)] * t.ndim
    if step > 0:
        pad[axis] = (1, 0)
        return lax.slice_in_dim(jnp.pad(t, pad), 0, n, axis=axis)
    pad[axis] = (0, 1)
    return lax.slice_in_dim(jnp.pad(t, pad), 1, n + 1, axis=axis)


def _qshift_grid(s, rows):
    b, l, ch = s.shape
    s4 = s.reshape(b, rows, GRID_W, ch // 4, 4)
    out = jnp.stack([_shift(s4[..., 0], 2, 1), _shift(s4[..., 1], 2, -1),
                     _shift(s4[..., 2], 1, 1), _shift(s4[..., 3], 1, -1)], axis=-1)
    return out.reshape(b, l, ch)


def _qshift_seq(s):
    b, l, ch = s.shape
    s4 = s.reshape(b, l, ch // 4, 4)
    out = jnp.stack([_shift(s4[..., 0], 1, 1), _shift(s4[..., 1], 1, -1),
                     _shift(s4[..., 2], 1, 1), _shift(s4[..., 3], 1, -1)], axis=-1)
    return out.reshape(b, l, ch)


def _s5_discretise(lam_re, lam_im, log_step, b_re, b_im):
    dt = jnp.exp(log_step)[:, None]
    mag = jnp.exp(lam_re * dt)
    ang = lam_im * dt
    lb_re, lb_im = mag * jnp.cos(ang), mag * jnp.sin(ang)
    den = lam_re * lam_re + lam_im * lam_im
    nr, ni = lb_re - 1.0, lb_im
    co_re = (nr * lam_re + ni * lam_im) / den
    co_im = (ni * lam_re - nr * lam_im) / den
    bb_re = co_re[..., None] * b_re - co_im[..., None] * b_im
    bb_im = co_re[..., None] * b_im + co_im[..., None] * b_re
    return lb_re, lb_im, bb_re, bb_im


def _complex_affine_combine(e1, e2):
    a1r, a1i, b1r, b1i = e1
    a2r, a2i, b2r, b2i = e2
    return (a2r * a1r - a2i * a1i, a2r * a1i + a2i * a1r,
            a2r * b1r - a2i * b1i + b2r, a2r * b1i + a2i * b1r + b2i)


def _s5_scan(u, h0, disc, c_re, c_im):
    lb_re, lb_im, bb_re, bb_im = disc
    bu_re = jnp.einsum('blgh,gph->blgp', u, bb_re)
    bu_im = jnp.einsum('blgh,gph->blgp', u, bb_im)
    a_re = jnp.broadcast_to(lb_re, bu_re.shape)
    a_im = jnp.broadcast_to(lb_im, bu_im.shape)
    p_re, p_im, s_re, s_im = lax.associative_scan(
        _complex_affine_combine, (a_re, a_im, bu_re, bu_im), axis=1)
    h_re, h_im = h0[0][:, None], h0[1][:, None]
    s_re = s_re + p_re * h_re - p_im * h_im
    s_im = s_im + p_re * h_im + p_im * h_re
    y = jnp.einsum('blgp,ghp->blgh', s_re, c_re) - jnp.einsum('blgp,ghp->blgh', s_im, c_im)
    return y, (s_re[:, -1], s_im[:, -1])


def _s5_mixer(uc, ux, lam_re, lam_im, log_step, b_re, b_im, c_re, c_im, d_skip, glu_w, glu_b):
    disc = [_s5_discretise(_f32(lam_re[d]), _f32(lam_im[d]), _f32(log_step[d]),
                           _f32(b_re[d]), _f32(b_im[d])) for d in range(2)]
    c_re, c_im = _f32(c_re), _f32(c_im)

    def run(u, h0s):
        ug = u.reshape(u.shape[0], u.shape[1], S5_G, S5_H)
        y_f, h_f = _s5_scan(ug, h0s[0], disc[0], c_re[0], c_im[0])
        y_b, h_b = _s5_scan(jnp.flip(ug, 1), h0s[1], disc[1], c_re[1], c_im[1])
        y = (y_f + jnp.flip(y_b, 1)).reshape(u.shape) + _f32(d_skip) * u
        y = jax.nn.gelu(y)
        return y * jax.nn.sigmoid(y @ _f32(glu_w) + _f32(glu_b)), (h_f, h_b)

    zero = jnp.zeros((uc.shape[0], S5_G, S5_P), jnp.float32)
    yc, h_ctx = run(_f32(uc), ((zero, zero), (zero, zero)))
    yx, _ = run(_f32(ux), h_ctx)
    return yc.astype(uc.dtype), yx.astype(ux.dtype)


def _gla_chunkwise(q, k, v, log_f, s0):
    b, h, l, _ = q.shape
    dv = v.shape[-1]
    n = l // HG_CHUNK
    blk = lambda t: t.reshape(b, h, n, HG_CHUNK, t.shape[-1])
    q, k, v, log_f = blk(q), blk(k), blk(v), blk(log_f)
    cum = jnp.cumsum(log_f, axis=3)
    lower = jnp.tril(jnp.ones((HG_CHUNK, HG_CHUNK), bool))[:, :, None]
    rel = jnp.exp(jnp.where(lower, cum[..., :, None, :] - cum[..., None, :, :], -jnp.inf))
    scores = jnp.einsum('bhntd,bhnsd,bhntsd->bhnts', q, k, rel)
    o_intra = jnp.einsum('bhnts,bhnsv->bhntv', scores, v)
    last = cum[..., -1:, :]
    chunk_kv = jnp.einsum('bhnsd,bhnsv->bhndv', k * jnp.exp(last - cum), v)
    chunk_decay = jnp.exp(last[..., 0, :])

    def step(state, inp):
        dec, kv = inp
        return dec[..., None] * state + kv, state

    s_fin, s_prev = lax.scan(step, s0, (jnp.moveaxis(chunk_decay, 2, 0), jnp.moveaxis(chunk_kv, 2, 0)))
    o_inter = jnp.einsum('bhntd,bhndv->bhntv', q * jnp.exp(cum), jnp.moveaxis(s_prev, 0, 2))
    return (o_intra + o_inter).reshape(b, h, l, dv), s_fin


def _hgrn2_mixer(pc, px, lb, norm_g):
    def heads(t):
        return t.reshape(t.shape[0], t.shape[1], HG_HEADS, HG_HEAD).transpose(0, 2, 1, 3)

    def fl(t):
        return jnp.flip(t, axis=2)

    def run(p, s0s):
        q, i, f_f, f_b, g = jnp.split(_f32(p), 5, axis=-1)
        q, i = heads(jax.nn.silu(q)), heads(i)
        logf_f = jnp.log(lb[0] + (1.0 - lb[0]) * jax.nn.sigmoid(f_f))
        logf_b = jnp.log(lb[1] + (1.0 - lb[1]) * jax.nn.sigmoid(f_b))
        o_f, s_f = _gla_chunkwise(q, heads(-jnp.expm1(logf_f)), i, heads(logf_f), s0s[0])
        o_b, s_b = _gla_chunkwise(fl(q), fl(heads(-jnp.expm1(logf_b))), fl(i), fl(heads(logf_b)), s0s[1])
        o = (o_f + fl(o_b)).transpose(0, 2, 1, 3)
        o = o * lax.rsqrt(jnp.mean(o * o, -1, keepdims=True) + RMS_EPS)
        o = o.reshape(o.shape[0], o.shape[1], D_HG) * _f32(norm_g) * jax.nn.silu(g)
        return o, (s_f, s_b)

    zero = jnp.zeros((pc.shape[0], HG_HEADS, HG_HEAD, HG_HEAD), jnp.float32)
    yc, s_ctx = run(pc, (zero, zero))
    yx, _ = run(px, s_ctx)
    return yc.astype(pc.dtype), yx.astype(px.dtype)


def _rwkv7_step(state, inp):
    r, w, k, v, kk, a = inp
    sk = jnp.einsum('bhvk,bhk->bhv', state, kk)
    state = (state * w[:, :, None, :] - sk[..., None] * (kk * a)[:, :, None, :]
             + v[..., None] * k[:, :, None, :])
    return state, jnp.einsum('bhvk,bhk->bhv', state, r)


def _rwkv7_scan(r, w, k, v, kk, a, s0, reverse):
    xs = tuple(jnp.moveaxis(t, 1, 0) for t in (r, w, k, v, kk, a))
    s_fin, y = lax.scan(_rwkv7_step, s0, xs, reverse=reverse)
    return jnp.moveaxis(y, 0, 1), s_fin


def _rwkv7_mixer(pc, px, rows, mu, w0, w_up, a0, a_up, g_up, k_k, k_a, r_k, gn_g, gn_b):
    split_at = [D_RW, 2 * D_RW, 3 * D_RW, 3 * D_RW + RW_G_LORA,
                3 * D_RW + RW_G_LORA + RW_W_LORA, 3 * D_RW + RW_G_LORA + 2 * RW_W_LORA,
                3 * D_RW + RW_G_LORA + 2 * RW_W_LORA + RW_A_LORA]

    def heads(t):
        return t.reshape(t.shape[0], t.shape[1], RW_HEADS, RW_HEAD)

    def run(p, shifted, s0s):
        s = p + _f32(mu) * (shifted - p)
        r, k, v, g_lo, wl_f, wl_b, al_f, al_b = jnp.split(s, split_at, axis=-1)
        g = jax.nn.sigmoid(g_lo) @ _f32(g_up)
        kk = heads(k * _f32(k_k))
        kk = kk / jnp.maximum(jnp.sqrt(jnp.sum(kk * kk, -1, keepdims=True)), 1e-12)
        r, v = heads(r), heads(v)
        ys, bonuses, finals = [], [], []
        for d, (wl, al) in enumerate(((wl_f, al_f), (wl_b, al_b))):
            w_log = -jax.nn.softplus(-(_f32(w0[d]) + jnp.tanh(wl) @ _f32(w_up[d]))) - 0.5
            decay = heads(jnp.exp(-jnp.exp(w_log)))
            a = jax.nn.sigmoid(_f32(a0[d]) + al @ _f32(a_up[d]))
            k_d = heads(k * (1.0 + (a - 1.0) * _f32(k_a)))
            y_d, s_d = _rwkv7_scan(r, decay, k_d, v, kk, heads(a), s0s[d], d == 1)
            ys.append(y_d)
            bonuses.append(jnp.sum(r * k_d * _f32(r_k), -1, keepdims=True))
            finals.append(s_d)
        y = ys[0] + ys[1]
        mu_y = jnp.mean(y, -1, keepdims=True)
        var_y = jnp.mean(jnp.square(y - mu_y), -1, keepdims=True)
        y = ((y - mu_y) * lax.rsqrt(var_y + RW_GN_EPS) * _f32(gn_g).reshape(RW_HEADS, RW_HEAD)
             + _f32(gn_b).reshape(RW_HEADS, RW_HEAD))
        y = (y + (bonuses[0] + bonuses[1]) * v).reshape(p.shape[0], p.shape[1], D_RW) * g
        return y, (finals[0], finals[1])

    pc32, px32 = _f32(pc), _f32(px)
    zero = jnp.zeros((pc.shape[0], RW_HEADS, RW_HEAD, RW_HEAD), jnp.float32)
    yc, s_ctx = run(pc32, _qshift_seq(pc32), (zero, zero))
    yx, _ = run(px32, _qshift_grid(px32, rows), s_ctx)
    return yc.astype(pc.dtype), yx.astype(px.dtype)


def _peer_ffn(h, wq, keys, u_tab, v_tab):
    b, l, d = h.shape
    t = b * l
    hf = h.reshape(t, d)
    q = _matmul(hf, wq).reshape(t, PEER_HEADS, 2, PEER_QDIM // 2)
    scores = _f32(jnp.einsum('thpd,hpkd->thpk', q, keys))
    s_top, i_top = lax.top_k(scores, PEER_TOPK)
    cand = s_top[..., 0, :, None] + s_top[..., 1, None, :]
    cand_s, cand_i = lax.top_k(cand.reshape(t, PEER_HEADS, PEER_TOPK * PEER_TOPK), PEER_TOPK)
    i1 = jnp.take_along_axis(i_top[..., 0, :], cand_i // PEER_TOPK, axis=-1)
    i2 = jnp.take_along_axis(i_top[..., 1, :], cand_i % PEER_TOPK, axis=-1)
    expert = (i1 * PEER_NKEYS + i2).reshape(t, PEER_HEADS * PEER_TOPK)
    gate = jax.nn.softmax(cand_s, axis=-1).reshape(t, PEER_HEADS * PEER_TOPK).astype(h.dtype)
    nb = t // PEER_BLOCK

    def block(args):
        hb, eb, gb = args
        z = jnp.einsum('tkd,td->tk', u_tab[eb], hb)
        return jnp.einsum('tk,tkd->td', jax.nn.gelu(z) * gb, v_tab[eb])

    out = lax.map(block, (hf.reshape(nb, PEER_BLOCK, d),
                          expert.reshape(nb, PEER_BLOCK, PEER_HEADS * PEER_TOPK),
                          gate.reshape(nb, PEER_BLOCK, PEER_HEADS * PEER_TOPK)))
    return out.reshape(b, l, d)


def kernel(x, c, ctx, c_ctx, ada_w, ada_b, w_in, w_out, ln1_g, ln1_b, ln2_g, ln2_b,
           s5_lam_re, s5_lam_im, s5_log_step, s5_b_re, s5_b_im, s5_c_re, s5_c_im,
           s5_d, s5_glu_w, s5_glu_b, hgrn_lb_logits, hgrn_norm_g,
           rwkv_mu, rwkv_w0, rwkv_w_up, rwkv_a0, rwkv_a_up, rwkv_g_up, rwkv_k_k, rwkv_k_a,
           rwkv_r_k, rwkv_gn_g, rwkv_gn_b, peer_wq, peer_keys, peer_u, peer_v):
    rows = x.shape[1] // GRID_W
    lb_cum = jnp.cumsum(jax.nn.softmax(_f32(hgrn_lb_logits), axis=1), axis=1)
    lb_all = lb_cum - lb_cum[:, :1]
    col_split = [D_S5, D_S5 + 5 * D_HG]
    xs, cs = x, ctx
    for l in range(DEPTH):
        mod_x = (jax.nn.silu(c) @ ada_w[l] + ada_b[l])[:, None, :]
        mod_c = (jax.nn.silu(c_ctx) @ ada_w[l] + ada_b[l])[None, None, :]
        sh1x, sc1x, g1x, sh2x, sc2x, g2x = jnp.split(mod_x, 6, axis=-1)
        sh1c, sc1c, g1c, sh2c, sc2c, g2c = jnp.split(mod_c, 6, axis=-1)

        px = _mm3(xs * (1.0 + sc1x) + sh1x, w_in[l])
        pc = _mm3(cs * (1.0 + sc1c) + sh1c, w_in[l])
        pxa, pxb, pxc = jnp.split(px, col_split, axis=-1)
        pca, pcb, pcc = jnp.split(pc, col_split, axis=-1)

        ya_c, ya_x = _s5_mixer(pca, pxa, s5_lam_re[l], s5_lam_im[l], s5_log_step[l],
                               s5_b_re[l], s5_b_im[l], s5_c_re[l], s5_c_im[l],
                               s5_d[l], s5_glu_w[l], s5_glu_b[l])
        yb_c, yb_x = _hgrn2_mixer(pcb, pxb, lb_all[:, l], hgrn_norm_g[l])
        yc_c, yc_x = _rwkv7_mixer(pcc, pxc, rows, rwkv_mu[l], rwkv_w0[l], rwkv_w_up[l],
                                  rwkv_a0[l], rwkv_a_up[l], rwkv_g_up[l], rwkv_k_k[l],
                                  rwkv_k_a[l], rwkv_r_k[l], rwkv_gn_g[l], rwkv_gn_b[l])

        mix_x = _mm3(jnp.concatenate([ya_x, yb_x, yc_x], axis=-1), w_out[l])
        xs = _layer_norm(DEEPNORM_ALPHA * xs + g1x * mix_x, ln1_g[l], ln1_b[l])
        ffn_x = _peer_ffn(xs * (1.0 + sc2x) + sh2x, peer_wq[l], peer_keys[l], peer_u[l], peer_v[l])
        xs = _layer_norm(DEEPNORM_ALPHA * xs + g2x * ffn_x, ln2_g[l], ln2_b[l])

        if l < DEPTH - 1:
            mix_c = _mm3(jnp.concatenate([ya_c, yb_c, yc_c], axis=-1), w_out[l])
            cs = _layer_norm(DEEPNORM_ALPHA * cs + g1c * mix_c, ln1_g[l], ln1_b[l])
            ffn_c = _peer_ffn(cs * (1.0 + sc2c) + sh2c, peer_wq[l], peer_keys[l], peer_u[l], peer_v[l])
            cs = _layer_norm(DEEPNORM_ALPHA * cs + g2c * ffn_c, ln2_g[l], ln2_b[l])
    return xs
```

```python
import functools
import math

import jax
import jax.numpy as jnp
from jax import lax
from jax.experimental import pallas as pl
from jax.experimental.pallas import tpu as pltpu

D_MODEL = 1024
BATCH = 4
SEQ = 4096
DEPTH = 2
GRID_W = 64
CTX_LEN = 256
D_S5 = D_MODEL // 4
D_HG = D_MODEL // 4
D_RW = D_MODEL // 2
D_MIX = D_S5 + D_HG + D_RW
S5_H = 16
S5_G = D_S5 // S5_H
S5_P = 64
HG_HEAD = 64
HG_HEADS = D_HG // HG_HEAD
HG_CHUNK = 16
RW_HEAD = 64
RW_HEADS = D_RW // RW_HEAD
RW_W_LORA = 64
RW_A_LORA = 64
RW_G_LORA = 128
RW_COLS = 3 * D_RW + RW_G_LORA + 2 * RW_W_LORA + 2 * RW_A_LORA
RW_GN_EPS = 64e-5
P_IN = D_S5 + 5 * D_HG + RW_COLS
PEER_HEADS = 8
PEER_NKEYS = 128
PEER_N = PEER_NKEYS * PEER_NKEYS
PEER_QDIM = 256
PEER_TOPK = 16
PEER_BLOCK = 128
LN_EPS = 1e-5
RMS_EPS = 1e-6
DEEPNORM_ALPHA = (2.0 * DEPTH) ** 0.25
DEEPNORM_BETA = (8.0 * DEPTH) ** -0.25


def _mm_kernel(x_ref, w_ref, o_ref):
    o_ref[...] = jnp.dot(x_ref[...].astype(jnp.bfloat16), w_ref[...],
                         preferred_element_type=jnp.float32)


def _pick_tile(n, cands):
    for c in cands:
        if n % c == 0:
            return c
    return n


def _matmul(x, w):
    m, k = x.shape
    n = w.shape[1]
    tm = _pick_tile(m, (512, 256, 128, 8))
    tn = _pick_tile(n, (1152, 1024, 512, 256, 128))
    return pl.pallas_call(
        _mm_kernel,
        grid=(m // tm, n // tn),
        in_specs=[pl.BlockSpec((tm, k), lambda i, j: (i, 0)),
                  pl.BlockSpec((k, tn), lambda i, j: (0, j))],
        out_specs=pl.BlockSpec((tm, tn), lambda i, j: (i, j)),
        out_shape=jax.ShapeDtypeStruct((m, n), jnp.float32),
        compiler_params=pltpu.CompilerParams(
            dimension_semantics=("parallel", "parallel"),
            vmem_limit_bytes=48 * 1024 * 1024),
    )(x, w.astype(jnp.bfloat16))


def _mm3(x, w):
    b, l, k = x.shape
    return _matmul(x.reshape(b * l, k), w).reshape(b, l, w.shape[1])


_BF = jnp.bfloat16
_NN = (((1,), (0,)), ((), ()))
_NT = (((1,), (1,)), ((), ()))
_TN = (((0,), (0,)), ((), ()))


def _dot(a, b, dims=_NN):
    return lax.dot_general(a.astype(_BF), b.astype(_BF), dims,
                           preferred_element_type=jnp.float32)


def _split_bf16(a):
    hi = a.astype(_BF)
    return hi, (a - hi.astype(jnp.float32)).astype(_BF)


def _dot3(a, b, dims=_NN):
    a_hi, a_lo = _split_bf16(a)
    b_hi, b_lo = _split_bf16(b)
    d = functools.partial(lax.dot_general, dimension_numbers=dims,
                          preferred_element_type=jnp.float32)
    return d(a_hi, b_hi) + (d(a_hi, b_lo) + d(a_lo, b_hi))


RW_CHUNK = 64


def _rwkv_scan_kernel(r_ref, lw_ref, k_ref, v_ref, kk_ref, a_ref, y_ref, st_ref, *, heads, chunk):
    @pl.when(pl.program_id(1) == 0)
    def _():
        st_ref[...] = jnp.zeros_like(st_ref)

    row = lax.broadcasted_iota(jnp.int32, (chunk, chunk), 0)
    col = lax.broadcasted_iota(jnp.int32, (chunk, chunk), 1)
    strict = row > col
    incl = row >= col
    tri = incl.astype(jnp.float32)
    eye = (row == col).astype(jnp.float32)
    blk8 = row // 8 == col // 8
    merge_masks = []
    size = 8
    while size < chunk:
        merge_masks.append((row // (2 * size) == col // (2 * size)) & (row // size != col // size))
        size *= 2
    for h in range(heads):
        lw = lw_ref[h]
        cum = _dot3(tri, lw)
        p = jnp.exp(cum)
        p_inv = jnp.exp(-cum)
        p_prev = jnp.exp(cum - lw)
        p_end = p[chunk - 1:chunk, :]
        kk = kk_ref[h]
        at = -kk * p_prev
        bh = kk * a_ref[h] * p_inv
        kh = k_ref[h] * p_inv
        rt = r_ref[h] * p
        v = v_ref[h]
        s0 = st_ref[h]
        x1 = jnp.concatenate([at, rt], axis=0)
        g1 = _dot(x1, bh, _NT)
        g2 = _dot(x1, kh, _NT)
        hm = _dot(x1, s0, _NT)
        ab = jnp.where(strict, g1[:chunk], 0.0)
        rb = jnp.where(incl, g1[chunk:], 0.0)
        ak = jnp.where(strict, g2[:chunk], 0.0)
        rk = jnp.where(incl, g2[chunk:], 0.0)
        d1 = jnp.where(blk8, ab, 0.0)
        d2 = _dot3(d1, d1)
        inv = eye + d1
        inv = inv + _dot3(inv, d2)
        inv = inv + _dot3(inv, _dot3(d2, d2))
        for m in merge_masks:
            inv = inv + _dot3(inv, _dot3(jnp.where(m, ab, 0.0), inv))
        e = _dot(inv, hm[:chunk] + _dot(ak, v))
        y_ref[h] = hm[chunk:] + _dot(rb, e) + _dot(rk, v)
        ev = jnp.concatenate([e, v], axis=0)
        x2 = jnp.concatenate([bh * p_end, kh * p_end], axis=0)
        st_ref[h] = s0 * p_end + _dot(ev, x2, _TN)


def _rwkv_scan(r, lw, k, v, kk, a, *, heads, interpret=False):
    gh, l, n = r.shape
    chunk = RW_CHUNK
    spec = pl.BlockSpec((heads, chunk, n), lambda g, c: (g, c, 0))
    return pl.pallas_call(
        functools.partial(_rwkv_scan_kernel, heads=heads, chunk=chunk),
        grid=(gh // heads, l // chunk),
        in_specs=[spec] * 6,
        out_specs=spec,
        out_shape=jax.ShapeDtypeStruct((gh, l, n), jnp.float32),
        scratch_shapes=[pltpu.VMEM((heads, n, n), jnp.float32)],
        compiler_params=pltpu.CompilerParams(dimension_semantics=("parallel", "arbitrary")),
        interpret=interpret,
        name="rwkv_scan",
    )(r, lw, k, v, kk, a)


PEER_TN = 512
PEER_EBLK = 1024
_NEG = -3.0e38


def _top_rows(s, n):
    vals = []
    for _ in range(n):
        m = jnp.max(s, axis=0, keepdims=True)
        vals.append(m)
        s = jnp.where(s == m, _NEG, s)
    return vals


def _peer_route_kernel(h_ref, wq_ref, keys_ref, xb_ref, s2_ref, tau_ref, c1_ref, p2_ref):
    hb = h_ref[...].astype(_BF)
    xb_ref[...] = hb
    q = jnp.dot(hb, wq_ref[...], preferred_element_type=jnp.float32)
    half = PEER_QDIM // 2
    for h in range(PEER_HEADS):
        s1 = _dot(keys_ref[2 * h], q[:, (2 * h) * half:(2 * h + 1) * half], _NT)
        s2 = _dot(keys_ref[2 * h + 1], q[:, (2 * h + 1) * half:(2 * h + 2) * half], _NT)
        t1 = _top_rows(s1, PEER_TOPK + 1)
        t2 = _top_rows(s2, PEER_TOPK + 1)
        cand = [t1[j1] + t2[j2] for j1 in range(PEER_TOPK + 1) for j2 in range(PEER_TOPK + 1)
                if (j1 + 1) * (j2 + 1) <= PEER_TOPK + 1]
        cand += [jnp.full_like(cand[0], _NEG)] * (-len(cand) % 8)
        top = _top_rows(jnp.concatenate(cand, axis=0), PEER_TOPK + 1)
        theta = 0.5 * (top[PEER_TOPK - 1] + top[PEER_TOPK])
        z = top[0] * 0.0
        for c in top[:PEER_TOPK]:
            z = z + jnp.exp(c - top[0])
        s2_ref[h] = s2
        tau_ref[h] = theta - s1
        c1_ref[h] = jnp.exp(s1 - t1[0]) / z
        p2_ref[h] = jnp.exp(s2 - t2[0])


def _peer_route(hf, wq, keys, interpret=False):
    t, d = hf.shape
    tn = PEER_TN
    nq = PEER_HEADS * PEER_QDIM
    aux = jax.ShapeDtypeStruct((PEER_HEADS, PEER_NKEYS, t), jnp.float32)
    aux_spec = pl.BlockSpec((PEER_HEADS, PEER_NKEYS, tn), lambda i: (0, 0, i))
    return pl.pallas_call(
        _peer_route_kernel,
        grid=(t // tn,),
        in_specs=[pl.BlockSpec((tn, d), lambda i: (i, 0)),
                  pl.BlockSpec((d, nq), lambda i: (0, 0)),
                  pl.BlockSpec((PEER_HEADS * 2, PEER_NKEYS, PEER_QDIM // 2), lambda i: (0, 0, 0))],
        out_specs=[pl.BlockSpec((tn, d), lambda i: (i, 0))] + [aux_spec] * 4,
        out_shape=[jax.ShapeDtypeStruct((t, d), _BF)] + [aux] * 4,
        compiler_params=pltpu.CompilerParams(dimension_semantics=("parallel",),
                                             vmem_limit_bytes=56 * 1024 * 1024),
        interpret=interpret,
        name="peer_route",
    )(hf, wq.astype(_BF), keys.reshape(PEER_HEADS * 2, PEER_NKEYS, PEER_QDIM // 2))


def _gelu_tanh(z):
    return 0.5 * z * (1.0 + jnp.tanh(math.sqrt(2.0 / math.pi) * (z + 0.044715 * (z * z * z))))


def _peer_expert_kernel(x_ref, u_ref, vt_ref, s2_ref, tau_ref, c1_ref, p2_ref, o_ref, acc_ref, at_ref):
    j = pl.program_id(1)

    @pl.when(j == 0)
    def _():
        acc_ref[...] = jnp.zeros_like(acc_ref)

    nk = PEER_NKEYS
    n_slab = PEER_EBLK // nk
    zt = lax.dot_general(u_ref[...], x_ref[...], _NT, preferred_element_type=jnp.float32)
    rows = pl.ds(pl.multiple_of(j * n_slab, n_slab), n_slab)
    for s in range(n_slab):
        for lt in range(PEER_TN // 128):
            lanes = pl.ds(lt * 128, 128)
            g = jnp.zeros((nk, 128), jnp.float32)
            for h in range(PEER_HEADS):
                tau = tau_ref[h, rows, lanes][s:s + 1]
                c1 = c1_ref[h, rows, lanes][s:s + 1]
                g = g + jnp.where(s2_ref[h, :, lanes] >= tau, p2_ref[h, :, lanes] * c1, 0.0)
            z = zt[s * nk:(s + 1) * nk, lt * 128:(lt + 1) * 128]
            at_ref[s * nk:(s + 1) * nk, lt * 128:(lt + 1) * 128] = (g * _gelu_tanh(z)).astype(_BF)
    acc_ref[...] += jnp.dot(vt_ref[...], at_ref[...], preferred_element_type=jnp.float32)

    @pl.when(j == pl.num_programs(1) - 1)
    def _():
        o_ref[...] = acc_ref[...].T


def _peer_expert(xb, u_b, vt_b, aux, interpret=False):
    t, d = xb.shape
    tn, eb = PEER_TN, PEER_EBLK
    aux_spec = pl.BlockSpec((PEER_HEADS, PEER_NKEYS, tn), lambda i, j: (0, 0, i))
    return pl.pallas_call(
        _peer_expert_kernel,
        grid=(t // tn, PEER_N // eb),
        in_specs=[pl.BlockSpec((tn, d), lambda i, j: (i, 0)),
                  pl.BlockSpec((eb, d), lambda i, j: (j, 0)),
                  pl.BlockSpec((d, eb), lambda i, j: (0, j))] + [aux_spec] * 4,
        out_specs=pl.BlockSpec((tn, d), lambda i, j: (i, 0)),
        out_shape=jax.ShapeDtypeStruct((t, d), jnp.float32),
        scratch_shapes=[pltpu.VMEM((d, tn), jnp.float32), pltpu.VMEM((eb, tn), _BF)],
        compiler_params=pltpu.CompilerParams(dimension_semantics=("parallel", "arbitrary"),
                                             vmem_limit_bytes=56 * 1024 * 1024),
        interpret=interpret,
        name="peer_expert",
    )(xb, u_b, vt_b, *aux)


def _peer_ffn_dense(hf, wq, keys, u_b, vt_b, interpret=False):
    xb, *aux = _peer_route(hf, wq, keys, interpret=interpret)
    return _peer_expert(xb, u_b, vt_b, aux, interpret=interpret)


S5_STEPS = 128
_S5_ROWS = 8
_S5_STATE = S5_G * S5_P


def _s5_scan_kernel(u_ref, wb_ref, lam_ref, wc_ref, y_ref, bu_ref, xs_ref, st_ref):
    @pl.when(pl.program_id(0) == 0)
    def _():
        st_ref[...] = jnp.zeros_like(st_ref)

    n = _S5_STATE
    bu_ref[...] = jnp.dot(u_ref[...].astype(_BF), wb_ref[...], preferred_element_type=jnp.float32)
    lr = lam_ref[0]
    li = lam_ref[1]

    def step(i, carry):
        xr, xi = carry
        rows = pl.ds(pl.multiple_of(i * _S5_ROWS, _S5_ROWS), _S5_ROWS)
        nr = lr * xr - li * xi + bu_ref[rows, :n]
        ni = lr * xi + li * xr + bu_ref[rows, n:]
        xs_ref[rows, :n] = nr
        xs_ref[rows, n:] = ni
        return nr, ni

    xr, xi = lax.fori_loop(0, S5_STEPS, step, (st_ref[0], st_ref[1]), unroll=4)
    st_ref[0] = xr
    st_ref[1] = xi
    y2 = jnp.dot(xs_ref[...].astype(_BF), wc_ref[...], preferred_element_type=jnp.float32)
    fwd = (lax.broadcasted_iota(jnp.int32, (S5_STEPS * _S5_ROWS, 1), 0) % _S5_ROWS) < (_S5_ROWS // 2)
    y_ref[...] = jnp.where(fwd, y2[:, :D_S5], y2[:, D_S5:])


def _s5_scan(u2, wb, lam, wc, interpret=False):
    rows = u2.shape[0]
    blk = S5_STEPS * _S5_ROWS
    n = _S5_STATE
    return pl.pallas_call(
        _s5_scan_kernel,
        grid=(rows // blk,),
        in_specs=[pl.BlockSpec((blk, 2 * D_S5), lambda i: (i, 0)),
                  pl.BlockSpec((2 * D_S5, 2 * n), lambda i: (0, 0)),
                  pl.BlockSpec((2, _S5_ROWS, n), lambda i: (0, 0, 0)),
                  pl.BlockSpec((2 * n, 2 * D_S5), lambda i: (0, 0))],
        out_specs=pl.BlockSpec((blk, D_S5), lambda i: (i, 0)),
        out_shape=jax.ShapeDtypeStruct((rows, D_S5), jnp.float32),
        scratch_shapes=[pltpu.VMEM((blk, 2 * n), jnp.float32), pltpu.VMEM((blk, 2 * n), jnp.float32),
                        pltpu.VMEM((2, _S5_ROWS, n), jnp.float32)],
        compiler_params=pltpu.CompilerParams(dimension_semantics=("arbitrary",),
                                             vmem_limit_bytes=48 * 1024 * 1024),
        interpret=interpret,
        name="s5_scan",
    )(u2, wb, lam, wc)


def _s5_weights(disc, c_re, c_im):
    eye = jnp.eye(S5_G, dtype=jnp.float32)

    def bdiag_in(bb):
        return jnp.einsum('gph,gk->ghkp', bb, eye).reshape(D_S5, _S5_STATE)

    def bdiag_out(cc):
        return jnp.einsum('ghp,gk->gpkh', cc, eye).reshape(_S5_STATE, D_S5)

    wb = jnp.concatenate([
        jnp.concatenate([bdiag_in(disc[d][2]), bdiag_in(disc[d][3])], axis=1) for d in range(2)], axis=0)
    wc = jnp.concatenate([
        jnp.concatenate([bdiag_out(c_re[0]), bdiag_out(c_re[1])], axis=1),
        jnp.concatenate([-bdiag_out(c_im[0]), -bdiag_out(c_im[1])], axis=1)], axis=0)
    half = _S5_ROWS // 2
    lam = jnp.stack([
        jnp.concatenate([jnp.broadcast_to(disc[d][i].reshape(1, _S5_STATE), (half, _S5_STATE))
                         for d in range(2)], axis=0) for i in range(2)])
    return wb.astype(_BF), lam, wc.astype(_BF)


HG_BLOCK = 128
_HG_PAIR = 2 * HG_HEAD


def _hgrn_scan_kernel(q_ref, k_ref, v_ref, lf_ref, o_ref, st_ref):
    @pl.when(pl.program_id(1) == 0)
    def _():
        st_ref[...] = jnp.zeros_like(st_ref)

    ch = HG_CHUNK
    r = lax.broadcasted_iota(jnp.int32, (HG_BLOCK, HG_BLOCK), 0)
    c = lax.broadcasted_iota(jnp.int32, (HG_BLOCK, HG_BLOCK), 1)
    tri = ((r // ch == c // ch) & (r >= c)).astype(jnp.float32)
    same_head = (lax.broadcasted_iota(jnp.int32, (_HG_PAIR, _HG_PAIR), 0) // HG_HEAD
                 == lax.broadcasted_iota(jnp.int32, (_HG_PAIR, _HG_PAIR), 1) // HG_HEAD)
    ones_blk = same_head.astype(_BF)
    tcol = lax.broadcasted_iota(jnp.int32, (ch, 1), 0)
    cum_all = _dot3(tri, lf_ref[0])
    for p in range(D_HG // _HG_PAIR):
        lanes = slice(p * _HG_PAIR, (p + 1) * _HG_PAIR)
        state = st_ref[p]
        for ci in range(HG_BLOCK // ch):
            rows = slice(ci * ch, (ci + 1) * ch)
            q = q_ref[0, rows, lanes]
            k = k_ref[0, rows, lanes]
            v = v_ref[0, rows, lanes]
            cum = cum_all[rows, lanes]
            last = cum[ch - 1:ch]
            o = _dot(q * jnp.exp(cum), state, _NT)
            w = [q * jnp.exp(jnp.minimum(cum - cum[s:s + 1], 0.0)) * k[s:s + 1] for s in range(ch)]
            rel = jnp.dot(jnp.concatenate(w, axis=0).astype(_BF), ones_blk,
                          preferred_element_type=jnp.float32)
            for s in range(ch):
                o = o + jnp.where(tcol >= s, rel[s * ch:(s + 1) * ch], 0.0) * v[s:s + 1]
            o_ref[0, rows, lanes] = o
            kv = _dot(v, k * jnp.exp(last - cum), _TN)
            state = state * jnp.exp(last) + jnp.where(same_head, kv, 0.0)
        st_ref[p] = state


def _hgrn_scan(q, k, v, lf):
    g, l, d = q.shape
    spec = pl.BlockSpec((1, HG_BLOCK, d), lambda i, j: (i, j, 0))
    return pl.pallas_call(
        _hgrn_scan_kernel,
        grid=(g, l // HG_BLOCK),
        in_specs=[spec] * 4,
        out_specs=spec,
        out_shape=jax.ShapeDtypeStruct((g, l, d), jnp.float32),
        scratch_shapes=[pltpu.VMEM((d // _HG_PAIR, _HG_PAIR, _HG_PAIR), jnp.float32)],
        compiler_params=pltpu.CompilerParams(dimension_semantics=("parallel", "arbitrary")),
        name="hgrn_scan",
    )(q, k, v, lf)


def _f32(t):
    return t.astype(jnp.float32)


def _layer_norm(x, g, b):
    xf = _f32(x)
    mu = jnp.mean(xf, -1, keepdims=True)
    var = jnp.mean(jnp.square(xf - mu), -1, keepdims=True)
    return ((xf - mu) * lax.rsqrt(var + LN_EPS) * _f32(g) + _f32(b)).astype(x.dtype)


def _shift(t, axis, step):
    n = t.shape[axis]
    pad = [(0, 0)] * t.ndim
    if step > 0:
        pad[axis] = (1, 0)
        return lax.slice_in_dim(jnp.pad(t, pad), 0, n, axis=axis)
    pad[axis] = (0, 1)
    return lax.slice_in_dim(jnp.pad(t, pad), 1, n + 1, axis=axis)


def _qshift_grid(s, rows):
    b, l, ch = s.shape
    s4 = s.reshape(b, rows, GRID_W, ch // 4, 4)
    out = jnp.stack([_shift(s4[..., 0], 2, 1), _shift(s4[..., 1], 2, -1),
                     _shift(s4[..., 2], 1, 1), _shift(s4[..., 3], 1, -1)], axis=-1)
    return out.reshape(b, l, ch)


def _qshift_seq(s):
    b, l, ch = s.shape
    s4 = s.reshape(b, l, ch // 4, 4)
    out = jnp.stack([_shift(s4[..., 0], 1, 1), _shift(s4[..., 1], 1, -1),
                     _shift(s4[..., 2], 1, 1), _shift(s4[..., 3], 1, -1)], axis=-1)
    return out.reshape(b, l, ch)


def _s5_discretise(lam_re, lam_im, log_step, b_re, b_im):
    dt = jnp.exp(log_step)[:, None]
    mag = jnp.exp(lam_re * dt)
    ang = lam_im * dt
    lb_re, lb_im = mag * jnp.cos(ang), mag * jnp.sin(ang)
    den = lam_re * lam_re + lam_im * lam_im
    nr, ni = lb_re - 1.0, lb_im
    co_re = (nr * lam_re + ni * lam_im) / den
    co_im = (ni * lam_re - nr * lam_im) / den
    bb_re = co_re[..., None] * b_re - co_im[..., None] * b_im
    bb_im = co_re[..., None] * b_im + co_im[..., None] * b_re
    return lb_re, lb_im, bb_re, bb_im


def _s5_mixer(uc, ux, lam_re, lam_im, log_step, b_re, b_im, c_re, c_im, d_skip, glu_w, glu_b,
              interpret=False):
    disc = [_s5_discretise(_f32(lam_re[d]), _f32(lam_im[d]), _f32(log_step[d]),
                           _f32(b_re[d]), _f32(b_im[d])) for d in range(2)]
    c_re, c_im = _f32(c_re), _f32(c_im)

    wb, lam, wc = _s5_weights(disc, c_re, c_im)
    uc, ux = _f32(uc), _f32(ux)
    nb, lc, lx = uc.shape[0], uc.shape[1], ux.shape[1]
    u_f = jnp.concatenate([uc, ux], axis=1)
    u_b = jnp.concatenate([jnp.flip(uc, 1), jnp.flip(ux, 1)], axis=1)
    zeros = jnp.zeros_like(u_f)
    u2 = jnp.concatenate([jnp.concatenate([u_f, zeros], axis=-1),
                          jnp.concatenate([zeros, u_b], axis=-1)], axis=0)
    u2 = u2.transpose(1, 0, 2).reshape((lc + lx) * 2 * nb, 2 * D_S5)
    y = _s5_scan(u2, wb, lam, wc, interpret=interpret)
    y = y.reshape(lc + lx, 2 * nb, D_S5).transpose(1, 0, 2)
    y_f, y_b = y[:nb], y[nb:]

    def post(y, u):
        y = jax.nn.gelu(y + _f32(d_skip) * u)
        return y * jax.nn.sigmoid(_mm3(y, glu_w) + _f32(glu_b))

    yc = post(y_f[:, :lc] + jnp.flip(y_b[:, :lc], 1), uc)
    yx = post(y_f[:, lc:] + jnp.flip(y_b[:, lc:], 1), ux)
    return yc, yx


def _gla_chunkwise(q, k, v, log_f, s0):
    b, h, l, _ = q.shape
    dv = v.shape[-1]
    n = l // HG_CHUNK
    blk = lambda t: t.reshape(b, h, n, HG_CHUNK, t.shape[-1])
    q, k, v, log_f = blk(q), blk(k), blk(v), blk(log_f)
    cum = jnp.cumsum(log_f, axis=3)
    lower = jnp.tril(jnp.ones((HG_CHUNK, HG_CHUNK), bool))[:, :, None]
    rel = jnp.exp(jnp.where(lower, cum[..., :, None, :] - cum[..., None, :, :], -jnp.inf))
    scores = jnp.einsum('bhntd,bhnsd,bhntsd->bhnts', q, k, rel)
    o_intra = jnp.einsum('bhnts,bhnsv->bhntv', scores, v)
    last = cum[..., -1:, :]
    chunk_kv = jnp.einsum('bhnsd,bhnsv->bhndv', k * jnp.exp(last - cum), v)
    chunk_decay = jnp.exp(last[..., 0, :])

    def step(state, inp):
        dec, kv = inp
        return dec[..., None] * state + kv, state

    s_fin, s_prev = lax.scan(step, s0, (jnp.moveaxis(chunk_decay, 2, 0), jnp.moveaxis(chunk_kv, 2, 0)))
    o_inter = jnp.einsum('bhntd,bhndv->bhntv', q * jnp.exp(cum), jnp.moveaxis(s_prev, 0, 2))
    return (o_intra + o_inter).reshape(b, h, l, dv), s_fin


def _hgrn2_mixer(pc, px, lb, norm_g):
    def pre(p):
        q, i, f_f, f_b, g = jnp.split(_f32(p), 5, axis=-1)
        logf = [jnp.log(lb[d] + (1.0 - lb[d]) * jax.nn.sigmoid(f)) for d, f in enumerate((f_f, f_b))]
        return jax.nn.silu(q), i, logf, g

    qc, ic, lfc, gc = pre(pc)
    qx, ix, lfx, gx = pre(px)
    nb, lc = pc.shape[0], pc.shape[1]

    def both(tc, tx):
        tc, tx = (tc, tc) if not isinstance(tc, list) else tc, (tx, tx) if not isinstance(tx, list) else tx
        fwd = jnp.concatenate([tc[0], tx[0]], axis=1)
        bwd = jnp.concatenate([jnp.flip(tc[1], 1), jnp.flip(tx[1], 1)], axis=1)
        return jnp.concatenate([fwd, bwd], axis=0)

    lf = both(lfc, lfx)
    o = _hgrn_scan(both(qc, qx), -jnp.expm1(lf), both(ic, ix), lf)
    o_f, o_b = o[:nb], o[nb:]

    def post(o, g):
        oh = o.reshape(o.shape[0], o.shape[1], HG_HEADS, HG_HEAD)
        oh = oh * lax.rsqrt(jnp.mean(oh * oh, -1, keepdims=True) + RMS_EPS)
        return oh.reshape(o.shape) * _f32(norm_g) * jax.nn.silu(g)

    yc = post(o_f[:, :lc] + jnp.flip(o_b[:, :lc], 1), gc)
    yx = post(o_f[:, lc:] + jnp.flip(o_b[:, lc:], 1), gx)
    return yc.astype(pc.dtype), yx.astype(px.dtype)


def _rwkv7_step(state, inp):
    r, w, k, v, kk, a = inp
    sk = jnp.einsum('bhvk,bhk->bhv', state, kk)
    state = (state * w[:, :, None, :] - sk[..., None] * (kk * a)[:, :, None, :]
             + v[..., None] * k[:, :, None, :])
    return state, jnp.einsum('bhvk,bhk->bhv', state, r)


def _rwkv7_scan(r, w, k, v, kk, a, s0, reverse):
    xs = tuple(jnp.moveaxis(t, 1, 0) for t in (r, w, k, v, kk, a))
    s_fin, y = lax.scan(_rwkv7_step, s0, xs, reverse=reverse)
    return jnp.moveaxis(y, 0, 1), s_fin


def _rwkv7_mixer(pc, px, rows, mu, w0, w_up, a0, a_up, g_up, k_k, k_a, r_k, gn_g, gn_b):
    split_at = [D_RW, 2 * D_RW, 3 * D_RW, 3 * D_RW + RW_G_LORA,
                3 * D_RW + RW_G_LORA + RW_W_LORA, 3 * D_RW + RW_G_LORA + 2 * RW_W_LORA,
                3 * D_RW + RW_G_LORA + 2 * RW_W_LORA + RW_A_LORA]

    def heads(t):
        return t.reshape(t.shape[0], t.shape[1], RW_HEADS, RW_HEAD)

    def pre(p, shifted):
        s = p + _f32(mu) * (shifted - p)
        r, k, v, g_lo, wl_f, wl_b, al_f, al_b = jnp.split(s, split_at, axis=-1)
        g = _mm3(jax.nn.sigmoid(g_lo), g_up)
        kk = heads(k * _f32(k_k))
        kk = kk / jnp.maximum(jnp.sqrt(jnp.sum(kk * kk, -1, keepdims=True)), 1e-12)
        r, v = heads(r), heads(v)
        dirs, bonus = [], 0.0
        for d, (wl, al) in enumerate(((wl_f, al_f), (wl_b, al_b))):
            w_log = -jax.nn.softplus(-(_f32(w0[d]) + _mm3(jnp.tanh(wl), w_up[d]))) - 0.5
            a = jax.nn.sigmoid(_f32(a0[d]) + _mm3(al, a_up[d]))
            k_d = heads(k * (1.0 + (a - 1.0) * _f32(k_a)))
            dirs.append((heads(-jnp.exp(w_log)), k_d, heads(a)))
            bonus = bonus + jnp.sum(r * k_d * _f32(r_k), -1, keepdims=True)
        return dict(r=r, v=v, kk=kk, g=g, dirs=dirs, bonus=bonus)

    def post(y, q, shape):
        mu_y = jnp.mean(y, -1, keepdims=True)
        var_y = jnp.mean(jnp.square(y - mu_y), -1, keepdims=True)
        y = ((y - mu_y) * lax.rsqrt(var_y + RW_GN_EPS) * _f32(gn_g).reshape(RW_HEADS, RW_HEAD)
             + _f32(gn_b).reshape(RW_HEADS, RW_HEAD))
        return (y + q['bonus'] * q['v']).reshape(shape[0], shape[1], D_RW) * q['g']

    pc32, px32 = _f32(pc), _f32(px)
    qc = pre(pc32, _qshift_seq(pc32))
    qx = pre(px32, _qshift_grid(px32, rows))
    lc, lx = pc.shape[1], px.shape[1]

    def seq(tc, tx, d):
        if d == 1:
            tc, tx = jnp.flip(tc, 1), jnp.flip(tx, 1)
        t = jnp.concatenate([tc, tx], axis=1)
        return t.transpose(0, 2, 1, 3).reshape(t.shape[0] * RW_HEADS, lc + lx, RW_HEAD)

    def both(fn):
        return jnp.concatenate([seq(*fn(0), 0), seq(*fn(1), 1)], axis=0)

    y = _rwkv_scan(both(lambda d: (qc['r'], qx['r'])),
                   both(lambda d: (qc['dirs'][d][0], qx['dirs'][d][0])),
                   both(lambda d: (qc['dirs'][d][1], qx['dirs'][d][1])),
                   both(lambda d: (qc['v'], qx['v'])),
                   both(lambda d: (qc['kk'], qx['kk'])),
                   both(lambda d: (qc['dirs'][d][2], qx['dirs'][d][2])),
                   heads=RW_HEADS)
    nb = pc.shape[0]
    y = y.reshape(2, nb, RW_HEADS, lc + lx, RW_HEAD).transpose(0, 1, 3, 2, 4)
    y_c = y[0, :, :lc] + jnp.flip(y[1, :, :lc], 1)
    y_x = y[0, :, lc:] + jnp.flip(y[1, :, lc:], 1)
    yc = post(y_c, qc, pc.shape)
    yx = post(y_x, qx, px.shape)
    return yc.astype(pc.dtype), yx.astype(px.dtype)


def _peer_ffn(h, wq, keys, u_tab, v_tab):
    b, l, d = h.shape
    t = b * l
    hf = h.reshape(t, d)
    q = _matmul(hf, wq).reshape(t, PEER_HEADS, 2, PEER_QDIM // 2)
    scores = _f32(jnp.einsum('thpd,hpkd->thpk', q, keys))
    s_top, i_top = lax.top_k(scores, PEER_TOPK)
    cand = s_top[..., 0, :, None] + s_top[..., 1, None, :]
    cand_s, cand_i = lax.top_k(cand.reshape(t, PEER_HEADS, PEER_TOPK * PEER_TOPK), PEER_TOPK)
    i1 = jnp.take_along_axis(i_top[..., 0, :], cand_i // PEER_TOPK, axis=-1)
    i2 = jnp.take_along_axis(i_top[..., 1, :], cand_i % PEER_TOPK, axis=-1)
    expert = (i1 * PEER_NKEYS + i2).reshape(t, PEER_HEADS * PEER_TOPK)
    gate = jax.nn.softmax(cand_s, axis=-1).reshape(t, PEER_HEADS * PEER_TOPK).astype(h.dtype)
    nb = t // PEER_BLOCK

    def block(args):
        hb, eb, gb = args
        z = jnp.einsum('tkd,td->tk', u_tab[eb], hb)
        return jnp.einsum('tk,tkd->td', jax.nn.gelu(z) * gb, v_tab[eb])

    out = lax.map(block, (hf.reshape(nb, PEER_BLOCK, d),
                          expert.reshape(nb, PEER_BLOCK, PEER_HEADS * PEER_TOPK),
                          gate.reshape(nb, PEER_BLOCK, PEER_HEADS * PEER_TOPK)))
    return out.reshape(b, l, d)


def kernel(x, c, ctx, c_ctx, ada_w, ada_b, w_in, w_out, ln1_g, ln1_b, ln2_g, ln2_b,
           s5_lam_re, s5_lam_im, s5_log_step, s5_b_re, s5_b_im, s5_c_re, s5_c_im,
           s5_d, s5_glu_w, s5_glu_b, hgrn_lb_logits, hgrn_norm_g,
           rwkv_mu, rwkv_w0, rwkv_w_up, rwkv_a0, rwkv_a_up, rwkv_g_up, rwkv_k_k, rwkv_k_a,
           rwkv_r_k, rwkv_gn_g, rwkv_gn_b, peer_wq, peer_keys, peer_u, peer_v):
    rows = x.shape[1] // GRID_W
    lb_cum = jnp.cumsum(jax.nn.softmax(_f32(hgrn_lb_logits), axis=1), axis=1)
    lb_all = lb_cum - lb_cum[:, :1]
    col_split = [D_S5, D_S5 + 5 * D_HG]
    xs, cs = x, ctx
    for l in range(DEPTH):
        mod_x = (jax.nn.silu(c) @ ada_w[l] + ada_b[l])[:, None, :]
        mod_c = (jax.nn.silu(c_ctx) @ ada_w[l] + ada_b[l])[None, None, :]
        sh1x, sc1x, g1x, sh2x, sc2x, g2x = jnp.split(mod_x, 6, axis=-1)
        sh1c, sc1c, g1c, sh2c, sc2c, g2c = jnp.split(mod_c, 6, axis=-1)

        px = _mm3(xs * (1.0 + sc1x) + sh1x, w_in[l])
        pc = _mm3(cs * (1.0 + sc1c) + sh1c, w_in[l])
        pxa, pxb, pxc = jnp.split(px, col_split, axis=-1)
        pca, pcb, pcc = jnp.split(pc, col_split, axis=-1)

        ya_c, ya_x = _s5_mixer(pca, pxa, s5_lam_re[l], s5_lam_im[l], s5_log_step[l],
                               s5_b_re[l], s5_b_im[l], s5_c_re[l], s5_c_im[l],
                               s5_d[l], s5_glu_w[l], s5_glu_b[l])
        yb_c, yb_x = _hgrn2_mixer(pcb, pxb, lb_all[:, l], hgrn_norm_g[l])
        yc_c, yc_x = _rwkv7_mixer(pcc, pxc, rows, rwkv_mu[l], rwkv_w0[l], rwkv_w_up[l],
                                  rwkv_a0[l], rwkv_a_up[l], rwkv_g_up[l], rwkv_k_k[l],
                                  rwkv_k_a[l], rwkv_r_k[l], rwkv_gn_g[l], rwkv_gn_b[l])

        u_b = peer_u[l].astype(_BF)
        vt_b = peer_v[l].T.astype(_BF)
        nx = xs.shape[0] * xs.shape[1]
        mix_x = _mm3(jnp.concatenate([ya_x, yb_x, yc_x], axis=-1), w_out[l])
        xs = _layer_norm(DEEPNORM_ALPHA * xs + g1x * mix_x, ln1_g[l], ln1_b[l])
        h2 = (xs * (1.0 + sc2x) + sh2x).reshape(nx, D_MODEL)
        if l < DEPTH - 1:
            mix_c = _mm3(jnp.concatenate([ya_c, yb_c, yc_c], axis=-1), w_out[l])
            cs = _layer_norm(DEEPNORM_ALPHA * cs + g1c * mix_c, ln1_g[l], ln1_b[l])
            h2 = jnp.concatenate([h2, (cs * (1.0 + sc2c) + sh2c).reshape(-1, D_MODEL)], axis=0)
        ffn = _peer_ffn_dense(h2, peer_wq[l], peer_keys[l], u_b, vt_b)
        xs = _layer_norm(DEEPNORM_ALPHA * xs + g2x * ffn[:nx].reshape(xs.shape), ln2_g[l], ln2_b[l])
        if l < DEPTH - 1:
            cs = _layer_norm(DEEPNORM_ALPHA * cs + g2c * ffn[nx:].reshape(cs.shape), ln2_g[l], ln2_b[l])
    return xs
```

```python
import functools
import math

import jax
import jax.numpy as jnp
from jax import lax
from jax.experimental import pallas as pl
from jax.experimental.pallas import tpu as pltpu

D_MODEL = 1024
BATCH = 4
SEQ = 4096
DEPTH = 2
GRID_W = 64
CTX_LEN = 256
D_S5 = D_MODEL // 4
D_HG = D_MODEL // 4
D_RW = D_MODEL // 2
D_MIX = D_S5 + D_HG + D_RW
S5_H = 16
S5_G = D_S5 // S5_H
S5_P = 64
HG_HEAD = 64
HG_HEADS = D_HG // HG_HEAD
HG_CHUNK = 16
RW_HEAD = 64
RW_HEADS = D_RW // RW_HEAD
RW_W_LORA = 64
RW_A_LORA = 64
RW_G_LORA = 128
RW_COLS = 3 * D_RW + RW_G_LORA + 2 * RW_W_LORA + 2 * RW_A_LORA
RW_GN_EPS = 64e-5
P_IN = D_S5 + 5 * D_HG + RW_COLS
PEER_HEADS = 8
PEER_NKEYS = 128
PEER_N = PEER_NKEYS * PEER_NKEYS
PEER_QDIM = 256
PEER_TOPK = 16
PEER_BLOCK = 128
LN_EPS = 1e-5
RMS_EPS = 1e-6
DEEPNORM_ALPHA = (2.0 * DEPTH) ** 0.25
DEEPNORM_BETA = (8.0 * DEPTH) ** -0.25


def _mm_kernel(x_ref, w_ref, o_ref):
    o_ref[...] = jnp.dot(x_ref[...].astype(jnp.bfloat16), w_ref[...],
                         preferred_element_type=jnp.float32)


def _pick_tile(n, cands):
    for c in cands:
        if n % c == 0:
            return c
    return n


def _matmul(x, w):
    m, k = x.shape
    n = w.shape[1]
    tm = _pick_tile(m, (512, 256, 128, 8))
    tn = _pick_tile(n, (1152, 1024, 512, 256, 128))
    return pl.pallas_call(
        _mm_kernel,
        grid=(m // tm, n // tn),
        in_specs=[pl.BlockSpec((tm, k), lambda i, j: (i, 0)),
                  pl.BlockSpec((k, tn), lambda i, j: (0, j))],
        out_specs=pl.BlockSpec((tm, tn), lambda i, j: (i, j)),
        out_shape=jax.ShapeDtypeStruct((m, n), jnp.float32),
        compiler_params=pltpu.CompilerParams(
            dimension_semantics=("parallel", "parallel"),
            vmem_limit_bytes=48 * 1024 * 1024),
    )(x, w.astype(jnp.bfloat16))


def _mm3(x, w):
    b, l, k = x.shape
    return _matmul(x.reshape(b * l, k), w).reshape(b, l, w.shape[1])


_BF = jnp.bfloat16
_NN = (((1,), (0,)), ((), ()))
_NT = (((1,), (1,)), ((), ()))
_TN = (((0,), (0,)), ((), ()))


def _dot(a, b, dims=_NN):
    return lax.dot_general(a.astype(_BF), b.astype(_BF), dims,
                           preferred_element_type=jnp.float32)


def _split_bf16(a):
    hi = a.astype(_BF)
    return hi, (a - hi.astype(jnp.float32)).astype(_BF)


def _dot3(a, b, dims=_NN):
    a_hi, a_lo = _split_bf16(a)
    b_hi, b_lo = _split_bf16(b)
    d = functools.partial(lax.dot_general, dimension_numbers=dims,
                          preferred_element_type=jnp.float32)
    return d(a_hi, b_hi) + (d(a_hi, b_lo) + d(a_lo, b_hi))


RW_CHUNK = 64


def _rwkv_scan_kernel(r_ref, lw_ref, k_ref, v_ref, kk_ref, a_ref, y_ref, st_ref, *, heads, chunk):
    @pl.when(pl.program_id(1) == 0)
    def _():
        st_ref[...] = jnp.zeros_like(st_ref)

    row = lax.broadcasted_iota(jnp.int32, (chunk, chunk), 0)
    col = lax.broadcasted_iota(jnp.int32, (chunk, chunk), 1)
    strict = row > col
    incl = row >= col
    tri = incl.astype(jnp.float32)
    eye = (row == col).astype(jnp.float32)
    blk8 = row // 8 == col // 8
    merge_masks = []
    size = 8
    while size < chunk:
        merge_masks.append((row // (2 * size) == col // (2 * size)) & (row // size != col // size))
        size *= 2
    hs = range(heads)
    lw = [lw_ref[h] for h in hs]
    cum = [_dot3(tri, lw[h]) for h in hs]
    p = [jnp.exp(cum[h]) for h in hs]
    p_inv = [jnp.exp(-cum[h]) for h in hs]
    p_end = [p[h][chunk - 1:chunk, :] for h in hs]
    kk = [kk_ref[h] for h in hs]
    bh = [kk[h] * a_ref[h] * p_inv[h] for h in hs]
    kh = [k_ref[h] * p_inv[h] for h in hs]
    v = [v_ref[h] for h in hs]
    s0 = [st_ref[h] for h in hs]
    x1 = [jnp.concatenate([-kk[h] * jnp.exp(cum[h] - lw[h]), r_ref[h] * p[h]], axis=0) for h in hs]
    g1 = [_dot(x1[h], bh[h], _NT) for h in hs]
    g2 = [_dot(x1[h], kh[h], _NT) for h in hs]
    hm = [_dot(x1[h], s0[h], _NT) for h in hs]
    ab = [jnp.where(strict, g1[h][:chunk], 0.0) for h in hs]
    d1 = [jnp.where(blk8, ab[h], 0.0) for h in hs]
    d2 = [_dot3(d1[h], d1[h]) for h in hs]
    inv = [eye + d1[h] for h in hs]
    inv = [inv[h] + _dot3(inv[h], d2[h]) for h in hs]
    d4 = [_dot3(d2[h], d2[h]) for h in hs]
    inv = [inv[h] + _dot3(inv[h], d4[h]) for h in hs]
    for m in merge_masks:
        li = [_dot(jnp.where(m, ab[h], 0.0), inv[h]) for h in hs]
        inv = [inv[h] + _dot(inv[h], li[h]) for h in hs]
    rhs = [hm[h][:chunk] + _dot(jnp.where(strict, g2[h][:chunk], 0.0), v[h]) for h in hs]
    e = [_dot(inv[h], rhs[h]) for h in hs]
    for h in hs:
        y_ref[h] = (hm[h][chunk:] + _dot(jnp.where(incl, g1[h][chunk:], 0.0), e[h])
                    + _dot(jnp.where(incl, g2[h][chunk:], 0.0), v[h]))
    for h in hs:
        ev = jnp.concatenate([e[h], v[h]], axis=0)
        x2 = jnp.concatenate([bh[h] * p_end[h], kh[h] * p_end[h]], axis=0)
        st_ref[h] = s0[h] * p_end[h] + _dot(ev, x2, _TN)


def _rwkv_scan(r, lw, k, v, kk, a, *, heads, interpret=False):
    gh, l, n = r.shape
    chunk = RW_CHUNK
    spec = pl.BlockSpec((heads, chunk, n), lambda g, c: (g, c, 0))
    return pl.pallas_call(
        functools.partial(_rwkv_scan_kernel, heads=heads, chunk=chunk),
        grid=(gh // heads, l // chunk),
        in_specs=[spec] * 6,
        out_specs=spec,
        out_shape=jax.ShapeDtypeStruct((gh, l, n), jnp.float32),
        scratch_shapes=[pltpu.VMEM((heads, n, n), jnp.float32)],
        compiler_params=pltpu.CompilerParams(dimension_semantics=("parallel", "arbitrary")),
        interpret=interpret,
        name="rwkv_scan",
    )(r, lw, k, v, kk, a)


PEER_TN = 512
PEER_EBLK = 1024
_NEG = -3.0e38


def _top_rows(s, n):
    vals = []
    for _ in range(n):
        m = jnp.max(s, axis=0, keepdims=True)
        vals.append(m)
        s = jnp.where(s == m, _NEG, s)
    return vals


def _peer_route_kernel(h_ref, wq_ref, keys_ref, xb_ref, s2_ref, tau_ref, c1_ref, p2_ref):
    hb = h_ref[...].astype(_BF)
    xb_ref[...] = hb
    q = jnp.dot(hb, wq_ref[...], preferred_element_type=jnp.float32)
    half = PEER_QDIM // 2
    for h in range(PEER_HEADS):
        s1 = _dot(keys_ref[2 * h], q[:, (2 * h) * half:(2 * h + 1) * half], _NT)
        s2 = _dot(keys_ref[2 * h + 1], q[:, (2 * h + 1) * half:(2 * h + 2) * half], _NT)
        t1 = _top_rows(s1, PEER_TOPK + 1)
        t2 = _top_rows(s2, PEER_TOPK + 1)
        cand = [t1[j1] + t2[j2] for j1 in range(PEER_TOPK + 1) for j2 in range(PEER_TOPK + 1)
                if (j1 + 1) * (j2 + 1) <= PEER_TOPK + 1]
        cand += [jnp.full_like(cand[0], _NEG)] * (-len(cand) % 8)
        top = _top_rows(jnp.concatenate(cand, axis=0), PEER_TOPK + 1)
        theta = 0.5 * (top[PEER_TOPK - 1] + top[PEER_TOPK])
        z = top[0] * 0.0
        for c in top[:PEER_TOPK]:
            z = z + jnp.exp(c - top[0])
        s2_ref[h] = s2
        tau_ref[h] = theta - s1
        c1_ref[h] = jnp.exp(s1 - t1[0]) / z
        p2_ref[h] = jnp.exp(s2 - t2[0])


def _peer_route(hf, wq, keys, interpret=False):
    t, d = hf.shape
    tn = PEER_TN
    nq = PEER_HEADS * PEER_QDIM
    aux = jax.ShapeDtypeStruct((PEER_HEADS, PEER_NKEYS, t), jnp.float32)
    aux_spec = pl.BlockSpec((PEER_HEADS, PEER_NKEYS, tn), lambda i: (0, 0, i))
    return pl.pallas_call(
        _peer_route_kernel,
        grid=(t // tn,),
        in_specs=[pl.BlockSpec((tn, d), lambda i: (i, 0)),
                  pl.BlockSpec((d, nq), lambda i: (0, 0)),
                  pl.BlockSpec((PEER_HEADS * 2, PEER_NKEYS, PEER_QDIM // 2), lambda i: (0, 0, 0))],
        out_specs=[pl.BlockSpec((tn, d), lambda i: (i, 0))] + [aux_spec] * 4,
        out_shape=[jax.ShapeDtypeStruct((t, d), _BF)] + [aux] * 4,
        compiler_params=pltpu.CompilerParams(dimension_semantics=("parallel",),
                                             vmem_limit_bytes=56 * 1024 * 1024),
        interpret=interpret,
        name="peer_route",
    )(hf, wq.astype(_BF), keys.reshape(PEER_HEADS * 2, PEER_NKEYS, PEER_QDIM // 2))


def _gelu_tanh(z):
    return 0.5 * z * (1.0 + jnp.tanh(math.sqrt(2.0 / math.pi) * (z + 0.044715 * (z * z * z))))


def _peer_expert_kernel(x_ref, u_ref, vt_ref, s2_ref, tau_ref, c1_ref, p2_ref, o_ref, acc_ref, at_ref):
    j = pl.program_id(1)

    @pl.when(j == 0)
    def _():
        acc_ref[...] = jnp.zeros_like(acc_ref)

    nk = PEER_NKEYS
    n_slab = PEER_EBLK // nk
    zt = lax.dot_general(u_ref[...], x_ref[...], _NT, preferred_element_type=jnp.float32)
    rows = pl.ds(pl.multiple_of(j * n_slab, n_slab), n_slab)
    for s in range(n_slab):
        for lt in range(PEER_TN // 128):
            lanes = pl.ds(lt * 128, 128)
            g = jnp.zeros((nk, 128), jnp.float32)
            for h in range(PEER_HEADS):
                tau = tau_ref[h, rows, lanes][s:s + 1]
                c1 = c1_ref[h, rows, lanes][s:s + 1]
                g = g + jnp.where(s2_ref[h, :, lanes] >= tau, p2_ref[h, :, lanes] * c1, 0.0)
            z = zt[s * nk:(s + 1) * nk, lt * 128:(lt + 1) * 128]
            at_ref[s * nk:(s + 1) * nk, lt * 128:(lt + 1) * 128] = (g * _gelu_tanh(z)).astype(_BF)
    acc_ref[...] += jnp.dot(vt_ref[...], at_ref[...], preferred_element_type=jnp.float32)

    @pl.when(j == pl.num_programs(1) - 1)
    def _():
        o_ref[...] = acc_ref[...].T


def _peer_expert(xb, u_b, vt_b, aux, interpret=False):
    t, d = xb.shape
    tn, eb = PEER_TN, PEER_EBLK
    aux_spec = pl.BlockSpec((PEER_HEADS, PEER_NKEYS, tn), lambda i, j: (0, 0, i))
    return pl.pallas_call(
        _peer_expert_kernel,
        grid=(t // tn, PEER_N // eb),
        in_specs=[pl.BlockSpec((tn, d), lambda i, j: (i, 0)),
                  pl.BlockSpec((eb, d), lambda i, j: (j, 0)),
                  pl.BlockSpec((d, eb), lambda i, j: (0, j))] + [aux_spec] * 4,
        out_specs=pl.BlockSpec((tn, d), lambda i, j: (i, 0)),
        out_shape=jax.ShapeDtypeStruct((t, d), jnp.float32),
        scratch_shapes=[pltpu.VMEM((d, tn), jnp.float32), pltpu.VMEM((eb, tn), _BF)],
        compiler_params=pltpu.CompilerParams(dimension_semantics=("parallel", "arbitrary"),
                                             vmem_limit_bytes=56 * 1024 * 1024),
        interpret=interpret,
        name="peer_expert",
    )(xb, u_b, vt_b, *aux)


def _peer_ffn_dense(hf, wq, keys, u_b, vt_b, interpret=False):
    xb, *aux = _peer_route(hf, wq, keys, interpret=interpret)
    return _peer_expert(xb, u_b, vt_b, aux, interpret=interpret)


S5_STEPS = 128
_S5_ROWS = 8
_S5_STATE = S5_G * S5_P


def _s5_scan_kernel(u_ref, wb_ref, lam_ref, wc_ref, y_ref, bu_ref, xs_ref, st_ref):
    @pl.when(pl.program_id(0) == 0)
    def _():
        st_ref[...] = jnp.zeros_like(st_ref)

    n = _S5_STATE
    bu_ref[...] = jnp.dot(u_ref[...].astype(_BF), wb_ref[...], preferred_element_type=jnp.float32)
    lr = lam_ref[0]
    li = lam_ref[1]

    def step(i, carry):
        xr, xi = carry
        rows = pl.ds(pl.multiple_of(i * _S5_ROWS, _S5_ROWS), _S5_ROWS)
        nr = lr * xr - li * xi + bu_ref[rows, :n]
        ni = lr * xi + li * xr + bu_ref[rows, n:]
        xs_ref[rows, :n] = nr
        xs_ref[rows, n:] = ni
        return nr, ni

    xr, xi = lax.fori_loop(0, S5_STEPS, step, (st_ref[0], st_ref[1]), unroll=4)
    st_ref[0] = xr
    st_ref[1] = xi
    y2 = jnp.dot(xs_ref[...].astype(_BF), wc_ref[...], preferred_element_type=jnp.float32)
    fwd = (lax.broadcasted_iota(jnp.int32, (S5_STEPS * _S5_ROWS, 1), 0) % _S5_ROWS) < (_S5_ROWS // 2)
    y_ref[...] = jnp.where(fwd, y2[:, :D_S5], y2[:, D_S5:])


def _s5_scan(u2, wb, lam, wc, interpret=False):
    rows = u2.shape[0]
    blk = S5_STEPS * _S5_ROWS
    n = _S5_STATE
    return pl.pallas_call(
        _s5_scan_kernel,
        grid=(rows // blk,),
        in_specs=[pl.BlockSpec((blk, 2 * D_S5), lambda i: (i, 0)),
                  pl.BlockSpec((2 * D_S5, 2 * n), lambda i: (0, 0)),
                  pl.BlockSpec((2, _S5_ROWS, n), lambda i: (0, 0, 0)),
                  pl.BlockSpec((2 * n, 2 * D_S5), lambda i: (0, 0))],
        out_specs=pl.BlockSpec((blk, D_S5), lambda i: (i, 0)),
        out_shape=jax.ShapeDtypeStruct((rows, D_S5), jnp.float32),
        scratch_shapes=[pltpu.VMEM((blk, 2 * n), jnp.float32), pltpu.VMEM((blk, 2 * n), jnp.float32),
                        pltpu.VMEM((2, _S5_ROWS, n), jnp.float32)],
        compiler_params=pltpu.CompilerParams(dimension_semantics=("arbitrary",),
                                             vmem_limit_bytes=48 * 1024 * 1024),
        interpret=interpret,
        name="s5_scan",
    )(u2, wb, lam, wc)


def _s5_weights(disc, c_re, c_im):
    eye = jnp.eye(S5_G, dtype=jnp.float32)

    def bdiag_in(bb):
        return jnp.einsum('gph,gk->ghkp', bb, eye).reshape(D_S5, _S5_STATE)

    def bdiag_out(cc):
        return jnp.einsum('ghp,gk->gpkh', cc, eye).reshape(_S5_STATE, D_S5)

    wb = jnp.concatenate([
        jnp.concatenate([bdiag_in(disc[d][2]), bdiag_in(disc[d][3])], axis=1) for d in range(2)], axis=0)
    wc = jnp.concatenate([
        jnp.concatenate([bdiag_out(c_re[0]), bdiag_out(c_re[1])], axis=1),
        jnp.concatenate([-bdiag_out(c_im[0]), -bdiag_out(c_im[1])], axis=1)], axis=0)
    half = _S5_ROWS // 2
    lam = jnp.stack([
        jnp.concatenate([jnp.broadcast_to(disc[d][i].reshape(1, _S5_STATE), (half, _S5_STATE))
                         for d in range(2)], axis=0) for i in range(2)])
    return wb.astype(_BF), lam, wc.astype(_BF)


HG_BLOCK = 128
_HG_PAIR = 2 * HG_HEAD


def _hgrn_scan_kernel(q_ref, k_ref, v_ref, lf_ref, o_ref, st_ref):
    @pl.when(pl.program_id(1) == 0)
    def _():
        st_ref[...] = jnp.zeros_like(st_ref)

    ch = HG_CHUNK
    r = lax.broadcasted_iota(jnp.int32, (HG_BLOCK, HG_BLOCK), 0)
    c = lax.broadcasted_iota(jnp.int32, (HG_BLOCK, HG_BLOCK), 1)
    tri = ((r // ch == c // ch) & (r >= c)).astype(jnp.float32)
    same_head = (lax.broadcasted_iota(jnp.int32, (_HG_PAIR, _HG_PAIR), 0) // HG_HEAD
                 == lax.broadcasted_iota(jnp.int32, (_HG_PAIR, _HG_PAIR), 1) // HG_HEAD)
    ones_blk = same_head.astype(_BF)
    tcol = lax.broadcasted_iota(jnp.int32, (ch, 1), 0)
    cum_all = _dot3(tri, lf_ref[0])
    for p in range(D_HG // _HG_PAIR):
        lanes = slice(p * _HG_PAIR, (p + 1) * _HG_PAIR)
        state = st_ref[p]
        for ci in range(HG_BLOCK // ch):
            rows = slice(ci * ch, (ci + 1) * ch)
            q = q_ref[0, rows, lanes]
            k = k_ref[0, rows, lanes]
            v = v_ref[0, rows, lanes]
            cum = cum_all[rows, lanes]
            last = cum[ch - 1:ch]
            o = _dot(q * jnp.exp(cum), state, _NT)
            w = [q * jnp.exp(jnp.minimum(cum - cum[s:s + 1], 0.0)) * k[s:s + 1] for s in range(ch)]
            rel = jnp.dot(jnp.concatenate(w, axis=0).astype(_BF), ones_blk,
                          preferred_element_type=jnp.float32)
            for s in range(ch):
                o = o + jnp.where(tcol >= s, rel[s * ch:(s + 1) * ch], 0.0) * v[s:s + 1]
            o_ref[0, rows, lanes] = o
            kv = _dot(v, k * jnp.exp(last - cum), _TN)
            state = state * jnp.exp(last) + jnp.where(same_head, kv, 0.0)
        st_ref[p] = state


def _hgrn_scan(q, k, v, lf):
    g, l, d = q.shape
    spec = pl.BlockSpec((1, HG_BLOCK, d), lambda i, j: (i, j, 0))
    return pl.pallas_call(
        _hgrn_scan_kernel,
        grid=(g, l // HG_BLOCK),
        in_specs=[spec] * 4,
        out_specs=spec,
        out_shape=jax.ShapeDtypeStruct((g, l, d), jnp.float32),
        scratch_shapes=[pltpu.VMEM((d // _HG_PAIR, _HG_PAIR, _HG_PAIR), jnp.float32)],
        compiler_params=pltpu.CompilerParams(dimension_semantics=("parallel", "arbitrary")),
        name="hgrn_scan",
    )(q, k, v, lf)


def _f32(t):
    return t.astype(jnp.float32)


def _layer_norm(x, g, b):
    xf = _f32(x)
    mu = jnp.mean(xf, -1, keepdims=True)
    var = jnp.mean(jnp.square(xf - mu), -1, keepdims=True)
    return ((xf - mu) * lax.rsqrt(var + LN_EPS) * _f32(g) + _f32(b)).astype(x.dtype)


def _shift(t, axis, step):
    n = t.shape[axis]
    pad = [(0, 0)] * t.ndim
    if step > 0:
        pad[axis] = (1, 0)
        return lax.slice_in_dim(jnp.pad(t, pad), 0, n, axis=axis)
    pad[axis] = (0, 1)
    return lax.slice_in_dim(jnp.pad(t, pad), 1, n + 1, axis=axis)


def _qshift_grid(s, rows):
    b, l, ch = s.shape
    s4 = s.reshape(b, rows, GRID_W, ch // 4, 4)
    out = jnp.stack([_shift(s4[..., 0], 2, 1), _shift(s4[..., 1], 2, -1),
                     _shift(s4[..., 2], 1, 1), _shift(s4[..., 3], 1, -1)], axis=-1)
    return out.reshape(b, l, ch)


def _qshift_seq(s):
    b, l, ch = s.shape
    s4 = s.reshape(b, l, ch // 4, 4)
    out = jnp.stack([_shift(s4[..., 0], 1, 1), _shift(s4[..., 1], 1, -1),
                     _shift(s4[..., 2], 1, 1), _shift(s4[..., 3], 1, -1)], axis=-1)
    return out.reshape(b, l, ch)


def _s5_discretise(lam_re, lam_im, log_step, b_re, b_im):
    dt = jnp.exp(log_step)[:, None]
    mag = jnp.exp(lam_re * dt)
    ang = lam_im * dt
    lb_re, lb_im = mag * jnp.cos(ang), mag * jnp.sin(ang)
    den = lam_re * lam_re + lam_im * lam_im
    nr, ni = lb_re - 1.0, lb_im
    co_re = (nr * lam_re + ni * lam_im) / den
    co_im = (ni * lam_re - nr * lam_im) / den
    bb_re = co_re[..., None] * b_re - co_im[..., None] * b_im
    bb_im = co_re[..., None] * b_im + co_im[..., None] * b_re
    return lb_re, lb_im, bb_re, bb_im


def _s5_mixer(uc, ux, lam_re, lam_im, log_step, b_re, b_im, c_re, c_im, d_skip, glu_w, glu_b,
              interpret=False):
    disc = [_s5_discretise(_f32(lam_re[d]), _f32(lam_im[d]), _f32(log_step[d]),
                           _f32(b_re[d]), _f32(b_im[d])) for d in range(2)]
    c_re, c_im = _f32(c_re), _f32(c_im)

    wb, lam, wc = _s5_weights(disc, c_re, c_im)
    uc, ux = _f32(uc), _f32(ux)
    nb, lc, lx = uc.shape[0], uc.shape[1], ux.shape[1]
    u_f = jnp.concatenate([uc, ux], axis=1)
    u_b = jnp.concatenate([jnp.flip(uc, 1), jnp.flip(ux, 1)], axis=1)
    zeros = jnp.zeros_like(u_f)
    u2 = jnp.concatenate([jnp.concatenate([u_f, zeros], axis=-1),
                          jnp.concatenate([zeros, u_b], axis=-1)], axis=0)
    u2 = u2.transpose(1, 0, 2).reshape((lc + lx) * 2 * nb, 2 * D_S5)
    y = _s5_scan(u2, wb, lam, wc, interpret=interpret)
    y = y.reshape(lc + lx, 2 * nb, D_S5).transpose(1, 0, 2)
    y_f, y_b = y[:nb], y[nb:]

    def post(y, u):
        y = jax.nn.gelu(y + _f32(d_skip) * u)
        return y * jax.nn.sigmoid(_mm3(y, glu_w) + _f32(glu_b))

    yc = post(y_f[:, :lc] + jnp.flip(y_b[:, :lc], 1), uc)
    yx = post(y_f[:, lc:] + jnp.flip(y_b[:, lc:], 1), ux)
    return yc, yx


def _gla_chunkwise(q, k, v, log_f, s0):
    b, h, l, _ = q.shape
    dv = v.shape[-1]
    n = l // HG_CHUNK
    blk = lambda t: t.reshape(b, h, n, HG_CHUNK, t.shape[-1])
    q, k, v, log_f = blk(q), blk(k), blk(v), blk(log_f)
    cum = jnp.cumsum(log_f, axis=3)
    lower = jnp.tril(jnp.ones((HG_CHUNK, HG_CHUNK), bool))[:, :, None]
    rel = jnp.exp(jnp.where(lower, cum[..., :, None, :] - cum[..., None, :, :], -jnp.inf))
    scores = jnp.einsum('bhntd,bhnsd,bhntsd->bhnts', q, k, rel)
    o_intra = jnp.einsum('bhnts,bhnsv->bhntv', scores, v)
    last = cum[..., -1:, :]
    chunk_kv = jnp.einsum('bhnsd,bhnsv->bhndv', k * jnp.exp(last - cum), v)
    chunk_decay = jnp.exp(last[..., 0, :])

    def step(state, inp):
        dec, kv = inp
        return dec[..., None] * state + kv, state

    s_fin, s_prev = lax.scan(step, s0, (jnp.moveaxis(chunk_decay, 2, 0), jnp.moveaxis(chunk_kv, 2, 0)))
    o_inter = jnp.einsum('bhntd,bhndv->bhntv', q * jnp.exp(cum), jnp.moveaxis(s_prev, 0, 2))
    return (o_intra + o_inter).reshape(b, h, l, dv), s_fin


def _hgrn2_mixer(pc, px, lb, norm_g):
    def pre(p):
        q, i, f_f, f_b, g = jnp.split(_f32(p), 5, axis=-1)
        logf = [jnp.log(lb[d] + (1.0 - lb[d]) * jax.nn.sigmoid(f)) for d, f in enumerate((f_f, f_b))]
        return jax.nn.silu(q), i, logf, g

    qc, ic, lfc, gc = pre(pc)
    qx, ix, lfx, gx = pre(px)
    nb, lc = pc.shape[0], pc.shape[1]

    def both(tc, tx):
        tc, tx = (tc, tc) if not isinstance(tc, list) else tc, (tx, tx) if not isinstance(tx, list) else tx
        fwd = jnp.concatenate([tc[0], tx[0]], axis=1)
        bwd = jnp.concatenate([jnp.flip(tc[1], 1), jnp.flip(tx[1], 1)], axis=1)
        return jnp.concatenate([fwd, bwd], axis=0)

    lf = both(lfc, lfx)
    o = _hgrn_scan(both(qc, qx), -jnp.expm1(lf), both(ic, ix), lf)
    o_f, o_b = o[:nb], o[nb:]

    def post(o, g):
        oh = o.reshape(o.shape[0], o.shape[1], HG_HEADS, HG_HEAD)
        oh = oh * lax.rsqrt(jnp.mean(oh * oh, -1, keepdims=True) + RMS_EPS)
        return oh.reshape(o.shape) * _f32(norm_g) * jax.nn.silu(g)

    yc = post(o_f[:, :lc] + jnp.flip(o_b[:, :lc], 1), gc)
    yx = post(o_f[:, lc:] + jnp.flip(o_b[:, lc:], 1), gx)
    return yc.astype(pc.dtype), yx.astype(px.dtype)


def _rwkv7_step(state, inp):
    r, w, k, v, kk, a = inp
    sk = jnp.einsum('bhvk,bhk->bhv', state, kk)
    state = (state * w[:, :, None, :] - sk[..., None] * (kk * a)[:, :, None, :]
             + v[..., None] * k[:, :, None, :])
    return state, jnp.einsum('bhvk,bhk->bhv', state, r)


def _rwkv7_scan(r, w, k, v, kk, a, s0, reverse):
    xs = tuple(jnp.moveaxis(t, 1, 0) for t in (r, w, k, v, kk, a))
    s_fin, y = lax.scan(_rwkv7_step, s0, xs, reverse=reverse)
    return jnp.moveaxis(y, 0, 1), s_fin


def _rwkv7_mixer(pc, px, rows, mu, w0, w_up, a0, a_up, g_up, k_k, k_a, r_k, gn_g, gn_b):
    split_at = [D_RW, 2 * D_RW, 3 * D_RW, 3 * D_RW + RW_G_LORA,
                3 * D_RW + RW_G_LORA + RW_W_LORA, 3 * D_RW + RW_G_LORA + 2 * RW_W_LORA,
                3 * D_RW + RW_G_LORA + 2 * RW_W_LORA + RW_A_LORA]

    def heads(t):
        return t.reshape(t.shape[0], t.shape[1], RW_HEADS, RW_HEAD)

    def pre(p, shifted):
        s = p + _f32(mu) * (shifted - p)
        r, k, v, g_lo, wl_f, wl_b, al_f, al_b = jnp.split(s, split_at, axis=-1)
        g = _mm3(jax.nn.sigmoid(g_lo), g_up)
        kk = heads(k * _f32(k_k))
        kk = kk / jnp.maximum(jnp.sqrt(jnp.sum(kk * kk, -1, keepdims=True)), 1e-12)
        r, v = heads(r), heads(v)
        dirs, bonus = [], 0.0
        for d, (wl, al) in enumerate(((wl_f, al_f), (wl_b, al_b))):
            w_log = -jax.nn.softplus(-(_f32(w0[d]) + _mm3(jnp.tanh(wl), w_up[d]))) - 0.5
            a = jax.nn.sigmoid(_f32(a0[d]) + _mm3(al, a_up[d]))
            k_d = heads(k * (1.0 + (a - 1.0) * _f32(k_a)))
            dirs.append((heads(-jnp.exp(w_log)), k_d, heads(a)))
            bonus = bonus + jnp.sum(r * k_d * _f32(r_k), -1, keepdims=True)
        return dict(r=r, v=v, kk=kk, g=g, dirs=dirs, bonus=bonus)

    def post(y, q, shape):
        mu_y = jnp.mean(y, -1, keepdims=True)
        var_y = jnp.mean(jnp.square(y - mu_y), -1, keepdims=True)
        y = ((y - mu_y) * lax.rsqrt(var_y + RW_GN_EPS) * _f32(gn_g).reshape(RW_HEADS, RW_HEAD)
             + _f32(gn_b).reshape(RW_HEADS, RW_HEAD))
        return (y + q['bonus'] * q['v']).reshape(shape[0], shape[1], D_RW) * q['g']

    pc32, px32 = _f32(pc), _f32(px)
    qc = pre(pc32, _qshift_seq(pc32))
    qx = pre(px32, _qshift_grid(px32, rows))
    lc, lx = pc.shape[1], px.shape[1]

    def seq(tc, tx, d):
        if d == 1:
            tc, tx = jnp.flip(tc, 1), jnp.flip(tx, 1)
        t = jnp.concatenate([tc, tx], axis=1)
        return t.transpose(0, 2, 1, 3).reshape(t.shape[0] * RW_HEADS, lc + lx, RW_HEAD)

    def both(fn):
        return jnp.concatenate([seq(*fn(0), 0), seq(*fn(1), 1)], axis=0)

    y = _rwkv_scan(both(lambda d: (qc['r'], qx['r'])),
                   both(lambda d: (qc['dirs'][d][0], qx['dirs'][d][0])),
                   both(lambda d: (qc['dirs'][d][1], qx['dirs'][d][1])),
                   both(lambda d: (qc['v'], qx['v'])),
                   both(lambda d: (qc['kk'], qx['kk'])),
                   both(lambda d: (qc['dirs'][d][2], qx['dirs'][d][2])),
                   heads=RW_HEADS)
    nb = pc.shape[0]
    y = y.reshape(2, nb, RW_HEADS, lc + lx, RW_HEAD).transpose(0, 1, 3, 2, 4)
    y_c = y[0, :, :lc] + jnp.flip(y[1, :, :lc], 1)
    y_x = y[0, :, lc:] + jnp.flip(y[1, :, lc:], 1)
    yc = post(y_c, qc, pc.shape)
    yx = post(y_x, qx, px.shape)
    return yc.astype(pc.dtype), yx.astype(px.dtype)


def _peer_ffn(h, wq, keys, u_tab, v_tab):
    b, l, d = h.shape
    t = b * l
    hf = h.reshape(t, d)
    q = _matmul(hf, wq).reshape(t, PEER_HEADS, 2, PEER_QDIM // 2)
    scores = _f32(jnp.einsum('thpd,hpkd->thpk', q, keys))
    s_top, i_top = lax.top_k(scores, PEER_TOPK)
    cand = s_top[..., 0, :, None] + s_top[..., 1, None, :]
    cand_s, cand_i = lax.top_k(cand.reshape(t, PEER_HEADS, PEER_TOPK * PEER_TOPK), PEER_TOPK)
    i1 = jnp.take_along_axis(i_top[..., 0, :], cand_i // PEER_TOPK, axis=-1)
    i2 = jnp.take_along_axis(i_top[..., 1, :], cand_i % PEER_TOPK, axis=-1)
    expert = (i1 * PEER_NKEYS + i2).reshape(t, PEER_HEADS * PEER_TOPK)
    gate = jax.nn.softmax(cand_s, axis=-1).reshape(t, PEER_HEADS * PEER_TOPK).astype(h.dtype)
    nb = t // PEER_BLOCK

    def block(args):
        hb, eb, gb = args
        z = jnp.einsum('tkd,td->tk', u_tab[eb], hb)
        return jnp.einsum('tk,tkd->td', jax.nn.gelu(z) * gb, v_tab[eb])

    out = lax.map(block, (hf.reshape(nb, PEER_BLOCK, d),
                          expert.reshape(nb, PEER_BLOCK, PEER_HEADS * PEER_TOPK),
                          gate.reshape(nb, PEER_BLOCK, PEER_HEADS * PEER_TOPK)))
    return out.reshape(b, l, d)


def kernel(x, c, ctx, c_ctx, ada_w, ada_b, w_in, w_out, ln1_g, ln1_b, ln2_g, ln2_b,
           s5_lam_re, s5_lam_im, s5_log_step, s5_b_re, s5_b_im, s5_c_re, s5_c_im,
           s5_d, s5_glu_w, s5_glu_b, hgrn_lb_logits, hgrn_norm_g,
           rwkv_mu, rwkv_w0, rwkv_w_up, rwkv_a0, rwkv_a_up, rwkv_g_up, rwkv_k_k, rwkv_k_a,
           rwkv_r_k, rwkv_gn_g, rwkv_gn_b, peer_wq, peer_keys, peer_u, peer_v):
    rows = x.shape[1] // GRID_W
    lb_cum = jnp.cumsum(jax.nn.softmax(_f32(hgrn_lb_logits), axis=1), axis=1)
    lb_all = lb_cum - lb_cum[:, :1]
    col_split = [D_S5, D_S5 + 5 * D_HG]
    xs, cs = x, ctx
    for l in range(DEPTH):
        mod_x = (jax.nn.silu(c) @ ada_w[l] + ada_b[l])[:, None, :]
        mod_c = (jax.nn.silu(c_ctx) @ ada_w[l] + ada_b[l])[None, None, :]
        sh1x, sc1x, g1x, sh2x, sc2x, g2x = jnp.split(mod_x, 6, axis=-1)
        sh1c, sc1c, g1c, sh2c, sc2c, g2c = jnp.split(mod_c, 6, axis=-1)

        px = _mm3(xs * (1.0 + sc1x) + sh1x, w_in[l])
        pc = _mm3(cs * (1.0 + sc1c) + sh1c, w_in[l])
        pxa, pxb, pxc = jnp.split(px, col_split, axis=-1)
        pca, pcb, pcc = jnp.split(pc, col_split, axis=-1)

        ya_c, ya_x = _s5_mixer(pca, pxa, s5_lam_re[l], s5_lam_im[l], s5_log_step[l],
                               s5_b_re[l], s5_b_im[l], s5_c_re[l], s5_c_im[l],
                               s5_d[l], s5_glu_w[l], s5_glu_b[l])
        yb_c, yb_x = _hgrn2_mixer(pcb, pxb, lb_all[:, l], hgrn_norm_g[l])
        yc_c, yc_x = _rwkv7_mixer(pcc, pxc, rows, rwkv_mu[l], rwkv_w0[l], rwkv_w_up[l],
                                  rwkv_a0[l], rwkv_a_up[l], rwkv_g_up[l], rwkv_k_k[l],
                                  rwkv_k_a[l], rwkv_r_k[l], rwkv_gn_g[l], rwkv_gn_b[l])

        u_b = peer_u[l].astype(_BF)
        vt_b = peer_v[l].T.astype(_BF)
        nx = xs.shape[0] * xs.shape[1]
        mix_x = _mm3(jnp.concatenate([ya_x, yb_x, yc_x], axis=-1), w_out[l])
        xs = _layer_norm(DEEPNORM_ALPHA * xs + g1x * mix_x, ln1_g[l], ln1_b[l])
        h2 = (xs * (1.0 + sc2x) + sh2x).reshape(nx, D_MODEL)
        if l < DEPTH - 1:
            mix_c = _mm3(jnp.concatenate([ya_c, yb_c, yc_c], axis=-1), w_out[l])
            cs = _layer_norm(DEEPNORM_ALPHA * cs + g1c * mix_c, ln1_g[l], ln1_b[l])
            h2 = jnp.concatenate([h2, (cs * (1.0 + sc2c) + sh2c).reshape(-1, D_MODEL)], axis=0)
        ffn = _peer_ffn_dense(h2, peer_wq[l], peer_keys[l], u_b, vt_b)
        xs = _layer_norm(DEEPNORM_ALPHA * xs + g2x * ffn[:nx].reshape(xs.shape), ln2_g[l], ln2_b[l])
        if l < DEPTH - 1:
            cs = _layer_norm(DEEPNORM_ALPHA * cs + g2c * ffn[nx:].reshape(cs.shape), ln2_g[l], ln2_b[l])
    return xs
```

```python
import functools
import math

import jax
import jax.numpy as jnp
from jax import lax
from jax.experimental import pallas as pl
from jax.experimental.pallas import tpu as pltpu

D_MODEL = 1024
BATCH = 4
SEQ = 4096
DEPTH = 2
GRID_W = 64
CTX_LEN = 256
D_S5 = D_MODEL // 4
D_HG = D_MODEL // 4
D_RW = D_MODEL // 2
D_MIX = D_S5 + D_HG + D_RW
S5_H = 16
S5_G = D_S5 // S5_H
S5_P = 64
HG_HEAD = 64
HG_HEADS = D_HG // HG_HEAD
HG_CHUNK = 16
RW_HEAD = 64
RW_HEADS = D_RW // RW_HEAD
RW_W_LORA = 64
RW_A_LORA = 64
RW_G_LORA = 128
RW_COLS = 3 * D_RW + RW_G_LORA + 2 * RW_W_LORA + 2 * RW_A_LORA
RW_GN_EPS = 64e-5
P_IN = D_S5 + 5 * D_HG + RW_COLS
PEER_HEADS = 8
PEER_NKEYS = 128
PEER_N = PEER_NKEYS * PEER_NKEYS
PEER_QDIM = 256
PEER_TOPK = 16
PEER_BLOCK = 128
LN_EPS = 1e-5
RMS_EPS = 1e-6
DEEPNORM_ALPHA = (2.0 * DEPTH) ** 0.25
DEEPNORM_BETA = (8.0 * DEPTH) ** -0.25


def _mm_kernel(x_ref, w_ref, o_ref):
    o_ref[...] = jnp.dot(x_ref[...].astype(jnp.bfloat16), w_ref[...],
                         preferred_element_type=jnp.float32)


def _pick_tile(n, cands):
    for c in cands:
        if n % c == 0:
            return c
    return n


def _matmul(x, w):
    m, k = x.shape
    n = w.shape[1]
    tm = _pick_tile(m, (512, 256, 128, 8))
    tn = _pick_tile(n, (1152, 1024, 512, 256, 128))
    return pl.pallas_call(
        _mm_kernel,
        grid=(m // tm, n // tn),
        in_specs=[pl.BlockSpec((tm, k), lambda i, j: (i, 0)),
                  pl.BlockSpec((k, tn), lambda i, j: (0, j))],
        out_specs=pl.BlockSpec((tm, tn), lambda i, j: (i, j)),
        out_shape=jax.ShapeDtypeStruct((m, n), jnp.float32),
        compiler_params=pltpu.CompilerParams(
            dimension_semantics=("parallel", "parallel"),
            vmem_limit_bytes=48 * 1024 * 1024),
    )(x, w.astype(jnp.bfloat16))


def _mm3(x, w):
    b, l, k = x.shape
    return _matmul(x.reshape(b * l, k), w).reshape(b, l, w.shape[1])


_BF = jnp.bfloat16
_NN = (((1,), (0,)), ((), ()))
_NT = (((1,), (1,)), ((), ()))
_TN = (((0,), (0,)), ((), ()))


def _dot(a, b, dims=_NN):
    return lax.dot_general(a.astype(_BF), b.astype(_BF), dims,
                           preferred_element_type=jnp.float32)


def _split_bf16(a):
    hi = a.astype(_BF)
    return hi, (a - hi.astype(jnp.float32)).astype(_BF)


def _dot3(a, b, dims=_NN):
    a_hi, a_lo = _split_bf16(a)
    b_hi, b_lo = _split_bf16(b)
    d = functools.partial(lax.dot_general, dimension_numbers=dims,
                          preferred_element_type=jnp.float32)
    return d(a_hi, b_hi) + (d(a_hi, b_lo) + d(a_lo, b_hi))


RW_CHUNK = 64


def _rwkv_scan_kernel(r_ref, lw_ref, k_ref, v_ref, kk_ref, a_ref, y_ref, st_ref, *, heads, chunk, reverse):
    @pl.when(pl.program_id(1) == 0)
    def _():
        st_ref[...] = jnp.zeros_like(st_ref)

    row = lax.broadcasted_iota(jnp.int32, (chunk, chunk), 0)
    col = lax.broadcasted_iota(jnp.int32, (chunk, chunk), 1)
    strict = row < col if reverse else row > col
    incl = row <= col if reverse else row >= col
    end = 0 if reverse else chunk - 1
    tri = incl.astype(jnp.float32)
    eye = (row == col).astype(jnp.float32)
    blk8 = row // 8 == col // 8
    merge_masks = []
    size = 8
    while size < chunk:
        merge_masks.append((row // (2 * size) == col // (2 * size)) & (row // size != col // size))
        size *= 2
    hs = range(heads)
    lw = [lw_ref[h] for h in hs]
    cum = [_dot3(tri, lw[h]) for h in hs]
    p = [jnp.exp(cum[h]) for h in hs]
    p_inv = [jnp.exp(-cum[h]) for h in hs]
    p_end = [p[h][end:end + 1, :] for h in hs]
    kk = [kk_ref[h] for h in hs]
    bh = [kk[h] * a_ref[h] * p_inv[h] for h in hs]
    kh = [k_ref[h] * p_inv[h] for h in hs]
    v = [v_ref[h] for h in hs]
    s0 = [st_ref[h] for h in hs]
    x1 = [jnp.concatenate([-kk[h] * jnp.exp(cum[h] - lw[h]), r_ref[h] * p[h]], axis=0) for h in hs]
    g1 = [_dot(x1[h], bh[h], _NT) for h in hs]
    g2 = [_dot(x1[h], kh[h], _NT) for h in hs]
    hm = [_dot(x1[h], s0[h], _NT) for h in hs]
    ab = [jnp.where(strict, g1[h][:chunk], 0.0) for h in hs]
    d1 = [jnp.where(blk8, ab[h], 0.0) for h in hs]
    d2 = [_dot3(d1[h], d1[h]) for h in hs]
    inv = [eye + d1[h] for h in hs]
    inv = [inv[h] + _dot3(inv[h], d2[h]) for h in hs]
    d4 = [_dot3(d2[h], d2[h]) for h in hs]
    inv = [inv[h] + _dot3(inv[h], d4[h]) for h in hs]
    for m in merge_masks:
        li = [_dot(jnp.where(m, ab[h], 0.0), inv[h]) for h in hs]
        inv = [inv[h] + _dot(inv[h], li[h]) for h in hs]
    rhs = [hm[h][:chunk] + _dot(jnp.where(strict, g2[h][:chunk], 0.0), v[h]) for h in hs]
    e = [_dot(inv[h], rhs[h]) for h in hs]
    for h in hs:
        y_ref[h] = (hm[h][chunk:] + _dot(jnp.where(incl, g1[h][chunk:], 0.0), e[h])
                    + _dot(jnp.where(incl, g2[h][chunk:], 0.0), v[h]))
    for h in hs:
        ev = jnp.concatenate([e[h], v[h]], axis=0)
        x2 = jnp.concatenate([bh[h] * p_end[h], kh[h] * p_end[h]], axis=0)
        st_ref[h] = s0[h] * p_end[h] + _dot(ev, x2, _TN)


def _scan_block_order(n_ctx, n_all, reverse):
    if not reverse:
        return lambda c: c
    return lambda c: jnp.where(c < n_ctx, n_ctx - 1 - c, n_all - 1 - c + n_ctx)


def _rwkv_scan(r, lw, k, v, kk, a, *, heads, ctx_len=0, reverse=False, interpret=False):
    gh, l, n = r.shape
    chunk = RW_CHUNK
    order = _scan_block_order(ctx_len // chunk, l // chunk, reverse)
    spec = pl.BlockSpec((heads, chunk, n), lambda g, c: (g, order(c), 0))
    return pl.pallas_call(
        functools.partial(_rwkv_scan_kernel, heads=heads, chunk=chunk, reverse=reverse),
        grid=(gh // heads, l // chunk),
        in_specs=[spec] * 6,
        out_specs=spec,
        out_shape=jax.ShapeDtypeStruct((gh, l, n), jnp.float32),
        scratch_shapes=[pltpu.VMEM((heads, n, n), jnp.float32)],
        compiler_params=pltpu.CompilerParams(dimension_semantics=("parallel", "arbitrary")),
        interpret=interpret,
        name="rwkv_scan",
    )(r, lw, k, v, kk, a)


PEER_TN = 512
PEER_EBLK = 1024
_NEG = -3.0e38


def _top_rows(s, n):
    vals = []
    for _ in range(n):
        m = jnp.max(s, axis=0, keepdims=True)
        vals.append(m)
        s = jnp.where(s == m, _NEG, s)
    return vals


def _peer_route_kernel(h_ref, wq_ref, keys_ref, xb_ref, s2_ref, tau_ref, c1_ref, p2_ref):
    hb = h_ref[...].astype(_BF)
    xb_ref[...] = hb
    q = jnp.dot(hb, wq_ref[...], preferred_element_type=jnp.float32)
    half = PEER_QDIM // 2
    for h in range(PEER_HEADS):
        s1 = _dot(keys_ref[2 * h], q[:, (2 * h) * half:(2 * h + 1) * half], _NT)
        s2 = _dot(keys_ref[2 * h + 1], q[:, (2 * h + 1) * half:(2 * h + 2) * half], _NT)
        t1 = _top_rows(s1, PEER_TOPK + 1)
        t2 = _top_rows(s2, PEER_TOPK + 1)
        cand = [t1[j1] + t2[j2] for j1 in range(PEER_TOPK + 1) for j2 in range(PEER_TOPK + 1)
                if (j1 + 1) * (j2 + 1) <= PEER_TOPK + 1]
        cand += [jnp.full_like(cand[0], _NEG)] * (-len(cand) % 8)
        top = _top_rows(jnp.concatenate(cand, axis=0), PEER_TOPK + 1)
        theta = 0.5 * (top[PEER_TOPK - 1] + top[PEER_TOPK])
        z = top[0] * 0.0
        for c in top[:PEER_TOPK]:
            z = z + jnp.exp(c - top[0])
        s2_ref[h] = s2
        tau_ref[h] = theta - s1
        c1_ref[h] = jnp.exp(s1 - t1[0]) / z
        p2_ref[h] = jnp.exp(s2 - t2[0])


def _peer_route(hf, wq, keys, interpret=False):
    t, d = hf.shape
    tn = PEER_TN
    nq = PEER_HEADS * PEER_QDIM
    aux = jax.ShapeDtypeStruct((PEER_HEADS, PEER_NKEYS, t), jnp.float32)
    aux_spec = pl.BlockSpec((PEER_HEADS, PEER_NKEYS, tn), lambda i: (0, 0, i))
    return pl.pallas_call(
        _peer_route_kernel,
        grid=(t // tn,),
        in_specs=[pl.BlockSpec((tn, d), lambda i: (i, 0)),
                  pl.BlockSpec((d, nq), lambda i: (0, 0)),
                  pl.BlockSpec((PEER_HEADS * 2, PEER_NKEYS, PEER_QDIM // 2), lambda i: (0, 0, 0))],
        out_specs=[pl.BlockSpec((tn, d), lambda i: (i, 0))] + [aux_spec] * 4,
        out_shape=[jax.ShapeDtypeStruct((t, d), _BF)] + [aux] * 4,
        compiler_params=pltpu.CompilerParams(dimension_semantics=("parallel",),
                                             vmem_limit_bytes=56 * 1024 * 1024),
        interpret=interpret,
        name="peer_route",
    )(hf, wq.astype(_BF), keys.reshape(PEER_HEADS * 2, PEER_NKEYS, PEER_QDIM // 2))


def _gelu_tanh(z):
    return 0.5 * z * (1.0 + jnp.tanh(math.sqrt(2.0 / math.pi) * (z + 0.044715 * (z * z * z))))


def _peer_expert_kernel(x_ref, u_ref, vt_ref, s2_ref, tau_ref, c1_ref, p2_ref, o_ref, acc_ref, at_ref):
    j = pl.program_id(1)

    @pl.when(j == 0)
    def _():
        acc_ref[...] = jnp.zeros_like(acc_ref)

    nk = PEER_NKEYS
    n_slab = PEER_EBLK // nk
    zt = lax.dot_general(u_ref[...], x_ref[...], _NT, preferred_element_type=jnp.float32)
    rows = pl.ds(pl.multiple_of(j * n_slab, n_slab), n_slab)
    for s in range(n_slab):
        for lt in range(PEER_TN // 128):
            lanes = pl.ds(lt * 128, 128)
            g = jnp.zeros((nk, 128), jnp.float32)
            for h in range(PEER_HEADS):
                tau = tau_ref[h, rows, lanes][s:s + 1]
                c1 = c1_ref[h, rows, lanes][s:s + 1]
                g = g + jnp.where(s2_ref[h, :, lanes] >= tau, p2_ref[h, :, lanes] * c1, 0.0)
            z = zt[s * nk:(s + 1) * nk, lt * 128:(lt + 1) * 128]
            at_ref[s * nk:(s + 1) * nk, lt * 128:(lt + 1) * 128] = (g * _gelu_tanh(z)).astype(_BF)
    acc_ref[...] += jnp.dot(vt_ref[...], at_ref[...], preferred_element_type=jnp.float32)

    @pl.when(j == pl.num_programs(1) - 1)
    def _():
        o_ref[...] = acc_ref[...].T


def _peer_expert(xb, u_b, vt_b, aux, interpret=False):
    t, d = xb.shape
    tn, eb = PEER_TN, PEER_EBLK
    aux_spec = pl.BlockSpec((PEER_HEADS, PEER_NKEYS, tn), lambda i, j: (0, 0, i))
    return pl.pallas_call(
        _peer_expert_kernel,
        grid=(t // tn, PEER_N // eb),
        in_specs=[pl.BlockSpec((tn, d), lambda i, j: (i, 0)),
                  pl.BlockSpec((eb, d), lambda i, j: (j, 0)),
                  pl.BlockSpec((d, eb), lambda i, j: (0, j))] + [aux_spec] * 4,
        out_specs=pl.BlockSpec((tn, d), lambda i, j: (i, 0)),
        out_shape=jax.ShapeDtypeStruct((t, d), jnp.float32),
        scratch_shapes=[pltpu.VMEM((d, tn), jnp.float32), pltpu.VMEM((eb, tn), _BF)],
        compiler_params=pltpu.CompilerParams(dimension_semantics=("parallel", "arbitrary"),
                                             vmem_limit_bytes=56 * 1024 * 1024),
        interpret=interpret,
        name="peer_expert",
    )(xb, u_b, vt_b, *aux)


def _peer_ffn_dense(hf, wq, keys, u_b, vt_b, interpret=False):
    xb, *aux = _peer_route(hf, wq, keys, interpret=interpret)
    return _peer_expert(xb, u_b, vt_b, aux, interpret=interpret)


S5_STEPS = 128
_S5_ROWS = 8
_S5_STATE = S5_G * S5_P


def _s5_scan_kernel(u_ref, wb_ref, lam_ref, wc_ref, y_ref, bu_ref, xs_ref, st_ref):
    @pl.when(pl.program_id(0) == 0)
    def _():
        st_ref[...] = jnp.zeros_like(st_ref)

    n = _S5_STATE
    bu_ref[...] = jnp.dot(u_ref[...].astype(_BF), wb_ref[...], preferred_element_type=jnp.float32)
    lr = lam_ref[0]
    li = lam_ref[1]

    def step(i, carry):
        xr, xi = carry
        rows = pl.ds(pl.multiple_of(i * _S5_ROWS, _S5_ROWS), _S5_ROWS)
        nr = lr * xr - li * xi + bu_ref[rows, :n]
        ni = lr * xi + li * xr + bu_ref[rows, n:]
        xs_ref[rows, :n] = nr
        xs_ref[rows, n:] = ni
        return nr, ni

    xr, xi = lax.fori_loop(0, S5_STEPS, step, (st_ref[0], st_ref[1]), unroll=4)
    st_ref[0] = xr
    st_ref[1] = xi
    y2 = jnp.dot(xs_ref[...].astype(_BF), wc_ref[...], preferred_element_type=jnp.float32)
    fwd = (lax.broadcasted_iota(jnp.int32, (S5_STEPS * _S5_ROWS, 1), 0) % _S5_ROWS) < (_S5_ROWS // 2)
    y_ref[...] = jnp.where(fwd, y2[:, :D_S5], y2[:, D_S5:])


def _s5_scan(u2, wb, lam, wc, interpret=False):
    rows = u2.shape[0]
    blk = S5_STEPS * _S5_ROWS
    n = _S5_STATE
    return pl.pallas_call(
        _s5_scan_kernel,
        grid=(rows // blk,),
        in_specs=[pl.BlockSpec((blk, 2 * D_S5), lambda i: (i, 0)),
                  pl.BlockSpec((2 * D_S5, 2 * n), lambda i: (0, 0)),
                  pl.BlockSpec((2, _S5_ROWS, n), lambda i: (0, 0, 0)),
                  pl.BlockSpec((2 * n, 2 * D_S5), lambda i: (0, 0))],
        out_specs=pl.BlockSpec((blk, D_S5), lambda i: (i, 0)),
        out_shape=jax.ShapeDtypeStruct((rows, D_S5), jnp.float32),
        scratch_shapes=[pltpu.VMEM((blk, 2 * n), jnp.float32), pltpu.VMEM((blk, 2 * n), jnp.float32),
                        pltpu.VMEM((2, _S5_ROWS, n), jnp.float32)],
        compiler_params=pltpu.CompilerParams(dimension_semantics=("arbitrary",),
                                             vmem_limit_bytes=48 * 1024 * 1024),
        interpret=interpret,
        name="s5_scan",
    )(u2, wb, lam, wc)


def _s5_weights(disc, c_re, c_im):
    eye = jnp.eye(S5_G, dtype=jnp.float32)

    def bdiag_in(bb):
        return jnp.einsum('gph,gk->ghkp', bb, eye).reshape(D_S5, _S5_STATE)

    def bdiag_out(cc):
        return jnp.einsum('ghp,gk->gpkh', cc, eye).reshape(_S5_STATE, D_S5)

    wb = jnp.concatenate([
        jnp.concatenate([bdiag_in(disc[d][2]), bdiag_in(disc[d][3])], axis=1) for d in range(2)], axis=0)
    wc = jnp.concatenate([
        jnp.concatenate([bdiag_out(c_re[0]), bdiag_out(c_re[1])], axis=1),
        jnp.concatenate([-bdiag_out(c_im[0]), -bdiag_out(c_im[1])], axis=1)], axis=0)
    half = _S5_ROWS // 2
    lam = jnp.stack([
        jnp.concatenate([jnp.broadcast_to(disc[d][i].reshape(1, _S5_STATE), (half, _S5_STATE))
                         for d in range(2)], axis=0) for i in range(2)])
    return wb.astype(_BF), lam, wc.astype(_BF)


HG_BLOCK = 128
_HG_PAIR = 2 * HG_HEAD


def _hgrn_scan_kernel(q_ref, k_ref, v_ref, lf_ref, o_ref, st_ref):
    @pl.when(pl.program_id(1) == 0)
    def _():
        st_ref[...] = jnp.zeros_like(st_ref)

    ch = HG_CHUNK
    r = lax.broadcasted_iota(jnp.int32, (HG_BLOCK, HG_BLOCK), 0)
    c = lax.broadcasted_iota(jnp.int32, (HG_BLOCK, HG_BLOCK), 1)
    tri = ((r // ch == c // ch) & (r >= c)).astype(jnp.float32)
    same_head = (lax.broadcasted_iota(jnp.int32, (_HG_PAIR, _HG_PAIR), 0) // HG_HEAD
                 == lax.broadcasted_iota(jnp.int32, (_HG_PAIR, _HG_PAIR), 1) // HG_HEAD)
    ones_blk = same_head.astype(_BF)
    tcol = lax.broadcasted_iota(jnp.int32, (ch, 1), 0)
    cum_all = _dot3(tri, lf_ref[0])
    for p in range(D_HG // _HG_PAIR):
        lanes = slice(p * _HG_PAIR, (p + 1) * _HG_PAIR)
        state = st_ref[p]
        for ci in range(HG_BLOCK // ch):
            rows = slice(ci * ch, (ci + 1) * ch)
            q = q_ref[0, rows, lanes]
            k = k_ref[0, rows, lanes]
            v = v_ref[0, rows, lanes]
            cum = cum_all[rows, lanes]
            last = cum[ch - 1:ch]
            o = _dot(q * jnp.exp(cum), state, _NT)
            w = [q * jnp.exp(jnp.minimum(cum - cum[s:s + 1], 0.0)) * k[s:s + 1] for s in range(ch)]
            rel = jnp.dot(jnp.concatenate(w, axis=0).astype(_BF), ones_blk,
                          preferred_element_type=jnp.float32)
            for s in range(ch):
                o = o + jnp.where(tcol >= s, rel[s * ch:(s + 1) * ch], 0.0) * v[s:s + 1]
            o_ref[0, rows, lanes] = o
            kv = _dot(v, k * jnp.exp(last - cum), _TN)
            state = state * jnp.exp(last) + jnp.where(same_head, kv, 0.0)
        st_ref[p] = state


def _hgrn_scan(q, k, v, lf):
    g, l, d = q.shape
    spec = pl.BlockSpec((1, HG_BLOCK, d), lambda i, j: (i, j, 0))
    return pl.pallas_call(
        _hgrn_scan_kernel,
        grid=(g, l // HG_BLOCK),
        in_specs=[spec] * 4,
        out_specs=spec,
        out_shape=jax.ShapeDtypeStruct((g, l, d), jnp.float32),
        scratch_shapes=[pltpu.VMEM((d // _HG_PAIR, _HG_PAIR, _HG_PAIR), jnp.float32)],
        compiler_params=pltpu.CompilerParams(dimension_semantics=("parallel", "arbitrary")),
        name="hgrn_scan",
    )(q, k, v, lf)


def _hgrn_dir_kernel(q_ref, k_ref, v_ref, lf_ref, o_ref, st_ref, *, reverse):
    @pl.when(pl.program_id(1) == 0)
    def _():
        st_ref[...] = jnp.zeros_like(st_ref)

    ch = HG_CHUNK
    n_ch = HG_BLOCK // ch
    r = lax.broadcasted_iota(jnp.int32, (HG_BLOCK, HG_BLOCK), 0)
    c = lax.broadcasted_iota(jnp.int32, (HG_BLOCK, HG_BLOCK), 1)
    tri = ((r // ch == c // ch) & ((r <= c) if reverse else (r >= c))).astype(jnp.float32)
    same_head = (lax.broadcasted_iota(jnp.int32, (_HG_PAIR, _HG_PAIR), 0) // HG_HEAD
                 == lax.broadcasted_iota(jnp.int32, (_HG_PAIR, _HG_PAIR), 1) // HG_HEAD)
    ones_blk = same_head.astype(_BF)
    tcol = lax.broadcasted_iota(jnp.int32, (ch, 1), 0)
    end = 0 if reverse else ch - 1
    cum_all = _dot3(tri, lf_ref[0])
    n_pair = D_HG // _HG_PAIR
    intra, kv, dec, qd = {}, {}, {}, {}
    for p in range(n_pair):
        lanes = slice(p * _HG_PAIR, (p + 1) * _HG_PAIR)
        for ci in range(n_ch):
            rows = slice(ci * ch, (ci + 1) * ch)
            q = q_ref[0, rows, lanes]
            k = k_ref[0, rows, lanes]
            v = v_ref[0, rows, lanes]
            cum = cum_all[rows, lanes]
            last = cum[end:end + 1]
            w = [q * jnp.exp(jnp.minimum(cum - cum[s:s + 1], 0.0)) * k[s:s + 1] for s in range(ch)]
            rel = jnp.dot(jnp.concatenate(w, axis=0).astype(_BF), ones_blk,
                          preferred_element_type=jnp.float32)
            o = jnp.zeros((ch, _HG_PAIR), jnp.float32)
            for s in range(ch):
                seen = (tcol <= s) if reverse else (tcol >= s)
                o = o + jnp.where(seen, rel[s * ch:(s + 1) * ch], 0.0) * v[s:s + 1]
            intra[p, ci] = o
            kv[p, ci] = jnp.where(same_head, _dot(v, k * jnp.exp(last - cum), _TN), 0.0)
            dec[p, ci] = jnp.exp(last)
            qd[p, ci] = q * jnp.exp(cum)
    for p in range(n_pair):
        lanes = slice(p * _HG_PAIR, (p + 1) * _HG_PAIR)
        state = st_ref[p]
        for ci in (range(n_ch - 1, -1, -1) if reverse else range(n_ch)):
            o_ref[0, ci * ch:(ci + 1) * ch, lanes] = intra[p, ci] + _dot(qd[p, ci], state, _NT)
            state = state * dec[p, ci] + kv[p, ci]
        st_ref[p] = state


def _hgrn_dir_scan(q, k, v, lf, *, ctx_len, reverse):
    g, l, d = q.shape
    order = _scan_block_order(ctx_len // HG_BLOCK, l // HG_BLOCK, reverse)
    spec = pl.BlockSpec((1, HG_BLOCK, d), lambda i, j: (i, order(j), 0))
    return pl.pallas_call(
        functools.partial(_hgrn_dir_kernel, reverse=reverse),
        grid=(g, l // HG_BLOCK),
        in_specs=[spec] * 4,
        out_specs=spec,
        out_shape=jax.ShapeDtypeStruct((g, l, d), jnp.float32),
        scratch_shapes=[pltpu.VMEM((d // _HG_PAIR, _HG_PAIR, _HG_PAIR), jnp.float32)],
        compiler_params=pltpu.CompilerParams(dimension_semantics=("parallel", "arbitrary")),
        name="hgrn_scan",
    )(q, k, v, lf)


def _f32(t):
    return t.astype(jnp.float32)


def _layer_norm(x, g, b):
    xf = _f32(x)
    mu = jnp.mean(xf, -1, keepdims=True)
    var = jnp.mean(jnp.square(xf - mu), -1, keepdims=True)
    return ((xf - mu) * lax.rsqrt(var + LN_EPS) * _f32(g) + _f32(b)).astype(x.dtype)


def _shift(t, axis, step):
    n = t.shape[axis]
    pad = [(0, 0)] * t.ndim
    if step > 0:
        pad[axis] = (1, 0)
        return lax.slice_in_dim(jnp.pad(t, pad), 0, n, axis=axis)
    pad[axis] = (0, 1)
    return lax.slice_in_dim(jnp.pad(t, pad), 1, n + 1, axis=axis)


def _qshift_grid(s, rows):
    b, l, ch = s.shape
    s4 = s.reshape(b, rows, GRID_W, ch // 4, 4)
    out = jnp.stack([_shift(s4[..., 0], 2, 1), _shift(s4[..., 1], 2, -1),
                     _shift(s4[..., 2], 1, 1), _shift(s4[..., 3], 1, -1)], axis=-1)
    return out.reshape(b, l, ch)


def _qshift_seq(s):
    b, l, ch = s.shape
    s4 = s.reshape(b, l, ch // 4, 4)
    out = jnp.stack([_shift(s4[..., 0], 1, 1), _shift(s4[..., 1], 1, -1),
                     _shift(s4[..., 2], 1, 1), _shift(s4[..., 3], 1, -1)], axis=-1)
    return out.reshape(b, l, ch)


def _s5_discretise(lam_re, lam_im, log_step, b_re, b_im):
    dt = jnp.exp(log_step)[:, None]
    mag = jnp.exp(lam_re * dt)
    ang = lam_im * dt
    lb_re, lb_im = mag * jnp.cos(ang), mag * jnp.sin(ang)
    den = lam_re * lam_re + lam_im * lam_im
    nr, ni = lb_re - 1.0, lb_im
    co_re = (nr * lam_re + ni * lam_im) / den
    co_im = (ni * lam_re - nr * lam_im) / den
    bb_re = co_re[..., None] * b_re - co_im[..., None] * b_im
    bb_im = co_re[..., None] * b_im + co_im[..., None] * b_re
    return lb_re, lb_im, bb_re, bb_im


def _s5_mixer(uc, ux, lam_re, lam_im, log_step, b_re, b_im, c_re, c_im, d_skip, glu_w, glu_b,
              interpret=False):
    disc = [_s5_discretise(_f32(lam_re[d]), _f32(lam_im[d]), _f32(log_step[d]),
                           _f32(b_re[d]), _f32(b_im[d])) for d in range(2)]
    c_re, c_im = _f32(c_re), _f32(c_im)

    wb, lam, wc = _s5_weights(disc, c_re, c_im)
    uc, ux = _f32(uc), _f32(ux)
    nb, lc, lx = uc.shape[0], uc.shape[1], ux.shape[1]
    u_f = jnp.concatenate([uc, ux], axis=1)
    u_b = jnp.concatenate([jnp.flip(uc, 1), jnp.flip(ux, 1)], axis=1)
    zeros = jnp.zeros_like(u_f)
    u2 = jnp.concatenate([jnp.concatenate([u_f, zeros], axis=-1),
                          jnp.concatenate([zeros, u_b], axis=-1)], axis=0)
    u2 = u2.transpose(1, 0, 2).reshape((lc + lx) * 2 * nb, 2 * D_S5)
    y = _s5_scan(u2, wb, lam, wc, interpret=interpret)
    y = y.reshape(lc + lx, 2 * nb, D_S5).transpose(1, 0, 2)
    y_f, y_b = y[:nb], y[nb:]

    def post(y, u):
        y = jax.nn.gelu(y + _f32(d_skip) * u)
        return y * jax.nn.sigmoid(_mm3(y, glu_w) + _f32(glu_b))

    yc = post(y_f[:, :lc] + jnp.flip(y_b[:, :lc], 1), uc)
    yx = post(y_f[:, lc:] + jnp.flip(y_b[:, lc:], 1), ux)
    return yc, yx


def _gla_chunkwise(q, k, v, log_f, s0):
    b, h, l, _ = q.shape
    dv = v.shape[-1]
    n = l // HG_CHUNK
    blk = lambda t: t.reshape(b, h, n, HG_CHUNK, t.shape[-1])
    q, k, v, log_f = blk(q), blk(k), blk(v), blk(log_f)
    cum = jnp.cumsum(log_f, axis=3)
    lower = jnp.tril(jnp.ones((HG_CHUNK, HG_CHUNK), bool))[:, :, None]
    rel = jnp.exp(jnp.where(lower, cum[..., :, None, :] - cum[..., None, :, :], -jnp.inf))
    scores = jnp.einsum('bhntd,bhnsd,bhntsd->bhnts', q, k, rel)
    o_intra = jnp.einsum('bhnts,bhnsv->bhntv', scores, v)
    last = cum[..., -1:, :]
    chunk_kv = jnp.einsum('bhnsd,bhnsv->bhndv', k * jnp.exp(last - cum), v)
    chunk_decay = jnp.exp(last[..., 0, :])

    def step(state, inp):
        dec, kv = inp
        return dec[..., None] * state + kv, state

    s_fin, s_prev = lax.scan(step, s0, (jnp.moveaxis(chunk_decay, 2, 0), jnp.moveaxis(chunk_kv, 2, 0)))
    o_inter = jnp.einsum('bhntd,bhndv->bhntv', q * jnp.exp(cum), jnp.moveaxis(s_prev, 0, 2))
    return (o_intra + o_inter).reshape(b, h, l, dv), s_fin


def _hgrn2_mixer(pc, px, lb, norm_g):
    def pre(p):
        q, i, f_f, f_b, g = jnp.split(_f32(p), 5, axis=-1)
        logf = [jnp.log(lb[d] + (1.0 - lb[d]) * jax.nn.sigmoid(f)) for d, f in enumerate((f_f, f_b))]
        return jax.nn.silu(q), i, logf, g

    qc, ic, lfc, gc = pre(pc)
    qx, ix, lfx, gx = pre(px)
    nb, lc = pc.shape[0], pc.shape[1]

    q = jnp.concatenate([qc, qx], axis=1)
    v = jnp.concatenate([ic, ix], axis=1)
    o = 0.0
    for d in range(2):
        lf = jnp.concatenate([lfc[d], lfx[d]], axis=1)
        o = o + _hgrn_dir_scan(q, -jnp.expm1(lf), v, lf, ctx_len=lc, reverse=d == 1)

    def post(o, g):
        oh = o.reshape(o.shape[0], o.shape[1], HG_HEADS, HG_HEAD)
        oh = oh * lax.rsqrt(jnp.mean(oh * oh, -1, keepdims=True) + RMS_EPS)
        return oh.reshape(o.shape) * _f32(norm_g) * jax.nn.silu(g)

    yc = post(o[:, :lc], gc)
    yx = post(o[:, lc:], gx)
    return yc.astype(pc.dtype), yx.astype(px.dtype)


def _rwkv7_step(state, inp):
    r, w, k, v, kk, a = inp
    sk = jnp.einsum('bhvk,bhk->bhv', state, kk)
    state = (state * w[:, :, None, :] - sk[..., None] * (kk * a)[:, :, None, :]
             + v[..., None] * k[:, :, None, :])
    return state, jnp.einsum('bhvk,bhk->bhv', state, r)


def _rwkv7_scan(r, w, k, v, kk, a, s0, reverse):
    xs = tuple(jnp.moveaxis(t, 1, 0) for t in (r, w, k, v, kk, a))
    s_fin, y = lax.scan(_rwkv7_step, s0, xs, reverse=reverse)
    return jnp.moveaxis(y, 0, 1), s_fin


def _rwkv7_mixer(pc, px, rows, mu, w0, w_up, a0, a_up, g_up, k_k, k_a, r_k, gn_g, gn_b):
    split_at = [D_RW, 2 * D_RW, 3 * D_RW, 3 * D_RW + RW_G_LORA,
                3 * D_RW + RW_G_LORA + RW_W_LORA, 3 * D_RW + RW_G_LORA + 2 * RW_W_LORA,
                3 * D_RW + RW_G_LORA + 2 * RW_W_LORA + RW_A_LORA]

    def heads(t):
        return t.reshape(t.shape[0], t.shape[1], RW_HEADS, RW_HEAD)

    def pre(p, shifted):
        s = p + _f32(mu) * (shifted - p)
        r, k, v, g_lo, wl_f, wl_b, al_f, al_b = jnp.split(s, split_at, axis=-1)
        g = _mm3(jax.nn.sigmoid(g_lo), g_up)
        kk = heads(k * _f32(k_k))
        kk = kk / jnp.maximum(jnp.sqrt(jnp.sum(kk * kk, -1, keepdims=True)), 1e-12)
        r, v = heads(r), heads(v)
        dirs, bonus = [], 0.0
        for d, (wl, al) in enumerate(((wl_f, al_f), (wl_b, al_b))):
            w_log = -jax.nn.softplus(-(_f32(w0[d]) + _mm3(jnp.tanh(wl), w_up[d]))) - 0.5
            a = jax.nn.sigmoid(_f32(a0[d]) + _mm3(al, a_up[d]))
            k_d = heads(k * (1.0 + (a - 1.0) * _f32(k_a)))
            dirs.append((heads(-jnp.exp(w_log)), k_d, heads(a)))
            bonus = bonus + jnp.sum(r * k_d * _f32(r_k), -1, keepdims=True)
        return dict(r=r, v=v, kk=kk, g=g, dirs=dirs, bonus=bonus)

    def post(y, q, shape):
        mu_y = jnp.mean(y, -1, keepdims=True)
        var_y = jnp.mean(jnp.square(y - mu_y), -1, keepdims=True)
        y = ((y - mu_y) * lax.rsqrt(var_y + RW_GN_EPS) * _f32(gn_g).reshape(RW_HEADS, RW_HEAD)
             + _f32(gn_b).reshape(RW_HEADS, RW_HEAD))
        return (y + q['bonus'] * q['v']).reshape(shape[0], shape[1], D_RW) * q['g']

    pc32, px32 = _f32(pc), _f32(px)
    qc = pre(pc32, _qshift_seq(pc32))
    qx = pre(px32, _qshift_grid(px32, rows))
    lc, lx = pc.shape[1], px.shape[1]

    def seq(tc, tx):
        t = jnp.concatenate([tc, tx], axis=1)
        return t.transpose(0, 2, 1, 3).reshape(t.shape[0] * RW_HEADS, lc + lx, RW_HEAD)

    r, v, kk = (seq(qc[n], qx[n]) for n in ('r', 'v', 'kk'))
    y = 0.0
    for d in range(2):
        lw, k_d, a = (seq(qc['dirs'][d][i], qx['dirs'][d][i]) for i in range(3))
        y = y + _rwkv_scan(r, lw, k_d, v, kk, a, heads=RW_HEADS, ctx_len=lc, reverse=d == 1)
    nb = pc.shape[0]
    y = y.reshape(nb, RW_HEADS, lc + lx, RW_HEAD).transpose(0, 2, 1, 3)
    yc = post(y[:, :lc], qc, pc.shape)
    yx = post(y[:, lc:], qx, px.shape)
    return yc.astype(pc.dtype), yx.astype(px.dtype)


def _peer_ffn(h, wq, keys, u_tab, v_tab):
    b, l, d = h.shape
    t = b * l
    hf = h.reshape(t, d)
    q = _matmul(hf, wq).reshape(t, PEER_HEADS, 2, PEER_QDIM // 2)
    scores = _f32(jnp.einsum('thpd,hpkd->thpk', q, keys))
    s_top, i_top = lax.top_k(scores, PEER_TOPK)
    cand = s_top[..., 0, :, None] + s_top[..., 1, None, :]
    cand_s, cand_i = lax.top_k(cand.reshape(t, PEER_HEADS, PEER_TOPK * PEER_TOPK), PEER_TOPK)
    i1 = jnp.take_along_axis(i_top[..., 0, :], cand_i // PEER_TOPK, axis=-1)
    i2 = jnp.take_along_axis(i_top[..., 1, :], cand_i % PEER_TOPK, axis=-1)
    expert = (i1 * PEER_NKEYS + i2).reshape(t, PEER_HEADS * PEER_TOPK)
    gate = jax.nn.softmax(cand_s, axis=-1).reshape(t, PEER_HEADS * PEER_TOPK).astype(h.dtype)
    nb = t // PEER_BLOCK

    def block(args):
        hb, eb, gb = args
        z = jnp.einsum('tkd,td->tk', u_tab[eb], hb)
        return jnp.einsum('tk,tkd->td', jax.nn.gelu(z) * gb, v_tab[eb])

    out = lax.map(block, (hf.reshape(nb, PEER_BLOCK, d),
                          expert.reshape(nb, PEER_BLOCK, PEER_HEADS * PEER_TOPK),
                          gate.reshape(nb, PEER_BLOCK, PEER_HEADS * PEER_TOPK)))
    return out.reshape(b, l, d)


def kernel(x, c, ctx, c_ctx, ada_w, ada_b, w_in, w_out, ln1_g, ln1_b, ln2_g, ln2_b,
           s5_lam_re, s5_lam_im, s5_log_step, s5_b_re, s5_b_im, s5_c_re, s5_c_im,
           s5_d, s5_glu_w, s5_glu_b, hgrn_lb_logits, hgrn_norm_g,
           rwkv_mu, rwkv_w0, rwkv_w_up, rwkv_a0, rwkv_a_up, rwkv_g_up, rwkv_k_k, rwkv_k_a,
           rwkv_r_k, rwkv_gn_g, rwkv_gn_b, peer_wq, peer_keys, peer_u, peer_v):
    rows = x.shape[1] // GRID_W
    lb_cum = jnp.cumsum(jax.nn.softmax(_f32(hgrn_lb_logits), axis=1), axis=1)
    lb_all = lb_cum - lb_cum[:, :1]
    col_split = [D_S5, D_S5 + 5 * D_HG]
    xs, cs = x, ctx
    for l in range(DEPTH):
        mod_x = (jax.nn.silu(c) @ ada_w[l] + ada_b[l])[:, None, :]
        mod_c = (jax.nn.silu(c_ctx) @ ada_w[l] + ada_b[l])[None, None, :]
        sh1x, sc1x, g1x, sh2x, sc2x, g2x = jnp.split(mod_x, 6, axis=-1)
        sh1c, sc1c, g1c, sh2c, sc2c, g2c = jnp.split(mod_c, 6, axis=-1)

        px = _mm3(xs * (1.0 + sc1x) + sh1x, w_in[l])
        pc = _mm3(cs * (1.0 + sc1c) + sh1c, w_in[l])
        pxa, pxb, pxc = jnp.split(px, col_split, axis=-1)
        pca, pcb, pcc = jnp.split(pc, col_split, axis=-1)

        ya_c, ya_x = _s5_mixer(pca, pxa, s5_lam_re[l], s5_lam_im[l], s5_log_step[l],
                               s5_b_re[l], s5_b_im[l], s5_c_re[l], s5_c_im[l],
                               s5_d[l], s5_glu_w[l], s5_glu_b[l])
        yb_c, yb_x = _hgrn2_mixer(pcb, pxb, lb_all[:, l], hgrn_norm_g[l])
        yc_c, yc_x = _rwkv7_mixer(pcc, pxc, rows, rwkv_mu[l], rwkv_w0[l], rwkv_w_up[l],
                                  rwkv_a0[l], rwkv_a_up[l], rwkv_g_up[l], rwkv_k_k[l],
                                  rwkv_k_a[l], rwkv_r_k[l], rwkv_gn_g[l], rwkv_gn_b[l])

        u_b = peer_u[l].astype(_BF)
        vt_b = peer_v[l].T.astype(_BF)
        nx = xs.shape[0] * xs.shape[1]
        mix_x = _mm3(jnp.concatenate([ya_x, yb_x, yc_x], axis=-1), w_out[l])
        xs = _layer_norm(DEEPNORM_ALPHA * xs + g1x * mix_x, ln1_g[l], ln1_b[l])
        h2 = (xs * (1.0 + sc2x) + sh2x).reshape(nx, D_MODEL)
        if l < DEPTH - 1:
            mix_c = _mm3(jnp.concatenate([ya_c, yb_c, yc_c], axis=-1), w_out[l])
            cs = _layer_norm(DEEPNORM_ALPHA * cs + g1c * mix_c, ln1_g[l], ln1_b[l])
            h2 = jnp.concatenate([h2, (cs * (1.0 + sc2c) + sh2c).reshape(-1, D_MODEL)], axis=0)
        ffn = _peer_ffn_dense(h2, peer_wq[l], peer_keys[l], u_b, vt_b)
        xs = _layer_norm(DEEPNORM_ALPHA * xs + g2x * ffn[:nx].reshape(xs.shape), ln2_g[l], ln2_b[l])
        if l < DEPTH - 1:
            cs = _layer_norm(DEEPNORM_ALPHA * cs + g2c * ffn[nx:].reshape(cs.shape), ln2_g[l], ln2_b[l])
    return xs
```

```python
import functools
import math

import jax
import jax.numpy as jnp
from jax import lax
from jax.experimental import pallas as pl
from jax.experimental.pallas import tpu as pltpu

D_MODEL = 1024
BATCH = 4
SEQ = 4096
DEPTH = 2
GRID_W = 64
CTX_LEN = 256
D_S5 = D_MODEL // 4
D_HG = D_MODEL // 4
D_RW = D_MODEL // 2
D_MIX = D_S5 + D_HG + D_RW
S5_H = 16
S5_G = D_S5 // S5_H
S5_P = 64
HG_HEAD = 64
HG_HEADS = D_HG // HG_HEAD
HG_CHUNK = 16
RW_HEAD = 64
RW_HEADS = D_RW // RW_HEAD
RW_W_LORA = 64
RW_A_LORA = 64
RW_G_LORA = 128
RW_COLS = 3 * D_RW + RW_G_LORA + 2 * RW_W_LORA + 2 * RW_A_LORA
RW_GN_EPS = 64e-5
P_IN = D_S5 + 5 * D_HG + RW_COLS
PEER_HEADS = 8
PEER_NKEYS = 128
PEER_N = PEER_NKEYS * PEER_NKEYS
PEER_QDIM = 256
PEER_TOPK = 16
PEER_BLOCK = 128
LN_EPS = 1e-5
RMS_EPS = 1e-6
DEEPNORM_ALPHA = (2.0 * DEPTH) ** 0.25
DEEPNORM_BETA = (8.0 * DEPTH) ** -0.25


def _mm_kernel(x_ref, w_ref, o_ref):
    o_ref[...] = jnp.dot(x_ref[...].astype(jnp.bfloat16), w_ref[...],
                         preferred_element_type=jnp.float32)


def _pick_tile(n, cands):
    for c in cands:
        if n % c == 0:
            return c
    return n


def _matmul(x, w):
    m, k = x.shape
    n = w.shape[1]
    tm = _pick_tile(m, (512, 256, 128, 8))
    tn = _pick_tile(n, (1152, 1024, 512, 256, 128))
    return pl.pallas_call(
        _mm_kernel,
        grid=(m // tm, n // tn),
        in_specs=[pl.BlockSpec((tm, k), lambda i, j: (i, 0)),
                  pl.BlockSpec((k, tn), lambda i, j: (0, j))],
        out_specs=pl.BlockSpec((tm, tn), lambda i, j: (i, j)),
        out_shape=jax.ShapeDtypeStruct((m, n), jnp.float32),
        compiler_params=pltpu.CompilerParams(
            dimension_semantics=("parallel", "parallel"),
            vmem_limit_bytes=48 * 1024 * 1024),
    )(x, w.astype(jnp.bfloat16))


def _mm3(x, w):
    b, l, k = x.shape
    return _matmul(x.reshape(b * l, k), w).reshape(b, l, w.shape[1])


_BF = jnp.bfloat16
_NN = (((1,), (0,)), ((), ()))
_NT = (((1,), (1,)), ((), ()))
_TN = (((0,), (0,)), ((), ()))


def _dot(a, b, dims=_NN):
    return lax.dot_general(a.astype(_BF), b.astype(_BF), dims,
                           preferred_element_type=jnp.float32)


def _split_bf16(a):
    hi = a.astype(_BF)
    return hi, (a - hi.astype(jnp.float32)).astype(_BF)


def _dot3(a, b, dims=_NN):
    a_hi, a_lo = _split_bf16(a)
    b_hi, b_lo = _split_bf16(b)
    d = functools.partial(lax.dot_general, dimension_numbers=dims,
                          preferred_element_type=jnp.float32)
    return d(a_hi, b_hi) + (d(a_hi, b_lo) + d(a_lo, b_hi))


RW_CHUNK = 64


def _rwkv_scan_kernel(r_ref, lw_ref, k_ref, v_ref, kk_ref, a_ref, y_ref, st_ref, *, heads, chunk, reverse):
    @pl.when(pl.program_id(1) == 0)
    def _():
        st_ref[...] = jnp.zeros_like(st_ref)

    row = lax.broadcasted_iota(jnp.int32, (chunk, chunk), 0)
    col = lax.broadcasted_iota(jnp.int32, (chunk, chunk), 1)
    strict = row < col if reverse else row > col
    incl = row <= col if reverse else row >= col
    end = 0 if reverse else chunk - 1
    tri = incl.astype(jnp.float32)
    eye = (row == col).astype(jnp.float32)
    blk8 = row // 8 == col // 8
    merge_masks = []
    size = 8
    while size < chunk:
        merge_masks.append((row // (2 * size) == col // (2 * size)) & (row // size != col // size))
        size *= 2
    hs = range(heads)
    lw = [lw_ref[h] for h in hs]
    cum = [_dot3(tri, lw[h]) for h in hs]
    p = [jnp.exp(cum[h]) for h in hs]
    p_inv = [jnp.exp(-cum[h]) for h in hs]
    p_end = [p[h][end:end + 1, :] for h in hs]
    kk = [kk_ref[h] for h in hs]
    bh = [kk[h] * a_ref[h] * p_inv[h] for h in hs]
    kh = [k_ref[h] * p_inv[h] for h in hs]
    v = [v_ref[h] for h in hs]
    s0 = [st_ref[h] for h in hs]
    x1 = [jnp.concatenate([-kk[h] * jnp.exp(cum[h] - lw[h]), r_ref[h] * p[h]], axis=0) for h in hs]
    g1 = [_dot(x1[h], bh[h], _NT) for h in hs]
    g2 = [_dot(x1[h], kh[h], _NT) for h in hs]
    hm = [_dot(x1[h], s0[h], _NT) for h in hs]
    ab = [jnp.where(strict, g1[h][:chunk], 0.0) for h in hs]
    d1 = [jnp.where(blk8, ab[h], 0.0) for h in hs]
    d2 = [_dot3(d1[h], d1[h]) for h in hs]
    inv = [eye + d1[h] for h in hs]
    inv = [inv[h] + _dot3(inv[h], d2[h]) for h in hs]
    d4 = [_dot3(d2[h], d2[h]) for h in hs]
    inv = [inv[h] + _dot3(inv[h], d4[h]) for h in hs]
    for m in merge_masks:
        li = [_dot(jnp.where(m, ab[h], 0.0), inv[h]) for h in hs]
        inv = [inv[h] + _dot(inv[h], li[h]) for h in hs]
    rhs = [hm[h][:chunk] + _dot(jnp.where(strict, g2[h][:chunk], 0.0), v[h]) for h in hs]
    e = [_dot(inv[h], rhs[h]) for h in hs]
    for h in hs:
        y_ref[h] = (hm[h][chunk:] + _dot(jnp.where(incl, g1[h][chunk:], 0.0), e[h])
                    + _dot(jnp.where(incl, g2[h][chunk:], 0.0), v[h]))
    for h in hs:
        ev = jnp.concatenate([e[h], v[h]], axis=0)
        x2 = jnp.concatenate([bh[h] * p_end[h], kh[h] * p_end[h]], axis=0)
        st_ref[h] = s0[h] * p_end[h] + _dot(ev, x2, _TN)


def _scan_block_order(n_ctx, n_all, reverse):
    if not reverse:
        return lambda c: c
    return lambda c: jnp.where(c < n_ctx, n_ctx - 1 - c, n_all - 1 - c + n_ctx)


def _rwkv_scan(r, lw, k, v, kk, a, *, heads, ctx_len=0, reverse=False, interpret=False):
    gh, l, n = r.shape
    chunk = RW_CHUNK
    order = _scan_block_order(ctx_len // chunk, l // chunk, reverse)
    spec = pl.BlockSpec((heads, chunk, n), lambda g, c: (g, order(c), 0))
    return pl.pallas_call(
        functools.partial(_rwkv_scan_kernel, heads=heads, chunk=chunk, reverse=reverse),
        grid=(gh // heads, l // chunk),
        in_specs=[spec] * 6,
        out_specs=spec,
        out_shape=jax.ShapeDtypeStruct((gh, l, n), jnp.float32),
        scratch_shapes=[pltpu.VMEM((heads, n, n), jnp.float32)],
        compiler_params=pltpu.CompilerParams(dimension_semantics=("parallel", "arbitrary")),
        interpret=interpret,
        name="rwkv_scan",
    )(r, lw, k, v, kk, a)


PEER_TN = 512
PEER_EBLK = 1024
_NEG = -3.0e38


def _top_rows(s, n):
    vals = []
    for _ in range(n):
        m = jnp.max(s, axis=0, keepdims=True)
        vals.append(m)
        s = jnp.where(s == m, _NEG, s)
    return vals


def _peer_route_kernel(h_ref, wq_ref, keys_ref, xb_ref, s2_ref, tau_ref, c1_ref, p2_ref):
    hb = h_ref[...].astype(_BF)
    xb_ref[...] = hb
    q = jnp.dot(hb, wq_ref[...], preferred_element_type=jnp.float32)
    half = PEER_QDIM // 2
    for h in range(PEER_HEADS):
        s1 = _dot(keys_ref[2 * h], q[:, (2 * h) * half:(2 * h + 1) * half], _NT)
        s2 = _dot(keys_ref[2 * h + 1], q[:, (2 * h + 1) * half:(2 * h + 2) * half], _NT)
        t1 = _top_rows(s1, PEER_TOPK + 1)
        t2 = _top_rows(s2, PEER_TOPK + 1)
        cand = [t1[j1] + t2[j2] for j1 in range(PEER_TOPK + 1) for j2 in range(PEER_TOPK + 1)
                if (j1 + 1) * (j2 + 1) <= PEER_TOPK + 1]
        cand += [jnp.full_like(cand[0], _NEG)] * (-len(cand) % 8)
        top = _top_rows(jnp.concatenate(cand, axis=0), PEER_TOPK + 1)
        theta = 0.5 * (top[PEER_TOPK - 1] + top[PEER_TOPK])
        z = top[0] * 0.0
        for c in top[:PEER_TOPK]:
            z = z + jnp.exp(c - top[0])
        s2_ref[h] = s2
        tau_ref[h] = theta - s1
        c1_ref[h] = jnp.exp(s1 - t1[0]) / z
        p2_ref[h] = jnp.exp(s2 - t2[0])


def _peer_route(hf, wq, keys, interpret=False):
    t, d = hf.shape
    tn = PEER_TN
    nq = PEER_HEADS * PEER_QDIM
    aux = jax.ShapeDtypeStruct((PEER_HEADS, PEER_NKEYS, t), jnp.float32)
    aux_spec = pl.BlockSpec((PEER_HEADS, PEER_NKEYS, tn), lambda i: (0, 0, i))
    return pl.pallas_call(
        _peer_route_kernel,
        grid=(t // tn,),
        in_specs=[pl.BlockSpec((tn, d), lambda i: (i, 0)),
                  pl.BlockSpec((d, nq), lambda i: (0, 0)),
                  pl.BlockSpec((PEER_HEADS * 2, PEER_NKEYS, PEER_QDIM // 2), lambda i: (0, 0, 0))],
        out_specs=[pl.BlockSpec((tn, d), lambda i: (i, 0))] + [aux_spec] * 4,
        out_shape=[jax.ShapeDtypeStruct((t, d), _BF)] + [aux] * 4,
        compiler_params=pltpu.CompilerParams(dimension_semantics=("parallel",),
                                             vmem_limit_bytes=56 * 1024 * 1024),
        interpret=interpret,
        name="peer_route",
    )(hf, wq.astype(_BF), keys.reshape(PEER_HEADS * 2, PEER_NKEYS, PEER_QDIM // 2))


def _gelu_tanh(z):
    return 0.5 * z * (1.0 + jnp.tanh(math.sqrt(2.0 / math.pi) * (z + 0.044715 * (z * z * z))))


def _peer_expert_kernel(x_ref, u_ref, vt_ref, s2_ref, tau_ref, c1_ref, p2_ref, o_ref, acc_ref, at_ref):
    j = pl.program_id(1)

    @pl.when(j == 0)
    def _():
        acc_ref[...] = jnp.zeros_like(acc_ref)

    nk = PEER_NKEYS
    n_slab = PEER_EBLK // nk
    zt = lax.dot_general(u_ref[...], x_ref[...], _NT, preferred_element_type=jnp.float32)
    rows = pl.ds(pl.multiple_of(j * n_slab, n_slab), n_slab)
    for s in range(n_slab):
        for lt in range(PEER_TN // 128):
            lanes = pl.ds(lt * 128, 128)
            g = jnp.zeros((nk, 128), jnp.float32)
            for h in range(PEER_HEADS):
                tau = tau_ref[h, rows, lanes][s:s + 1]
                c1 = c1_ref[h, rows, lanes][s:s + 1]
                g = g + jnp.where(s2_ref[h, :, lanes] >= tau, p2_ref[h, :, lanes] * c1, 0.0)
            z = zt[s * nk:(s + 1) * nk, lt * 128:(lt + 1) * 128]
            at_ref[s * nk:(s + 1) * nk, lt * 128:(lt + 1) * 128] = (g * _gelu_tanh(z)).astype(_BF)
    acc_ref[...] += jnp.dot(vt_ref[...], at_ref[...], preferred_element_type=jnp.float32)

    @pl.when(j == pl.num_programs(1) - 1)
    def _():
        o_ref[...] = acc_ref[...].T


def _peer_expert(xb, u_b, vt_b, aux, interpret=False):
    t, d = xb.shape
    tn, eb = PEER_TN, PEER_EBLK
    aux_spec = pl.BlockSpec((PEER_HEADS, PEER_NKEYS, tn), lambda i, j: (0, 0, i))
    return pl.pallas_call(
        _peer_expert_kernel,
        grid=(t // tn, PEER_N // eb),
        in_specs=[pl.BlockSpec((tn, d), lambda i, j: (i, 0)),
                  pl.BlockSpec((eb, d), lambda i, j: (j, 0)),
                  pl.BlockSpec((d, eb), lambda i, j: (0, j))] + [aux_spec] * 4,
        out_specs=pl.BlockSpec((tn, d), lambda i, j: (i, 0)),
        out_shape=jax.ShapeDtypeStruct((t, d), jnp.float32),
        scratch_shapes=[pltpu.VMEM((d, tn), jnp.float32), pltpu.VMEM((eb, tn), _BF)],
        compiler_params=pltpu.CompilerParams(dimension_semantics=("parallel", "arbitrary"),
                                             vmem_limit_bytes=56 * 1024 * 1024),
        interpret=interpret,
        name="peer_expert",
    )(xb, u_b, vt_b, *aux)


def _peer_ffn_dense(hf, wq, keys, u_b, vt_b, interpret=False):
    xb, *aux = _peer_route(hf, wq, keys, interpret=interpret)
    return _peer_expert(xb, u_b, vt_b, aux, interpret=interpret)


S5_STEPS = 128
_S5_ROWS = 8
_S5_STATE = S5_G * S5_P


def _s5_scan_kernel(u_ref, wb_ref, lam_ref, wc_ref, y_ref, bu_ref, xs_ref, st_ref):
    @pl.when(pl.program_id(0) == 0)
    def _():
        st_ref[...] = jnp.zeros_like(st_ref)

    n = _S5_STATE
    bu_ref[...] = jnp.dot(u_ref[...].astype(_BF), wb_ref[...], preferred_element_type=jnp.float32)
    lr = lam_ref[0]
    li = lam_ref[1]

    def step(i, carry):
        xr, xi = carry
        rows = pl.ds(pl.multiple_of(i * _S5_ROWS, _S5_ROWS), _S5_ROWS)
        nr = lr * xr - li * xi + bu_ref[rows, :n]
        ni = lr * xi + li * xr + bu_ref[rows, n:]
        xs_ref[rows, :n] = nr
        xs_ref[rows, n:] = ni
        return nr, ni

    xr, xi = lax.fori_loop(0, S5_STEPS, step, (st_ref[0], st_ref[1]), unroll=4)
    st_ref[0] = xr
    st_ref[1] = xi
    y2 = jnp.dot(xs_ref[...].astype(_BF), wc_ref[...], preferred_element_type=jnp.float32)
    fwd = (lax.broadcasted_iota(jnp.int32, (S5_STEPS * _S5_ROWS, 1), 0) % _S5_ROWS) < (_S5_ROWS // 2)
    y_ref[...] = jnp.where(fwd, y2[:, :D_S5], y2[:, D_S5:])


def _s5_scan(u2, wb, lam, wc, interpret=False):
    rows = u2.shape[0]
    blk = S5_STEPS * _S5_ROWS
    n = _S5_STATE
    return pl.pallas_call(
        _s5_scan_kernel,
        grid=(rows // blk,),
        in_specs=[pl.BlockSpec((blk, 2 * D_S5), lambda i: (i, 0)),
                  pl.BlockSpec((2 * D_S5, 2 * n), lambda i: (0, 0)),
                  pl.BlockSpec((2, _S5_ROWS, n), lambda i: (0, 0, 0)),
                  pl.BlockSpec((2 * n, 2 * D_S5), lambda i: (0, 0))],
        out_specs=pl.BlockSpec((blk, D_S5), lambda i: (i, 0)),
        out_shape=jax.ShapeDtypeStruct((rows, D_S5), jnp.float32),
        scratch_shapes=[pltpu.VMEM((blk, 2 * n), jnp.float32), pltpu.VMEM((blk, 2 * n), jnp.float32),
                        pltpu.VMEM((2, _S5_ROWS, n), jnp.float32)],
        compiler_params=pltpu.CompilerParams(dimension_semantics=("arbitrary",),
                                             vmem_limit_bytes=48 * 1024 * 1024),
        interpret=interpret,
        name="s5_scan",
    )(u2, wb, lam, wc)


def _s5_weights(disc, c_re, c_im):
    eye = jnp.eye(S5_G, dtype=jnp.float32)

    def bdiag_in(bb):
        return jnp.einsum('gph,gk->ghkp', bb, eye).reshape(D_S5, _S5_STATE)

    def bdiag_out(cc):
        return jnp.einsum('ghp,gk->gpkh', cc, eye).reshape(_S5_STATE, D_S5)

    wb = jnp.concatenate([
        jnp.concatenate([bdiag_in(disc[d][2]), bdiag_in(disc[d][3])], axis=1) for d in range(2)], axis=0)
    wc = jnp.concatenate([
        jnp.concatenate([bdiag_out(c_re[0]), bdiag_out(c_re[1])], axis=1),
        jnp.concatenate([-bdiag_out(c_im[0]), -bdiag_out(c_im[1])], axis=1)], axis=0)
    half = _S5_ROWS // 2
    lam = jnp.stack([
        jnp.concatenate([jnp.broadcast_to(disc[d][i].reshape(1, _S5_STATE), (half, _S5_STATE))
                         for d in range(2)], axis=0) for i in range(2)])
    return wb.astype(_BF), lam, wc.astype(_BF)


HG_BLOCK = 128
_HG_PAIR = 2 * HG_HEAD


def _hgrn_scan_kernel(q_ref, k_ref, v_ref, lf_ref, o_ref, st_ref):
    @pl.when(pl.program_id(1) == 0)
    def _():
        st_ref[...] = jnp.zeros_like(st_ref)

    ch = HG_CHUNK
    r = lax.broadcasted_iota(jnp.int32, (HG_BLOCK, HG_BLOCK), 0)
    c = lax.broadcasted_iota(jnp.int32, (HG_BLOCK, HG_BLOCK), 1)
    tri = ((r // ch == c // ch) & (r >= c)).astype(jnp.float32)
    same_head = (lax.broadcasted_iota(jnp.int32, (_HG_PAIR, _HG_PAIR), 0) // HG_HEAD
                 == lax.broadcasted_iota(jnp.int32, (_HG_PAIR, _HG_PAIR), 1) // HG_HEAD)
    ones_blk = same_head.astype(_BF)
    tcol = lax.broadcasted_iota(jnp.int32, (ch, 1), 0)
    cum_all = _dot3(tri, lf_ref[0])
    for p in range(D_HG // _HG_PAIR):
        lanes = slice(p * _HG_PAIR, (p + 1) * _HG_PAIR)
        state = st_ref[p]
        for ci in range(HG_BLOCK // ch):
            rows = slice(ci * ch, (ci + 1) * ch)
            q = q_ref[0, rows, lanes]
            k = k_ref[0, rows, lanes]
            v = v_ref[0, rows, lanes]
            cum = cum_all[rows, lanes]
            last = cum[ch - 1:ch]
            o = _dot(q * jnp.exp(cum), state, _NT)
            w = [q * jnp.exp(jnp.minimum(cum - cum[s:s + 1], 0.0)) * k[s:s + 1] for s in range(ch)]
            rel = jnp.dot(jnp.concatenate(w, axis=0).astype(_BF), ones_blk,
                          preferred_element_type=jnp.float32)
            for s in range(ch):
                o = o + jnp.where(tcol >= s, rel[s * ch:(s + 1) * ch], 0.0) * v[s:s + 1]
            o_ref[0, rows, lanes] = o
            kv = _dot(v, k * jnp.exp(last - cum), _TN)
            state = state * jnp.exp(last) + jnp.where(same_head, kv, 0.0)
        st_ref[p] = state


def _hgrn_scan(q, k, v, lf):
    g, l, d = q.shape
    spec = pl.BlockSpec((1, HG_BLOCK, d), lambda i, j: (i, j, 0))
    return pl.pallas_call(
        _hgrn_scan_kernel,
        grid=(g, l // HG_BLOCK),
        in_specs=[spec] * 4,
        out_specs=spec,
        out_shape=jax.ShapeDtypeStruct((g, l, d), jnp.float32),
        scratch_shapes=[pltpu.VMEM((d // _HG_PAIR, _HG_PAIR, _HG_PAIR), jnp.float32)],
        compiler_params=pltpu.CompilerParams(dimension_semantics=("parallel", "arbitrary")),
        name="hgrn_scan",
    )(q, k, v, lf)


def _hgrn_dir_kernel(q_ref, k_ref, v_ref, lf_ref, o_ref, st_ref, *, reverse):
    @pl.when(pl.program_id(1) == 0)
    def _():
        st_ref[...] = jnp.zeros_like(st_ref)

    ch = HG_CHUNK
    n_ch = HG_BLOCK // ch
    r = lax.broadcasted_iota(jnp.int32, (HG_BLOCK, HG_BLOCK), 0)
    c = lax.broadcasted_iota(jnp.int32, (HG_BLOCK, HG_BLOCK), 1)
    tri = ((r // ch == c // ch) & ((r <= c) if reverse else (r >= c))).astype(jnp.float32)
    same_head = (lax.broadcasted_iota(jnp.int32, (_HG_PAIR, _HG_PAIR), 0) // HG_HEAD
                 == lax.broadcasted_iota(jnp.int32, (_HG_PAIR, _HG_PAIR), 1) // HG_HEAD)
    ones_blk = same_head.astype(_BF)
    tcol = lax.broadcasted_iota(jnp.int32, (ch, 1), 0)
    end = 0 if reverse else ch - 1
    cum_all = _dot3(tri, lf_ref[0])
    n_pair = D_HG // _HG_PAIR
    intra, kv, dec, qd = {}, {}, {}, {}
    for p in range(n_pair):
        lanes = slice(p * _HG_PAIR, (p + 1) * _HG_PAIR)
        for ci in range(n_ch):
            rows = slice(ci * ch, (ci + 1) * ch)
            q = q_ref[0, rows, lanes]
            k = k_ref[0, rows, lanes]
            v = v_ref[0, rows, lanes]
            cum = cum_all[rows, lanes]
            last = cum[end:end + 1]
            w = [q * jnp.exp(jnp.minimum(cum - cum[s:s + 1], 0.0)) * k[s:s + 1] for s in range(ch)]
            rel = jnp.dot(jnp.concatenate(w, axis=0).astype(_BF), ones_blk,
                          preferred_element_type=jnp.float32)
            o = jnp.zeros((ch, _HG_PAIR), jnp.float32)
            for s in range(ch):
                seen = (tcol <= s) if reverse else (tcol >= s)
                o = o + jnp.where(seen, rel[s * ch:(s + 1) * ch], 0.0) * v[s:s + 1]
            intra[p, ci] = o
            kv[p, ci] = jnp.where(same_head, _dot(v, k * jnp.exp(last - cum), _TN), 0.0)
            dec[p, ci] = jnp.exp(last)
            qd[p, ci] = q * jnp.exp(cum)
    for p in range(n_pair):
        lanes = slice(p * _HG_PAIR, (p + 1) * _HG_PAIR)
        state = st_ref[p]
        for ci in (range(n_ch - 1, -1, -1) if reverse else range(n_ch)):
            o_ref[0, ci * ch:(ci + 1) * ch, lanes] = intra[p, ci] + _dot(qd[p, ci], state, _NT)
            state = state * dec[p, ci] + kv[p, ci]
        st_ref[p] = state


def _hgrn_dir_scan(q, k, v, lf, *, ctx_len, reverse):
    g, l, d = q.shape
    order = _scan_block_order(ctx_len // HG_BLOCK, l // HG_BLOCK, reverse)
    spec = pl.BlockSpec((1, HG_BLOCK, d), lambda i, j: (i, order(j), 0))
    return pl.pallas_call(
        functools.partial(_hgrn_dir_kernel, reverse=reverse),
        grid=(g, l // HG_BLOCK),
        in_specs=[spec] * 4,
        out_specs=spec,
        out_shape=jax.ShapeDtypeStruct((g, l, d), jnp.float32),
        scratch_shapes=[pltpu.VMEM((d // _HG_PAIR, _HG_PAIR, _HG_PAIR), jnp.float32)],
        compiler_params=pltpu.CompilerParams(dimension_semantics=("parallel", "arbitrary")),
        name="hgrn_scan",
    )(q, k, v, lf)


_RW_PAIR = 2 * RW_HEAD
RW_PRE_T = 128


def _rwkv_pair_kernel(r_ref, lw_ref, k_ref, v_ref, kk_ref, a_ref, y_ref, st_ref, *, chunk, reverse):
    @pl.when(pl.program_id(1) == 0)
    def _():
        st_ref[...] = jnp.zeros_like(st_ref)

    row = lax.broadcasted_iota(jnp.int32, (chunk, chunk), 0)
    col = lax.broadcasted_iota(jnp.int32, (chunk, chunk), 1)
    strict = row < col if reverse else row > col
    incl = row <= col if reverse else row >= col
    end = 0 if reverse else chunk - 1
    tri = incl.astype(jnp.float32)
    eye = (row == col).astype(jnp.float32)
    blk8 = row // 8 == col // 8
    merge_masks = []
    size = 8
    while size < chunk:
        merge_masks.append((row // (2 * size) == col // (2 * size)) & (row // size != col // size))
        size *= 2
    lane_head = lax.broadcasted_iota(jnp.int32, (1, _RW_PAIR), 1) // RW_HEAD
    head0 = lane_head == 0
    same_head = (lax.broadcasted_iota(jnp.int32, (_RW_PAIR, _RW_PAIR), 0) // RW_HEAD
                 == lax.broadcasted_iota(jnp.int32, (_RW_PAIR, _RW_PAIR), 1) // RW_HEAD)
    ps = range(D_RW // _RW_PAIR)
    ph = [(p, h) for p in ps for h in range(2)]
    sl = [slice(p * _RW_PAIR, (p + 1) * _RW_PAIR) for p in ps]
    lw = [lw_ref[0, :, sl[p]] for p in ps]
    cum = [_dot3(tri, lw[p]) for p in ps]
    pw = [jnp.exp(cum[p]) for p in ps]
    p_inv = [jnp.exp(-cum[p]) for p in ps]
    p_end = [pw[p][end:end + 1, :] for p in ps]
    kk = [kk_ref[0, :, sl[p]] for p in ps]
    bh = [kk[p] * a_ref[0, :, sl[p]] * p_inv[p] for p in ps]
    kh = [k_ref[0, :, sl[p]] * p_inv[p] for p in ps]
    v = [v_ref[0, :, sl[p]] for p in ps]
    s0 = [st_ref[p] for p in ps]
    x1 = [jnp.concatenate([-kk[p] * jnp.exp(cum[p] - lw[p]), r_ref[0, :, sl[p]] * pw[p]], axis=0) for p in ps]
    x1h = {(p, h): jnp.where(lane_head == h, x1[p], 0.0) for p, h in ph}
    g1 = {q: _dot(x1h[q], bh[q[0]], _NT) for q in ph}
    g2 = {q: _dot(x1h[q], kh[q[0]], _NT) for q in ph}
    hm = [_dot(x1[p], s0[p], _NT) for p in ps]
    ab = {q: jnp.where(strict, g1[q][:chunk], 0.0) for q in ph}
    d1 = {q: jnp.where(blk8, ab[q], 0.0) for q in ph}
    d2 = {q: _dot3(d1[q], d1[q]) for q in ph}
    inv = {q: eye + d1[q] for q in ph}
    inv = {q: inv[q] + _dot3(inv[q], d2[q]) for q in ph}
    d4 = {q: _dot3(d2[q], d2[q]) for q in ph}
    inv = {q: inv[q] + _dot3(inv[q], d4[q]) for q in ph}
    for m in merge_masks:
        li = {q: _dot(jnp.where(m, ab[q], 0.0), inv[q]) for q in ph}
        inv = {q: inv[q] + _dot(inv[q], li[q]) for q in ph}
    akv = {q: _dot(jnp.where(strict, g2[q][:chunk], 0.0), v[q[0]]) for q in ph}
    rhs = [hm[p][:chunk] + jnp.where(head0, akv[p, 0], akv[p, 1]) for p in ps]
    eh = {q: _dot(inv[q], rhs[q[0]]) for q in ph}
    e = [jnp.where(head0, eh[p, 0], eh[p, 1]) for p in ps]
    yh = {q: _dot(jnp.where(incl, g1[q][chunk:], 0.0), e[q[0]])
          + _dot(jnp.where(incl, g2[q][chunk:], 0.0), v[q[0]]) for q in ph}
    for p in ps:
        y_ref[0, :, sl[p]] = hm[p][chunk:] + jnp.where(head0, yh[p, 0], yh[p, 1])
    for p in ps:
        ev = jnp.concatenate([e[p], v[p]], axis=0)
        x2 = jnp.concatenate([bh[p] * p_end[p], kh[p] * p_end[p]], axis=0)
        st_ref[p] = s0[p] * p_end[p] + jnp.where(same_head, _dot(ev, x2, _TN), 0.0)


def _rwkv_pair_scan(r, lw, k, v, kk, a, *, ctx_len, reverse):
    nb, l, d = r.shape
    chunk = RW_CHUNK
    order = _scan_block_order(ctx_len // chunk, l // chunk, reverse)
    spec = pl.BlockSpec((1, chunk, d), lambda b, c: (b, order(c), 0))
    return pl.pallas_call(
        functools.partial(_rwkv_pair_kernel, chunk=chunk, reverse=reverse),
        grid=(nb, l // chunk),
        in_specs=[spec] * 6,
        out_specs=spec,
        out_shape=jax.ShapeDtypeStruct((nb, l, d), jnp.float32),
        scratch_shapes=[pltpu.VMEM((d // _RW_PAIR, _RW_PAIR, _RW_PAIR), jnp.float32)],
        compiler_params=pltpu.CompilerParams(dimension_semantics=("parallel", "arbitrary")),
        name="rwkv_scan",
    )(r, lw, k, v, kk, a)


def _softplus(z):
    return jnp.maximum(z, 0.0) + jnp.log(1.0 + jnp.exp(-jnp.abs(z)))


def _sigmoid(z):
    return 1.0 / (1.0 + jnp.exp(-z))


def _rwkv_pre_kernel(prev_ref, cur_ref, next_ref, mu_ref, vec_ref, gup_ref, wup_ref, aup_ref, ones_ref,
                     r_ref, v_ref, kk_ref, g_ref, bonus_ref, lw0_ref, k0_ref, a0_ref, lw1_ref, k1_ref, a1_ref,
                     ext_ref, s_ref, *, ctx_len, seq_len):
    t_rows, w = RW_PRE_T, GRID_W
    ext_ref[0:w] = prev_ref[0]
    ext_ref[w:w + t_rows] = cur_ref[0]
    ext_ref[w + t_rows:w + t_rows + w] = next_ref[0]
    t = pl.program_id(1) * t_rows + lax.broadcasted_iota(jnp.int32, (t_rows, 128), 0)
    lane4 = lax.broadcasted_iota(jnp.int32, (t_rows, 128), 1) % 4
    is_ctx = t < ctx_len
    tx = t - ctx_len
    col = tx % w
    is_lat = jnp.logical_not(is_ctx)
    use_p1 = (is_ctx & (lane4 % 2 == 0) & (t > 0)) | (is_lat & (lane4 == 0) & (col != 0))
    use_n1 = (is_ctx & (lane4 % 2 == 1) & (t < ctx_len - 1)) | (is_lat & (lane4 == 1) & (col != w - 1))
    use_p64 = is_lat & (lane4 == 2) & (tx >= w)
    use_n64 = is_lat & (lane4 == 3) & (tx < seq_len - w)
    for lt in range(RW_COLS // 128):
        lanes = slice(lt * 128, (lt + 1) * 128)
        p = ext_ref[w:w + t_rows, lanes]
        shifted = jnp.where(use_p1, ext_ref[w - 1:w - 1 + t_rows, lanes],
                            jnp.where(use_n1, ext_ref[w + 1:w + 1 + t_rows, lanes],
                                      jnp.where(use_p64, ext_ref[0:t_rows, lanes],
                                                jnp.where(use_n64, ext_ref[2 * w:2 * w + t_rows, lanes], 0.0))))
        s_ref[:, lanes] = p + mu_ref[:, lanes] * (shifted - p)
    d = D_RW
    r = s_ref[:, 0:d]
    k = s_ref[:, d:2 * d]
    v = s_ref[:, 2 * d:3 * d]
    o = 3 * d
    g_lo = s_ref[:, o:o + RW_G_LORA]
    wl = s_ref[:, o + RW_G_LORA:o + RW_G_LORA + 2 * RW_W_LORA]
    al = s_ref[:, o + RW_G_LORA + 2 * RW_W_LORA:o + RW_G_LORA + 2 * RW_W_LORA + 2 * RW_A_LORA]
    k_k, k_a, r_k = vec_ref[0:1], vec_ref[1:2], vec_ref[2:3]
    ones = ones_ref[...]
    r_ref[0] = r
    v_ref[0] = v
    g_ref[0] = jnp.dot(_sigmoid(g_lo).astype(_BF), gup_ref[...], preferred_element_type=jnp.float32)
    kkr = k * k_k
    ss = jnp.dot((kkr * kkr).astype(_BF), ones, preferred_element_type=jnp.float32)
    kk_ref[0] = kkr / jnp.maximum(jnp.sqrt(ss), 1e-12)
    w_pre = jnp.dot(jnp.tanh(wl).astype(_BF), wup_ref[...], preferred_element_type=jnp.float32)
    a_pre = jnp.dot(al.astype(_BF), aup_ref[...], preferred_element_type=jnp.float32)
    k_sum = 0.0
    for dr, (lw_o, k_o, a_o) in enumerate(((lw0_ref, k0_ref, a0_ref), (lw1_ref, k1_ref, a1_ref))):
        w_log = -_softplus(-(vec_ref[3 + dr:4 + dr] + w_pre[:, dr * d:(dr + 1) * d])) - 0.5
        a = _sigmoid(vec_ref[5 + dr:6 + dr] + a_pre[:, dr * d:(dr + 1) * d])
        k_d = k * (1.0 + (a - 1.0) * k_a)
        lw_o[0] = -jnp.exp(w_log)
        k_o[0] = k_d
        a_o[0] = a
        k_sum = k_sum + k_d
    bonus_ref[0] = jnp.dot((r * k_sum * r_k).astype(_BF), ones, preferred_element_type=jnp.float32)


def _head_ones(width, head):
    i = jnp.arange(width) // head
    return (i[:, None] == i[None, :]).astype(_BF)


def _rwkv_pre(p, mu, k_k, k_a, r_k, w0, a0, g_up, w_up, a_up, *, ctx_len):
    nb, l, cols = p.shape
    t_rows, w, d = RW_PRE_T, GRID_W, D_RW
    nblk = l // w
    per = t_rows // w
    zeros = jnp.zeros((RW_W_LORA, d), jnp.float32)
    wup = jnp.concatenate([jnp.concatenate([w_up[0], zeros], axis=1),
                           jnp.concatenate([zeros, w_up[1]], axis=1)], axis=0).astype(_BF)
    aup = jnp.concatenate([jnp.concatenate([a_up[0], zeros], axis=1),
                           jnp.concatenate([zeros, a_up[1]], axis=1)], axis=0).astype(_BF)
    vec = jnp.stack([k_k, k_a, r_k.reshape(d), w0[0], w0[1], a0[0], a0[1], jnp.zeros((d,), jnp.float32)])
    out = jax.ShapeDtypeStruct((nb, l, d), jnp.float32)
    out_spec = pl.BlockSpec((1, t_rows, d), lambda b, j: (b, j, 0))
    full = lambda shape: pl.BlockSpec(shape, lambda b, j: (0,) * len(shape))
    return pl.pallas_call(
        functools.partial(_rwkv_pre_kernel, ctx_len=ctx_len, seq_len=l - ctx_len),
        grid=(nb, l // t_rows),
        in_specs=[pl.BlockSpec((1, w, cols), lambda b, j: (b, jnp.maximum(j * per - 1, 0), 0)),
                  pl.BlockSpec((1, t_rows, cols), lambda b, j: (b, j, 0)),
                  pl.BlockSpec((1, w, cols), lambda b, j: (b, jnp.minimum(j * per + per, nblk - 1), 0)),
                  full((1, cols)), full((8, d)), full((RW_G_LORA, d)),
                  full((2 * RW_W_LORA, 2 * d)), full((2 * RW_A_LORA, 2 * d)), full((d, d))],
        out_specs=[out_spec] * 11,
        out_shape=[out] * 11,
        scratch_shapes=[pltpu.VMEM((t_rows + 2 * w, cols), jnp.float32), pltpu.VMEM((t_rows, cols), jnp.float32)],
        compiler_params=pltpu.CompilerParams(dimension_semantics=("parallel", "parallel"),
                                             vmem_limit_bytes=48 * 1024 * 1024),
        name="rwkv_pre",
    )(p, p, p, mu.reshape(1, cols), vec, g_up.astype(_BF), wup, aup, _head_ones(d, RW_HEAD))


def _rwkv_post_kernel(y0_ref, y1_ref, v_ref, g_ref, bonus_ref, gn_ref, ones_ref, o_ref):
    y = y0_ref[0] + y1_ref[0]
    ones = ones_ref[...]
    inv_n = 1.0 / RW_HEAD
    mu_y = jnp.dot(y.astype(_BF), ones, preferred_element_type=jnp.float32) * inv_n
    yc = y - mu_y
    var_y = jnp.dot((yc * yc).astype(_BF), ones, preferred_element_type=jnp.float32) * inv_n
    yn = yc * lax.rsqrt(var_y + RW_GN_EPS) * gn_ref[0:1] + gn_ref[1:2]
    o_ref[0] = (yn + bonus_ref[0] * v_ref[0]) * g_ref[0]


def _rwkv_post(y0, y1, v, g, bonus, gn_g, gn_b):
    nb, l, d = y0.shape
    spec = pl.BlockSpec((1, RW_PRE_T, d), lambda b, j: (b, j, 0))
    return pl.pallas_call(
        _rwkv_post_kernel,
        grid=(nb, l // RW_PRE_T),
        in_specs=[spec] * 5 + [pl.BlockSpec((2, d), lambda b, j: (0, 0)), pl.BlockSpec((d, d), lambda b, j: (0, 0))],
        out_specs=spec,
        out_shape=jax.ShapeDtypeStruct((nb, l, d), jnp.float32),
        compiler_params=pltpu.CompilerParams(dimension_semantics=("parallel", "parallel")),
        name="rwkv_post",
    )(y0, y1, v, g, bonus, jnp.stack([gn_g, gn_b]), _head_ones(d, RW_HEAD))


def _rwkv7_mixer_fused(pc, px, mu, w0, w_up, a0, a_up, g_up, k_k, k_a, r_k, gn_g, gn_b):
    lc = pc.shape[1]
    p = jnp.concatenate([_f32(pc), _f32(px)], axis=1)
    r, v, kk, g, bonus, lw0, k0, a0_, lw1, k1, a1_ = _rwkv_pre(
        p, _f32(mu), _f32(k_k), _f32(k_a), _f32(r_k), _f32(w0), _f32(a0), _f32(g_up), _f32(w_up), _f32(a_up),
        ctx_len=lc)
    y0 = _rwkv_pair_scan(r, lw0, k0, v, kk, a0_, ctx_len=lc, reverse=False)
    y1 = _rwkv_pair_scan(r, lw1, k1, v, kk, a1_, ctx_len=lc, reverse=True)
    y = _rwkv_post(y0, y1, v, g, bonus, _f32(gn_g), _f32(gn_b))
    return y[:, :lc], y[:, lc:]


def _f32(t):
    return t.astype(jnp.float32)


def _layer_norm(x, g, b):
    xf = _f32(x)
    mu = jnp.mean(xf, -1, keepdims=True)
    var = jnp.mean(jnp.square(xf - mu), -1, keepdims=True)
    return ((xf - mu) * lax.rsqrt(var + LN_EPS) * _f32(g) + _f32(b)).astype(x.dtype)


def _shift(t, axis, step):
    n = t.shape[axis]
    pad = [(0, 0)] * t.ndim
    if step > 0:
        pad[axis] = (1, 0)
        return lax.slice_in_dim(jnp.pad(t, pad), 0, n, axis=axis)
    pad[axis] = (0, 1)
    return lax.slice_in_dim(jnp.pad(t, pad), 1, n + 1, axis=axis)


def _qshift_grid(s, rows):
    b, l, ch = s.shape
    s4 = s.reshape(b, rows, GRID_W, ch // 4, 4)
    out = jnp.stack([_shift(s4[..., 0], 2, 1), _shift(s4[..., 1], 2, -1),
                     _shift(s4[..., 2], 1, 1), _shift(s4[..., 3], 1, -1)], axis=-1)
    return out.reshape(b, l, ch)


def _qshift_seq(s):
    b, l, ch = s.shape
    s4 = s.reshape(b, l, ch // 4, 4)
    out = jnp.stack([_shift(s4[..., 0], 1, 1), _shift(s4[..., 1], 1, -1),
                     _shift(s4[..., 2], 1, 1), _shift(s4[..., 3], 1, -1)], axis=-1)
    return out.reshape(b, l, ch)


def _s5_discretise(lam_re, lam_im, log_step, b_re, b_im):
    dt = jnp.exp(log_step)[:, None]
    mag = jnp.exp(lam_re * dt)
    ang = lam_im * dt
    lb_re, lb_im = mag * jnp.cos(ang), mag * jnp.sin(ang)
    den = lam_re * lam_re + lam_im * lam_im
    nr, ni = lb_re - 1.0, lb_im
    co_re = (nr * lam_re + ni * lam_im) / den
    co_im = (ni * lam_re - nr * lam_im) / den
    bb_re = co_re[..., None] * b_re - co_im[..., None] * b_im
    bb_im = co_re[..., None] * b_im + co_im[..., None] * b_re
    return lb_re, lb_im, bb_re, bb_im


def _s5_mixer(uc, ux, lam_re, lam_im, log_step, b_re, b_im, c_re, c_im, d_skip, glu_w, glu_b,
              interpret=False):
    disc = [_s5_discretise(_f32(lam_re[d]), _f32(lam_im[d]), _f32(log_step[d]),
                           _f32(b_re[d]), _f32(b_im[d])) for d in range(2)]
    c_re, c_im = _f32(c_re), _f32(c_im)

    wb, lam, wc = _s5_weights(disc, c_re, c_im)
    uc, ux = _f32(uc), _f32(ux)
    nb, lc, lx = uc.shape[0], uc.shape[1], ux.shape[1]
    u_f = jnp.concatenate([uc, ux], axis=1)
    u_b = jnp.concatenate([jnp.flip(uc, 1), jnp.flip(ux, 1)], axis=1)
    zeros = jnp.zeros_like(u_f)
    u2 = jnp.concatenate([jnp.concatenate([u_f, zeros], axis=-1),
                          jnp.concatenate([zeros, u_b], axis=-1)], axis=0)
    u2 = u2.transpose(1, 0, 2).reshape((lc + lx) * 2 * nb, 2 * D_S5)
    y = _s5_scan(u2, wb, lam, wc, interpret=interpret)
    y = y.reshape(lc + lx, 2 * nb, D_S5).transpose(1, 0, 2)
    y_f, y_b = y[:nb], y[nb:]

    def post(y, u):
        y = jax.nn.gelu(y + _f32(d_skip) * u)
        return y * jax.nn.sigmoid(_mm3(y, glu_w) + _f32(glu_b))

    yc = post(y_f[:, :lc] + jnp.flip(y_b[:, :lc], 1), uc)
    yx = post(y_f[:, lc:] + jnp.flip(y_b[:, lc:], 1), ux)
    return yc, yx


def _gla_chunkwise(q, k, v, log_f, s0):
    b, h, l, _ = q.shape
    dv = v.shape[-1]
    n = l // HG_CHUNK
    blk = lambda t: t.reshape(b, h, n, HG_CHUNK, t.shape[-1])
    q, k, v, log_f = blk(q), blk(k), blk(v), blk(log_f)
    cum = jnp.cumsum(log_f, axis=3)
    lower = jnp.tril(jnp.ones((HG_CHUNK, HG_CHUNK), bool))[:, :, None]
    rel = jnp.exp(jnp.where(lower, cum[..., :, None, :] - cum[..., None, :, :], -jnp.inf))
    scores = jnp.einsum('bhntd,bhnsd,bhntsd->bhnts', q, k, rel)
    o_intra = jnp.einsum('bhnts,bhnsv->bhntv', scores, v)
    last = cum[..., -1:, :]
    chunk_kv = jnp.einsum('bhnsd,bhnsv->bhndv', k * jnp.exp(last - cum), v)
    chunk_decay = jnp.exp(last[..., 0, :])

    def step(state, inp):
        dec, kv = inp
        return dec[..., None] * state + kv, state

    s_fin, s_prev = lax.scan(step, s0, (jnp.moveaxis(chunk_decay, 2, 0), jnp.moveaxis(chunk_kv, 2, 0)))
    o_inter = jnp.einsum('bhntd,bhndv->bhntv', q * jnp.exp(cum), jnp.moveaxis(s_prev, 0, 2))
    return (o_intra + o_inter).reshape(b, h, l, dv), s_fin


def _hgrn2_mixer(pc, px, lb, norm_g):
    def pre(p):
        q, i, f_f, f_b, g = jnp.split(_f32(p), 5, axis=-1)
        logf = [jnp.log(lb[d] + (1.0 - lb[d]) * jax.nn.sigmoid(f)) for d, f in enumerate((f_f, f_b))]
        return jax.nn.silu(q), i, logf, g

    qc, ic, lfc, gc = pre(pc)
    qx, ix, lfx, gx = pre(px)
    nb, lc = pc.shape[0], pc.shape[1]

    q = jnp.concatenate([qc, qx], axis=1)
    v = jnp.concatenate([ic, ix], axis=1)
    o = 0.0
    for d in range(2):
        lf = jnp.concatenate([lfc[d], lfx[d]], axis=1)
        o = o + _hgrn_dir_scan(q, -jnp.expm1(lf), v, lf, ctx_len=lc, reverse=d == 1)

    def post(o, g):
        oh = o.reshape(o.shape[0], o.shape[1], HG_HEADS, HG_HEAD)
        oh = oh * lax.rsqrt(jnp.mean(oh * oh, -1, keepdims=True) + RMS_EPS)
        return oh.reshape(o.shape) * _f32(norm_g) * jax.nn.silu(g)

    yc = post(o[:, :lc], gc)
    yx = post(o[:, lc:], gx)
    return yc.astype(pc.dtype), yx.astype(px.dtype)


def _rwkv7_step(state, inp):
    r, w, k, v, kk, a = inp
    sk = jnp.einsum('bhvk,bhk->bhv', state, kk)
    state = (state * w[:, :, None, :] - sk[..., None] * (kk * a)[:, :, None, :]
             + v[..., None] * k[:, :, None, :])
    return state, jnp.einsum('bhvk,bhk->bhv', state, r)


def _rwkv7_scan(r, w, k, v, kk, a, s0, reverse):
    xs = tuple(jnp.moveaxis(t, 1, 0) for t in (r, w, k, v, kk, a))
    s_fin, y = lax.scan(_rwkv7_step, s0, xs, reverse=reverse)
    return jnp.moveaxis(y, 0, 1), s_fin


def _rwkv7_mixer(pc, px, rows, mu, w0, w_up, a0, a_up, g_up, k_k, k_a, r_k, gn_g, gn_b):
    split_at = [D_RW, 2 * D_RW, 3 * D_RW, 3 * D_RW + RW_G_LORA,
                3 * D_RW + RW_G_LORA + RW_W_LORA, 3 * D_RW + RW_G_LORA + 2 * RW_W_LORA,
                3 * D_RW + RW_G_LORA + 2 * RW_W_LORA + RW_A_LORA]

    def heads(t):
        return t.reshape(t.shape[0], t.shape[1], RW_HEADS, RW_HEAD)

    def pre(p, shifted):
        s = p + _f32(mu) * (shifted - p)
        r, k, v, g_lo, wl_f, wl_b, al_f, al_b = jnp.split(s, split_at, axis=-1)
        g = _mm3(jax.nn.sigmoid(g_lo), g_up)
        kk = heads(k * _f32(k_k))
        kk = kk / jnp.maximum(jnp.sqrt(jnp.sum(kk * kk, -1, keepdims=True)), 1e-12)
        r, v = heads(r), heads(v)
        dirs, bonus = [], 0.0
        for d, (wl, al) in enumerate(((wl_f, al_f), (wl_b, al_b))):
            w_log = -jax.nn.softplus(-(_f32(w0[d]) + _mm3(jnp.tanh(wl), w_up[d]))) - 0.5
            a = jax.nn.sigmoid(_f32(a0[d]) + _mm3(al, a_up[d]))
            k_d = heads(k * (1.0 + (a - 1.0) * _f32(k_a)))
            dirs.append((heads(-jnp.exp(w_log)), k_d, heads(a)))
            bonus = bonus + jnp.sum(r * k_d * _f32(r_k), -1, keepdims=True)
        return dict(r=r, v=v, kk=kk, g=g, dirs=dirs, bonus=bonus)

    def post(y, q, shape):
        mu_y = jnp.mean(y, -1, keepdims=True)
        var_y = jnp.mean(jnp.square(y - mu_y), -1, keepdims=True)
        y = ((y - mu_y) * lax.rsqrt(var_y + RW_GN_EPS) * _f32(gn_g).reshape(RW_HEADS, RW_HEAD)
             + _f32(gn_b).reshape(RW_HEADS, RW_HEAD))
        return (y + q['bonus'] * q['v']).reshape(shape[0], shape[1], D_RW) * q['g']

    pc32, px32 = _f32(pc), _f32(px)
    qc = pre(pc32, _qshift_seq(pc32))
    qx = pre(px32, _qshift_grid(px32, rows))
    lc, lx = pc.shape[1], px.shape[1]

    def seq(tc, tx):
        t = jnp.concatenate([tc, tx], axis=1)
        return t.transpose(0, 2, 1, 3).reshape(t.shape[0] * RW_HEADS, lc + lx, RW_HEAD)

    r, v, kk = (seq(qc[n], qx[n]) for n in ('r', 'v', 'kk'))
    y = 0.0
    for d in range(2):
        lw, k_d, a = (seq(qc['dirs'][d][i], qx['dirs'][d][i]) for i in range(3))
        y = y + _rwkv_scan(r, lw, k_d, v, kk, a, heads=RW_HEADS, ctx_len=lc, reverse=d == 1)
    nb = pc.shape[0]
    y = y.reshape(nb, RW_HEADS, lc + lx, RW_HEAD).transpose(0, 2, 1, 3)
    yc = post(y[:, :lc], qc, pc.shape)
    yx = post(y[:, lc:], qx, px.shape)
    return yc.astype(pc.dtype), yx.astype(px.dtype)


def _peer_ffn(h, wq, keys, u_tab, v_tab):
    b, l, d = h.shape
    t = b * l
    hf = h.reshape(t, d)
    q = _matmul(hf, wq).reshape(t, PEER_HEADS, 2, PEER_QDIM // 2)
    scores = _f32(jnp.einsum('thpd,hpkd->thpk', q, keys))
    s_top, i_top = lax.top_k(scores, PEER_TOPK)
    cand = s_top[..., 0, :, None] + s_top[..., 1, None, :]
    cand_s, cand_i = lax.top_k(cand.reshape(t, PEER_HEADS, PEER_TOPK * PEER_TOPK), PEER_TOPK)
    i1 = jnp.take_along_axis(i_top[..., 0, :], cand_i // PEER_TOPK, axis=-1)
    i2 = jnp.take_along_axis(i_top[..., 1, :], cand_i % PEER_TOPK, axis=-1)
    expert = (i1 * PEER_NKEYS + i2).reshape(t, PEER_HEADS * PEER_TOPK)
    gate = jax.nn.softmax(cand_s, axis=-1).reshape(t, PEER_HEADS * PEER_TOPK).astype(h.dtype)
    nb = t // PEER_BLOCK

    def block(args):
        hb, eb, gb = args
        z = jnp.einsum('tkd,td->tk', u_tab[eb], hb)
        return jnp.einsum('tk,tkd->td', jax.nn.gelu(z) * gb, v_tab[eb])

    out = lax.map(block, (hf.reshape(nb, PEER_BLOCK, d),
                          expert.reshape(nb, PEER_BLOCK, PEER_HEADS * PEER_TOPK),
                          gate.reshape(nb, PEER_BLOCK, PEER_HEADS * PEER_TOPK)))
    return out.reshape(b, l, d)


def kernel(x, c, ctx, c_ctx, ada_w, ada_b, w_in, w_out, ln1_g, ln1_b, ln2_g, ln2_b,
           s5_lam_re, s5_lam_im, s5_log_step, s5_b_re, s5_b_im, s5_c_re, s5_c_im,
           s5_d, s5_glu_w, s5_glu_b, hgrn_lb_logits, hgrn_norm_g,
           rwkv_mu, rwkv_w0, rwkv_w_up, rwkv_a0, rwkv_a_up, rwkv_g_up, rwkv_k_k, rwkv_k_a,
           rwkv_r_k, rwkv_gn_g, rwkv_gn_b, peer_wq, peer_keys, peer_u, peer_v):
    rows = x.shape[1] // GRID_W
    lb_cum = jnp.cumsum(jax.nn.softmax(_f32(hgrn_lb_logits), axis=1), axis=1)
    lb_all = lb_cum - lb_cum[:, :1]
    col_split = [D_S5, D_S5 + 5 * D_HG]
    xs, cs = x, ctx
    for l in range(DEPTH):
        mod_x = (jax.nn.silu(c) @ ada_w[l] + ada_b[l])[:, None, :]
        mod_c = (jax.nn.silu(c_ctx) @ ada_w[l] + ada_b[l])[None, None, :]
        sh1x, sc1x, g1x, sh2x, sc2x, g2x = jnp.split(mod_x, 6, axis=-1)
        sh1c, sc1c, g1c, sh2c, sc2c, g2c = jnp.split(mod_c, 6, axis=-1)

        px = _mm3(xs * (1.0 + sc1x) + sh1x, w_in[l])
        pc = _mm3(cs * (1.0 + sc1c) + sh1c, w_in[l])
        pxa, pxb, pxc = jnp.split(px, col_split, axis=-1)
        pca, pcb, pcc = jnp.split(pc, col_split, axis=-1)

        ya_c, ya_x = _s5_mixer(pca, pxa, s5_lam_re[l], s5_lam_im[l], s5_log_step[l],
                               s5_b_re[l], s5_b_im[l], s5_c_re[l], s5_c_im[l],
                               s5_d[l], s5_glu_w[l], s5_glu_b[l])
        yb_c, yb_x = _hgrn2_mixer(pcb, pxb, lb_all[:, l], hgrn_norm_g[l])
        assert rows == x.shape[1] // GRID_W and pxc.shape[1] % RW_PRE_T == 0 and pcc.shape[1] % RW_PRE_T == 0
        yc_c, yc_x = _rwkv7_mixer_fused(pcc, pxc, rwkv_mu[l], rwkv_w0[l], rwkv_w_up[l],
                                        rwkv_a0[l], rwkv_a_up[l], rwkv_g_up[l], rwkv_k_k[l],
                                        rwkv_k_a[l], rwkv_r_k[l], rwkv_gn_g[l], rwkv_gn_b[l])

        u_b = peer_u[l].astype(_BF)
        vt_b = peer_v[l].T.astype(_BF)
        nx = xs.shape[0] * xs.shape[1]
        mix_x = _mm3(jnp.concatenate([ya_x, yb_x, yc_x], axis=-1), w_out[l])
        xs = _layer_norm(DEEPNORM_ALPHA * xs + g1x * mix_x, ln1_g[l], ln1_b[l])
        h2 = (xs * (1.0 + sc2x) + sh2x).reshape(nx, D_MODEL)
        if l < DEPTH - 1:
            mix_c = _mm3(jnp.concatenate([ya_c, yb_c, yc_c], axis=-1), w_out[l])
            cs = _layer_norm(DEEPNORM_ALPHA * cs + g1c * mix_c, ln1_g[l], ln1_b[l])
            h2 = jnp.concatenate([h2, (cs * (1.0 + sc2c) + sh2c).reshape(-1, D_MODEL)], axis=0)
        ffn = _peer_ffn_dense(h2, peer_wq[l], peer_keys[l], u_b, vt_b)
        xs = _layer_norm(DEEPNORM_ALPHA * xs + g2x * ffn[:nx].reshape(xs.shape), ln2_g[l], ln2_b[l])
        if l < DEPTH - 1:
            cs = _layer_norm(DEEPNORM_ALPHA * cs + g2c * ffn[nx:].reshape(cs.shape), ln2_g[l], ln2_b[l])
    return xs
```

```python
import functools
import math

import jax
import jax.numpy as jnp
from jax import lax
from jax.experimental import pallas as pl
from jax.experimental.pallas import tpu as pltpu

D_MODEL = 1024
BATCH = 4
SEQ = 4096
DEPTH = 2
GRID_W = 64
CTX_LEN = 256
D_S5 = D_MODEL // 4
D_HG = D_MODEL // 4
D_RW = D_MODEL // 2
D_MIX = D_S5 + D_HG + D_RW
S5_H = 16
S5_G = D_S5 // S5_H
S5_P = 64
HG_HEAD = 64
HG_HEADS = D_HG // HG_HEAD
HG_CHUNK = 16
RW_HEAD = 64
RW_HEADS = D_RW // RW_HEAD
RW_W_LORA = 64
RW_A_LORA = 64
RW_G_LORA = 128
RW_COLS = 3 * D_RW + RW_G_LORA + 2 * RW_W_LORA + 2 * RW_A_LORA
RW_GN_EPS = 64e-5
P_IN = D_S5 + 5 * D_HG + RW_COLS
PEER_HEADS = 8
PEER_NKEYS = 128
PEER_N = PEER_NKEYS * PEER_NKEYS
PEER_QDIM = 256
PEER_TOPK = 16
PEER_BLOCK = 128
LN_EPS = 1e-5
RMS_EPS = 1e-6
DEEPNORM_ALPHA = (2.0 * DEPTH) ** 0.25
DEEPNORM_BETA = (8.0 * DEPTH) ** -0.25


def _mm_kernel(x_ref, w_ref, o_ref):
    o_ref[...] = jnp.dot(x_ref[...].astype(jnp.bfloat16), w_ref[...],
                         preferred_element_type=jnp.float32)


def _pick_tile(n, cands):
    for c in cands:
        if n % c == 0:
            return c
    return n


def _matmul(x, w):
    m, k = x.shape
    n = w.shape[1]
    tm = _pick_tile(m, (512, 256, 128, 8))
    tn = _pick_tile(n, (1152, 1024, 512, 256, 128))
    return pl.pallas_call(
        _mm_kernel,
        grid=(m // tm, n // tn),
        in_specs=[pl.BlockSpec((tm, k), lambda i, j: (i, 0)),
                  pl.BlockSpec((k, tn), lambda i, j: (0, j))],
        out_specs=pl.BlockSpec((tm, tn), lambda i, j: (i, j)),
        out_shape=jax.ShapeDtypeStruct((m, n), jnp.float32),
        compiler_params=pltpu.CompilerParams(
            dimension_semantics=("parallel", "parallel"),
            vmem_limit_bytes=48 * 1024 * 1024),
    )(x, w.astype(jnp.bfloat16))


def _mm3(x, w):
    b, l, k = x.shape
    return _matmul(x.reshape(b * l, k), w).reshape(b, l, w.shape[1])


_BF = jnp.bfloat16
_NN = (((1,), (0,)), ((), ()))
_NT = (((1,), (1,)), ((), ()))
_TN = (((0,), (0,)), ((), ()))


def _dot(a, b, dims=_NN):
    return lax.dot_general(a.astype(_BF), b.astype(_BF), dims,
                           preferred_element_type=jnp.float32)


def _split_bf16(a):
    hi = a.astype(_BF)
    return hi, (a - hi.astype(jnp.float32)).astype(_BF)


def _dot3(a, b, dims=_NN):
    a_hi, a_lo = _split_bf16(a)
    b_hi, b_lo = _split_bf16(b)
    d = functools.partial(lax.dot_general, dimension_numbers=dims,
                          preferred_element_type=jnp.float32)
    return d(a_hi, b_hi) + (d(a_hi, b_lo) + d(a_lo, b_hi))


RW_CHUNK = 64


def _rwkv_scan_kernel(r_ref, lw_ref, k_ref, v_ref, kk_ref, a_ref, y_ref, st_ref, *, heads, chunk, reverse):
    @pl.when(pl.program_id(1) == 0)
    def _():
        st_ref[...] = jnp.zeros_like(st_ref)

    row = lax.broadcasted_iota(jnp.int32, (chunk, chunk), 0)
    col = lax.broadcasted_iota(jnp.int32, (chunk, chunk), 1)
    strict = row < col if reverse else row > col
    incl = row <= col if reverse else row >= col
    end = 0 if reverse else chunk - 1
    tri = incl.astype(jnp.float32)
    eye = (row == col).astype(jnp.float32)
    blk8 = row // 8 == col // 8
    merge_masks = []
    size = 8
    while size < chunk:
        merge_masks.append((row // (2 * size) == col // (2 * size)) & (row // size != col // size))
        size *= 2
    hs = range(heads)
    lw = [lw_ref[h] for h in hs]
    cum = [_dot3(tri, lw[h]) for h in hs]
    p = [jnp.exp(cum[h]) for h in hs]
    p_inv = [jnp.exp(-cum[h]) for h in hs]
    p_end = [p[h][end:end + 1, :] for h in hs]
    kk = [kk_ref[h] for h in hs]
    bh = [kk[h] * a_ref[h] * p_inv[h] for h in hs]
    kh = [k_ref[h] * p_inv[h] for h in hs]
    v = [v_ref[h] for h in hs]
    s0 = [st_ref[h] for h in hs]
    x1 = [jnp.concatenate([-kk[h] * jnp.exp(cum[h] - lw[h]), r_ref[h] * p[h]], axis=0) for h in hs]
    g1 = [_dot(x1[h], bh[h], _NT) for h in hs]
    g2 = [_dot(x1[h], kh[h], _NT) for h in hs]
    hm = [_dot(x1[h], s0[h], _NT) for h in hs]
    ab = [jnp.where(strict, g1[h][:chunk], 0.0) for h in hs]
    d1 = [jnp.where(blk8, ab[h], 0.0) for h in hs]
    d2 = [_dot3(d1[h], d1[h]) for h in hs]
    inv = [eye + d1[h] for h in hs]
    inv = [inv[h] + _dot3(inv[h], d2[h]) for h in hs]
    d4 = [_dot3(d2[h], d2[h]) for h in hs]
    inv = [inv[h] + _dot3(inv[h], d4[h]) for h in hs]
    for m in merge_masks:
        li = [_dot(jnp.where(m, ab[h], 0.0), inv[h]) for h in hs]
        inv = [inv[h] + _dot(inv[h], li[h]) for h in hs]
    rhs = [hm[h][:chunk] + _dot(jnp.where(strict, g2[h][:chunk], 0.0), v[h]) for h in hs]
    e = [_dot(inv[h], rhs[h]) for h in hs]
    for h in hs:
        y_ref[h] = (hm[h][chunk:] + _dot(jnp.where(incl, g1[h][chunk:], 0.0), e[h])
                    + _dot(jnp.where(incl, g2[h][chunk:], 0.0), v[h]))
    for h in hs:
        ev = jnp.concatenate([e[h], v[h]], axis=0)
        x2 = jnp.concatenate([bh[h] * p_end[h], kh[h] * p_end[h]], axis=0)
        st_ref[h] = s0[h] * p_end[h] + _dot(ev, x2, _TN)


def _scan_block_order(n_ctx, n_all, reverse):
    if not reverse:
        return lambda c: c
    return lambda c: jnp.where(c < n_ctx, n_ctx - 1 - c, n_all - 1 - c + n_ctx)


def _rwkv_scan(r, lw, k, v, kk, a, *, heads, ctx_len=0, reverse=False, interpret=False):
    gh, l, n = r.shape
    chunk = RW_CHUNK
    order = _scan_block_order(ctx_len // chunk, l // chunk, reverse)
    spec = pl.BlockSpec((heads, chunk, n), lambda g, c: (g, order(c), 0))
    return pl.pallas_call(
        functools.partial(_rwkv_scan_kernel, heads=heads, chunk=chunk, reverse=reverse),
        grid=(gh // heads, l // chunk),
        in_specs=[spec] * 6,
        out_specs=spec,
        out_shape=jax.ShapeDtypeStruct((gh, l, n), jnp.float32),
        scratch_shapes=[pltpu.VMEM((heads, n, n), jnp.float32)],
        compiler_params=pltpu.CompilerParams(dimension_semantics=("parallel", "arbitrary")),
        interpret=interpret,
        name="rwkv_scan",
    )(r, lw, k, v, kk, a)


PEER_TN = 512
PEER_EBLK = 1024
_NEG = -3.0e38


def _top_rows(s, n):
    vals = []
    for _ in range(n):
        m = jnp.max(s, axis=0, keepdims=True)
        vals.append(m)
        s = jnp.where(s == m, _NEG, s)
    return vals


def _peer_route_kernel(h_ref, wq_ref, keys_ref, xb_ref, s2_ref, tau_ref, c1_ref, p2_ref):
    hb = h_ref[...].astype(_BF)
    xb_ref[...] = hb
    q = jnp.dot(hb, wq_ref[...], preferred_element_type=jnp.float32)
    half = PEER_QDIM // 2
    for h in range(PEER_HEADS):
        s1 = _dot(keys_ref[2 * h], q[:, (2 * h) * half:(2 * h + 1) * half], _NT)
        s2 = _dot(keys_ref[2 * h + 1], q[:, (2 * h + 1) * half:(2 * h + 2) * half], _NT)
        t1 = _top_rows(s1, PEER_TOPK + 1)
        t2 = _top_rows(s2, PEER_TOPK + 1)
        cand = [t1[j1] + t2[j2] for j1 in range(PEER_TOPK + 1) for j2 in range(PEER_TOPK + 1)
                if (j1 + 1) * (j2 + 1) <= PEER_TOPK + 1]
        cand += [jnp.full_like(cand[0], _NEG)] * (-len(cand) % 8)
        top = _top_rows(jnp.concatenate(cand, axis=0), PEER_TOPK + 1)
        theta = 0.5 * (top[PEER_TOPK - 1] + top[PEER_TOPK])
        z = top[0] * 0.0
        for c in top[:PEER_TOPK]:
            z = z + jnp.exp(c - top[0])
        s2_ref[h] = s2
        tau_ref[h] = theta - s1
        c1_ref[h] = jnp.exp(s1 - t1[0]) / z
        p2_ref[h] = jnp.exp(s2 - t2[0])


def _peer_route(hf, wq, keys, interpret=False):
    t, d = hf.shape
    tn = PEER_TN
    nq = PEER_HEADS * PEER_QDIM
    aux = jax.ShapeDtypeStruct((PEER_HEADS, PEER_NKEYS, t), jnp.float32)
    aux_spec = pl.BlockSpec((PEER_HEADS, PEER_NKEYS, tn), lambda i: (0, 0, i))
    return pl.pallas_call(
        _peer_route_kernel,
        grid=(t // tn,),
        in_specs=[pl.BlockSpec((tn, d), lambda i: (i, 0)),
                  pl.BlockSpec((d, nq), lambda i: (0, 0)),
                  pl.BlockSpec((PEER_HEADS * 2, PEER_NKEYS, PEER_QDIM // 2), lambda i: (0, 0, 0))],
        out_specs=[pl.BlockSpec((tn, d), lambda i: (i, 0))] + [aux_spec] * 4,
        out_shape=[jax.ShapeDtypeStruct((t, d), _BF)] + [aux] * 4,
        compiler_params=pltpu.CompilerParams(dimension_semantics=("parallel",),
                                             vmem_limit_bytes=56 * 1024 * 1024),
        interpret=interpret,
        name="peer_route",
    )(hf, wq.astype(_BF), keys.reshape(PEER_HEADS * 2, PEER_NKEYS, PEER_QDIM // 2))


def _gelu_tanh(z):
    return 0.5 * z * (1.0 + jnp.tanh(math.sqrt(2.0 / math.pi) * (z + 0.044715 * (z * z * z))))


def _peer_expert_kernel(x_ref, u_ref, vt_ref, s2_ref, tau_ref, c1_ref, p2_ref, o_ref, acc_ref, at_ref):
    j = pl.program_id(1)

    @pl.when(j == 0)
    def _():
        acc_ref[...] = jnp.zeros_like(acc_ref)

    nk = PEER_NKEYS
    n_slab = PEER_EBLK // nk
    rows = pl.ds(pl.multiple_of(j * n_slab, n_slab), n_slab)
    zt = lax.dot_general(u_ref[...], x_ref[...], _NT, preferred_element_type=jnp.float32)
    for s in range(n_slab):
        for lt in range(PEER_TN // 128):
            lanes = pl.ds(lt * 128, 128)
            g = jnp.zeros((nk, 128), jnp.float32)
            for h in range(PEER_HEADS):
                tau = tau_ref[h, rows, lanes][s:s + 1]
                c1 = c1_ref[h, rows, lanes][s:s + 1]
                g = g + jnp.where(s2_ref[h, :, lanes] >= tau, p2_ref[h, :, lanes] * c1, 0.0)
            z = zt[s * nk:(s + 1) * nk, lt * 128:(lt + 1) * 128]
            at_ref[s * nk:(s + 1) * nk, lt * 128:(lt + 1) * 128] = (g * _gelu_tanh(z)).astype(_BF)
    acc_ref[...] += jnp.dot(vt_ref[...], at_ref[...], preferred_element_type=jnp.float32)

    @pl.when(j == pl.num_programs(1) - 1)
    def _():
        o_ref[...] = acc_ref[...].T


def _peer_expert(xb, u_b, vt_b, aux, interpret=False):
    t, d = xb.shape
    tn, eb = PEER_TN, PEER_EBLK
    aux_spec = pl.BlockSpec((PEER_HEADS, PEER_NKEYS, tn), lambda i, j: (0, 0, i))
    return pl.pallas_call(
        _peer_expert_kernel,
        grid=(t // tn, PEER_N // eb),
        in_specs=[pl.BlockSpec((tn, d), lambda i, j: (i, 0)),
                  pl.BlockSpec((eb, d), lambda i, j: (j, 0)),
                  pl.BlockSpec((d, eb), lambda i, j: (0, j))] + [aux_spec] * 4,
        out_specs=pl.BlockSpec((tn, d), lambda i, j: (i, 0)),
        out_shape=jax.ShapeDtypeStruct((t, d), jnp.float32),
        scratch_shapes=[pltpu.VMEM((d, tn), jnp.float32), pltpu.VMEM((eb, tn), _BF)],
        compiler_params=pltpu.CompilerParams(dimension_semantics=("parallel", "arbitrary"),
                                             vmem_limit_bytes=56 * 1024 * 1024),
        interpret=interpret,
        name="peer_expert",
    )(xb, u_b, vt_b, *aux)


def _peer_ffn_dense(hf, wq, keys, u_b, vt_b, interpret=False):
    xb, *aux = _peer_route(hf, wq, keys, interpret=interpret)
    return _peer_expert(xb, u_b, vt_b, aux, interpret=interpret)


S5_STEPS = 128
_S5_ROWS = 8
_S5_STATE = S5_G * S5_P


def _s5_scan_kernel(u_ref, wb_ref, lam_ref, wc_ref, y_ref, bu_ref, xs_ref, st_ref):
    @pl.when(pl.program_id(0) == 0)
    def _():
        st_ref[...] = jnp.zeros_like(st_ref)

    n = _S5_STATE
    bu_ref[...] = jnp.dot(u_ref[...].astype(_BF), wb_ref[...], preferred_element_type=jnp.float32)
    lr = lam_ref[0]
    li = lam_ref[1]

    def step(i, carry):
        xr, xi = carry
        rows = pl.ds(pl.multiple_of(i * _S5_ROWS, _S5_ROWS), _S5_ROWS)
        nr = lr * xr - li * xi + bu_ref[rows, :n]
        ni = lr * xi + li * xr + bu_ref[rows, n:]
        xs_ref[rows, :n] = nr
        xs_ref[rows, n:] = ni
        return nr, ni

    xr, xi = lax.fori_loop(0, S5_STEPS, step, (st_ref[0], st_ref[1]), unroll=4)
    st_ref[0] = xr
    st_ref[1] = xi
    y2 = jnp.dot(xs_ref[...].astype(_BF), wc_ref[...], preferred_element_type=jnp.float32)
    fwd = (lax.broadcasted_iota(jnp.int32, (S5_STEPS * _S5_ROWS, 1), 0) % _S5_ROWS) < (_S5_ROWS // 2)
    y_ref[...] = jnp.where(fwd, y2[:, :D_S5], y2[:, D_S5:])


def _s5_scan(u2, wb, lam, wc, interpret=False):
    rows = u2.shape[0]
    blk = S5_STEPS * _S5_ROWS
    n = _S5_STATE
    return pl.pallas_call(
        _s5_scan_kernel,
        grid=(rows // blk,),
        in_specs=[pl.BlockSpec((blk, 2 * D_S5), lambda i: (i, 0)),
                  pl.BlockSpec((2 * D_S5, 2 * n), lambda i: (0, 0)),
                  pl.BlockSpec((2, _S5_ROWS, n), lambda i: (0, 0, 0)),
                  pl.BlockSpec((2 * n, 2 * D_S5), lambda i: (0, 0))],
        out_specs=pl.BlockSpec((blk, D_S5), lambda i: (i, 0)),
        out_shape=jax.ShapeDtypeStruct((rows, D_S5), jnp.float32),
        scratch_shapes=[pltpu.VMEM((blk, 2 * n), jnp.float32), pltpu.VMEM((blk, 2 * n), jnp.float32),
                        pltpu.VMEM((2, _S5_ROWS, n), jnp.float32)],
        compiler_params=pltpu.CompilerParams(dimension_semantics=("arbitrary",),
                                             vmem_limit_bytes=48 * 1024 * 1024),
        interpret=interpret,
        name="s5_scan",
    )(u2, wb, lam, wc)


def _s5_weights(disc, c_re, c_im):
    eye = jnp.eye(S5_G, dtype=jnp.float32)

    def bdiag_in(bb):
        return jnp.einsum('gph,gk->ghkp', bb, eye).reshape(D_S5, _S5_STATE)

    def bdiag_out(cc):
        return jnp.einsum('ghp,gk->gpkh', cc, eye).reshape(_S5_STATE, D_S5)

    wb = jnp.stack([jnp.concatenate([bdiag_in(disc[d][2]), bdiag_in(disc[d][3])], axis=1) for d in range(2)])
    wc = jnp.stack([jnp.concatenate([bdiag_out(c_re[d]), -bdiag_out(c_im[d])], axis=0) for d in range(2)])
    half = _S5_ROWS // 2
    lam = jnp.stack([
        jnp.concatenate([jnp.broadcast_to(disc[d][i].reshape(1, _S5_STATE), (half, _S5_STATE))
                         for d in range(2)], axis=0) for i in range(2)])
    return wb.astype(_BF), lam, wc.astype(_BF)


def _s5_dir_kernel(uf_ref, ub_ref, wb_ref, lam_ref, wc_ref, yf_ref, yb_ref, buf_ref, bub_ref, xf_ref, xb_ref, st_ref):
    @pl.when(pl.program_id(0) == 0)
    def _():
        st_ref[...] = jnp.zeros_like(st_ref)

    n = _S5_STATE
    buf_ref[...] = jnp.dot(uf_ref[...].astype(_BF), wb_ref[0], preferred_element_type=jnp.float32)
    bub_ref[...] = jnp.dot(ub_ref[...].astype(_BF), wb_ref[1], preferred_element_type=jnp.float32)
    lr = lam_ref[0]
    li = lam_ref[1]
    fwd = lax.broadcasted_iota(jnp.int32, (_S5_ROWS, n), 0) < (_S5_ROWS // 2)

    def step(i, carry):
        xr, xi = carry
        rf = pl.ds(pl.multiple_of(i * _S5_ROWS, _S5_ROWS), _S5_ROWS)
        rb = pl.ds(pl.multiple_of((S5_STEPS - 1 - i) * _S5_ROWS, _S5_ROWS), _S5_ROWS)
        nr = lr * xr - li * xi + jnp.where(fwd, buf_ref[rf, :n], bub_ref[rb, :n])
        ni = lr * xi + li * xr + jnp.where(fwd, buf_ref[rf, n:], bub_ref[rb, n:])
        xf_ref[rf, :n] = nr
        xf_ref[rf, n:] = ni
        xb_ref[rb, :n] = nr
        xb_ref[rb, n:] = ni
        return nr, ni

    xr, xi = lax.fori_loop(0, S5_STEPS, step, (st_ref[0], st_ref[1]), unroll=4)
    st_ref[0] = xr
    st_ref[1] = xi
    yf_ref[...] = jnp.dot(xf_ref[...].astype(_BF), wc_ref[0], preferred_element_type=jnp.float32)
    yb_ref[...] = jnp.dot(xb_ref[...].astype(_BF), wc_ref[1], preferred_element_type=jnp.float32)


def _s5_dir_scan(u8, wb, lam, wc, *, ctx_len):
    rows = u8.shape[0]
    blk = S5_STEPS * _S5_ROWS
    n = _S5_STATE
    order = _scan_block_order(ctx_len // S5_STEPS, rows // blk, True)
    spec_f = pl.BlockSpec((blk, D_S5), lambda i: (i, 0))
    spec_b = pl.BlockSpec((blk, D_S5), lambda i: (order(i), 0))
    out = jax.ShapeDtypeStruct((rows, D_S5), jnp.float32)
    big = pltpu.VMEM((blk, 2 * n), jnp.float32)
    return pl.pallas_call(
        _s5_dir_kernel,
        grid=(rows // blk,),
        in_specs=[spec_f, spec_b,
                  pl.BlockSpec((2, D_S5, 2 * n), lambda i: (0, 0, 0)),
                  pl.BlockSpec((2, _S5_ROWS, n), lambda i: (0, 0, 0)),
                  pl.BlockSpec((2, 2 * n, D_S5), lambda i: (0, 0, 0))],
        out_specs=[spec_f, spec_b],
        out_shape=[out, out],
        scratch_shapes=[big, big, big, big, pltpu.VMEM((2, _S5_ROWS, n), jnp.float32)],
        compiler_params=pltpu.CompilerParams(dimension_semantics=("arbitrary",),
                                             vmem_limit_bytes=56 * 1024 * 1024),
        name="s5_scan",
    )(u8, u8, wb, lam, wc)


HG_BLOCK = 128
_HG_PAIR = 2 * HG_HEAD


def _hgrn_scan_kernel(q_ref, k_ref, v_ref, lf_ref, o_ref, st_ref):
    @pl.when(pl.program_id(1) == 0)
    def _():
        st_ref[...] = jnp.zeros_like(st_ref)

    ch = HG_CHUNK
    r = lax.broadcasted_iota(jnp.int32, (HG_BLOCK, HG_BLOCK), 0)
    c = lax.broadcasted_iota(jnp.int32, (HG_BLOCK, HG_BLOCK), 1)
    tri = ((r // ch == c // ch) & (r >= c)).astype(jnp.float32)
    same_head = (lax.broadcasted_iota(jnp.int32, (_HG_PAIR, _HG_PAIR), 0) // HG_HEAD
                 == lax.broadcasted_iota(jnp.int32, (_HG_PAIR, _HG_PAIR), 1) // HG_HEAD)
    ones_blk = same_head.astype(_BF)
    tcol = lax.broadcasted_iota(jnp.int32, (ch, 1), 0)
    cum_all = _dot3(tri, lf_ref[0])
    for p in range(D_HG // _HG_PAIR):
        lanes = slice(p * _HG_PAIR, (p + 1) * _HG_PAIR)
        state = st_ref[p]
        for ci in range(HG_BLOCK // ch):
            rows = slice(ci * ch, (ci + 1) * ch)
            q = q_ref[0, rows, lanes]
            k = k_ref[0, rows, lanes]
            v = v_ref[0, rows, lanes]
            cum = cum_all[rows, lanes]
            last = cum[ch - 1:ch]
            o = _dot(q * jnp.exp(cum), state, _NT)
            w = [q * jnp.exp(jnp.minimum(cum - cum[s:s + 1], 0.0)) * k[s:s + 1] for s in range(ch)]
            rel = jnp.dot(jnp.concatenate(w, axis=0).astype(_BF), ones_blk,
                          preferred_element_type=jnp.float32)
            for s in range(ch):
                o = o + jnp.where(tcol >= s, rel[s * ch:(s + 1) * ch], 0.0) * v[s:s + 1]
            o_ref[0, rows, lanes] = o
            kv = _dot(v, k * jnp.exp(last - cum), _TN)
            state = state * jnp.exp(last) + jnp.where(same_head, kv, 0.0)
        st_ref[p] = state


def _hgrn_scan(q, k, v, lf):
    g, l, d = q.shape
    spec = pl.BlockSpec((1, HG_BLOCK, d), lambda i, j: (i, j, 0))
    return pl.pallas_call(
        _hgrn_scan_kernel,
        grid=(g, l // HG_BLOCK),
        in_specs=[spec] * 4,
        out_specs=spec,
        out_shape=jax.ShapeDtypeStruct((g, l, d), jnp.float32),
        scratch_shapes=[pltpu.VMEM((d // _HG_PAIR, _HG_PAIR, _HG_PAIR), jnp.float32)],
        compiler_params=pltpu.CompilerParams(dimension_semantics=("parallel", "arbitrary")),
        name="hgrn_scan",
    )(q, k, v, lf)


def _hgrn_dir_kernel(q_ref, k_ref, v_ref, lf_ref, o_ref, st_ref, *, reverse):
    @pl.when(pl.program_id(1) == 0)
    def _():
        st_ref[...] = jnp.zeros_like(st_ref)

    ch = HG_CHUNK
    n_ch = HG_BLOCK // ch
    r = lax.broadcasted_iota(jnp.int32, (HG_BLOCK, HG_BLOCK), 0)
    c = lax.broadcasted_iota(jnp.int32, (HG_BLOCK, HG_BLOCK), 1)
    tri = ((r // ch == c // ch) & ((r <= c) if reverse else (r >= c))).astype(jnp.float32)
    same_head = (lax.broadcasted_iota(jnp.int32, (_HG_PAIR, _HG_PAIR), 0) // HG_HEAD
                 == lax.broadcasted_iota(jnp.int32, (_HG_PAIR, _HG_PAIR), 1) // HG_HEAD)
    ones_blk = same_head.astype(_BF)
    tcol = lax.broadcasted_iota(jnp.int32, (ch, 1), 0)
    end = 0 if reverse else ch - 1
    cum_all = _dot3(tri, lf_ref[0])
    n_pair = D_HG // _HG_PAIR
    intra, kv, dec, qd = {}, {}, {}, {}
    for p in range(n_pair):
        lanes = slice(p * _HG_PAIR, (p + 1) * _HG_PAIR)
        for ci in range(n_ch):
            rows = slice(ci * ch, (ci + 1) * ch)
            q = q_ref[0, rows, lanes]
            k = k_ref[0, rows, lanes]
            v = v_ref[0, rows, lanes]
            cum = cum_all[rows, lanes]
            last = cum[end:end + 1]
            w = [q * jnp.exp(jnp.minimum(cum - cum[s:s + 1], 0.0)) * k[s:s + 1] for s in range(ch)]
            rel = jnp.dot(jnp.concatenate(w, axis=0).astype(_BF), ones_blk,
                          preferred_element_type=jnp.float32)
            o = jnp.zeros((ch, _HG_PAIR), jnp.float32)
            for s in range(ch):
                seen = (tcol <= s) if reverse else (tcol >= s)
                o = o + jnp.where(seen, rel[s * ch:(s + 1) * ch], 0.0) * v[s:s + 1]
            intra[p, ci] = o
            kv[p, ci] = jnp.where(same_head, _dot(v, k * jnp.exp(last - cum), _TN), 0.0)
            dec[p, ci] = jnp.exp(last)
            qd[p, ci] = q * jnp.exp(cum)
    for p in range(n_pair):
        lanes = slice(p * _HG_PAIR, (p + 1) * _HG_PAIR)
        state = st_ref[p]
        for ci in (range(n_ch - 1, -1, -1) if reverse else range(n_ch)):
            o_ref[0, ci * ch:(ci + 1) * ch, lanes] = intra[p, ci] + _dot(qd[p, ci], state, _NT)
            state = state * dec[p, ci] + kv[p, ci]
        st_ref[p] = state


def _hgrn_dir_scan(q, k, v, lf, *, ctx_len, reverse):
    g, l, d = q.shape
    order = _scan_block_order(ctx_len // HG_BLOCK, l // HG_BLOCK, reverse)
    spec = pl.BlockSpec((1, HG_BLOCK, d), lambda i, j: (i, order(j), 0))
    return pl.pallas_call(
        functools.partial(_hgrn_dir_kernel, reverse=reverse),
        grid=(g, l // HG_BLOCK),
        in_specs=[spec] * 4,
        out_specs=spec,
        out_shape=jax.ShapeDtypeStruct((g, l, d), jnp.float32),
        scratch_shapes=[pltpu.VMEM((d // _HG_PAIR, _HG_PAIR, _HG_PAIR), jnp.float32)],
        compiler_params=pltpu.CompilerParams(dimension_semantics=("parallel", "arbitrary")),
        name="hgrn_scan",
    )(q, k, v, lf)


_RW_PAIR = 2 * RW_HEAD
RW_PRE_T = 128
RW_BATCH_BLOCK = 4


def _rwkv_pair_kernel(r_ref, lw_ref, k_ref, v_ref, kk_ref, a_ref, y_ref, st_ref, *, chunk, reverse):
    @pl.when(pl.program_id(1) == 0)
    def _():
        st_ref[...] = jnp.zeros_like(st_ref)

    row = lax.broadcasted_iota(jnp.int32, (chunk, chunk), 0)
    col = lax.broadcasted_iota(jnp.int32, (chunk, chunk), 1)
    strict = row < col if reverse else row > col
    incl = row <= col if reverse else row >= col
    end = 0 if reverse else chunk - 1
    tri = incl.astype(jnp.float32)
    eye = (row == col).astype(jnp.float32)
    blk8 = row // 8 == col // 8
    merge_masks = []
    size = 8
    while size < chunk:
        merge_masks.append((row // (2 * size) == col // (2 * size)) & (row // size != col // size))
        size *= 2
    lane_head = lax.broadcasted_iota(jnp.int32, (1, _RW_PAIR), 1) // RW_HEAD
    head0 = lane_head == 0
    same_head = (lax.broadcasted_iota(jnp.int32, (_RW_PAIR, _RW_PAIR), 0) // RW_HEAD
                 == lax.broadcasted_iota(jnp.int32, (_RW_PAIR, _RW_PAIR), 1) // RW_HEAD)
    n_pair = D_RW // _RW_PAIR
    ps = range(r_ref.shape[0] * n_pair)
    ph = [(p, h) for p in ps for h in range(2)]
    bi = [p // n_pair for p in ps]
    sl = [slice((p % n_pair) * _RW_PAIR, (p % n_pair + 1) * _RW_PAIR) for p in ps]
    lw = [lw_ref[bi[p], :, sl[p]] for p in ps]
    cum = [_dot3(tri, lw[p]) for p in ps]
    pw = [jnp.exp(cum[p]) for p in ps]
    p_inv = [jnp.exp(-cum[p]) for p in ps]
    p_end = [pw[p][end:end + 1, :] for p in ps]
    kk = [kk_ref[bi[p], :, sl[p]] for p in ps]
    bh = [kk[p] * a_ref[bi[p], :, sl[p]] * p_inv[p] for p in ps]
    kh = [k_ref[bi[p], :, sl[p]] * p_inv[p] for p in ps]
    v = [v_ref[bi[p], :, sl[p]] for p in ps]
    s0 = [st_ref[p] for p in ps]
    x1 = [jnp.concatenate([-kk[p] * jnp.exp(cum[p] - lw[p]), r_ref[bi[p], :, sl[p]] * pw[p]], axis=0) for p in ps]
    x1h = {(p, h): jnp.where(lane_head == h, x1[p], 0.0) for p, h in ph}
    g1 = {q: _dot(x1h[q], bh[q[0]], _NT) for q in ph}
    g2 = {q: _dot(x1h[q], kh[q[0]], _NT) for q in ph}
    hm = [_dot(x1[p], s0[p], _NT) for p in ps]
    ab = {q: jnp.where(strict, g1[q][:chunk], 0.0) for q in ph}
    d1 = {q: jnp.where(blk8, ab[q], 0.0) for q in ph}
    d2 = {q: _dot3(d1[q], d1[q]) for q in ph}
    inv = {q: eye + d1[q] for q in ph}
    inv = {q: inv[q] + _dot3(inv[q], d2[q]) for q in ph}
    d4 = {q: _dot3(d2[q], d2[q]) for q in ph}
    inv = {q: inv[q] + _dot3(inv[q], d4[q]) for q in ph}
    for m in merge_masks:
        li = {q: _dot(jnp.where(m, ab[q], 0.0), inv[q]) for q in ph}
        inv = {q: inv[q] + _dot(inv[q], li[q]) for q in ph}
    akv = {q: _dot(jnp.where(strict, g2[q][:chunk], 0.0), v[q[0]]) for q in ph}
    rhs = [hm[p][:chunk] + jnp.where(head0, akv[p, 0], akv[p, 1]) for p in ps]
    eh = {q: _dot(inv[q], rhs[q[0]]) for q in ph}
    e = [jnp.where(head0, eh[p, 0], eh[p, 1]) for p in ps]
    yh = {q: _dot(jnp.where(incl, g1[q][chunk:], 0.0), e[q[0]])
          + _dot(jnp.where(incl, g2[q][chunk:], 0.0), v[q[0]]) for q in ph}
    for p in ps:
        y_ref[bi[p], :, sl[p]] = hm[p][chunk:] + jnp.where(head0, yh[p, 0], yh[p, 1])
    for p in ps:
        ev = jnp.concatenate([e[p], v[p]], axis=0)
        x2 = jnp.concatenate([bh[p] * p_end[p], kh[p] * p_end[p]], axis=0)
        st_ref[p] = s0[p] * p_end[p] + jnp.where(same_head, _dot(ev, x2, _TN), 0.0)


def _rwkv_pair_scan(r, lw, k, v, kk, a, *, ctx_len, reverse):
    nb, l, d = r.shape
    chunk = RW_CHUNK
    order = _scan_block_order(ctx_len // chunk, l // chunk, reverse)
    bb = RW_BATCH_BLOCK if nb % RW_BATCH_BLOCK == 0 else 1
    spec = pl.BlockSpec((bb, chunk, d), lambda b, c: (b, order(c), 0))
    return pl.pallas_call(
        functools.partial(_rwkv_pair_kernel, chunk=chunk, reverse=reverse),
        grid=(nb // bb, l // chunk),
        in_specs=[spec] * 6,
        out_specs=spec,
        out_shape=jax.ShapeDtypeStruct((nb, l, d), jnp.float32),
        scratch_shapes=[pltpu.VMEM((bb * d // _RW_PAIR, _RW_PAIR, _RW_PAIR), jnp.float32)],
        compiler_params=pltpu.CompilerParams(dimension_semantics=("parallel", "arbitrary")),
        name="rwkv_scan",
    )(r, lw, k, v, kk, a)


def _softplus(z):
    return jnp.maximum(z, 0.0) + jnp.log(1.0 + jnp.exp(-jnp.abs(z)))


def _sigmoid(z):
    return 1.0 / (1.0 + jnp.exp(-z))


def _rwkv_pre_kernel(prev_ref, cur_ref, next_ref, mu_ref, vec_ref, gup_ref, wup_ref, aup_ref, ones_ref,
                     r_ref, v_ref, kk_ref, g_ref, bonus_ref, lw0_ref, k0_ref, a0_ref, lw1_ref, k1_ref, a1_ref,
                     ext_ref, s_ref, *, ctx_len, seq_len):
    t_rows, w = RW_PRE_T, GRID_W
    ext_ref[0:w] = prev_ref[0]
    ext_ref[w:w + t_rows] = cur_ref[0]
    ext_ref[w + t_rows:w + t_rows + w] = next_ref[0]
    t = pl.program_id(1) * t_rows + lax.broadcasted_iota(jnp.int32, (t_rows, 128), 0)
    lane4 = lax.broadcasted_iota(jnp.int32, (t_rows, 128), 1) % 4
    is_ctx = t < ctx_len
    tx = t - ctx_len
    col = tx % w
    is_lat = jnp.logical_not(is_ctx)
    use_p1 = (is_ctx & (lane4 % 2 == 0) & (t > 0)) | (is_lat & (lane4 == 0) & (col != 0))
    use_n1 = (is_ctx & (lane4 % 2 == 1) & (t < ctx_len - 1)) | (is_lat & (lane4 == 1) & (col != w - 1))
    use_p64 = is_lat & (lane4 == 2) & (tx >= w)
    use_n64 = is_lat & (lane4 == 3) & (tx < seq_len - w)
    for lt in range(RW_COLS // 128):
        lanes = slice(lt * 128, (lt + 1) * 128)
        p = ext_ref[w:w + t_rows, lanes]
        shifted = jnp.where(use_p1, ext_ref[w - 1:w - 1 + t_rows, lanes],
                            jnp.where(use_n1, ext_ref[w + 1:w + 1 + t_rows, lanes],
                                      jnp.where(use_p64, ext_ref[0:t_rows, lanes],
                                                jnp.where(use_n64, ext_ref[2 * w:2 * w + t_rows, lanes], 0.0))))
        s_ref[:, lanes] = p + mu_ref[:, lanes] * (shifted - p)
    d = D_RW
    r = s_ref[:, 0:d]
    k = s_ref[:, d:2 * d]
    v = s_ref[:, 2 * d:3 * d]
    o = 3 * d
    g_lo = s_ref[:, o:o + RW_G_LORA]
    wl = s_ref[:, o + RW_G_LORA:o + RW_G_LORA + 2 * RW_W_LORA]
    al = s_ref[:, o + RW_G_LORA + 2 * RW_W_LORA:o + RW_G_LORA + 2 * RW_W_LORA + 2 * RW_A_LORA]
    k_k, k_a, r_k = vec_ref[0:1], vec_ref[1:2], vec_ref[2:3]
    ones = ones_ref[...]
    r_ref[0] = r
    v_ref[0] = v
    g_ref[0] = jnp.dot(_sigmoid(g_lo).astype(_BF), gup_ref[...], preferred_element_type=jnp.float32)
    kkr = k * k_k
    ss = jnp.dot((kkr * kkr).astype(_BF), ones, preferred_element_type=jnp.float32)
    kk_ref[0] = kkr / jnp.maximum(jnp.sqrt(ss), 1e-12)
    w_pre = jnp.dot(jnp.tanh(wl).astype(_BF), wup_ref[...], preferred_element_type=jnp.float32)
    a_pre = jnp.dot(al.astype(_BF), aup_ref[...], preferred_element_type=jnp.float32)
    k_sum = 0.0
    for dr, (lw_o, k_o, a_o) in enumerate(((lw0_ref, k0_ref, a0_ref), (lw1_ref, k1_ref, a1_ref))):
        w_log = -_softplus(-(vec_ref[3 + dr:4 + dr] + w_pre[:, dr * d:(dr + 1) * d])) - 0.5
        a = _sigmoid(vec_ref[5 + dr:6 + dr] + a_pre[:, dr * d:(dr + 1) * d])
        k_d = k * (1.0 + (a - 1.0) * k_a)
        lw_o[0] = -jnp.exp(w_log)
        k_o[0] = k_d
        a_o[0] = a
        k_sum = k_sum + k_d
    bonus_ref[0] = jnp.dot((r * k_sum * r_k).astype(_BF), ones, preferred_element_type=jnp.float32)


def _head_ones(width, head):
    i = jnp.arange(width) // head
    return (i[:, None] == i[None, :]).astype(_BF)


def _rwkv_pre(p, mu, k_k, k_a, r_k, w0, a0, g_up, w_up, a_up, *, ctx_len):
    nb, l, cols = p.shape
    t_rows, w, d = RW_PRE_T, GRID_W, D_RW
    nblk = l // w
    per = t_rows // w
    zeros = jnp.zeros((RW_W_LORA, d), jnp.float32)
    wup = jnp.concatenate([jnp.concatenate([w_up[0], zeros], axis=1),
                           jnp.concatenate([zeros, w_up[1]], axis=1)], axis=0).astype(_BF)
    aup = jnp.concatenate([jnp.concatenate([a_up[0], zeros], axis=1),
                           jnp.concatenate([zeros, a_up[1]], axis=1)], axis=0).astype(_BF)
    vec = jnp.stack([k_k, k_a, r_k.reshape(d), w0[0], w0[1], a0[0], a0[1], jnp.zeros((d,), jnp.float32)])
    out = jax.ShapeDtypeStruct((nb, l, d), jnp.float32)
    out_spec = pl.BlockSpec((1, t_rows, d), lambda b, j: (b, j, 0))
    full = lambda shape: pl.BlockSpec(shape, lambda b, j: (0,) * len(shape))
    return pl.pallas_call(
        functools.partial(_rwkv_pre_kernel, ctx_len=ctx_len, seq_len=l - ctx_len),
        grid=(nb, l // t_rows),
        in_specs=[pl.BlockSpec((1, w, cols), lambda b, j: (b, jnp.maximum(j * per - 1, 0), 0)),
                  pl.BlockSpec((1, t_rows, cols), lambda b, j: (b, j, 0)),
                  pl.BlockSpec((1, w, cols), lambda b, j: (b, jnp.minimum(j * per + per, nblk - 1), 0)),
                  full((1, cols)), full((8, d)), full((RW_G_LORA, d)),
                  full((2 * RW_W_LORA, 2 * d)), full((2 * RW_A_LORA, 2 * d)), full((d, d))],
        out_specs=[out_spec] * 11,
        out_shape=[out] * 11,
        scratch_shapes=[pltpu.VMEM((t_rows + 2 * w, cols), jnp.float32), pltpu.VMEM((t_rows, cols), jnp.float32)],
        compiler_params=pltpu.CompilerParams(dimension_semantics=("parallel", "parallel"),
                                             vmem_limit_bytes=48 * 1024 * 1024),
        name="rwkv_pre",
    )(p, p, p, mu.reshape(1, cols), vec, g_up.astype(_BF), wup, aup, _head_ones(d, RW_HEAD))


def _rwkv_post_kernel(y0_ref, y1_ref, v_ref, g_ref, bonus_ref, gn_ref, ones_ref, o_ref):
    y = y0_ref[0] + y1_ref[0]
    ones = ones_ref[...]
    inv_n = 1.0 / RW_HEAD
    mu_y = jnp.dot(y.astype(_BF), ones, preferred_element_type=jnp.float32) * inv_n
    yc = y - mu_y
    var_y = jnp.dot((yc * yc).astype(_BF), ones, preferred_element_type=jnp.float32) * inv_n
    yn = yc * lax.rsqrt(var_y + RW_GN_EPS) * gn_ref[0:1] + gn_ref[1:2]
    o_ref[0] = (yn + bonus_ref[0] * v_ref[0]) * g_ref[0]


def _rwkv_post(y0, y1, v, g, bonus, gn_g, gn_b):
    nb, l, d = y0.shape
    spec = pl.BlockSpec((1, RW_PRE_T, d), lambda b, j: (b, j, 0))
    return pl.pallas_call(
        _rwkv_post_kernel,
        grid=(nb, l // RW_PRE_T),
        in_specs=[spec] * 5 + [pl.BlockSpec((2, d), lambda b, j: (0, 0)), pl.BlockSpec((d, d), lambda b, j: (0, 0))],
        out_specs=spec,
        out_shape=jax.ShapeDtypeStruct((nb, l, d), jnp.float32),
        compiler_params=pltpu.CompilerParams(dimension_semantics=("parallel", "parallel")),
        name="rwkv_post",
    )(y0, y1, v, g, bonus, jnp.stack([gn_g, gn_b]), _head_ones(d, RW_HEAD))


def _rwkv7_mixer_fused(pc, px, mu, w0, w_up, a0, a_up, g_up, k_k, k_a, r_k, gn_g, gn_b):
    lc = pc.shape[1]
    p = jnp.concatenate([_f32(pc), _f32(px)], axis=1)
    r, v, kk, g, bonus, lw0, k0, a0_, lw1, k1, a1_ = _rwkv_pre(
        p, _f32(mu), _f32(k_k), _f32(k_a), _f32(r_k), _f32(w0), _f32(a0), _f32(g_up), _f32(w_up), _f32(a_up),
        ctx_len=lc)
    y0 = _rwkv_pair_scan(r, lw0, k0, v, kk, a0_, ctx_len=lc, reverse=False)
    y1 = _rwkv_pair_scan(r, lw1, k1, v, kk, a1_, ctx_len=lc, reverse=True)
    y = _rwkv_post(y0, y1, v, g, bonus, _f32(gn_g), _f32(gn_b))
    return y[:, :lc], y[:, lc:]


def _f32(t):
    return t.astype(jnp.float32)


def _layer_norm(x, g, b):
    xf = _f32(x)
    mu = jnp.mean(xf, -1, keepdims=True)
    var = jnp.mean(jnp.square(xf - mu), -1, keepdims=True)
    return ((xf - mu) * lax.rsqrt(var + LN_EPS) * _f32(g) + _f32(b)).astype(x.dtype)


def _shift(t, axis, step):
    n = t.shape[axis]
    pad = [(0, 0)] * t.ndim
    if step > 0:
        pad[axis] = (1, 0)
        return lax.slice_in_dim(jnp.pad(t, pad), 0, n, axis=axis)
    pad[axis] = (0, 1)
    return lax.slice_in_dim(jnp.pad(t, pad), 1, n + 1, axis=axis)


def _qshift_grid(s, rows):
    b, l, ch = s.shape
    s4 = s.reshape(b, rows, GRID_W, ch // 4, 4)
    out = jnp.stack([_shift(s4[..., 0], 2, 1), _shift(s4[..., 1], 2, -1),
                     _shift(s4[..., 2], 1, 1), _shift(s4[..., 3], 1, -1)], axis=-1)
    return out.reshape(b, l, ch)


def _qshift_seq(s):
    b, l, ch = s.shape
    s4 = s.reshape(b, l, ch // 4, 4)
    out = jnp.stack([_shift(s4[..., 0], 1, 1), _shift(s4[..., 1], 1, -1),
                     _shift(s4[..., 2], 1, 1), _shift(s4[..., 3], 1, -1)], axis=-1)
    return out.reshape(b, l, ch)


def _s5_discretise(lam_re, lam_im, log_step, b_re, b_im):
    dt = jnp.exp(log_step)[:, None]
    mag = jnp.exp(lam_re * dt)
    ang = lam_im * dt
    lb_re, lb_im = mag * jnp.cos(ang), mag * jnp.sin(ang)
    den = lam_re * lam_re + lam_im * lam_im
    nr, ni = lb_re - 1.0, lb_im
    co_re = (nr * lam_re + ni * lam_im) / den
    co_im = (ni * lam_re - nr * lam_im) / den
    bb_re = co_re[..., None] * b_re - co_im[..., None] * b_im
    bb_im = co_re[..., None] * b_im + co_im[..., None] * b_re
    return lb_re, lb_im, bb_re, bb_im


def _s5_mixer(uc, ux, lam_re, lam_im, log_step, b_re, b_im, c_re, c_im, d_skip, glu_w, glu_b,
              interpret=False):
    disc = [_s5_discretise(_f32(lam_re[d]), _f32(lam_im[d]), _f32(log_step[d]),
                           _f32(b_re[d]), _f32(b_im[d])) for d in range(2)]
    c_re, c_im = _f32(c_re), _f32(c_im)

    wb, lam, wc = _s5_weights(disc, c_re, c_im)
    uc, ux = _f32(uc), _f32(ux)
    nb, lc, lx = uc.shape[0], uc.shape[1], ux.shape[1]
    assert 2 * nb == _S5_ROWS
    u_t = jnp.concatenate([uc, ux], axis=1).transpose(1, 0, 2)
    u8 = jnp.concatenate([u_t, u_t], axis=1).reshape((lc + lx) * _S5_ROWS, D_S5)
    y_f, y_b = _s5_dir_scan(u8, wb, lam, wc, ctx_len=lc)
    y = (y_f.reshape(lc + lx, _S5_ROWS, D_S5)[:, :nb] + y_b.reshape(lc + lx, _S5_ROWS, D_S5)[:, nb:])
    y = y.transpose(1, 0, 2)

    def post(y, u):
        y = jax.nn.gelu(y + _f32(d_skip) * u)
        return y * jax.nn.sigmoid(_mm3(y, glu_w) + _f32(glu_b))

    return post(y[:, :lc], uc), post(y[:, lc:], ux)


def _gla_chunkwise(q, k, v, log_f, s0):
    b, h, l, _ = q.shape
    dv = v.shape[-1]
    n = l // HG_CHUNK
    blk = lambda t: t.reshape(b, h, n, HG_CHUNK, t.shape[-1])
    q, k, v, log_f = blk(q), blk(k), blk(v), blk(log_f)
    cum = jnp.cumsum(log_f, axis=3)
    lower = jnp.tril(jnp.ones((HG_CHUNK, HG_CHUNK), bool))[:, :, None]
    rel = jnp.exp(jnp.where(lower, cum[..., :, None, :] - cum[..., None, :, :], -jnp.inf))
    scores = jnp.einsum('bhntd,bhnsd,bhntsd->bhnts', q, k, rel)
    o_intra = jnp.einsum('bhnts,bhnsv->bhntv', scores, v)
    last = cum[..., -1:, :]
    chunk_kv = jnp.einsum('bhnsd,bhnsv->bhndv', k * jnp.exp(last - cum), v)
    chunk_decay = jnp.exp(last[..., 0, :])

    def step(state, inp):
        dec, kv = inp
        return dec[..., None] * state + kv, state

    s_fin, s_prev = lax.scan(step, s0, (jnp.moveaxis(chunk_decay, 2, 0), jnp.moveaxis(chunk_kv, 2, 0)))
    o_inter = jnp.einsum('bhntd,bhndv->bhntv', q * jnp.exp(cum), jnp.moveaxis(s_prev, 0, 2))
    return (o_intra + o_inter).reshape(b, h, l, dv), s_fin


def _hgrn2_mixer(pc, px, lb, norm_g):
    def pre(p):
        q, i, f_f, f_b, g = jnp.split(_f32(p), 5, axis=-1)
        logf = [jnp.log(lb[d] + (1.0 - lb[d]) * jax.nn.sigmoid(f)) for d, f in enumerate((f_f, f_b))]
        return jax.nn.silu(q), i, logf, g

    qc, ic, lfc, gc = pre(pc)
    qx, ix, lfx, gx = pre(px)
    nb, lc = pc.shape[0], pc.shape[1]

    q = jnp.concatenate([qc, qx], axis=1)
    v = jnp.concatenate([ic, ix], axis=1)
    o = 0.0
    for d in range(2):
        lf = jnp.concatenate([lfc[d], lfx[d]], axis=1)
        o = o + _hgrn_dir_scan(q, -jnp.expm1(lf), v, lf, ctx_len=lc, reverse=d == 1)

    def post(o, g):
        oh = o.reshape(o.shape[0], o.shape[1], HG_HEADS, HG_HEAD)
        oh = oh * lax.rsqrt(jnp.mean(oh * oh, -1, keepdims=True) + RMS_EPS)
        return oh.reshape(o.shape) * _f32(norm_g) * jax.nn.silu(g)

    yc = post(o[:, :lc], gc)
    yx = post(o[:, lc:], gx)
    return yc.astype(pc.dtype), yx.astype(px.dtype)


def _rwkv7_step(state, inp):
    r, w, k, v, kk, a = inp
    sk = jnp.einsum('bhvk,bhk->bhv', state, kk)
    state = (state * w[:, :, None, :] - sk[..., None] * (kk * a)[:, :, None, :]
             + v[..., None] * k[:, :, None, :])
    return state, jnp.einsum('bhvk,bhk->bhv', state, r)


def _rwkv7_scan(r, w, k, v, kk, a, s0, reverse):
    xs = tuple(jnp.moveaxis(t, 1, 0) for t in (r, w, k, v, kk, a))
    s_fin, y = lax.scan(_rwkv7_step, s0, xs, reverse=reverse)
    return jnp.moveaxis(y, 0, 1), s_fin


def _rwkv7_mixer(pc, px, rows, mu, w0, w_up, a0, a_up, g_up, k_k, k_a, r_k, gn_g, gn_b):
    split_at = [D_RW, 2 * D_RW, 3 * D_RW, 3 * D_RW + RW_G_LORA,
                3 * D_RW + RW_G_LORA + RW_W_LORA, 3 * D_RW + RW_G_LORA + 2 * RW_W_LORA,
                3 * D_RW + RW_G_LORA + 2 * RW_W_LORA + RW_A_LORA]

    def heads(t):
        return t.reshape(t.shape[0], t.shape[1], RW_HEADS, RW_HEAD)

    def pre(p, shifted):
        s = p + _f32(mu) * (shifted - p)
        r, k, v, g_lo, wl_f, wl_b, al_f, al_b = jnp.split(s, split_at, axis=-1)
        g = _mm3(jax.nn.sigmoid(g_lo), g_up)
        kk = heads(k * _f32(k_k))
        kk = kk / jnp.maximum(jnp.sqrt(jnp.sum(kk * kk, -1, keepdims=True)), 1e-12)
        r, v = heads(r), heads(v)
        dirs, bonus = [], 0.0
        for d, (wl, al) in enumerate(((wl_f, al_f), (wl_b, al_b))):
            w_log = -jax.nn.softplus(-(_f32(w0[d]) + _mm3(jnp.tanh(wl), w_up[d]))) - 0.5
            a = jax.nn.sigmoid(_f32(a0[d]) + _mm3(al, a_up[d]))
            k_d = heads(k * (1.0 + (a - 1.0) * _f32(k_a)))
            dirs.append((heads(-jnp.exp(w_log)), k_d, heads(a)))
            bonus = bonus + jnp.sum(r * k_d * _f32(r_k), -1, keepdims=True)
        return dict(r=r, v=v, kk=kk, g=g, dirs=dirs, bonus=bonus)

    def post(y, q, shape):
        mu_y = jnp.mean(y, -1, keepdims=True)
        var_y = jnp.mean(jnp.square(y - mu_y), -1, keepdims=True)
        y = ((y - mu_y) * lax.rsqrt(var_y + RW_GN_EPS) * _f32(gn_g).reshape(RW_HEADS, RW_HEAD)
             + _f32(gn_b).reshape(RW_HEADS, RW_HEAD))
        return (y + q['bonus'] * q['v']).reshape(shape[0], shape[1], D_RW) * q['g']

    pc32, px32 = _f32(pc), _f32(px)
    qc = pre(pc32, _qshift_seq(pc32))
    qx = pre(px32, _qshift_grid(px32, rows))
    lc, lx = pc.shape[1], px.shape[1]

    def seq(tc, tx):
        t = jnp.concatenate([tc, tx], axis=1)
        return t.transpose(0, 2, 1, 3).reshape(t.shape[0] * RW_HEADS, lc + lx, RW_HEAD)

    r, v, kk = (seq(qc[n], qx[n]) for n in ('r', 'v', 'kk'))
    y = 0.0
    for d in range(2):
        lw, k_d, a = (seq(qc['dirs'][d][i], qx['dirs'][d][i]) for i in range(3))
        y = y + _rwkv_scan(r, lw, k_d, v, kk, a, heads=RW_HEADS, ctx_len=lc, reverse=d == 1)
    nb = pc.shape[0]
    y = y.reshape(nb, RW_HEADS, lc + lx, RW_HEAD).transpose(0, 2, 1, 3)
    yc = post(y[:, :lc], qc, pc.shape)
    yx = post(y[:, lc:], qx, px.shape)
    return yc.astype(pc.dtype), yx.astype(px.dtype)


def _peer_ffn(h, wq, keys, u_tab, v_tab):
    b, l, d = h.shape
    t = b * l
    hf = h.reshape(t, d)
    q = _matmul(hf, wq).reshape(t, PEER_HEADS, 2, PEER_QDIM // 2)
    scores = _f32(jnp.einsum('thpd,hpkd->thpk', q, keys))
    s_top, i_top = lax.top_k(scores, PEER_TOPK)
    cand = s_top[..., 0, :, None] + s_top[..., 1, None, :]
    cand_s, cand_i = lax.top_k(cand.reshape(t, PEER_HEADS, PEER_TOPK * PEER_TOPK), PEER_TOPK)
    i1 = jnp.take_along_axis(i_top[..., 0, :], cand_i // PEER_TOPK, axis=-1)
    i2 = jnp.take_along_axis(i_top[..., 1, :], cand_i % PEER_TOPK, axis=-1)
    expert = (i1 * PEER_NKEYS + i2).reshape(t, PEER_HEADS * PEER_TOPK)
    gate = jax.nn.softmax(cand_s, axis=-1).reshape(t, PEER_HEADS * PEER_TOPK).astype(h.dtype)
    nb = t // PEER_BLOCK

    def block(args):
        hb, eb, gb = args
        z = jnp.einsum('tkd,td->tk', u_tab[eb], hb)
        return jnp.einsum('tk,tkd->td', jax.nn.gelu(z) * gb, v_tab[eb])

    out = lax.map(block, (hf.reshape(nb, PEER_BLOCK, d),
                          expert.reshape(nb, PEER_BLOCK, PEER_HEADS * PEER_TOPK),
                          gate.reshape(nb, PEER_BLOCK, PEER_HEADS * PEER_TOPK)))
    return out.reshape(b, l, d)


def kernel(x, c, ctx, c_ctx, ada_w, ada_b, w_in, w_out, ln1_g, ln1_b, ln2_g, ln2_b,
           s5_lam_re, s5_lam_im, s5_log_step, s5_b_re, s5_b_im, s5_c_re, s5_c_im,
           s5_d, s5_glu_w, s5_glu_b, hgrn_lb_logits, hgrn_norm_g,
           rwkv_mu, rwkv_w0, rwkv_w_up, rwkv_a0, rwkv_a_up, rwkv_g_up, rwkv_k_k, rwkv_k_a,
           rwkv_r_k, rwkv_gn_g, rwkv_gn_b, peer_wq, peer_keys, peer_u, peer_v):
    rows = x.shape[1] // GRID_W
    lb_cum = jnp.cumsum(jax.nn.softmax(_f32(hgrn_lb_logits), axis=1), axis=1)
    lb_all = lb_cum - lb_cum[:, :1]
    col_split = [D_S5, D_S5 + 5 * D_HG]
    xs, cs = x, ctx
    for l in range(DEPTH):
        mod_x = (jax.nn.silu(c) @ ada_w[l] + ada_b[l])[:, None, :]
        mod_c = (jax.nn.silu(c_ctx) @ ada_w[l] + ada_b[l])[None, None, :]
        sh1x, sc1x, g1x, sh2x, sc2x, g2x = jnp.split(mod_x, 6, axis=-1)
        sh1c, sc1c, g1c, sh2c, sc2c, g2c = jnp.split(mod_c, 6, axis=-1)

        px = _mm3(xs * (1.0 + sc1x) + sh1x, w_in[l])
        pc = _mm3(cs * (1.0 + sc1c) + sh1c, w_in[l])
        pxa, pxb, pxc = jnp.split(px, col_split, axis=-1)
        pca, pcb, pcc = jnp.split(pc, col_split, axis=-1)

        ya_c, ya_x = _s5_mixer(pca, pxa, s5_lam_re[l], s5_lam_im[l], s5_log_step[l],
                               s5_b_re[l], s5_b_im[l], s5_c_re[l], s5_c_im[l],
                               s5_d[l], s5_glu_w[l], s5_glu_b[l])
        yb_c, yb_x = _hgrn2_mixer(pcb, pxb, lb_all[:, l], hgrn_norm_g[l])
        assert rows == x.shape[1] // GRID_W and pxc.shape[1] % RW_PRE_T == 0 and pcc.shape[1] % RW_PRE_T == 0
        yc_c, yc_x = _rwkv7_mixer_fused(pcc, pxc, rwkv_mu[l], rwkv_w0[l], rwkv_w_up[l],
                                        rwkv_a0[l], rwkv_a_up[l], rwkv_g_up[l], rwkv_k_k[l],
                                        rwkv_k_a[l], rwkv_r_k[l], rwkv_gn_g[l], rwkv_gn_b[l])

        u_b = peer_u[l].astype(_BF)
        vt_b = peer_v[l].T.astype(_BF)
        nx = xs.shape[0] * xs.shape[1]
        mix_x = _mm3(jnp.concatenate([ya_x, yb_x, yc_x], axis=-1), w_out[l])
        xs = _layer_norm(DEEPNORM_ALPHA * xs + g1x * mix_x, ln1_g[l], ln1_b[l])
        h2 = (xs * (1.0 + sc2x) + sh2x).reshape(nx, D_MODEL)
        if l < DEPTH - 1:
            mix_c = _mm3(jnp.concatenate([ya_c, yb_c, yc_c], axis=-1), w_out[l])
            cs = _layer_norm(DEEPNORM_ALPHA * cs + g1c * mix_c, ln1_g[l], ln1_b[l])
            h2 = jnp.concatenate([h2, (cs * (1.0 + sc2c) + sh2c).reshape(-1, D_MODEL)], axis=0)
        ffn = _peer_ffn_dense(h2, peer_wq[l], peer_keys[l], u_b, vt_b)
        xs = _layer_norm(DEEPNORM_ALPHA * xs + g2x * ffn[:nx].reshape(xs.shape), ln2_g[l], ln2_b[l])
        if l < DEPTH - 1:
            cs = _layer_norm(DEEPNORM_ALPHA * cs + g2c * ffn[nx:].reshape(cs.shape), ln2_g[l], ln2_b[l])
    return xs
```

```python
import functools
import math

import jax
import jax.numpy as jnp
from jax import lax
from jax.experimental import pallas as pl
from jax.experimental.pallas import tpu as pltpu

D_MODEL = 1024
BATCH = 4
SEQ = 4096
DEPTH = 2
GRID_W = 64
CTX_LEN = 256
D_S5 = D_MODEL // 4
D_HG = D_MODEL // 4
D_RW = D_MODEL // 2
D_MIX = D_S5 + D_HG + D_RW
S5_H = 16
S5_G = D_S5 // S5_H
S5_P = 64
HG_HEAD = 64
HG_HEADS = D_HG // HG_HEAD
HG_CHUNK = 16
RW_HEAD = 64
RW_HEADS = D_RW // RW_HEAD
RW_W_LORA = 64
RW_A_LORA = 64
RW_G_LORA = 128
RW_COLS = 3 * D_RW + RW_G_LORA + 2 * RW_W_LORA + 2 * RW_A_LORA
RW_GN_EPS = 64e-5
P_IN = D_S5 + 5 * D_HG + RW_COLS
PEER_HEADS = 8
PEER_NKEYS = 128
PEER_N = PEER_NKEYS * PEER_NKEYS
PEER_QDIM = 256
PEER_TOPK = 16
PEER_BLOCK = 128
LN_EPS = 1e-5
RMS_EPS = 1e-6
DEEPNORM_ALPHA = (2.0 * DEPTH) ** 0.25
DEEPNORM_BETA = (8.0 * DEPTH) ** -0.25


def _mm_kernel(x_ref, w_ref, o_ref):
    o_ref[...] = jnp.dot(x_ref[...].astype(jnp.bfloat16), w_ref[...],
                         preferred_element_type=jnp.float32)


def _pick_tile(n, cands):
    for c in cands:
        if n % c == 0:
            return c
    return n


def _matmul(x, w):
    m, k = x.shape
    n = w.shape[1]
    tm = _pick_tile(m, (512, 256, 128, 8))
    tn = _pick_tile(n, (1152, 1024, 512, 256, 128))
    return pl.pallas_call(
        _mm_kernel,
        grid=(m // tm, n // tn),
        in_specs=[pl.BlockSpec((tm, k), lambda i, j: (i, 0)),
                  pl.BlockSpec((k, tn), lambda i, j: (0, j))],
        out_specs=pl.BlockSpec((tm, tn), lambda i, j: (i, j)),
        out_shape=jax.ShapeDtypeStruct((m, n), jnp.float32),
        compiler_params=pltpu.CompilerParams(
            dimension_semantics=("parallel", "parallel"),
            vmem_limit_bytes=48 * 1024 * 1024),
    )(x, w.astype(jnp.bfloat16))


def _mm3(x, w):
    b, l, k = x.shape
    return _matmul(x.reshape(b * l, k), w).reshape(b, l, w.shape[1])


_BF = jnp.bfloat16
_NN = (((1,), (0,)), ((), ()))
_NT = (((1,), (1,)), ((), ()))
_TN = (((0,), (0,)), ((), ()))


def _dot(a, b, dims=_NN):
    return lax.dot_general(a.astype(_BF), b.astype(_BF), dims,
                           preferred_element_type=jnp.float32)


def _split_bf16(a):
    hi = a.astype(_BF)
    return hi, (a - hi.astype(jnp.float32)).astype(_BF)


def _dot3(a, b, dims=_NN):
    a_hi, a_lo = _split_bf16(a)
    b_hi, b_lo = _split_bf16(b)
    d = functools.partial(lax.dot_general, dimension_numbers=dims,
                          preferred_element_type=jnp.float32)
    return d(a_hi, b_hi) + (d(a_hi, b_lo) + d(a_lo, b_hi))


RW_CHUNK = 64


def _rwkv_scan_kernel(r_ref, lw_ref, k_ref, v_ref, kk_ref, a_ref, y_ref, st_ref, *, heads, chunk, reverse):
    @pl.when(pl.program_id(1) == 0)
    def _():
        st_ref[...] = jnp.zeros_like(st_ref)

    row = lax.broadcasted_iota(jnp.int32, (chunk, chunk), 0)
    col = lax.broadcasted_iota(jnp.int32, (chunk, chunk), 1)
    strict = row < col if reverse else row > col
    incl = row <= col if reverse else row >= col
    end = 0 if reverse else chunk - 1
    tri = incl.astype(jnp.float32)
    eye = (row == col).astype(jnp.float32)
    blk8 = row // 8 == col // 8
    merge_masks = []
    size = 8
    while size < chunk:
        merge_masks.append((row // (2 * size) == col // (2 * size)) & (row // size != col // size))
        size *= 2
    hs = range(heads)
    lw = [lw_ref[h] for h in hs]
    cum = [_dot3(tri, lw[h]) for h in hs]
    p = [jnp.exp(cum[h]) for h in hs]
    p_inv = [jnp.exp(-cum[h]) for h in hs]
    p_end = [p[h][end:end + 1, :] for h in hs]
    kk = [kk_ref[h] for h in hs]
    bh = [kk[h] * a_ref[h] * p_inv[h] for h in hs]
    kh = [k_ref[h] * p_inv[h] for h in hs]
    v = [v_ref[h] for h in hs]
    s0 = [st_ref[h] for h in hs]
    x1 = [jnp.concatenate([-kk[h] * jnp.exp(cum[h] - lw[h]), r_ref[h] * p[h]], axis=0) for h in hs]
    g1 = [_dot(x1[h], bh[h], _NT) for h in hs]
    g2 = [_dot(x1[h], kh[h], _NT) for h in hs]
    hm = [_dot(x1[h], s0[h], _NT) for h in hs]
    ab = [jnp.where(strict, g1[h][:chunk], 0.0) for h in hs]
    d1 = [jnp.where(blk8, ab[h], 0.0) for h in hs]
    d2 = [_dot3(d1[h], d1[h]) for h in hs]
    inv = [eye + d1[h] for h in hs]
    inv = [inv[h] + _dot3(inv[h], d2[h]) for h in hs]
    d4 = [_dot3(d2[h], d2[h]) for h in hs]
    inv = [inv[h] + _dot3(inv[h], d4[h]) for h in hs]
    for m in merge_masks:
        li = [_dot(jnp.where(m, ab[h], 0.0), inv[h]) for h in hs]
        inv = [inv[h] + _dot(inv[h], li[h]) for h in hs]
    rhs = [hm[h][:chunk] + _dot(jnp.where(strict, g2[h][:chunk], 0.0), v[h]) for h in hs]
    e = [_dot(inv[h], rhs[h]) for h in hs]
    for h in hs:
        y_ref[h] = (hm[h][chunk:] + _dot(jnp.where(incl, g1[h][chunk:], 0.0), e[h])
                    + _dot(jnp.where(incl, g2[h][chunk:], 0.0), v[h]))
    for h in hs:
        ev = jnp.concatenate([e[h], v[h]], axis=0)
        x2 = jnp.concatenate([bh[h] * p_end[h], kh[h] * p_end[h]], axis=0)
        st_ref[h] = s0[h] * p_end[h] + _dot(ev, x2, _TN)


def _scan_block_order(n_ctx, n_all, reverse):
    if not reverse:
        return lambda c: c
    return lambda c: jnp.where(c < n_ctx, n_ctx - 1 - c, n_all - 1 - c + n_ctx)


def _rwkv_scan(r, lw, k, v, kk, a, *, heads, ctx_len=0, reverse=False, interpret=False):
    gh, l, n = r.shape
    chunk = RW_CHUNK
    order = _scan_block_order(ctx_len // chunk, l // chunk, reverse)
    spec = pl.BlockSpec((heads, chunk, n), lambda g, c: (g, order(c), 0))
    return pl.pallas_call(
        functools.partial(_rwkv_scan_kernel, heads=heads, chunk=chunk, reverse=reverse),
        grid=(gh // heads, l // chunk),
        in_specs=[spec] * 6,
        out_specs=spec,
        out_shape=jax.ShapeDtypeStruct((gh, l, n), jnp.float32),
        scratch_shapes=[pltpu.VMEM((heads, n, n), jnp.float32)],
        compiler_params=pltpu.CompilerParams(dimension_semantics=("parallel", "arbitrary")),
        interpret=interpret,
        name="rwkv_scan",
    )(r, lw, k, v, kk, a)


PEER_TN = 512
PEER_EBLK = 1024
_NEG = -3.0e38
_LOG2E = 1.4426950408889634


def _top_rows(s, n):
    vals = []
    for _ in range(n):
        m = jnp.max(s, axis=0, keepdims=True)
        vals.append(m)
        s = jnp.where(s == m, _NEG, s)
    return vals


def _peer_route_kernel(h_ref, wq_ref, keys_ref, xb_ref, s2_ref, tau_ref, l1_ref):
    hb = h_ref[...].astype(_BF)
    xb_ref[...] = hb
    q = jnp.dot(hb, wq_ref[...], preferred_element_type=jnp.float32)
    half = PEER_QDIM // 2
    for h in range(PEER_HEADS):
        s1 = _dot(keys_ref[2 * h], q[:, (2 * h) * half:(2 * h + 1) * half], _NT)
        s2 = _dot(keys_ref[2 * h + 1], q[:, (2 * h + 1) * half:(2 * h + 2) * half], _NT)
        t1 = _top_rows(s1, PEER_TOPK + 1)
        t2 = _top_rows(s2, PEER_TOPK + 1)
        cand = [t1[j1] + t2[j2] for j1 in range(PEER_TOPK + 1) for j2 in range(PEER_TOPK + 1)
                if (j1 + 1) * (j2 + 1) <= PEER_TOPK + 1]
        cand += [jnp.full_like(cand[0], _NEG)] * (-len(cand) % 8)
        top = _top_rows(jnp.concatenate(cand, axis=0), PEER_TOPK + 1)
        theta = 0.5 * (top[PEER_TOPK - 1] + top[PEER_TOPK])
        z = top[0] * 0.0
        for c in top[:PEER_TOPK]:
            z = z + jnp.exp(c - top[0])
        s2_ref[h] = (s2 - t2[0]) * _LOG2E
        tau_ref[h] = (theta - s1 - t2[0]) * _LOG2E
        l1_ref[h] = (s1 - t1[0] - jnp.log(z)) * _LOG2E


def _peer_route(hf, wq, keys, interpret=False):
    t, d = hf.shape
    tn = PEER_TN
    nq = PEER_HEADS * PEER_QDIM
    aux = jax.ShapeDtypeStruct((PEER_HEADS, PEER_NKEYS, t), jnp.float32)
    aux_spec = pl.BlockSpec((PEER_HEADS, PEER_NKEYS, tn), lambda i: (0, 0, i))
    return pl.pallas_call(
        _peer_route_kernel,
        grid=(t // tn,),
        in_specs=[pl.BlockSpec((tn, d), lambda i: (i, 0)),
                  pl.BlockSpec((d, nq), lambda i: (0, 0)),
                  pl.BlockSpec((PEER_HEADS * 2, PEER_NKEYS, PEER_QDIM // 2), lambda i: (0, 0, 0))],
        out_specs=[pl.BlockSpec((tn, d), lambda i: (i, 0))] + [aux_spec] * 3,
        out_shape=[jax.ShapeDtypeStruct((t, d), _BF)] + [aux] * 3,
        compiler_params=pltpu.CompilerParams(dimension_semantics=("parallel",),
                                             vmem_limit_bytes=56 * 1024 * 1024),
        interpret=interpret,
        name="peer_route",
    )(hf, wq.astype(_BF), keys.reshape(PEER_HEADS * 2, PEER_NKEYS, PEER_QDIM // 2))


def _gelu_tanh(z):
    return 0.5 * z * (1.0 + jnp.tanh(math.sqrt(2.0 / math.pi) * (z + 0.044715 * (z * z * z))))


def _peer_expert_kernel(x_ref, u_ref, vt_ref, s2_ref, tau_ref, l1_ref, o_ref, acc_ref, at_ref):
    j = pl.program_id(1)

    @pl.when(j == 0)
    def _():
        acc_ref[...] = jnp.zeros_like(acc_ref)

    nk = PEER_NKEYS
    n_slab = PEER_EBLK // nk
    rows = pl.ds(pl.multiple_of(j * n_slab, n_slab), n_slab)
    zt = lax.dot_general(u_ref[...], x_ref[...], _NT, preferred_element_type=jnp.float32)
    for s in range(n_slab):
        for lt in range(PEER_TN // 128):
            lanes = pl.ds(lt * 128, 128)
            g = jnp.zeros((nk, 128), jnp.float32)
            for h in range(PEER_HEADS):
                tau = tau_ref[h, rows, lanes][s:s + 1]
                l1 = l1_ref[h, rows, lanes][s:s + 1]
                s2 = s2_ref[h, :, lanes]
                g = g + jnp.where(s2 >= tau, jnp.exp2(s2 + l1), 0.0)
            z = zt[s * nk:(s + 1) * nk, lt * 128:(lt + 1) * 128]
            at_ref[s * nk:(s + 1) * nk, lt * 128:(lt + 1) * 128] = (g * _gelu_tanh(z)).astype(_BF)
    acc_ref[...] += jnp.dot(vt_ref[...], at_ref[...], preferred_element_type=jnp.float32)

    @pl.when(j == pl.num_programs(1) - 1)
    def _():
        o_ref[...] = acc_ref[...].T


def _peer_expert(xb, u_b, vt_b, aux, interpret=False):
    t, d = xb.shape
    tn, eb = PEER_TN, PEER_EBLK
    aux_spec = pl.BlockSpec((PEER_HEADS, PEER_NKEYS, tn), lambda i, j: (0, 0, i))
    return pl.pallas_call(
        _peer_expert_kernel,
        grid=(t // tn, PEER_N // eb),
        in_specs=[pl.BlockSpec((tn, d), lambda i, j: (i, 0)),
                  pl.BlockSpec((eb, d), lambda i, j: (j, 0)),
                  pl.BlockSpec((d, eb), lambda i, j: (0, j))] + [aux_spec] * 3,
        out_specs=pl.BlockSpec((tn, d), lambda i, j: (i, 0)),
        out_shape=jax.ShapeDtypeStruct((t, d), jnp.float32),
        scratch_shapes=[pltpu.VMEM((d, tn), jnp.float32), pltpu.VMEM((eb, tn), _BF)],
        compiler_params=pltpu.CompilerParams(dimension_semantics=("parallel", "arbitrary"),
                                             vmem_limit_bytes=56 * 1024 * 1024),
        interpret=interpret,
        name="peer_expert",
    )(xb, u_b, vt_b, *aux)


def _peer_ffn_dense(hf, wq, keys, u_b, vt_b, interpret=False):
    xb, *aux = _peer_route(hf, wq, keys, interpret=interpret)
    return _peer_expert(xb, u_b, vt_b, aux, interpret=interpret)


S5_STEPS = 128
_S5_ROWS = 8
_S5_STATE = S5_G * S5_P


def _s5_scan_kernel(u_ref, wb_ref, lam_ref, wc_ref, y_ref, bu_ref, xs_ref, st_ref):
    @pl.when(pl.program_id(0) == 0)
    def _():
        st_ref[...] = jnp.zeros_like(st_ref)

    n = _S5_STATE
    bu_ref[...] = jnp.dot(u_ref[...].astype(_BF), wb_ref[...], preferred_element_type=jnp.float32)
    lr = lam_ref[0]
    li = lam_ref[1]

    def step(i, carry):
        xr, xi = carry
        rows = pl.ds(pl.multiple_of(i * _S5_ROWS, _S5_ROWS), _S5_ROWS)
        nr = lr * xr - li * xi + bu_ref[rows, :n]
        ni = lr * xi + li * xr + bu_ref[rows, n:]
        xs_ref[rows, :n] = nr
        xs_ref[rows, n:] = ni
        return nr, ni

    xr, xi = lax.fori_loop(0, S5_STEPS, step, (st_ref[0], st_ref[1]), unroll=4)
    st_ref[0] = xr
    st_ref[1] = xi
    y2 = jnp.dot(xs_ref[...].astype(_BF), wc_ref[...], preferred_element_type=jnp.float32)
    fwd = (lax.broadcasted_iota(jnp.int32, (S5_STEPS * _S5_ROWS, 1), 0) % _S5_ROWS) < (_S5_ROWS // 2)
    y_ref[...] = jnp.where(fwd, y2[:, :D_S5], y2[:, D_S5:])


def _s5_scan(u2, wb, lam, wc, interpret=False):
    rows = u2.shape[0]
    blk = S5_STEPS * _S5_ROWS
    n = _S5_STATE
    return pl.pallas_call(
        _s5_scan_kernel,
        grid=(rows // blk,),
        in_specs=[pl.BlockSpec((blk, 2 * D_S5), lambda i: (i, 0)),
                  pl.BlockSpec((2 * D_S5, 2 * n), lambda i: (0, 0)),
                  pl.BlockSpec((2, _S5_ROWS, n), lambda i: (0, 0, 0)),
                  pl.BlockSpec((2 * n, 2 * D_S5), lambda i: (0, 0))],
        out_specs=pl.BlockSpec((blk, D_S5), lambda i: (i, 0)),
        out_shape=jax.ShapeDtypeStruct((rows, D_S5), jnp.float32),
        scratch_shapes=[pltpu.VMEM((blk, 2 * n), jnp.float32), pltpu.VMEM((blk, 2 * n), jnp.float32),
                        pltpu.VMEM((2, _S5_ROWS, n), jnp.float32)],
        compiler_params=pltpu.CompilerParams(dimension_semantics=("arbitrary",),
                                             vmem_limit_bytes=48 * 1024 * 1024),
        interpret=interpret,
        name="s5_scan",
    )(u2, wb, lam, wc)


def _s5_weights(disc, c_re, c_im):
    eye = jnp.eye(S5_G, dtype=jnp.float32)

    def bdiag_in(bb):
        return jnp.einsum('gph,gk->ghkp', bb, eye).reshape(D_S5, _S5_STATE)

    def bdiag_out(cc):
        return jnp.einsum('ghp,gk->gpkh', cc, eye).reshape(_S5_STATE, D_S5)

    wb = jnp.stack([jnp.concatenate([bdiag_in(disc[d][2]), bdiag_in(disc[d][3])], axis=1) for d in range(2)])
    wc = jnp.stack([jnp.concatenate([bdiag_out(c_re[d]), -bdiag_out(c_im[d])], axis=0) for d in range(2)])
    half = _S5_ROWS // 2
    lam = jnp.stack([
        jnp.concatenate([jnp.broadcast_to(disc[d][i].reshape(1, _S5_STATE), (half, _S5_STATE))
                         for d in range(2)], axis=0) for i in range(2)])
    return wb.astype(_BF), lam, wc.astype(_BF)


def _s5_dir_kernel(uf_ref, ub_ref, wb_ref, lam_ref, wc_ref, yf_ref, yb_ref, buf_ref, bub_ref, xf_ref, xb_ref, st_ref):
    @pl.when(pl.program_id(0) == 0)
    def _():
        st_ref[...] = jnp.zeros_like(st_ref)

    n = _S5_STATE
    buf_ref[...] = jnp.dot(uf_ref[...].astype(_BF), wb_ref[0], preferred_element_type=jnp.float32)
    bub_ref[...] = jnp.dot(ub_ref[...].astype(_BF), wb_ref[1], preferred_element_type=jnp.float32)
    lr = lam_ref[0]
    li = lam_ref[1]
    fwd = lax.broadcasted_iota(jnp.int32, (_S5_ROWS, n), 0) < (_S5_ROWS // 2)

    def step(i, carry):
        xr, xi = carry
        rf = pl.ds(pl.multiple_of(i * _S5_ROWS, _S5_ROWS), _S5_ROWS)
        rb = pl.ds(pl.multiple_of((S5_STEPS - 1 - i) * _S5_ROWS, _S5_ROWS), _S5_ROWS)
        nr = lr * xr - li * xi + jnp.where(fwd, buf_ref[rf, :n], bub_ref[rb, :n])
        ni = lr * xi + li * xr + jnp.where(fwd, buf_ref[rf, n:], bub_ref[rb, n:])
        xf_ref[rf, :n] = nr
        xf_ref[rf, n:] = ni
        xb_ref[rb, :n] = nr
        xb_ref[rb, n:] = ni
        return nr, ni

    xr, xi = lax.fori_loop(0, S5_STEPS, step, (st_ref[0], st_ref[1]), unroll=4)
    st_ref[0] = xr
    st_ref[1] = xi
    yf_ref[...] = jnp.dot(xf_ref[...].astype(_BF), wc_ref[0], preferred_element_type=jnp.float32)
    yb_ref[...] = jnp.dot(xb_ref[...].astype(_BF), wc_ref[1], preferred_element_type=jnp.float32)


def _s5_dir_scan(u8, wb, lam, wc, *, ctx_len):
    rows = u8.shape[0]
    blk = S5_STEPS * _S5_ROWS
    n = _S5_STATE
    order = _scan_block_order(ctx_len // S5_STEPS, rows // blk, True)
    spec_f = pl.BlockSpec((blk, D_S5), lambda i: (i, 0))
    spec_b = pl.BlockSpec((blk, D_S5), lambda i: (order(i), 0))
    out = jax.ShapeDtypeStruct((rows, D_S5), jnp.float32)
    big = pltpu.VMEM((blk, 2 * n), jnp.float32)
    return pl.pallas_call(
        _s5_dir_kernel,
        grid=(rows // blk,),
        in_specs=[spec_f, spec_b,
                  pl.BlockSpec((2, D_S5, 2 * n), lambda i: (0, 0, 0)),
                  pl.BlockSpec((2, _S5_ROWS, n), lambda i: (0, 0, 0)),
                  pl.BlockSpec((2, 2 * n, D_S5), lambda i: (0, 0, 0))],
        out_specs=[spec_f, spec_b],
        out_shape=[out, out],
        scratch_shapes=[big, big, big, big, pltpu.VMEM((2, _S5_ROWS, n), jnp.float32)],
        compiler_params=pltpu.CompilerParams(dimension_semantics=("arbitrary",),
                                             vmem_limit_bytes=56 * 1024 * 1024),
        name="s5_scan",
    )(u8, u8, wb, lam, wc)


HG_BLOCK = 128
_HG_PAIR = 2 * HG_HEAD


def _hgrn_scan_kernel(q_ref, k_ref, v_ref, lf_ref, o_ref, st_ref):
    @pl.when(pl.program_id(1) == 0)
    def _():
        st_ref[...] = jnp.zeros_like(st_ref)

    ch = HG_CHUNK
    r = lax.broadcasted_iota(jnp.int32, (HG_BLOCK, HG_BLOCK), 0)
    c = lax.broadcasted_iota(jnp.int32, (HG_BLOCK, HG_BLOCK), 1)
    tri = ((r // ch == c // ch) & (r >= c)).astype(jnp.float32)
    same_head = (lax.broadcasted_iota(jnp.int32, (_HG_PAIR, _HG_PAIR), 0) // HG_HEAD
                 == lax.broadcasted_iota(jnp.int32, (_HG_PAIR, _HG_PAIR), 1) // HG_HEAD)
    ones_blk = same_head.astype(_BF)
    tcol = lax.broadcasted_iota(jnp.int32, (ch, 1), 0)
    cum_all = _dot3(tri, lf_ref[0])
    for p in range(D_HG // _HG_PAIR):
        lanes = slice(p * _HG_PAIR, (p + 1) * _HG_PAIR)
        state = st_ref[p]
        for ci in range(HG_BLOCK // ch):
            rows = slice(ci * ch, (ci + 1) * ch)
            q = q_ref[0, rows, lanes]
            k = k_ref[0, rows, lanes]
            v = v_ref[0, rows, lanes]
            cum = cum_all[rows, lanes]
            last = cum[ch - 1:ch]
            o = _dot(q * jnp.exp(cum), state, _NT)
            w = [q * jnp.exp(jnp.minimum(cum - cum[s:s + 1], 0.0)) * k[s:s + 1] for s in range(ch)]
            rel = jnp.dot(jnp.concatenate(w, axis=0).astype(_BF), ones_blk,
                          preferred_element_type=jnp.float32)
            for s in range(ch):
                o = o + jnp.where(tcol >= s, rel[s * ch:(s + 1) * ch], 0.0) * v[s:s + 1]
            o_ref[0, rows, lanes] = o
            kv = _dot(v, k * jnp.exp(last - cum), _TN)
            state = state * jnp.exp(last) + jnp.where(same_head, kv, 0.0)
        st_ref[p] = state


def _hgrn_scan(q, k, v, lf):
    g, l, d = q.shape
    spec = pl.BlockSpec((1, HG_BLOCK, d), lambda i, j: (i, j, 0))
    return pl.pallas_call(
        _hgrn_scan_kernel,
        grid=(g, l // HG_BLOCK),
        in_specs=[spec] * 4,
        out_specs=spec,
        out_shape=jax.ShapeDtypeStruct((g, l, d), jnp.float32),
        scratch_shapes=[pltpu.VMEM((d // _HG_PAIR, _HG_PAIR, _HG_PAIR), jnp.float32)],
        compiler_params=pltpu.CompilerParams(dimension_semantics=("parallel", "arbitrary")),
        name="hgrn_scan",
    )(q, k, v, lf)


def _hgrn_dir_kernel(q_ref, v_ref, f_ref, lb_ref, o_ref, st_ref, *, reverse):
    @pl.when(pl.program_id(1) == 0)
    def _():
        st_ref[...] = jnp.zeros_like(st_ref)

    ch = HG_CHUNK
    n_ch = HG_BLOCK // ch
    r = lax.broadcasted_iota(jnp.int32, (HG_BLOCK, HG_BLOCK), 0)
    c = lax.broadcasted_iota(jnp.int32, (HG_BLOCK, HG_BLOCK), 1)
    tri = ((r // ch == c // ch) & ((r <= c) if reverse else (r >= c))).astype(jnp.float32)
    same_head = (lax.broadcasted_iota(jnp.int32, (_HG_PAIR, _HG_PAIR), 0) // HG_HEAD
                 == lax.broadcasted_iota(jnp.int32, (_HG_PAIR, _HG_PAIR), 1) // HG_HEAD)
    ones_blk = same_head.astype(_BF)
    tcol = lax.broadcasted_iota(jnp.int32, (ch, 1), 0)
    end = 0 if reverse else ch - 1
    q_raw = q_ref[0]
    q_all = q_raw * _sigmoid(q_raw)
    forget = lb_ref[...] + (1.0 - lb_ref[...]) * _sigmoid(f_ref[0])
    k_all = 1.0 - forget
    cum_all = _dot3(tri, jnp.log(forget))
    n_pair = D_HG // _HG_PAIR
    intra, kv, dec, qd = {}, {}, {}, {}
    for p in range(n_pair):
        lanes = slice(p * _HG_PAIR, (p + 1) * _HG_PAIR)
        for ci in range(n_ch):
            rows = slice(ci * ch, (ci + 1) * ch)
            q = q_all[rows, lanes]
            k = k_all[rows, lanes]
            v = v_ref[0, rows, lanes]
            cum = cum_all[rows, lanes]
            last = cum[end:end + 1]
            w = [q * jnp.exp(jnp.minimum(cum - cum[s:s + 1], 0.0)) * k[s:s + 1] for s in range(ch)]
            rel = jnp.dot(jnp.concatenate(w, axis=0).astype(_BF), ones_blk,
                          preferred_element_type=jnp.float32)
            o = jnp.zeros((ch, _HG_PAIR), jnp.float32)
            for s in range(ch):
                seen = (tcol <= s) if reverse else (tcol >= s)
                o = o + jnp.where(seen, rel[s * ch:(s + 1) * ch], 0.0) * v[s:s + 1]
            intra[p, ci] = o
            kv[p, ci] = jnp.where(same_head, _dot(v, k * jnp.exp(last - cum), _TN), 0.0)
            dec[p, ci] = jnp.exp(last)
            qd[p, ci] = q * jnp.exp(cum)
    for p in range(n_pair):
        lanes = slice(p * _HG_PAIR, (p + 1) * _HG_PAIR)
        state = st_ref[p]
        for ci in (range(n_ch - 1, -1, -1) if reverse else range(n_ch)):
            o_ref[0, ci * ch:(ci + 1) * ch, lanes] = intra[p, ci] + _dot(qd[p, ci], state, _NT)
            state = state * dec[p, ci] + kv[p, ci]
        st_ref[p] = state


def _hgrn_dir_scan(p, lb, *, ctx_len, reverse, col0=0):
    g, l, _ = p.shape
    d = D_HG
    order = _scan_block_order(ctx_len // HG_BLOCK, l // HG_BLOCK, reverse)
    col = lambda c: pl.BlockSpec((1, HG_BLOCK, d), lambda i, j: (i, order(j), col0 + c))
    dr = 1 if reverse else 0
    return pl.pallas_call(
        functools.partial(_hgrn_dir_kernel, reverse=reverse),
        grid=(g, l // HG_BLOCK),
        in_specs=[col(0), col(1), col(2 + dr), pl.BlockSpec((1, d), lambda i, j: (0, 0))],
        out_specs=pl.BlockSpec((1, HG_BLOCK, d), lambda i, j: (i, order(j), 0)),
        out_shape=jax.ShapeDtypeStruct((g, l, d), jnp.float32),
        scratch_shapes=[pltpu.VMEM((d // _HG_PAIR, _HG_PAIR, _HG_PAIR), jnp.float32)],
        compiler_params=pltpu.CompilerParams(dimension_semantics=("parallel", "arbitrary")),
        name="hgrn_scan",
    )(p, p, p, lb[dr:dr + 1])


def _hgrn_post_kernel(of_ref, ob_ref, g_ref, ng_ref, ones_ref, y_ref):
    o = of_ref[0] + ob_ref[0]
    ms = jnp.dot((o * o).astype(_BF), ones_ref[...], preferred_element_type=jnp.float32) * (1.0 / HG_HEAD)
    g = g_ref[0]
    y_ref[0] = o * lax.rsqrt(ms + RMS_EPS) * ng_ref[...] * (g * _sigmoid(g))


def _hgrn_post(o_f, o_b, p, norm_g, col0=0):
    nb, l, d = o_f.shape
    spec = pl.BlockSpec((1, HG_BLOCK, d), lambda i, j: (i, j, 0))
    return pl.pallas_call(
        _hgrn_post_kernel,
        grid=(nb, l // HG_BLOCK),
        in_specs=[spec, spec, pl.BlockSpec((1, HG_BLOCK, d), lambda i, j: (i, j, col0 + 4)),
                  pl.BlockSpec((1, d), lambda i, j: (0, 0)), pl.BlockSpec((d, d), lambda i, j: (0, 0))],
        out_specs=spec,
        out_shape=jax.ShapeDtypeStruct((nb, l, d), jnp.float32),
        compiler_params=pltpu.CompilerParams(dimension_semantics=("parallel", "parallel")),
        name="hgrn_post",
    )(o_f, o_b, p, norm_g.reshape(1, d), _head_ones(d, HG_HEAD))


_RW_PAIR = 2 * RW_HEAD
RW_PRE_T = 128
RW_BATCH_BLOCK = 4


def _rwkv_pair_kernel(r_ref, lw_ref, k_ref, v_ref, kk_ref, a_ref, y_ref, st_ref, *, chunk, reverse):
    @pl.when(pl.program_id(1) == 0)
    def _():
        st_ref[...] = jnp.zeros_like(st_ref)

    row = lax.broadcasted_iota(jnp.int32, (chunk, chunk), 0)
    col = lax.broadcasted_iota(jnp.int32, (chunk, chunk), 1)
    strict = row < col if reverse else row > col
    incl = row <= col if reverse else row >= col
    end = 0 if reverse else chunk - 1
    tri = incl.astype(jnp.float32)
    eye = (row == col).astype(jnp.float32)
    blk8 = row // 8 == col // 8
    merge_masks = []
    size = 8
    while size < chunk:
        merge_masks.append((row // (2 * size) == col // (2 * size)) & (row // size != col // size))
        size *= 2
    lane_head = lax.broadcasted_iota(jnp.int32, (1, _RW_PAIR), 1) // RW_HEAD
    head0 = lane_head == 0
    same_head = (lax.broadcasted_iota(jnp.int32, (_RW_PAIR, _RW_PAIR), 0) // RW_HEAD
                 == lax.broadcasted_iota(jnp.int32, (_RW_PAIR, _RW_PAIR), 1) // RW_HEAD)
    n_pair = D_RW // _RW_PAIR
    ps = range(r_ref.shape[0] * n_pair)
    ph = [(p, h) for p in ps for h in range(2)]
    bi = [p // n_pair for p in ps]
    sl = [slice((p % n_pair) * _RW_PAIR, (p % n_pair + 1) * _RW_PAIR) for p in ps]
    lw = [lw_ref[bi[p], :, sl[p]] for p in ps]
    cum = [_dot3(tri, lw[p]) for p in ps]
    pw = [jnp.exp(cum[p]) for p in ps]
    p_inv = [jnp.exp(-cum[p]) for p in ps]
    p_end = [pw[p][end:end + 1, :] for p in ps]
    kk = [kk_ref[bi[p], :, sl[p]] for p in ps]
    bh = [kk[p] * a_ref[bi[p], :, sl[p]] * p_inv[p] for p in ps]
    kh = [k_ref[bi[p], :, sl[p]] * p_inv[p] for p in ps]
    v = [v_ref[bi[p], :, sl[p]] for p in ps]
    s0 = [st_ref[p] for p in ps]
    x1 = [jnp.concatenate([-kk[p] * jnp.exp(cum[p] - lw[p]), r_ref[bi[p], :, sl[p]] * pw[p]], axis=0) for p in ps]
    x1h = {(p, h): jnp.where(lane_head == h, x1[p], 0.0) for p, h in ph}
    g1 = {q: _dot(x1h[q], bh[q[0]], _NT) for q in ph}
    g2 = {q: _dot(x1h[q], kh[q[0]], _NT) for q in ph}
    hm = [_dot(x1[p], s0[p], _NT) for p in ps]
    ab = {q: jnp.where(strict, g1[q][:chunk], 0.0) for q in ph}
    d1 = {q: jnp.where(blk8, ab[q], 0.0) for q in ph}
    d2 = {q: _dot3(d1[q], d1[q]) for q in ph}
    inv = {q: eye + d1[q] for q in ph}
    inv = {q: inv[q] + _dot3(inv[q], d2[q]) for q in ph}
    d4 = {q: _dot3(d2[q], d2[q]) for q in ph}
    inv = {q: inv[q] + _dot3(inv[q], d4[q]) for q in ph}
    for m in merge_masks:
        li = {q: _dot(jnp.where(m, ab[q], 0.0), inv[q]) for q in ph}
        inv = {q: inv[q] + _dot(inv[q], li[q]) for q in ph}
    akv = {q: _dot(jnp.where(strict, g2[q][:chunk], 0.0), v[q[0]]) for q in ph}
    rhs = [hm[p][:chunk] + jnp.where(head0, akv[p, 0], akv[p, 1]) for p in ps]
    eh = {q: _dot(inv[q], rhs[q[0]]) for q in ph}
    e = [jnp.where(head0, eh[p, 0], eh[p, 1]) for p in ps]
    yh = {q: _dot(jnp.where(incl, g1[q][chunk:], 0.0), e[q[0]])
          + _dot(jnp.where(incl, g2[q][chunk:], 0.0), v[q[0]]) for q in ph}
    for p in ps:
        y_ref[bi[p], :, sl[p]] = hm[p][chunk:] + jnp.where(head0, yh[p, 0], yh[p, 1])
    for p in ps:
        ev = jnp.concatenate([e[p], v[p]], axis=0)
        x2 = jnp.concatenate([bh[p] * p_end[p], kh[p] * p_end[p]], axis=0)
        st_ref[p] = s0[p] * p_end[p] + jnp.where(same_head, _dot(ev, x2, _TN), 0.0)


def _rwkv_pair_scan(r, lw, k, v, kk, a, *, ctx_len, reverse):
    nb, l, d = r.shape
    chunk = RW_CHUNK
    order = _scan_block_order(ctx_len // chunk, l // chunk, reverse)
    bb = RW_BATCH_BLOCK if nb % RW_BATCH_BLOCK == 0 else 1
    spec = pl.BlockSpec((bb, chunk, d), lambda b, c: (b, order(c), 0))
    return pl.pallas_call(
        functools.partial(_rwkv_pair_kernel, chunk=chunk, reverse=reverse),
        grid=(nb // bb, l // chunk),
        in_specs=[spec] * 6,
        out_specs=spec,
        out_shape=jax.ShapeDtypeStruct((nb, l, d), jnp.float32),
        scratch_shapes=[pltpu.VMEM((bb * d // _RW_PAIR, _RW_PAIR, _RW_PAIR), jnp.float32)],
        compiler_params=pltpu.CompilerParams(dimension_semantics=("parallel", "arbitrary")),
        name="rwkv_scan",
    )(r, lw, k, v, kk, a)


def _softplus(z):
    return jnp.maximum(z, 0.0) + jnp.log(1.0 + jnp.exp(-jnp.abs(z)))


def _sigmoid(z):
    return 1.0 / (1.0 + jnp.exp(-z))


def _rwkv_pre_kernel(prev_ref, cur_ref, next_ref, mu_ref, vec_ref, gup_ref, wup_ref, aup_ref, ones_ref,
                     r_ref, v_ref, kk_ref, g_ref, bonus_ref, lw0_ref, k0_ref, a0_ref, lw1_ref, k1_ref, a1_ref,
                     ext_ref, s_ref, *, ctx_len, seq_len):
    t_rows, w = RW_PRE_T, GRID_W
    ext_ref[0:w] = prev_ref[0]
    ext_ref[w:w + t_rows] = cur_ref[0]
    ext_ref[w + t_rows:w + t_rows + w] = next_ref[0]
    t = pl.program_id(1) * t_rows + lax.broadcasted_iota(jnp.int32, (t_rows, 128), 0)
    lane4 = lax.broadcasted_iota(jnp.int32, (t_rows, 128), 1) % 4
    is_ctx = t < ctx_len
    tx = t - ctx_len
    col = tx % w
    is_lat = jnp.logical_not(is_ctx)
    use_p1 = (is_ctx & (lane4 % 2 == 0) & (t > 0)) | (is_lat & (lane4 == 0) & (col != 0))
    use_n1 = (is_ctx & (lane4 % 2 == 1) & (t < ctx_len - 1)) | (is_lat & (lane4 == 1) & (col != w - 1))
    use_p64 = is_lat & (lane4 == 2) & (tx >= w)
    use_n64 = is_lat & (lane4 == 3) & (tx < seq_len - w)
    for lt in range(RW_COLS // 128):
        lanes = slice(lt * 128, (lt + 1) * 128)
        p = ext_ref[w:w + t_rows, lanes]
        shifted = jnp.where(use_p1, ext_ref[w - 1:w - 1 + t_rows, lanes],
                            jnp.where(use_n1, ext_ref[w + 1:w + 1 + t_rows, lanes],
                                      jnp.where(use_p64, ext_ref[0:t_rows, lanes],
                                                jnp.where(use_n64, ext_ref[2 * w:2 * w + t_rows, lanes], 0.0))))
        s_ref[:, lanes] = p + mu_ref[:, lanes] * (shifted - p)
    d = D_RW
    r = s_ref[:, 0:d]
    k = s_ref[:, d:2 * d]
    v = s_ref[:, 2 * d:3 * d]
    o = 3 * d
    g_lo = s_ref[:, o:o + RW_G_LORA]
    wl = s_ref[:, o + RW_G_LORA:o + RW_G_LORA + 2 * RW_W_LORA]
    al = s_ref[:, o + RW_G_LORA + 2 * RW_W_LORA:o + RW_G_LORA + 2 * RW_W_LORA + 2 * RW_A_LORA]
    k_k, k_a, r_k = vec_ref[0:1], vec_ref[1:2], vec_ref[2:3]
    ones = ones_ref[...]
    r_ref[0] = r
    v_ref[0] = v
    g_ref[0] = jnp.dot(_sigmoid(g_lo).astype(_BF), gup_ref[...], preferred_element_type=jnp.float32)
    kkr = k * k_k
    ss = jnp.dot((kkr * kkr).astype(_BF), ones, preferred_element_type=jnp.float32)
    kk_ref[0] = kkr / jnp.maximum(jnp.sqrt(ss), 1e-12)
    w_pre = jnp.dot(jnp.tanh(wl).astype(_BF), wup_ref[...], preferred_element_type=jnp.float32)
    a_pre = jnp.dot(al.astype(_BF), aup_ref[...], preferred_element_type=jnp.float32)
    k_sum = 0.0
    for dr, (lw_o, k_o, a_o) in enumerate(((lw0_ref, k0_ref, a0_ref), (lw1_ref, k1_ref, a1_ref))):
        w_log = -_softplus(-(vec_ref[3 + dr:4 + dr] + w_pre[:, dr * d:(dr + 1) * d])) - 0.5
        a = _sigmoid(vec_ref[5 + dr:6 + dr] + a_pre[:, dr * d:(dr + 1) * d])
        k_d = k * (1.0 + (a - 1.0) * k_a)
        lw_o[0] = -jnp.exp(w_log)
        k_o[0] = k_d
        a_o[0] = a
        k_sum = k_sum + k_d
    bonus_ref[0] = jnp.dot((r * k_sum * r_k).astype(_BF), ones, preferred_element_type=jnp.float32)


def _head_ones(width, head):
    i = jnp.arange(width) // head
    return (i[:, None] == i[None, :]).astype(_BF)


def _rwkv_pre(p, mu, k_k, k_a, r_k, w0, a0, g_up, w_up, a_up, *, ctx_len):
    nb, l, cols = p.shape
    t_rows, w, d = RW_PRE_T, GRID_W, D_RW
    nblk = l // w
    per = t_rows // w
    zeros = jnp.zeros((RW_W_LORA, d), jnp.float32)
    wup = jnp.concatenate([jnp.concatenate([w_up[0], zeros], axis=1),
                           jnp.concatenate([zeros, w_up[1]], axis=1)], axis=0).astype(_BF)
    aup = jnp.concatenate([jnp.concatenate([a_up[0], zeros], axis=1),
                           jnp.concatenate([zeros, a_up[1]], axis=1)], axis=0).astype(_BF)
    vec = jnp.stack([k_k, k_a, r_k.reshape(d), w0[0], w0[1], a0[0], a0[1], jnp.zeros((d,), jnp.float32)])
    out = jax.ShapeDtypeStruct((nb, l, d), jnp.float32)
    out_spec = pl.BlockSpec((1, t_rows, d), lambda b, j: (b, j, 0))
    full = lambda shape: pl.BlockSpec(shape, lambda b, j: (0,) * len(shape))
    return pl.pallas_call(
        functools.partial(_rwkv_pre_kernel, ctx_len=ctx_len, seq_len=l - ctx_len),
        grid=(nb, l // t_rows),
        in_specs=[pl.BlockSpec((1, w, cols), lambda b, j: (b, jnp.maximum(j * per - 1, 0), 0)),
                  pl.BlockSpec((1, t_rows, cols), lambda b, j: (b, j, 0)),
                  pl.BlockSpec((1, w, cols), lambda b, j: (b, jnp.minimum(j * per + per, nblk - 1), 0)),
                  full((1, cols)), full((8, d)), full((RW_G_LORA, d)),
                  full((2 * RW_W_LORA, 2 * d)), full((2 * RW_A_LORA, 2 * d)), full((d, d))],
        out_specs=[out_spec] * 11,
        out_shape=[out] * 11,
        scratch_shapes=[pltpu.VMEM((t_rows + 2 * w, cols), jnp.float32), pltpu.VMEM((t_rows, cols), jnp.float32)],
        compiler_params=pltpu.CompilerParams(dimension_semantics=("parallel", "parallel"),
                                             vmem_limit_bytes=48 * 1024 * 1024),
        name="rwkv_pre",
    )(p, p, p, mu.reshape(1, cols), vec, g_up.astype(_BF), wup, aup, _head_ones(d, RW_HEAD))


def _rwkv_post_kernel(y0_ref, y1_ref, v_ref, g_ref, bonus_ref, gn_ref, ones_ref, o_ref):
    y = y0_ref[0] + y1_ref[0]
    ones = ones_ref[...]
    inv_n = 1.0 / RW_HEAD
    mu_y = jnp.dot(y.astype(_BF), ones, preferred_element_type=jnp.float32) * inv_n
    yc = y - mu_y
    var_y = jnp.dot((yc * yc).astype(_BF), ones, preferred_element_type=jnp.float32) * inv_n
    yn = yc * lax.rsqrt(var_y + RW_GN_EPS) * gn_ref[0:1] + gn_ref[1:2]
    o_ref[0] = (yn + bonus_ref[0] * v_ref[0]) * g_ref[0]


def _rwkv_post(y0, y1, v, g, bonus, gn_g, gn_b):
    nb, l, d = y0.shape
    spec = pl.BlockSpec((1, RW_PRE_T, d), lambda b, j: (b, j, 0))
    return pl.pallas_call(
        _rwkv_post_kernel,
        grid=(nb, l // RW_PRE_T),
        in_specs=[spec] * 5 + [pl.BlockSpec((2, d), lambda b, j: (0, 0)), pl.BlockSpec((d, d), lambda b, j: (0, 0))],
        out_specs=spec,
        out_shape=jax.ShapeDtypeStruct((nb, l, d), jnp.float32),
        compiler_params=pltpu.CompilerParams(dimension_semantics=("parallel", "parallel")),
        name="rwkv_post",
    )(y0, y1, v, g, bonus, jnp.stack([gn_g, gn_b]), _head_ones(d, RW_HEAD))


def _rwkv7_mixer_fused(p, lc, mu, w0, w_up, a0, a_up, g_up, k_k, k_a, r_k, gn_g, gn_b):
    r, v, kk, g, bonus, lw0, k0, a0_, lw1, k1, a1_ = _rwkv_pre(
        p, _f32(mu), _f32(k_k), _f32(k_a), _f32(r_k), _f32(w0), _f32(a0), _f32(g_up), _f32(w_up), _f32(a_up),
        ctx_len=lc)
    y0 = _rwkv_pair_scan(r, lw0, k0, v, kk, a0_, ctx_len=lc, reverse=False)
    y1 = _rwkv_pair_scan(r, lw1, k1, v, kk, a1_, ctx_len=lc, reverse=True)
    return _rwkv_post(y0, y1, v, g, bonus, _f32(gn_g), _f32(gn_b))


def _f32(t):
    return t.astype(jnp.float32)


def _layer_norm(x, g, b):
    xf = _f32(x)
    mu = jnp.mean(xf, -1, keepdims=True)
    var = jnp.mean(jnp.square(xf - mu), -1, keepdims=True)
    return ((xf - mu) * lax.rsqrt(var + LN_EPS) * _f32(g) + _f32(b)).astype(x.dtype)


def _shift(t, axis, step):
    n = t.shape[axis]
    pad = [(0, 0)] * t.ndim
    if step > 0:
        pad[axis] = (1, 0)
        return lax.slice_in_dim(jnp.pad(t, pad), 0, n, axis=axis)
    pad[axis] = (0, 1)
    return lax.slice_in_dim(jnp.pad(t, pad), 1, n + 1, axis=axis)


def _qshift_grid(s, rows):
    b, l, ch = s.shape
    s4 = s.reshape(b, rows, GRID_W, ch // 4, 4)
    out = jnp.stack([_shift(s4[..., 0], 2, 1), _shift(s4[..., 1], 2, -1),
                     _shift(s4[..., 2], 1, 1), _shift(s4[..., 3], 1, -1)], axis=-1)
    return out.reshape(b, l, ch)


def _qshift_seq(s):
    b, l, ch = s.shape
    s4 = s.reshape(b, l, ch // 4, 4)
    out = jnp.stack([_shift(s4[..., 0], 1, 1), _shift(s4[..., 1], 1, -1),
                     _shift(s4[..., 2], 1, 1), _shift(s4[..., 3], 1, -1)], axis=-1)
    return out.reshape(b, l, ch)


def _s5_discretise(lam_re, lam_im, log_step, b_re, b_im):
    dt = jnp.exp(log_step)[:, None]
    mag = jnp.exp(lam_re * dt)
    ang = lam_im * dt
    lb_re, lb_im = mag * jnp.cos(ang), mag * jnp.sin(ang)
    den = lam_re * lam_re + lam_im * lam_im
    nr, ni = lb_re - 1.0, lb_im
    co_re = (nr * lam_re + ni * lam_im) / den
    co_im = (ni * lam_re - nr * lam_im) / den
    bb_re = co_re[..., None] * b_re - co_im[..., None] * b_im
    bb_im = co_re[..., None] * b_im + co_im[..., None] * b_re
    return lb_re, lb_im, bb_re, bb_im


def _s5_mixer(u, lc, lam_re, lam_im, log_step, b_re, b_im, c_re, c_im, d_skip, glu_w, glu_b):
    disc = [_s5_discretise(_f32(lam_re[d]), _f32(lam_im[d]), _f32(log_step[d]),
                           _f32(b_re[d]), _f32(b_im[d])) for d in range(2)]
    wb, lam, wc = _s5_weights(disc, _f32(c_re), _f32(c_im))
    u = _f32(u)
    nb, l, _ = u.shape
    assert 2 * nb == _S5_ROWS
    u_t = u.transpose(1, 0, 2)
    u8 = jnp.concatenate([u_t, u_t], axis=1).reshape(l * _S5_ROWS, D_S5)
    y_f, y_b = _s5_dir_scan(u8, wb, lam, wc, ctx_len=lc)
    y = (y_f.reshape(l, _S5_ROWS, D_S5)[:, :nb] + y_b.reshape(l, _S5_ROWS, D_S5)[:, nb:]).transpose(1, 0, 2)
    y = jax.nn.gelu(y + _f32(d_skip) * u)
    return y * jax.nn.sigmoid(_mm3(y, glu_w) + _f32(glu_b))


def _gla_chunkwise(q, k, v, log_f, s0):
    b, h, l, _ = q.shape
    dv = v.shape[-1]
    n = l // HG_CHUNK
    blk = lambda t: t.reshape(b, h, n, HG_CHUNK, t.shape[-1])
    q, k, v, log_f = blk(q), blk(k), blk(v), blk(log_f)
    cum = jnp.cumsum(log_f, axis=3)
    lower = jnp.tril(jnp.ones((HG_CHUNK, HG_CHUNK), bool))[:, :, None]
    rel = jnp.exp(jnp.where(lower, cum[..., :, None, :] - cum[..., None, :, :], -jnp.inf))
    scores = jnp.einsum('bhntd,bhnsd,bhntsd->bhnts', q, k, rel)
    o_intra = jnp.einsum('bhnts,bhnsv->bhntv', scores, v)
    last = cum[..., -1:, :]
    chunk_kv = jnp.einsum('bhnsd,bhnsv->bhndv', k * jnp.exp(last - cum), v)
    chunk_decay = jnp.exp(last[..., 0, :])

    def step(state, inp):
        dec, kv = inp
        return dec[..., None] * state + kv, state

    s_fin, s_prev = lax.scan(step, s0, (jnp.moveaxis(chunk_decay, 2, 0), jnp.moveaxis(chunk_kv, 2, 0)))
    o_inter = jnp.einsum('bhntd,bhndv->bhntv', q * jnp.exp(cum), jnp.moveaxis(s_prev, 0, 2))
    return (o_intra + o_inter).reshape(b, h, l, dv), s_fin


def _hgrn2_mixer(p, lc, lb, norm_g, col0=0):
    lb = _f32(lb)
    o_f = _hgrn_dir_scan(p, lb, ctx_len=lc, reverse=False, col0=col0)
    o_b = _hgrn_dir_scan(p, lb, ctx_len=lc, reverse=True, col0=col0)
    return _hgrn_post(o_f, o_b, p, _f32(norm_g), col0=col0)


def _rwkv7_step(state, inp):
    r, w, k, v, kk, a = inp
    sk = jnp.einsum('bhvk,bhk->bhv', state, kk)
    state = (state * w[:, :, None, :] - sk[..., None] * (kk * a)[:, :, None, :]
             + v[..., None] * k[:, :, None, :])
    return state, jnp.einsum('bhvk,bhk->bhv', state, r)


def _rwkv7_scan(r, w, k, v, kk, a, s0, reverse):
    xs = tuple(jnp.moveaxis(t, 1, 0) for t in (r, w, k, v, kk, a))
    s_fin, y = lax.scan(_rwkv7_step, s0, xs, reverse=reverse)
    return jnp.moveaxis(y, 0, 1), s_fin


def _rwkv7_mixer(pc, px, rows, mu, w0, w_up, a0, a_up, g_up, k_k, k_a, r_k, gn_g, gn_b):
    split_at = [D_RW, 2 * D_RW, 3 * D_RW, 3 * D_RW + RW_G_LORA,
                3 * D_RW + RW_G_LORA + RW_W_LORA, 3 * D_RW + RW_G_LORA + 2 * RW_W_LORA,
                3 * D_RW + RW_G_LORA + 2 * RW_W_LORA + RW_A_LORA]

    def heads(t):
        return t.reshape(t.shape[0], t.shape[1], RW_HEADS, RW_HEAD)

    def pre(p, shifted):
        s = p + _f32(mu) * (shifted - p)
        r, k, v, g_lo, wl_f, wl_b, al_f, al_b = jnp.split(s, split_at, axis=-1)
        g = _mm3(jax.nn.sigmoid(g_lo), g_up)
        kk = heads(k * _f32(k_k))
        kk = kk / jnp.maximum(jnp.sqrt(jnp.sum(kk * kk, -1, keepdims=True)), 1e-12)
        r, v = heads(r), heads(v)
        dirs, bonus = [], 0.0
        for d, (wl, al) in enumerate(((wl_f, al_f), (wl_b, al_b))):
            w_log = -jax.nn.softplus(-(_f32(w0[d]) + _mm3(jnp.tanh(wl), w_up[d]))) - 0.5
            a = jax.nn.sigmoid(_f32(a0[d]) + _mm3(al, a_up[d]))
            k_d = heads(k * (1.0 + (a - 1.0) * _f32(k_a)))
            dirs.append((heads(-jnp.exp(w_log)), k_d, heads(a)))
            bonus = bonus + jnp.sum(r * k_d * _f32(r_k), -1, keepdims=True)
        return dict(r=r, v=v, kk=kk, g=g, dirs=dirs, bonus=bonus)

    def post(y, q, shape):
        mu_y = jnp.mean(y, -1, keepdims=True)
        var_y = jnp.mean(jnp.square(y - mu_y), -1, keepdims=True)
        y = ((y - mu_y) * lax.rsqrt(var_y + RW_GN_EPS) * _f32(gn_g).reshape(RW_HEADS, RW_HEAD)
             + _f32(gn_b).reshape(RW_HEADS, RW_HEAD))
        return (y + q['bonus'] * q['v']).reshape(shape[0], shape[1], D_RW) * q['g']

    pc32, px32 = _f32(pc), _f32(px)
    qc = pre(pc32, _qshift_seq(pc32))
    qx = pre(px32, _qshift_grid(px32, rows))
    lc, lx = pc.shape[1], px.shape[1]

    def seq(tc, tx):
        t = jnp.concatenate([tc, tx], axis=1)
        return t.transpose(0, 2, 1, 3).reshape(t.shape[0] * RW_HEADS, lc + lx, RW_HEAD)

    r, v, kk = (seq(qc[n], qx[n]) for n in ('r', 'v', 'kk'))
    y = 0.0
    for d in range(2):
        lw, k_d, a = (seq(qc['dirs'][d][i], qx['dirs'][d][i]) for i in range(3))
        y = y + _rwkv_scan(r, lw, k_d, v, kk, a, heads=RW_HEADS, ctx_len=lc, reverse=d == 1)
    nb = pc.shape[0]
    y = y.reshape(nb, RW_HEADS, lc + lx, RW_HEAD).transpose(0, 2, 1, 3)
    yc = post(y[:, :lc], qc, pc.shape)
    yx = post(y[:, lc:], qx, px.shape)
    return yc.astype(pc.dtype), yx.astype(px.dtype)


def _peer_ffn(h, wq, keys, u_tab, v_tab):
    b, l, d = h.shape
    t = b * l
    hf = h.reshape(t, d)
    q = _matmul(hf, wq).reshape(t, PEER_HEADS, 2, PEER_QDIM // 2)
    scores = _f32(jnp.einsum('thpd,hpkd->thpk', q, keys))
    s_top, i_top = lax.top_k(scores, PEER_TOPK)
    cand = s_top[..., 0, :, None] + s_top[..., 1, None, :]
    cand_s, cand_i = lax.top_k(cand.reshape(t, PEER_HEADS, PEER_TOPK * PEER_TOPK), PEER_TOPK)
    i1 = jnp.take_along_axis(i_top[..., 0, :], cand_i // PEER_TOPK, axis=-1)
    i2 = jnp.take_along_axis(i_top[..., 1, :], cand_i % PEER_TOPK, axis=-1)
    expert = (i1 * PEER_NKEYS + i2).reshape(t, PEER_HEADS * PEER_TOPK)
    gate = jax.nn.softmax(cand_s, axis=-1).reshape(t, PEER_HEADS * PEER_TOPK).astype(h.dtype)
    nb = t // PEER_BLOCK

    def block(args):
        hb, eb, gb = args
        z = jnp.einsum('tkd,td->tk', u_tab[eb], hb)
        return jnp.einsum('tk,tkd->td', jax.nn.gelu(z) * gb, v_tab[eb])

    out = lax.map(block, (hf.reshape(nb, PEER_BLOCK, d),
                          expert.reshape(nb, PEER_BLOCK, PEER_HEADS * PEER_TOPK),
                          gate.reshape(nb, PEER_BLOCK, PEER_HEADS * PEER_TOPK)))
    return out.reshape(b, l, d)


ROW_TILE = 256


def _mod_spec(n_batch, tiles_per_batch, ctx_tiles):
    def index(i):
        return (jnp.where(i % tiles_per_batch < ctx_tiles, n_batch, i // tiles_per_batch), 0, 0)
    return pl.BlockSpec((1, 1, 6 * D_MODEL), index)


def _inproj_kernel(x_ref, mod_ref, w_ref, pa_ref, pc_ref):
    d = D_MODEL
    m = mod_ref[0]
    xm = (x_ref[...] * (1.0 + m[:, d:2 * d]) + m[:, 0:d]).astype(_BF)
    split = pa_ref.shape[1]
    pa_ref[...] = jnp.dot(xm, w_ref[:, :split], preferred_element_type=jnp.float32)
    pc_ref[...] = jnp.dot(xm, w_ref[:, split:], preferred_element_type=jnp.float32)


def _inproj(xs, mod, w, *, n_batch, ctx_len):
    t, d = xs.shape
    n_a = D_S5 + 5 * D_HG
    n_c = w.shape[1] - n_a
    tiles = t // ROW_TILE
    return pl.pallas_call(
        _inproj_kernel,
        grid=(tiles,),
        in_specs=[pl.BlockSpec((ROW_TILE, d), lambda i: (i, 0)),
                  _mod_spec(n_batch, tiles // n_batch, ctx_len // ROW_TILE),
                  pl.BlockSpec((d, n_a + n_c), lambda i: (0, 0))],
        out_specs=[pl.BlockSpec((ROW_TILE, n_a), lambda i: (i, 0)), pl.BlockSpec((ROW_TILE, n_c), lambda i: (i, 0))],
        out_shape=[jax.ShapeDtypeStruct((t, n_a), jnp.float32), jax.ShapeDtypeStruct((t, n_c), jnp.float32)],
        compiler_params=pltpu.CompilerParams(dimension_semantics=("parallel",),
                                             vmem_limit_bytes=48 * 1024 * 1024),
        name="in_proj",
    )(xs, mod, w.astype(_BF))


def _ln_rows(h, g, b):
    mu = jnp.mean(h, axis=-1, keepdims=True)
    hc = h - mu
    var = jnp.mean(hc * hc, axis=-1, keepdims=True)
    return hc * lax.rsqrt(var + LN_EPS) * g + b


def _outproj_ln_kernel(ya_ref, yb_ref, yc_ref, w_ref, xs_ref, mod_ref, ln_ref, x1_ref, h2_ref):
    d = D_MODEL
    mix = (jnp.dot(ya_ref[...].astype(_BF), w_ref[0:D_S5], preferred_element_type=jnp.float32)
           + jnp.dot(yb_ref[...].astype(_BF), w_ref[D_S5:D_S5 + D_HG], preferred_element_type=jnp.float32)
           + jnp.dot(yc_ref[...].astype(_BF), w_ref[D_S5 + D_HG:D_MIX], preferred_element_type=jnp.float32))
    m = mod_ref[0]
    x1 = _ln_rows(DEEPNORM_ALPHA * xs_ref[...] + m[:, 2 * d:3 * d] * mix, ln_ref[0:1], ln_ref[1:2])
    x1_ref[...] = x1
    h2_ref[...] = x1 * (1.0 + m[:, 4 * d:5 * d]) + m[:, 3 * d:4 * d]


def _outproj_ln(ya, yb, yc, w, xs, mod, ln_g, ln_b, *, n_batch, ctx_len):
    t, d = xs.shape
    tiles = t // ROW_TILE
    row = lambda n: pl.BlockSpec((ROW_TILE, n), lambda i: (i, 0))
    out = jax.ShapeDtypeStruct((t, d), jnp.float32)
    return pl.pallas_call(
        _outproj_ln_kernel,
        grid=(tiles,),
        in_specs=[row(D_S5), row(D_HG), row(D_RW), pl.BlockSpec((D_MIX, d), lambda i: (0, 0)), row(d),
                  _mod_spec(n_batch, tiles // n_batch, ctx_len // ROW_TILE),
                  pl.BlockSpec((2, d), lambda i: (0, 0))],
        out_specs=[row(d), row(d)],
        out_shape=[out, out],
        compiler_params=pltpu.CompilerParams(dimension_semantics=("parallel",)),
        name="out_proj_ln",
    )(ya, yb, yc, w.astype(_BF), xs, mod, jnp.stack([ln_g, ln_b]))


def kernel(x, c, ctx, c_ctx, ada_w, ada_b, w_in, w_out, ln1_g, ln1_b, ln2_g, ln2_b,
           s5_lam_re, s5_lam_im, s5_log_step, s5_b_re, s5_b_im, s5_c_re, s5_c_im,
           s5_d, s5_glu_w, s5_glu_b, hgrn_lb_logits, hgrn_norm_g,
           rwkv_mu, rwkv_w0, rwkv_w_up, rwkv_a0, rwkv_a_up, rwkv_g_up, rwkv_k_k, rwkv_k_a,
           rwkv_r_k, rwkv_gn_g, rwkv_gn_b, peer_wq, peer_keys, peer_u, peer_v):
    nb, lx, d = x.shape
    lc = ctx.shape[1]
    l_all = lc + lx
    assert lx % GRID_W == 0 and lc % ROW_TILE == 0 and lx % ROW_TILE == 0 and (nb * l_all) % PEER_TN == 0
    lb_cum = jnp.cumsum(jax.nn.softmax(_f32(hgrn_lb_logits), axis=1), axis=1)
    lb_all = lb_cum - lb_cum[:, :1]
    xs = jnp.concatenate([ctx, x], axis=1).reshape(nb * l_all, d)
    seq = lambda t: t.reshape(nb, l_all, t.shape[-1])
    for l in range(DEPTH):
        last = l == DEPTH - 1
        mod = jnp.concatenate([jax.nn.silu(c), jax.nn.silu(c_ctx)[None]], axis=0) @ ada_w[l] + ada_b[l]
        g2 = mod[:, None, 5 * d:]
        mod = mod[:, None, :]
        pa, pc = _inproj(xs, mod, w_in[l], n_batch=nb, ctx_len=lc)
        pa = seq(pa)
        ya = _s5_mixer(pa[..., :D_S5], lc, s5_lam_re[l], s5_lam_im[l], s5_log_step[l],
                       s5_b_re[l], s5_b_im[l], s5_c_re[l], s5_c_im[l], s5_d[l], s5_glu_w[l], s5_glu_b[l])
        yb = _hgrn2_mixer(pa, lc, lb_all[:, l], hgrn_norm_g[l], col0=D_S5 // D_HG)
        yc = _rwkv7_mixer_fused(seq(pc), lc, rwkv_mu[l], rwkv_w0[l], rwkv_w_up[l], rwkv_a0[l], rwkv_a_up[l],
                                rwkv_g_up[l], rwkv_k_k[l], rwkv_k_a[l], rwkv_r_k[l], rwkv_gn_g[l], rwkv_gn_b[l])
        flat = lambda t: t.reshape(nb * l_all, t.shape[-1])
        xs, h2 = _outproj_ln(flat(ya), flat(yb), flat(yc), w_out[l], xs, mod, ln1_g[l], ln1_b[l],
                             n_batch=nb, ctx_len=lc)
        if last:
            xs, h2 = seq(xs)[:, lc:], seq(h2)[:, lc:].reshape(nb * lx, d)
            gate = g2[:nb]
        else:
            xs = seq(xs)
            gate = jnp.where((jnp.arange(l_all) < lc)[None, :, None], g2[nb:], g2[:nb])
        ffn = _peer_ffn_dense(h2, peer_wq[l], peer_keys[l], peer_u[l].astype(_BF), peer_v[l].T.astype(_BF))
        xs = _layer_norm(DEEPNORM_ALPHA * xs + gate * ffn.reshape(xs.shape), ln2_g[l], ln2_b[l])
        xs = xs.reshape(-1, d)
    return xs.reshape(nb, lx, d)
```

```python
import functools
import math

import jax
import jax.numpy as jnp
from jax import lax
from jax.experimental import pallas as pl
from jax.experimental.pallas import tpu as pltpu

D_MODEL = 1024
DEPTH = 2
GRID_W = 64
D_S5 = D_MODEL // 4
D_HG = D_MODEL // 4
D_RW = D_MODEL // 2
D_MIX = D_S5 + D_HG + D_RW
S5_H = 16
S5_G = D_S5 // S5_H
S5_P = 64
HG_HEAD = 64
HG_CHUNK = 16
RW_HEAD = 64
RW_W_LORA = 64
RW_A_LORA = 64
RW_G_LORA = 128
RW_COLS = 3 * D_RW + RW_G_LORA + 2 * RW_W_LORA + 2 * RW_A_LORA
RW_GN_EPS = 64e-5
PEER_HEADS = 8
PEER_NKEYS = 128
PEER_N = PEER_NKEYS * PEER_NKEYS
PEER_QDIM = 256
PEER_TOPK = 16
LN_EPS = 1e-5
RMS_EPS = 1e-6
DEEPNORM_ALPHA = (2.0 * DEPTH) ** 0.25


def _mm_kernel(x_ref, w_ref, o_ref):
    o_ref[...] = jnp.dot(x_ref[...].astype(jnp.bfloat16), w_ref[...],
                         preferred_element_type=jnp.float32)


def _pick_tile(n, cands):
    for c in cands:
        if n % c == 0:
            return c
    return n


def _matmul(x, w):
    m, k = x.shape
    n = w.shape[1]
    tm = _pick_tile(m, (512, 256, 128, 8))
    tn = _pick_tile(n, (1152, 1024, 512, 256, 128))
    return pl.pallas_call(
        _mm_kernel,
        grid=(m // tm, n // tn),
        in_specs=[pl.BlockSpec((tm, k), lambda i, j: (i, 0)),
                  pl.BlockSpec((k, tn), lambda i, j: (0, j))],
        out_specs=pl.BlockSpec((tm, tn), lambda i, j: (i, j)),
        out_shape=jax.ShapeDtypeStruct((m, n), jnp.float32),
        compiler_params=pltpu.CompilerParams(
            dimension_semantics=("parallel", "parallel"),
            vmem_limit_bytes=48 * 1024 * 1024),
    )(x, w.astype(jnp.bfloat16))


def _mm3(x, w):
    b, l, k = x.shape
    return _matmul(x.reshape(b * l, k), w).reshape(b, l, w.shape[1])


_BF = jnp.bfloat16
_NN = (((1,), (0,)), ((), ()))
_NT = (((1,), (1,)), ((), ()))
_TN = (((0,), (0,)), ((), ()))


def _dot(a, b, dims=_NN):
    return lax.dot_general(a.astype(_BF), b.astype(_BF), dims,
                           preferred_element_type=jnp.float32)


def _split_bf16(a):
    hi = a.astype(_BF)
    return hi, (a - hi.astype(jnp.float32)).astype(_BF)


def _dot3(a, b, dims=_NN):
    a_hi, a_lo = _split_bf16(a)
    b_hi, b_lo = _split_bf16(b)
    d = functools.partial(lax.dot_general, dimension_numbers=dims,
                          preferred_element_type=jnp.float32)
    return d(a_hi, b_hi) + (d(a_hi, b_lo) + d(a_lo, b_hi))


RW_CHUNK = 64


def _scan_block_order(n_ctx, n_all, reverse):
    if not reverse:
        return lambda c: c
    return lambda c: jnp.where(c < n_ctx, n_ctx - 1 - c, n_all - 1 - c + n_ctx)


PEER_TN = 512
PEER_EBLK = 1024
_NEG = -3.0e38


def _top_rows(s, n):
    vals = []
    for _ in range(n):
        m = jnp.max(s, axis=0, keepdims=True)
        vals.append(m)
        s = jnp.where(s == m, _NEG, s)
    return vals


def _peer_route_kernel(h_ref, wq_ref, keys_ref, xb_ref, s2_ref, tau_ref, c1_ref, p2_ref):
    hb = h_ref[...].astype(_BF)
    xb_ref[...] = hb
    q = jnp.dot(hb, wq_ref[...], preferred_element_type=jnp.float32)
    half = PEER_QDIM // 2
    for h in range(PEER_HEADS):
        s1 = _dot(keys_ref[2 * h], q[:, (2 * h) * half:(2 * h + 1) * half], _NT)
        s2 = _dot(keys_ref[2 * h + 1], q[:, (2 * h + 1) * half:(2 * h + 2) * half], _NT)
        t1 = _top_rows(s1, PEER_TOPK + 1)
        t2 = _top_rows(s2, PEER_TOPK + 1)
        cand = [t1[j1] + t2[j2] for j1 in range(PEER_TOPK + 1) for j2 in range(PEER_TOPK + 1)
                if (j1 + 1) * (j2 + 1) <= PEER_TOPK + 1]
        cand += [jnp.full_like(cand[0], _NEG)] * (-len(cand) % 8)
        top = _top_rows(jnp.concatenate(cand, axis=0), PEER_TOPK + 1)
        theta = 0.5 * (top[PEER_TOPK - 1] + top[PEER_TOPK])
        z = top[0] * 0.0
        for c in top[:PEER_TOPK]:
            z = z + jnp.exp(c - top[0])
        s2_ref[h] = s2
        tau_ref[h] = theta - s1
        c1_ref[h] = jnp.exp(s1 - t1[0]) / z
        p2_ref[h] = jnp.exp(s2 - t2[0])


def _peer_route(hf, wq, keys):
    t, d = hf.shape
    tn = PEER_TN
    nq = PEER_HEADS * PEER_QDIM
    aux = jax.ShapeDtypeStruct((PEER_HEADS, PEER_NKEYS, t), jnp.float32)
    aux_spec = pl.BlockSpec((PEER_HEADS, PEER_NKEYS, tn), lambda i: (0, 0, i))
    return pl.pallas_call(
        _peer_route_kernel,
        grid=(t // tn,),
        in_specs=[pl.BlockSpec((tn, d), lambda i: (i, 0)),
                  pl.BlockSpec((d, nq), lambda i: (0, 0)),
                  pl.BlockSpec((PEER_HEADS * 2, PEER_NKEYS, PEER_QDIM // 2), lambda i: (0, 0, 0))],
        out_specs=[pl.BlockSpec((tn, d), lambda i: (i, 0))] + [aux_spec] * 4,
        out_shape=[jax.ShapeDtypeStruct((t, d), _BF)] + [aux] * 4,
        compiler_params=pltpu.CompilerParams(dimension_semantics=("parallel",),
                                             vmem_limit_bytes=56 * 1024 * 1024),
        name="peer_route",
    )(hf, wq.astype(_BF), keys.reshape(PEER_HEADS * 2, PEER_NKEYS, PEER_QDIM // 2))


def _gelu_tanh(z):
    return 0.5 * z * (1.0 + jnp.tanh(math.sqrt(2.0 / math.pi) * (z + 0.044715 * (z * z * z))))


def _peer_expert_kernel(x_ref, u_ref, vt_ref, s2_ref, tau_ref, c1_ref, p2_ref, o_ref, acc_ref, at_ref):
    j = pl.program_id(1)

    @pl.when(j == 0)
    def _():
        acc_ref[...] = jnp.zeros_like(acc_ref)

    nk = PEER_NKEYS
    n_slab = PEER_EBLK // nk
    rows = pl.ds(pl.multiple_of(j * n_slab, n_slab), n_slab)
    zt = lax.dot_general(u_ref[...], x_ref[...], _NT, preferred_element_type=jnp.float32)
    for s in range(n_slab):
        for lt in range(PEER_TN // 128):
            lanes = pl.ds(lt * 128, 128)
            g = jnp.zeros((nk, 128), jnp.float32)
            for h in range(PEER_HEADS):
                tau = tau_ref[h, rows, lanes][s:s + 1]
                c1 = c1_ref[h, rows, lanes][s:s + 1]
                g = g + jnp.where(s2_ref[h, :, lanes] >= tau, p2_ref[h, :, lanes] * c1, 0.0)
            z = zt[s * nk:(s + 1) * nk, lt * 128:(lt + 1) * 128]
            at_ref[s * nk:(s + 1) * nk, lt * 128:(lt + 1) * 128] = (g * _gelu_tanh(z)).astype(_BF)
    acc_ref[...] += jnp.dot(vt_ref[...], at_ref[...], preferred_element_type=jnp.float32)

    @pl.when(j == pl.num_programs(1) - 1)
    def _():
        o_ref[...] = acc_ref[...].T


def _peer_expert(xb, u_b, vt_b, aux):
    t, d = xb.shape
    tn, eb = PEER_TN, PEER_EBLK
    aux_spec = pl.BlockSpec((PEER_HEADS, PEER_NKEYS, tn), lambda i, j: (0, 0, i))
    return pl.pallas_call(
        _peer_expert_kernel,
        grid=(t // tn, PEER_N // eb),
        in_specs=[pl.BlockSpec((tn, d), lambda i, j: (i, 0)),
                  pl.BlockSpec((eb, d), lambda i, j: (j, 0)),
                  pl.BlockSpec((d, eb), lambda i, j: (0, j))] + [aux_spec] * 4,
        out_specs=pl.BlockSpec((tn, d), lambda i, j: (i, 0)),
        out_shape=jax.ShapeDtypeStruct((t, d), jnp.float32),
        scratch_shapes=[pltpu.VMEM((d, tn), jnp.float32), pltpu.VMEM((eb, tn), _BF)],
        compiler_params=pltpu.CompilerParams(dimension_semantics=("parallel", "arbitrary"),
                                             vmem_limit_bytes=56 * 1024 * 1024),
        name="peer_expert",
    )(xb, u_b, vt_b, *aux)


def _peer_ffn_dense(hf, wq, keys, u_b, vt_b):
    xb, *aux = _peer_route(hf, wq, keys)
    return _peer_expert(xb, u_b, vt_b, aux)


S5_STEPS = 128
_S5_ROWS = 8
_S5_STATE = S5_G * S5_P


def _s5_weights(disc, c_re, c_im):
    eye = jnp.eye(S5_G, dtype=jnp.float32)

    def bdiag_in(bb):
        return jnp.einsum('gph,gk->ghkp', bb, eye).reshape(D_S5, _S5_STATE)

    def bdiag_out(cc):
        return jnp.einsum('ghp,gk->gpkh', cc, eye).reshape(_S5_STATE, D_S5)

    wb = jnp.stack([jnp.concatenate([bdiag_in(disc[d][2]), bdiag_in(disc[d][3])], axis=1) for d in range(2)])
    wc = jnp.stack([jnp.concatenate([bdiag_out(c_re[d]), -bdiag_out(c_im[d])], axis=0) for d in range(2)])
    half = _S5_ROWS // 2
    lam = jnp.stack([
        jnp.concatenate([jnp.broadcast_to(disc[d][i].reshape(1, _S5_STATE), (half, _S5_STATE))
                         for d in range(2)], axis=0) for i in range(2)])
    return wb.astype(_BF), lam, wc.astype(_BF)


def _s5_dir_kernel(uf_ref, ub_ref, wb_ref, lam_ref, wc_ref, yf_ref, yb_ref, buf_ref, bub_ref, xf_ref, xb_ref, st_ref):
    @pl.when(pl.program_id(0) == 0)
    def _():
        st_ref[...] = jnp.zeros_like(st_ref)

    n = _S5_STATE
    buf_ref[...] = jnp.dot(uf_ref[...].astype(_BF), wb_ref[0], preferred_element_type=jnp.float32)
    bub_ref[...] = jnp.dot(ub_ref[...].astype(_BF), wb_ref[1], preferred_element_type=jnp.float32)
    lr = lam_ref[0]
    li = lam_ref[1]
    fwd = lax.broadcasted_iota(jnp.int32, (_S5_ROWS, n), 0) < (_S5_ROWS // 2)

    def step(i, carry):
        xr, xi = carry
        rf = pl.ds(pl.multiple_of(i * _S5_ROWS, _S5_ROWS), _S5_ROWS)
        rb = pl.ds(pl.multiple_of((S5_STEPS - 1 - i) * _S5_ROWS, _S5_ROWS), _S5_ROWS)
        nr = lr * xr - li * xi + jnp.where(fwd, buf_ref[rf, :n], bub_ref[rb, :n])
        ni = lr * xi + li * xr + jnp.where(fwd, buf_ref[rf, n:], bub_ref[rb, n:])
        xf_ref[rf, :n] = nr
        xf_ref[rf, n:] = ni
        xb_ref[rb, :n] = nr
        xb_ref[rb, n:] = ni
        return nr, ni

    xr, xi = lax.fori_loop(0, S5_STEPS, step, (st_ref[0], st_ref[1]), unroll=4)
    st_ref[0] = xr
    st_ref[1] = xi
    yf_ref[...] = jnp.dot(xf_ref[...].astype(_BF), wc_ref[0], preferred_element_type=jnp.float32)
    yb_ref[...] = jnp.dot(xb_ref[...].astype(_BF), wc_ref[1], preferred_element_type=jnp.float32)


def _s5_dir_scan(u8, wb, lam, wc, *, ctx_len):
    rows = u8.shape[0]
    blk = S5_STEPS * _S5_ROWS
    n = _S5_STATE
    order = _scan_block_order(ctx_len // S5_STEPS, rows // blk, True)
    spec_f = pl.BlockSpec((blk, D_S5), lambda i: (i, 0))
    spec_b = pl.BlockSpec((blk, D_S5), lambda i: (order(i), 0))
    out = jax.ShapeDtypeStruct((rows, D_S5), jnp.float32)
    big = pltpu.VMEM((blk, 2 * n), jnp.float32)
    return pl.pallas_call(
        _s5_dir_kernel,
        grid=(rows // blk,),
        in_specs=[spec_f, spec_b,
                  pl.BlockSpec((2, D_S5, 2 * n), lambda i: (0, 0, 0)),
                  pl.BlockSpec((2, _S5_ROWS, n), lambda i: (0, 0, 0)),
                  pl.BlockSpec((2, 2 * n, D_S5), lambda i: (0, 0, 0))],
        out_specs=[spec_f, spec_b],
        out_shape=[out, out],
        scratch_shapes=[big, big, big, big, pltpu.VMEM((2, _S5_ROWS, n), jnp.float32)],
        compiler_params=pltpu.CompilerParams(dimension_semantics=("arbitrary",),
                                             vmem_limit_bytes=56 * 1024 * 1024),
        name="s5_scan",
    )(u8, u8, wb, lam, wc)


HG_BLOCK = 128
HG_BATCH_BLOCK = 2
_HG_PAIR = 2 * HG_HEAD


def _hgrn_dir_kernel(q_ref, v_ref, f_ref, lb_ref, o_ref, st_ref, *, reverse):
    @pl.when(pl.program_id(1) == 0)
    def _():
        st_ref[...] = jnp.zeros_like(st_ref)

    ch = HG_CHUNK
    n_ch = HG_BLOCK // ch
    r = lax.broadcasted_iota(jnp.int32, (HG_BLOCK, HG_BLOCK), 0)
    c = lax.broadcasted_iota(jnp.int32, (HG_BLOCK, HG_BLOCK), 1)
    tri = ((r // ch == c // ch) & ((r <= c) if reverse else (r >= c))).astype(jnp.float32)
    same_head = (lax.broadcasted_iota(jnp.int32, (_HG_PAIR, _HG_PAIR), 0) // HG_HEAD
                 == lax.broadcasted_iota(jnp.int32, (_HG_PAIR, _HG_PAIR), 1) // HG_HEAD)
    ones_blk = same_head.astype(_BF)
    tcol = lax.broadcasted_iota(jnp.int32, (ch, 1), 0)
    end = 0 if reverse else ch - 1
    n_pair = D_HG // _HG_PAIR
    chains = [(bi, p) for bi in range(q_ref.shape[0]) for p in range(n_pair)]
    pair_lanes = lambda p: slice(p * _HG_PAIR, (p + 1) * _HG_PAIR)
    intra, kv, dec, qd = {}, {}, {}, {}
    for bi in range(q_ref.shape[0]):
        q_raw = q_ref[bi]
        q_all = q_raw * _sigmoid(q_raw)
        forget = lb_ref[...] + (1.0 - lb_ref[...]) * _sigmoid(f_ref[bi])
        k_all = 1.0 - forget
        cum_all = _dot3(tri, jnp.log(forget))
        for p in range(n_pair):
            lanes = pair_lanes(p)
            for ci in range(n_ch):
                rows = slice(ci * ch, (ci + 1) * ch)
                q = q_all[rows, lanes]
                k = k_all[rows, lanes]
                v = v_ref[bi, rows, lanes]
                cum = cum_all[rows, lanes]
                last = cum[end:end + 1]
                w = [q * jnp.exp(jnp.minimum(cum - cum[s:s + 1], 0.0)) * k[s:s + 1] for s in range(ch)]
                rel = jnp.dot(jnp.concatenate(w, axis=0).astype(_BF), ones_blk,
                              preferred_element_type=jnp.float32)
                o = jnp.zeros((ch, _HG_PAIR), jnp.float32)
                for s in range(ch):
                    seen = (tcol <= s) if reverse else (tcol >= s)
                    o = o + jnp.where(seen, rel[s * ch:(s + 1) * ch], 0.0) * v[s:s + 1]
                intra[bi, p, ci] = o
                kv[bi, p, ci] = jnp.where(same_head, _dot(v, k * jnp.exp(last - cum), _TN), 0.0)
                dec[bi, p, ci] = jnp.exp(last)
                qd[bi, p, ci] = q * jnp.exp(cum)
    state = [st_ref[i] for i in range(len(chains))]
    for ci in (range(n_ch - 1, -1, -1) if reverse else range(n_ch)):
        for i, (bi, p) in enumerate(chains):
            o_ref[bi, ci * ch:(ci + 1) * ch, pair_lanes(p)] = intra[bi, p, ci] + _dot(qd[bi, p, ci], state[i], _NT)
            state[i] = state[i] * dec[bi, p, ci] + kv[bi, p, ci]
    for i in range(len(chains)):
        st_ref[i] = state[i]


def _hgrn_dir_scan(p, lb, *, ctx_len, reverse, col0=0):
    g, l, _ = p.shape
    d = D_HG
    order = _scan_block_order(ctx_len // HG_BLOCK, l // HG_BLOCK, reverse)
    bb = HG_BATCH_BLOCK if g % HG_BATCH_BLOCK == 0 else 1
    col = lambda c: pl.BlockSpec((bb, HG_BLOCK, d), lambda i, j: (i, order(j), col0 + c))
    dr = 1 if reverse else 0
    return pl.pallas_call(
        functools.partial(_hgrn_dir_kernel, reverse=reverse),
        grid=(g // bb, l // HG_BLOCK),
        in_specs=[col(0), col(1), col(2 + dr), pl.BlockSpec((1, d), lambda i, j: (0, 0))],
        out_specs=pl.BlockSpec((bb, HG_BLOCK, d), lambda i, j: (i, order(j), 0)),
        out_shape=jax.ShapeDtypeStruct((g, l, d), jnp.float32),
        scratch_shapes=[pltpu.VMEM((bb * d // _HG_PAIR, _HG_PAIR, _HG_PAIR), jnp.float32)],
        compiler_params=pltpu.CompilerParams(dimension_semantics=("parallel", "arbitrary")),
        name="hgrn_scan",
    )(p, p, p, lb[dr:dr + 1])


def _hgrn_post_kernel(of_ref, ob_ref, g_ref, ng_ref, ones_ref, y_ref):
    o = of_ref[0] + ob_ref[0]
    ms = jnp.dot((o * o).astype(_BF), ones_ref[...], preferred_element_type=jnp.float32) * (1.0 / HG_HEAD)
    g = g_ref[0]
    y_ref[0] = o * lax.rsqrt(ms + RMS_EPS) * ng_ref[...] * (g * _sigmoid(g))


def _hgrn_post(o_f, o_b, p, norm_g, col0=0):
    nb, l, d = o_f.shape
    spec = pl.BlockSpec((1, HG_BLOCK, d), lambda i, j: (i, j, 0))
    return pl.pallas_call(
        _hgrn_post_kernel,
        grid=(nb, l // HG_BLOCK),
        in_specs=[spec, spec, pl.BlockSpec((1, HG_BLOCK, d), lambda i, j: (i, j, col0 + 4)),
                  pl.BlockSpec((1, d), lambda i, j: (0, 0)), pl.BlockSpec((d, d), lambda i, j: (0, 0))],
        out_specs=spec,
        out_shape=jax.ShapeDtypeStruct((nb, l, d), jnp.float32),
        compiler_params=pltpu.CompilerParams(dimension_semantics=("parallel", "parallel")),
        name="hgrn_post",
    )(o_f, o_b, p, norm_g.reshape(1, d), _head_ones(d, HG_HEAD))


_RW_PAIR = 2 * RW_HEAD
RW_PRE_T = 128
RW_BATCH_BLOCK = 4


def _rwkv_pair_kernel(r_ref, lw_ref, k_ref, v_ref, kk_ref, a_ref, y_ref, st_ref, *, chunk, reverse):
    @pl.when(pl.program_id(1) == 0)
    def _():
        st_ref[...] = jnp.zeros_like(st_ref)

    row = lax.broadcasted_iota(jnp.int32, (chunk, chunk), 0)
    col = lax.broadcasted_iota(jnp.int32, (chunk, chunk), 1)
    strict = row < col if reverse else row > col
    incl = row <= col if reverse else row >= col
    end = 0 if reverse else chunk - 1
    tri = incl.astype(jnp.float32)
    eye = (row == col).astype(jnp.float32)
    blk8 = row // 8 == col // 8
    merge_masks = []
    size = 8
    while size < chunk:
        merge_masks.append((row // (2 * size) == col // (2 * size)) & (row // size != col // size))
        size *= 2
    lane_head = lax.broadcasted_iota(jnp.int32, (1, _RW_PAIR), 1) // RW_HEAD
    head0 = lane_head == 0
    same_head = (lax.broadcasted_iota(jnp.int32, (_RW_PAIR, _RW_PAIR), 0) // RW_HEAD
                 == lax.broadcasted_iota(jnp.int32, (_RW_PAIR, _RW_PAIR), 1) // RW_HEAD)
    n_pair = D_RW // _RW_PAIR
    ps = range(r_ref.shape[0] * n_pair)
    ph = [(p, h) for p in ps for h in range(2)]
    bi = [p // n_pair for p in ps]
    sl = [slice((p % n_pair) * _RW_PAIR, (p % n_pair + 1) * _RW_PAIR) for p in ps]
    lw = [lw_ref[bi[p], :, sl[p]] for p in ps]
    cum = [_dot3(tri, lw[p]) for p in ps]
    pw = [jnp.exp(cum[p]) for p in ps]
    p_inv = [jnp.exp(-cum[p]) for p in ps]
    p_end = [pw[p][end:end + 1, :] for p in ps]
    kk = [kk_ref[bi[p], :, sl[p]] for p in ps]
    bh = [kk[p] * a_ref[bi[p], :, sl[p]] * p_inv[p] for p in ps]
    kh = [k_ref[bi[p], :, sl[p]] * p_inv[p] for p in ps]
    v = [v_ref[bi[p], :, sl[p]] for p in ps]
    s0 = [st_ref[p] for p in ps]
    x1 = [jnp.concatenate([-kk[p] * jnp.exp(cum[p] - lw[p]), r_ref[bi[p], :, sl[p]] * pw[p]], axis=0) for p in ps]
    x1h = {(p, h): jnp.where(lane_head == h, x1[p], 0.0) for p, h in ph}
    g1 = {q: _dot(x1h[q], bh[q[0]], _NT) for q in ph}
    g2 = {q: _dot(x1h[q], kh[q[0]], _NT) for q in ph}
    hm = [_dot(x1[p], s0[p], _NT) for p in ps]
    ab = {q: jnp.where(strict, g1[q][:chunk], 0.0) for q in ph}
    d1 = {q: jnp.where(blk8, ab[q], 0.0) for q in ph}
    d2 = {q: _dot3(d1[q], d1[q]) for q in ph}
    inv = {q: eye + d1[q] for q in ph}
    inv = {q: inv[q] + _dot3(inv[q], d2[q]) for q in ph}
    d4 = {q: _dot3(d2[q], d2[q]) for q in ph}
    inv = {q: inv[q] + _dot3(inv[q], d4[q]) for q in ph}
    for m in merge_masks:
        li = {q: _dot(jnp.where(m, ab[q], 0.0), inv[q]) for q in ph}
        inv = {q: inv[q] + _dot(inv[q], li[q]) for q in ph}
    akv = {q: _dot(jnp.where(strict, g2[q][:chunk], 0.0), v[q[0]]) for q in ph}
    rhs = [hm[p][:chunk] + jnp.where(head0, akv[p, 0], akv[p, 1]) for p in ps]
    eh = {q: _dot(inv[q], rhs[q[0]]) for q in ph}
    e = [jnp.where(head0, eh[p, 0], eh[p, 1]) for p in ps]
    yh = {q: _dot(jnp.where(incl, g1[q][chunk:], 0.0), e[q[0]])
          + _dot(jnp.where(incl, g2[q][chunk:], 0.0), v[q[0]]) for q in ph}
    for p in ps:
        y_ref[bi[p], :, sl[p]] = hm[p][chunk:] + jnp.where(head0, yh[p, 0], yh[p, 1])
    for p in ps:
        ev = jnp.concatenate([e[p], v[p]], axis=0)
        x2 = jnp.concatenate([bh[p] * p_end[p], kh[p] * p_end[p]], axis=0)
        st_ref[p] = s0[p] * p_end[p] + jnp.where(same_head, _dot(ev, x2, _TN), 0.0)


def _rwkv_pair_scan(r, lw, k, v, kk, a, *, ctx_len, reverse):
    nb, l, d = r.shape
    chunk = RW_CHUNK
    order = _scan_block_order(ctx_len // chunk, l // chunk, reverse)
    bb = RW_BATCH_BLOCK if nb % RW_BATCH_BLOCK == 0 else 1
    spec = pl.BlockSpec((bb, chunk, d), lambda b, c: (b, order(c), 0))
    return pl.pallas_call(
        functools.partial(_rwkv_pair_kernel, chunk=chunk, reverse=reverse),
        grid=(nb // bb, l // chunk),
        in_specs=[spec] * 6,
        out_specs=spec,
        out_shape=jax.ShapeDtypeStruct((nb, l, d), jnp.float32),
        scratch_shapes=[pltpu.VMEM((bb * d // _RW_PAIR, _RW_PAIR, _RW_PAIR), jnp.float32)],
        compiler_params=pltpu.CompilerParams(dimension_semantics=("parallel", "arbitrary")),
        name="rwkv_scan",
    )(r, lw, k, v, kk, a)


def _softplus(z):
    return jnp.maximum(z, 0.0) + jnp.log(1.0 + jnp.exp(-jnp.abs(z)))


def _sigmoid(z):
    return 1.0 / (1.0 + jnp.exp(-z))


def _rwkv_pre_kernel(prev_ref, cur_ref, next_ref, mu_ref, vec_ref, gup_ref, wup_ref, aup_ref, ones_ref,
                     r_ref, v_ref, kk_ref, g_ref, bonus_ref, lw0_ref, k0_ref, a0_ref, lw1_ref, k1_ref, a1_ref,
                     ext_ref, s_ref, *, ctx_len, seq_len):
    t_rows, w = RW_PRE_T, GRID_W
    ext_ref[0:w] = prev_ref[0]
    ext_ref[w:w + t_rows] = cur_ref[0]
    ext_ref[w + t_rows:w + t_rows + w] = next_ref[0]
    t = pl.program_id(1) * t_rows + lax.broadcasted_iota(jnp.int32, (t_rows, 128), 0)
    lane4 = lax.broadcasted_iota(jnp.int32, (t_rows, 128), 1) % 4
    is_ctx = t < ctx_len
    tx = t - ctx_len
    col = tx % w
    is_lat = jnp.logical_not(is_ctx)
    use_p1 = (is_ctx & (lane4 % 2 == 0) & (t > 0)) | (is_lat & (lane4 == 0) & (col != 0))
    use_n1 = (is_ctx & (lane4 % 2 == 1) & (t < ctx_len - 1)) | (is_lat & (lane4 == 1) & (col != w - 1))
    use_p64 = is_lat & (lane4 == 2) & (tx >= w)
    use_n64 = is_lat & (lane4 == 3) & (tx < seq_len - w)
    for lt in range(RW_COLS // 128):
        lanes = slice(lt * 128, (lt + 1) * 128)
        p = ext_ref[w:w + t_rows, lanes]
        shifted = jnp.where(use_p1, ext_ref[w - 1:w - 1 + t_rows, lanes],
                            jnp.where(use_n1, ext_ref[w + 1:w + 1 + t_rows, lanes],
                                      jnp.where(use_p64, ext_ref[0:t_rows, lanes],
                                                jnp.where(use_n64, ext_ref[2 * w:2 * w + t_rows, lanes], 0.0))))
        s_ref[:, lanes] = p + mu_ref[:, lanes] * (shifted - p)
    d = D_RW
    r = s_ref[:, 0:d]
    k = s_ref[:, d:2 * d]
    v = s_ref[:, 2 * d:3 * d]
    o = 3 * d
    g_lo = s_ref[:, o:o + RW_G_LORA]
    wl = s_ref[:, o + RW_G_LORA:o + RW_G_LORA + 2 * RW_W_LORA]
    al = s_ref[:, o + RW_G_LORA + 2 * RW_W_LORA:o + RW_G_LORA + 2 * RW_W_LORA + 2 * RW_A_LORA]
    k_k, k_a, r_k = vec_ref[0:1], vec_ref[1:2], vec_ref[2:3]
    ones = ones_ref[...]
    r_ref[0] = r
    v_ref[0] = v
    g_ref[0] = jnp.dot(_sigmoid(g_lo).astype(_BF), gup_ref[...], preferred_element_type=jnp.float32)
    kkr = k * k_k
    ss = jnp.dot((kkr * kkr).astype(_BF), ones, preferred_element_type=jnp.float32)
    kk_ref[0] = kkr / jnp.maximum(jnp.sqrt(ss), 1e-12)
    w_pre = jnp.dot(jnp.tanh(wl).astype(_BF), wup_ref[...], preferred_element_type=jnp.float32)
    a_pre = jnp.dot(al.astype(_BF), aup_ref[...], preferred_element_type=jnp.float32)
    k_sum = 0.0
    for dr, (lw_o, k_o, a_o) in enumerate(((lw0_ref, k0_ref, a0_ref), (lw1_ref, k1_ref, a1_ref))):
        w_log = -_softplus(-(vec_ref[3 + dr:4 + dr] + w_pre[:, dr * d:(dr + 1) * d])) - 0.5
        a = _sigmoid(vec_ref[5 + dr:6 + dr] + a_pre[:, dr * d:(dr + 1) * d])
        k_d = k * (1.0 + (a - 1.0) * k_a)
        lw_o[0] = -jnp.exp(w_log)
        k_o[0] = k_d
        a_o[0] = a
        k_sum = k_sum + k_d
    bonus_ref[0] = jnp.dot((r * k_sum * r_k).astype(_BF), ones, preferred_element_type=jnp.float32)


def _head_ones(width, head):
    i = jnp.arange(width) // head
    return (i[:, None] == i[None, :]).astype(_BF)


def _rwkv_pre(p, mu, k_k, k_a, r_k, w0, a0, g_up, w_up, a_up, *, ctx_len):
    nb, l, cols = p.shape
    t_rows, w, d = RW_PRE_T, GRID_W, D_RW
    nblk = l // w
    per = t_rows // w
    zeros = jnp.zeros((RW_W_LORA, d), jnp.float32)
    wup = jnp.concatenate([jnp.concatenate([w_up[0], zeros], axis=1),
                           jnp.concatenate([zeros, w_up[1]], axis=1)], axis=0).astype(_BF)
    aup = jnp.concatenate([jnp.concatenate([a_up[0], zeros], axis=1),
                           jnp.concatenate([zeros, a_up[1]], axis=1)], axis=0).astype(_BF)
    vec = jnp.stack([k_k, k_a, r_k.reshape(d), w0[0], w0[1], a0[0], a0[1], jnp.zeros((d,), jnp.float32)])
    out = jax.ShapeDtypeStruct((nb, l, d), jnp.float32)
    out_spec = pl.BlockSpec((1, t_rows, d), lambda b, j: (b, j, 0))
    full = lambda shape: pl.BlockSpec(shape, lambda b, j: (0,) * len(shape))
    return pl.pallas_call(
        functools.partial(_rwkv_pre_kernel, ctx_len=ctx_len, seq_len=l - ctx_len),
        grid=(nb, l // t_rows),
        in_specs=[pl.BlockSpec((1, w, cols), lambda b, j: (b, jnp.maximum(j * per - 1, 0), 0)),
                  pl.BlockSpec((1, t_rows, cols), lambda b, j: (b, j, 0)),
                  pl.BlockSpec((1, w, cols), lambda b, j: (b, jnp.minimum(j * per + per, nblk - 1), 0)),
                  full((1, cols)), full((8, d)), full((RW_G_LORA, d)),
                  full((2 * RW_W_LORA, 2 * d)), full((2 * RW_A_LORA, 2 * d)), full((d, d))],
        out_specs=[out_spec] * 11,
        out_shape=[out] * 11,
        scratch_shapes=[pltpu.VMEM((t_rows + 2 * w, cols), jnp.float32), pltpu.VMEM((t_rows, cols), jnp.float32)],
        compiler_params=pltpu.CompilerParams(dimension_semantics=("parallel", "parallel"),
                                             vmem_limit_bytes=48 * 1024 * 1024),
        name="rwkv_pre",
    )(p, p, p, mu.reshape(1, cols), vec, g_up.astype(_BF), wup, aup, _head_ones(d, RW_HEAD))


def _rwkv_post_kernel(y0_ref, y1_ref, v_ref, g_ref, bonus_ref, gn_ref, ones_ref, o_ref):
    y = y0_ref[0] + y1_ref[0]
    ones = ones_ref[...]
    inv_n = 1.0 / RW_HEAD
    mu_y = jnp.dot(y.astype(_BF), ones, preferred_element_type=jnp.float32) * inv_n
    yc = y - mu_y
    var_y = jnp.dot((yc * yc).astype(_BF), ones, preferred_element_type=jnp.float32) * inv_n
    yn = yc * lax.rsqrt(var_y + RW_GN_EPS) * gn_ref[0:1] + gn_ref[1:2]
    o_ref[0] = (yn + bonus_ref[0] * v_ref[0]) * g_ref[0]


def _rwkv_post(y0, y1, v, g, bonus, gn_g, gn_b):
    nb, l, d = y0.shape
    spec = pl.BlockSpec((1, RW_PRE_T, d), lambda b, j: (b, j, 0))
    return pl.pallas_call(
        _rwkv_post_kernel,
        grid=(nb, l // RW_PRE_T),
        in_specs=[spec] * 5 + [pl.BlockSpec((2, d), lambda b, j: (0, 0)), pl.BlockSpec((d, d), lambda b, j: (0, 0))],
        out_specs=spec,
        out_shape=jax.ShapeDtypeStruct((nb, l, d), jnp.float32),
        compiler_params=pltpu.CompilerParams(dimension_semantics=("parallel", "parallel")),
        name="rwkv_post",
    )(y0, y1, v, g, bonus, jnp.stack([gn_g, gn_b]), _head_ones(d, RW_HEAD))


def _rwkv7_mixer_fused(p, lc, mu, w0, w_up, a0, a_up, g_up, k_k, k_a, r_k, gn_g, gn_b):
    r, v, kk, g, bonus, lw0, k0, a0_, lw1, k1, a1_ = _rwkv_pre(
        p, _f32(mu), _f32(k_k), _f32(k_a), _f32(r_k), _f32(w0), _f32(a0), _f32(g_up), _f32(w_up), _f32(a_up),
        ctx_len=lc)
    y0 = _rwkv_pair_scan(r, lw0, k0, v, kk, a0_, ctx_len=lc, reverse=False)
    y1 = _rwkv_pair_scan(r, lw1, k1, v, kk, a1_, ctx_len=lc, reverse=True)
    return _rwkv_post(y0, y1, v, g, bonus, _f32(gn_g), _f32(gn_b))


def _f32(t):
    return t.astype(jnp.float32)


def _layer_norm(x, g, b):
    xf = _f32(x)
    mu = jnp.mean(xf, -1, keepdims=True)
    var = jnp.mean(jnp.square(xf - mu), -1, keepdims=True)
    return ((xf - mu) * lax.rsqrt(var + LN_EPS) * _f32(g) + _f32(b)).astype(x.dtype)


def _s5_discretise(lam_re, lam_im, log_step, b_re, b_im):
    dt = jnp.exp(log_step)[:, None]
    mag = jnp.exp(lam_re * dt)
    ang = lam_im * dt
    lb_re, lb_im = mag * jnp.cos(ang), mag * jnp.sin(ang)
    den = lam_re * lam_re + lam_im * lam_im
    nr, ni = lb_re - 1.0, lb_im
    co_re = (nr * lam_re + ni * lam_im) / den
    co_im = (ni * lam_re - nr * lam_im) / den
    bb_re = co_re[..., None] * b_re - co_im[..., None] * b_im
    bb_im = co_re[..., None] * b_im + co_im[..., None] * b_re
    return lb_re, lb_im, bb_re, bb_im


def _s5_mixer(u, lc, lam_re, lam_im, log_step, b_re, b_im, c_re, c_im, d_skip, glu_w, glu_b):
    disc = [_s5_discretise(_f32(lam_re[d]), _f32(lam_im[d]), _f32(log_step[d]),
                           _f32(b_re[d]), _f32(b_im[d])) for d in range(2)]
    wb, lam, wc = _s5_weights(disc, _f32(c_re), _f32(c_im))
    u = _f32(u)
    nb, l, _ = u.shape
    assert 2 * nb == _S5_ROWS
    u_t = u.transpose(1, 0, 2)
    u8 = jnp.concatenate([u_t, u_t], axis=1).reshape(l * _S5_ROWS, D_S5)
    y_f, y_b = _s5_dir_scan(u8, wb, lam, wc, ctx_len=lc)
    y = (y_f.reshape(l, _S5_ROWS, D_S5)[:, :nb] + y_b.reshape(l, _S5_ROWS, D_S5)[:, nb:]).transpose(1, 0, 2)
    y = jax.nn.gelu(y + _f32(d_skip) * u)
    return y * jax.nn.sigmoid(_mm3(y, glu_w) + _f32(glu_b))


def _hgrn2_mixer(p, lc, lb, norm_g, col0=0):
    lb = _f32(lb)
    o_f = _hgrn_dir_scan(p, lb, ctx_len=lc, reverse=False, col0=col0)
    o_b = _hgrn_dir_scan(p, lb, ctx_len=lc, reverse=True, col0=col0)
    return _hgrn_post(o_f, o_b, p, _f32(norm_g), col0=col0)


ROW_TILE = 256


def _mod_spec(n_batch, tiles_per_batch, ctx_tiles):
    def index(i):
        return (jnp.where(i % tiles_per_batch < ctx_tiles, n_batch, i // tiles_per_batch), 0, 0)
    return pl.BlockSpec((1, 1, 6 * D_MODEL), index)


def _inproj_kernel(x_ref, mod_ref, w_ref, pa_ref, pc_ref):
    d = D_MODEL
    m = mod_ref[0]
    xm = (x_ref[...] * (1.0 + m[:, d:2 * d]) + m[:, 0:d]).astype(_BF)
    split = pa_ref.shape[1]
    pa_ref[...] = jnp.dot(xm, w_ref[:, :split], preferred_element_type=jnp.float32)
    pc_ref[...] = jnp.dot(xm, w_ref[:, split:], preferred_element_type=jnp.float32)


def _inproj(xs, mod, w, *, n_batch, ctx_len):
    t, d = xs.shape
    n_a = D_S5 + 5 * D_HG
    n_c = w.shape[1] - n_a
    tiles = t // ROW_TILE
    return pl.pallas_call(
        _inproj_kernel,
        grid=(tiles,),
        in_specs=[pl.BlockSpec((ROW_TILE, d), lambda i: (i, 0)),
                  _mod_spec(n_batch, tiles // n_batch, ctx_len // ROW_TILE),
                  pl.BlockSpec((d, n_a + n_c), lambda i: (0, 0))],
        out_specs=[pl.BlockSpec((ROW_TILE, n_a), lambda i: (i, 0)), pl.BlockSpec((ROW_TILE, n_c), lambda i: (i, 0))],
        out_shape=[jax.ShapeDtypeStruct((t, n_a), jnp.float32), jax.ShapeDtypeStruct((t, n_c), jnp.float32)],
        compiler_params=pltpu.CompilerParams(dimension_semantics=("parallel",),
                                             vmem_limit_bytes=48 * 1024 * 1024),
        name="in_proj",
    )(xs, mod, w.astype(_BF))


def _ln_rows(h, g, b):
    mu = jnp.mean(h, axis=-1, keepdims=True)
    hc = h - mu
    var = jnp.mean(hc * hc, axis=-1, keepdims=True)
    return hc * lax.rsqrt(var + LN_EPS) * g + b


def _outproj_ln_kernel(ya_ref, yb_ref, yc_ref, w_ref, xs_ref, mod_ref, ln_ref, x1_ref, h2_ref):
    d = D_MODEL
    mix = (jnp.dot(ya_ref[...].astype(_BF), w_ref[0:D_S5], preferred_element_type=jnp.float32)
           + jnp.dot(yb_ref[...].astype(_BF), w_ref[D_S5:D_S5 + D_HG], preferred_element_type=jnp.float32)
           + jnp.dot(yc_ref[...].astype(_BF), w_ref[D_S5 + D_HG:D_MIX], preferred_element_type=jnp.float32))
    m = mod_ref[0]
    x1 = _ln_rows(DEEPNORM_ALPHA * xs_ref[...] + m[:, 2 * d:3 * d] * mix, ln_ref[0:1], ln_ref[1:2])
    x1_ref[...] = x1
    h2_ref[...] = x1 * (1.0 + m[:, 4 * d:5 * d]) + m[:, 3 * d:4 * d]


def _outproj_ln(ya, yb, yc, w, xs, mod, ln_g, ln_b, *, n_batch, ctx_len):
    t, d = xs.shape
    tiles = t // ROW_TILE
    row = lambda n: pl.BlockSpec((ROW_TILE, n), lambda i: (i, 0))
    out = jax.ShapeDtypeStruct((t, d), jnp.float32)
    return pl.pallas_call(
        _outproj_ln_kernel,
        grid=(tiles,),
        in_specs=[row(D_S5), row(D_HG), row(D_RW), pl.BlockSpec((D_MIX, d), lambda i: (0, 0)), row(d),
                  _mod_spec(n_batch, tiles // n_batch, ctx_len // ROW_TILE),
                  pl.BlockSpec((2, d), lambda i: (0, 0))],
        out_specs=[row(d), row(d)],
        out_shape=[out, out],
        compiler_params=pltpu.CompilerParams(dimension_semantics=("parallel",)),
        name="out_proj_ln",
    )(ya, yb, yc, w.astype(_BF), xs, mod, jnp.stack([ln_g, ln_b]))


def kernel(x, c, ctx, c_ctx, ada_w, ada_b, w_in, w_out, ln1_g, ln1_b, ln2_g, ln2_b,
           s5_lam_re, s5_lam_im, s5_log_step, s5_b_re, s5_b_im, s5_c_re, s5_c_im,
           s5_d, s5_glu_w, s5_glu_b, hgrn_lb_logits, hgrn_norm_g,
           rwkv_mu, rwkv_w0, rwkv_w_up, rwkv_a0, rwkv_a_up, rwkv_g_up, rwkv_k_k, rwkv_k_a,
           rwkv_r_k, rwkv_gn_g, rwkv_gn_b, peer_wq, peer_keys, peer_u, peer_v):
    nb, lx, d = x.shape
    lc = ctx.shape[1]
    l_all = lc + lx
    assert lx % GRID_W == 0 and lc % ROW_TILE == 0 and lx % ROW_TILE == 0 and (nb * l_all) % PEER_TN == 0
    lb_cum = jnp.cumsum(jax.nn.softmax(_f32(hgrn_lb_logits), axis=1), axis=1)
    lb_all = lb_cum - lb_cum[:, :1]
    xs = jnp.concatenate([ctx, x], axis=1).reshape(nb * l_all, d)
    seq = lambda t: t.reshape(nb, l_all, t.shape[-1])
    for l in range(DEPTH):
        last = l == DEPTH - 1
        mod = jnp.concatenate([jax.nn.silu(c), jax.nn.silu(c_ctx)[None]], axis=0) @ ada_w[l] + ada_b[l]
        g2 = mod[:, None, 5 * d:]
        mod = mod[:, None, :]
        pa, pc = _inproj(xs, mod, w_in[l], n_batch=nb, ctx_len=lc)
        pa = seq(pa)
        ya = _s5_mixer(pa[..., :D_S5], lc, s5_lam_re[l], s5_lam_im[l], s5_log_step[l],
                       s5_b_re[l], s5_b_im[l], s5_c_re[l], s5_c_im[l], s5_d[l], s5_glu_w[l], s5_glu_b[l])
        yb = _hgrn2_mixer(pa, lc, lb_all[:, l], hgrn_norm_g[l], col0=D_S5 // D_HG)
        yc = _rwkv7_mixer_fused(seq(pc), lc, rwkv_mu[l], rwkv_w0[l], rwkv_w_up[l], rwkv_a0[l], rwkv_a_up[l],
                                rwkv_g_up[l], rwkv_k_k[l], rwkv_k_a[l], rwkv_r_k[l], rwkv_gn_g[l], rwkv_gn_b[l])
        flat = lambda t: t.reshape(nb * l_all, t.shape[-1])
        xs, h2 = _outproj_ln(flat(ya), flat(yb), flat(yc), w_out[l], xs, mod, ln1_g[l], ln1_b[l],
                             n_batch=nb, ctx_len=lc)
        if last:
            xs, h2 = seq(xs)[:, lc:], seq(h2)[:, lc:].reshape(nb * lx, d)
            gate = g2[:nb]
        else:
            xs = seq(xs)
            gate = jnp.where((jnp.arange(l_all) < lc)[None, :, None], g2[nb:], g2[:nb])
        ffn = _peer_ffn_dense(h2, peer_wq[l], peer_keys[l], peer_u[l].astype(_BF), peer_v[l].T.astype(_BF))
        xs = _layer_norm(DEEPNORM_ALPHA * xs + gate * ffn.reshape(xs.shape), ln2_g[l], ln2_b[l])
        xs = xs.reshape(-1, d)
    return xs.reshape(nb, lx, d)
```

```python
import functools
import math

import jax
import jax.numpy as jnp
from jax import lax
from jax.experimental import pallas as pl
from jax.experimental.pallas import tpu as pltpu

D_MODEL = 1024
DEPTH = 2
GRID_W = 64
D_S5 = D_MODEL // 4
D_HG = D_MODEL // 4
D_RW = D_MODEL // 2
D_MIX = D_S5 + D_HG + D_RW
S5_H = 16
S5_G = D_S5 // S5_H
S5_P = 64
HG_HEAD = 64
HG_CHUNK = 16
RW_HEAD = 64
RW_W_LORA = 64
RW_A_LORA = 64
RW_G_LORA = 128
RW_COLS = 3 * D_RW + RW_G_LORA + 2 * RW_W_LORA + 2 * RW_A_LORA
RW_GN_EPS = 64e-5
PEER_HEADS = 8
PEER_NKEYS = 128
PEER_N = PEER_NKEYS * PEER_NKEYS
PEER_QDIM = 256
PEER_TOPK = 16
LN_EPS = 1e-5
RMS_EPS = 1e-6
DEEPNORM_ALPHA = (2.0 * DEPTH) ** 0.25


def _mm_kernel(x_ref, w_ref, o_ref):
    o_ref[...] = jnp.dot(x_ref[...].astype(jnp.bfloat16), w_ref[...],
                         preferred_element_type=jnp.float32)


def _pick_tile(n, cands):
    for c in cands:
        if n % c == 0:
            return c
    return n


def _matmul(x, w):
    m, k = x.shape
    n = w.shape[1]
    tm = _pick_tile(m, (512, 256, 128, 8))
    tn = _pick_tile(n, (1152, 1024, 512, 256, 128))
    return pl.pallas_call(
        _mm_kernel,
        grid=(m // tm, n // tn),
        in_specs=[pl.BlockSpec((tm, k), lambda i, j: (i, 0)),
                  pl.BlockSpec((k, tn), lambda i, j: (0, j))],
        out_specs=pl.BlockSpec((tm, tn), lambda i, j: (i, j)),
        out_shape=jax.ShapeDtypeStruct((m, n), jnp.float32),
        compiler_params=pltpu.CompilerParams(
            dimension_semantics=("parallel", "parallel"),
            vmem_limit_bytes=48 * 1024 * 1024),
    )(x, w.astype(jnp.bfloat16))


def _mm3(x, w):
    b, l, k = x.shape
    return _matmul(x.reshape(b * l, k), w).reshape(b, l, w.shape[1])


_BF = jnp.bfloat16
_NN = (((1,), (0,)), ((), ()))
_NT = (((1,), (1,)), ((), ()))
_TN = (((0,), (0,)), ((), ()))


def _dot(a, b, dims=_NN):
    return lax.dot_general(a.astype(_BF), b.astype(_BF), dims,
                           preferred_element_type=jnp.float32)


def _split_bf16(a):
    hi = a.astype(_BF)
    return hi, (a - hi.astype(jnp.float32)).astype(_BF)


def _dot3(a, b, dims=_NN):
    a_hi, a_lo = _split_bf16(a)
    b_hi, b_lo = _split_bf16(b)
    d = functools.partial(lax.dot_general, dimension_numbers=dims,
                          preferred_element_type=jnp.float32)
    return d(a_hi, b_hi) + (d(a_hi, b_lo) + d(a_lo, b_hi))


RW_CHUNK = 64


def _scan_block_order(n_ctx, n_all, reverse):
    if not reverse:
        return lambda c: c
    return lambda c: jnp.where(c < n_ctx, n_ctx - 1 - c, n_all - 1 - c + n_ctx)


PEER_TN = 512
PEER_EBLK = 1024
_NEG = -3.0e38


def _top_rows(s, n):
    vals = []
    for _ in range(n):
        m = jnp.max(s, axis=0, keepdims=True)
        vals.append(m)
        s = jnp.where(s == m, _NEG, s)
    return vals


def _peer_route_kernel(h_ref, wq_ref, keys_ref, xb_ref, pi_ref, c1_ref, p2_ref):
    hb = h_ref[...].astype(_BF)
    xb_ref[...] = hb
    q = jnp.dot(hb, wq_ref[...], preferred_element_type=jnp.float32)
    half = PEER_QDIM // 2
    for h in range(PEER_HEADS):
        s1 = _dot(keys_ref[2 * h], q[:, (2 * h) * half:(2 * h + 1) * half], _NT)
        s2 = _dot(keys_ref[2 * h + 1], q[:, (2 * h + 1) * half:(2 * h + 2) * half], _NT)
        t1 = _top_rows(s1, PEER_TOPK + 1)
        t2 = _top_rows(s2, PEER_TOPK + 1)
        cand = [t1[j1] + t2[j2] for j1 in range(PEER_TOPK + 1) for j2 in range(PEER_TOPK + 1)
                if (j1 + 1) * (j2 + 1) <= PEER_TOPK + 1]
        cand += [jnp.full_like(cand[0], _NEG)] * (-len(cand) % 8)
        top = _top_rows(jnp.concatenate(cand, axis=0), PEER_TOPK + 1)
        theta = 0.5 * (top[PEER_TOPK - 1] + top[PEER_TOPK])
        z = top[0] * 0.0
        for c in top[:PEER_TOPK]:
            z = z + jnp.exp(c - top[0])
        pi_ref[h] = jnp.exp(jnp.maximum(theta - s1 - t2[0], -80.0))
        c1_ref[h] = jnp.exp(s1 - t1[0]) / z
        p2_ref[h] = jnp.exp(s2 - t2[0])


def _peer_route(hf, wq, keys):
    t, d = hf.shape
    tn = PEER_TN
    nq = PEER_HEADS * PEER_QDIM
    aux = jax.ShapeDtypeStruct((PEER_HEADS, PEER_NKEYS, t), jnp.float32)
    aux_spec = pl.BlockSpec((PEER_HEADS, PEER_NKEYS, tn), lambda i: (0, 0, i))
    return pl.pallas_call(
        _peer_route_kernel,
        grid=(t // tn,),
        in_specs=[pl.BlockSpec((tn, d), lambda i: (i, 0)),
                  pl.BlockSpec((d, nq), lambda i: (0, 0)),
                  pl.BlockSpec((PEER_HEADS * 2, PEER_NKEYS, PEER_QDIM // 2), lambda i: (0, 0, 0))],
        out_specs=[pl.BlockSpec((tn, d), lambda i: (i, 0))] + [aux_spec] * 3,
        out_shape=[jax.ShapeDtypeStruct((t, d), _BF)] + [aux] * 3,
        compiler_params=pltpu.CompilerParams(dimension_semantics=("parallel",),
                                             vmem_limit_bytes=56 * 1024 * 1024),
        name="peer_route",
    )(hf, wq.astype(_BF), keys.reshape(PEER_HEADS * 2, PEER_NKEYS, PEER_QDIM // 2))


def _gelu_tanh(z):
    c = math.sqrt(2.0 / math.pi)
    hz = 0.5 * z
    return hz + hz * jnp.tanh(z * (c + (c * 0.044715) * (z * z)))


def _peer_expert_kernel(x_ref, u_ref, vt_ref, pi_ref, c1_ref, p2_ref, o_ref, acc_ref, at_ref):
    j = pl.program_id(1)

    @pl.when(j == 0)
    def _():
        acc_ref[...] = jnp.zeros_like(acc_ref)

    nk = PEER_NKEYS
    n_slab = PEER_EBLK // nk
    rows = pl.ds(pl.multiple_of(j * n_slab, n_slab), n_slab)
    zt = lax.dot_general(u_ref[...], x_ref[...], _NT, preferred_element_type=jnp.float32)
    for s in range(n_slab):
        for lt in range(PEER_TN // 128):
            lanes = pl.ds(lt * 128, 128)
            g = jnp.zeros((nk, 128), jnp.float32)
            for h in range(PEER_HEADS):
                pi = pi_ref[h, rows, lanes][s:s + 1]
                c1 = c1_ref[h, rows, lanes][s:s + 1]
                p2 = p2_ref[h, :, lanes]
                g = g + jnp.where(p2 >= pi, p2 * c1, 0.0)
            z = zt[s * nk:(s + 1) * nk, lt * 128:(lt + 1) * 128]
            at_ref[s * nk:(s + 1) * nk, lt * 128:(lt + 1) * 128] = (g * _gelu_tanh(z)).astype(_BF)
    acc_ref[...] += jnp.dot(vt_ref[...], at_ref[...], preferred_element_type=jnp.float32)

    @pl.when(j == pl.num_programs(1) - 1)
    def _():
        o_ref[...] = acc_ref[...].T


def _peer_expert(xb, u_b, vt_b, aux):
    t, d = xb.shape
    tn, eb = PEER_TN, PEER_EBLK
    aux_spec = pl.BlockSpec((PEER_HEADS, PEER_NKEYS, tn), lambda i, j: (0, 0, i))
    return pl.pallas_call(
        _peer_expert_kernel,
        grid=(t // tn, PEER_N // eb),
        in_specs=[pl.BlockSpec((tn, d), lambda i, j: (i, 0)),
                  pl.BlockSpec((eb, d), lambda i, j: (j, 0)),
                  pl.BlockSpec((d, eb), lambda i, j: (0, j))] + [aux_spec] * 3,
        out_specs=pl.BlockSpec((tn, d), lambda i, j: (i, 0)),
        out_shape=jax.ShapeDtypeStruct((t, d), jnp.float32),
        scratch_shapes=[pltpu.VMEM((d, tn), jnp.float32), pltpu.VMEM((eb, tn), _BF)],
        compiler_params=pltpu.CompilerParams(dimension_semantics=("parallel", "arbitrary"),
                                             vmem_limit_bytes=56 * 1024 * 1024),
        name="peer_expert",
    )(xb, u_b, vt_b, *aux)


def _peer_ffn_dense(hf, wq, keys, u_b, vt_b):
    xb, *aux = _peer_route(hf, wq, keys)
    return _peer_expert(xb, u_b, vt_b, aux)


S5_STEPS = 128
_S5_ROWS = 8
_S5_STATE = S5_G * S5_P


def _s5_weights(disc, c_re, c_im):
    eye = jnp.eye(S5_G, dtype=jnp.float32)

    def bdiag_in(bb):
        return jnp.einsum('gph,gk->ghkp', bb, eye).reshape(D_S5, _S5_STATE)

    def bdiag_out(cc):
        return jnp.einsum('ghp,gk->gpkh', cc, eye).reshape(_S5_STATE, D_S5)

    wb = jnp.stack([jnp.concatenate([bdiag_in(disc[d][2]), bdiag_in(disc[d][3])], axis=1) for d in range(2)])
    wc = jnp.stack([jnp.concatenate([bdiag_out(c_re[d]), -bdiag_out(c_im[d])], axis=0) for d in range(2)])
    half = _S5_ROWS // 2
    lam = jnp.stack([
        jnp.concatenate([jnp.broadcast_to(disc[d][i].reshape(1, _S5_STATE), (half, _S5_STATE))
                         for d in range(2)], axis=0) for i in range(2)])
    return wb.astype(_BF), lam, wc.astype(_BF)


def _s5_dir_kernel(uf_ref, ub_ref, wb_ref, lam_ref, wc_ref, yf_ref, yb_ref, buf_ref, bub_ref, xf_ref, xb_ref, st_ref):
    @pl.when(pl.program_id(0) == 0)
    def _():
        st_ref[...] = jnp.zeros_like(st_ref)

    n = _S5_STATE
    buf_ref[...] = jnp.dot(uf_ref[...].astype(_BF), wb_ref[0], preferred_element_type=jnp.float32)
    bub_ref[...] = jnp.dot(ub_ref[...].astype(_BF), wb_ref[1], preferred_element_type=jnp.float32)
    lr = lam_ref[0]
    li = lam_ref[1]
    fwd = lax.broadcasted_iota(jnp.int32, (_S5_ROWS, n), 0) < (_S5_ROWS // 2)

    def step(i, carry):
        xr, xi = carry
        rf = pl.ds(pl.multiple_of(i * _S5_ROWS, _S5_ROWS), _S5_ROWS)
        rb = pl.ds(pl.multiple_of((S5_STEPS - 1 - i) * _S5_ROWS, _S5_ROWS), _S5_ROWS)
        nr = lr * xr - li * xi + jnp.where(fwd, buf_ref[rf, :n], bub_ref[rb, :n])
        ni = lr * xi + li * xr + jnp.where(fwd, buf_ref[rf, n:], bub_ref[rb, n:])
        xf_ref[rf, :n] = nr
        xf_ref[rf, n:] = ni
        xb_ref[rb, :n] = nr
        xb_ref[rb, n:] = ni
        return nr, ni

    xr, xi = lax.fori_loop(0, S5_STEPS, step, (st_ref[0], st_ref[1]), unroll=4)
    st_ref[0] = xr
    st_ref[1] = xi
    yf_ref[...] = jnp.dot(xf_ref[...].astype(_BF), wc_ref[0], preferred_element_type=jnp.float32)
    yb_ref[...] = jnp.dot(xb_ref[...].astype(_BF), wc_ref[1], preferred_element_type=jnp.float32)


def _s5_dir_scan(u8, wb, lam, wc, *, ctx_len):
    rows = u8.shape[0]
    blk = S5_STEPS * _S5_ROWS
    n = _S5_STATE
    order = _scan_block_order(ctx_len // S5_STEPS, rows // blk, True)
    spec_f = pl.BlockSpec((blk, D_S5), lambda i: (i, 0))
    spec_b = pl.BlockSpec((blk, D_S5), lambda i: (order(i), 0))
    out = jax.ShapeDtypeStruct((rows, D_S5), jnp.float32)
    big = pltpu.VMEM((blk, 2 * n), jnp.float32)
    return pl.pallas_call(
        _s5_dir_kernel,
        grid=(rows // blk,),
        in_specs=[spec_f, spec_b,
                  pl.BlockSpec((2, D_S5, 2 * n), lambda i: (0, 0, 0)),
                  pl.BlockSpec((2, _S5_ROWS, n), lambda i: (0, 0, 0)),
                  pl.BlockSpec((2, 2 * n, D_S5), lambda i: (0, 0, 0))],
        out_specs=[spec_f, spec_b],
        out_shape=[out, out],
        scratch_shapes=[big, big, big, big, pltpu.VMEM((2, _S5_ROWS, n), jnp.float32)],
        compiler_params=pltpu.CompilerParams(dimension_semantics=("arbitrary",),
                                             vmem_limit_bytes=56 * 1024 * 1024),
        name="s5_scan",
    )(u8, u8, wb, lam, wc)


HG_BLOCK = 128
HG_BATCH_BLOCK = 2
_HG_PAIR = 2 * HG_HEAD


def _hgrn_dir_kernel(q_ref, v_ref, f_ref, lb_ref, o_ref, st_ref, *, reverse):
    @pl.when(pl.program_id(1) == 0)
    def _():
        st_ref[...] = jnp.zeros_like(st_ref)

    ch = HG_CHUNK
    n_ch = HG_BLOCK // ch
    r = lax.broadcasted_iota(jnp.int32, (HG_BLOCK, HG_BLOCK), 0)
    c = lax.broadcasted_iota(jnp.int32, (HG_BLOCK, HG_BLOCK), 1)
    tri = ((r // ch == c // ch) & ((r <= c) if reverse else (r >= c))).astype(jnp.float32)
    same_head = (lax.broadcasted_iota(jnp.int32, (_HG_PAIR, _HG_PAIR), 0) // HG_HEAD
                 == lax.broadcasted_iota(jnp.int32, (_HG_PAIR, _HG_PAIR), 1) // HG_HEAD)
    ones_blk = same_head.astype(_BF)
    tcol = lax.broadcasted_iota(jnp.int32, (ch, 1), 0)
    end = 0 if reverse else ch - 1
    n_pair = D_HG // _HG_PAIR
    chains = [(bi, p) for bi in range(q_ref.shape[0]) for p in range(n_pair)]
    pair_lanes = lambda p: slice(p * _HG_PAIR, (p + 1) * _HG_PAIR)
    intra, kv, dec, qd = {}, {}, {}, {}
    for bi in range(q_ref.shape[0]):
        q_raw = q_ref[bi]
        q_all = q_raw * _sigmoid(q_raw)
        forget = lb_ref[...] + (1.0 - lb_ref[...]) * _sigmoid(f_ref[bi])
        k_all = 1.0 - forget
        cum_all = _dot3(tri, jnp.log(forget))
        for p in range(n_pair):
            lanes = pair_lanes(p)
            for ci in range(n_ch):
                rows = slice(ci * ch, (ci + 1) * ch)
                q = q_all[rows, lanes]
                k = k_all[rows, lanes]
                v = v_ref[bi, rows, lanes]
                cum = cum_all[rows, lanes]
                last = cum[end:end + 1]
                w = [q * jnp.exp(jnp.minimum(cum - cum[s:s + 1], 0.0)) * k[s:s + 1] for s in range(ch)]
                rel = jnp.dot(jnp.concatenate(w, axis=0).astype(_BF), ones_blk,
                              preferred_element_type=jnp.float32)
                o = jnp.zeros((ch, _HG_PAIR), jnp.float32)
                for s in range(ch):
                    seen = (tcol <= s) if reverse else (tcol >= s)
                    o = o + jnp.where(seen, rel[s * ch:(s + 1) * ch], 0.0) * v[s:s + 1]
                intra[bi, p, ci] = o
                kv[bi, p, ci] = jnp.where(same_head, _dot(v, k * jnp.exp(last - cum), _TN), 0.0)
                dec[bi, p, ci] = jnp.exp(last)
                qd[bi, p, ci] = q * jnp.exp(cum)
    state = [st_ref[i] for i in range(len(chains))]
    for ci in (range(n_ch - 1, -1, -1) if reverse else range(n_ch)):
        for i, (bi, p) in enumerate(chains):
            o_ref[bi, ci * ch:(ci + 1) * ch, pair_lanes(p)] = intra[bi, p, ci] + _dot(qd[bi, p, ci], state[i], _NT)
            state[i] = state[i] * dec[bi, p, ci] + kv[bi, p, ci]
    for i in range(len(chains)):
        st_ref[i] = state[i]


def _hgrn_dir_scan(p, lb, *, ctx_len, reverse, col0=0):
    g, l, _ = p.shape
    d = D_HG
    order = _scan_block_order(ctx_len // HG_BLOCK, l // HG_BLOCK, reverse)
    bb = HG_BATCH_BLOCK if g % HG_BATCH_BLOCK == 0 else 1
    col = lambda c: pl.BlockSpec((bb, HG_BLOCK, d), lambda i, j: (i, order(j), col0 + c))
    dr = 1 if reverse else 0
    return pl.pallas_call(
        functools.partial(_hgrn_dir_kernel, reverse=reverse),
        grid=(g // bb, l // HG_BLOCK),
        in_specs=[col(0), col(1), col(2 + dr), pl.BlockSpec((1, d), lambda i, j: (0, 0))],
        out_specs=pl.BlockSpec((bb, HG_BLOCK, d), lambda i, j: (i, order(j), 0)),
        out_shape=jax.ShapeDtypeStruct((g, l, d), jnp.float32),
        scratch_shapes=[pltpu.VMEM((bb * d // _HG_PAIR, _HG_PAIR, _HG_PAIR), jnp.float32)],
        compiler_params=pltpu.CompilerParams(dimension_semantics=("parallel", "arbitrary")),
        name="hgrn_scan",
    )(p, p, p, lb[dr:dr + 1])


def _hgrn_post_kernel(of_ref, ob_ref, g_ref, ng_ref, ones_ref, y_ref):
    o = of_ref[0] + ob_ref[0]
    ms = jnp.dot((o * o).astype(_BF), ones_ref[...], preferred_element_type=jnp.float32) * (1.0 / HG_HEAD)
    g = g_ref[0]
    y_ref[0] = o * lax.rsqrt(ms + RMS_EPS) * ng_ref[...] * (g * _sigmoid(g))


def _hgrn_post(o_f, o_b, p, norm_g, col0=0):
    nb, l, d = o_f.shape
    spec = pl.BlockSpec((1, HG_BLOCK, d), lambda i, j: (i, j, 0))
    return pl.pallas_call(
        _hgrn_post_kernel,
        grid=(nb, l // HG_BLOCK),
        in_specs=[spec, spec, pl.BlockSpec((1, HG_BLOCK, d), lambda i, j: (i, j, col0 + 4)),
                  pl.BlockSpec((1, d), lambda i, j: (0, 0)), pl.BlockSpec((d, d), lambda i, j: (0, 0))],
        out_specs=spec,
        out_shape=jax.ShapeDtypeStruct((nb, l, d), jnp.float32),
        compiler_params=pltpu.CompilerParams(dimension_semantics=("parallel", "parallel")),
        name="hgrn_post",
    )(o_f, o_b, p, norm_g.reshape(1, d), _head_ones(d, HG_HEAD))


_RW_PAIR = 2 * RW_HEAD
RW_PRE_T = 128
RW_BATCH_BLOCK = 4


def _rwkv_pair_kernel(r_ref, lw_ref, k_ref, v_ref, kk_ref, a_ref, y_ref, st_ref, *, chunk, reverse):
    @pl.when(pl.program_id(1) == 0)
    def _():
        st_ref[...] = jnp.zeros_like(st_ref)

    row = lax.broadcasted_iota(jnp.int32, (chunk, chunk), 0)
    col = lax.broadcasted_iota(jnp.int32, (chunk, chunk), 1)
    strict = row < col if reverse else row > col
    incl = row <= col if reverse else row >= col
    end = 0 if reverse else chunk - 1
    tri = incl.astype(jnp.float32)
    eye = (row == col).astype(jnp.float32)
    blk8 = row // 8 == col // 8
    merge_masks = []
    size = 8
    while size < chunk:
        merge_masks.append((row // (2 * size) == col // (2 * size)) & (row // size != col // size))
        size *= 2
    lane_head = lax.broadcasted_iota(jnp.int32, (1, _RW_PAIR), 1) // RW_HEAD
    head0 = lane_head == 0
    same_head = (lax.broadcasted_iota(jnp.int32, (_RW_PAIR, _RW_PAIR), 0) // RW_HEAD
                 == lax.broadcasted_iota(jnp.int32, (_RW_PAIR, _RW_PAIR), 1) // RW_HEAD)
    n_pair = D_RW // _RW_PAIR
    ps = range(r_ref.shape[0] * n_pair)
    ph = [(p, h) for p in ps for h in range(2)]
    bi = [p // n_pair for p in ps]
    sl = [slice((p % n_pair) * _RW_PAIR, (p % n_pair + 1) * _RW_PAIR) for p in ps]
    lw = [lw_ref[bi[p], :, sl[p]] for p in ps]
    cum = [_dot3(tri, lw[p]) for p in ps]
    pw = [jnp.exp(cum[p]) for p in ps]
    p_inv = [jnp.exp(-cum[p]) for p in ps]
    p_end = [pw[p][end:end + 1, :] for p in ps]
    kk = [kk_ref[bi[p], :, sl[p]] for p in ps]
    bh = [kk[p] * a_ref[bi[p], :, sl[p]] * p_inv[p] for p in ps]
    kh = [k_ref[bi[p], :, sl[p]] * p_inv[p] for p in ps]
    v = [v_ref[bi[p], :, sl[p]] for p in ps]
    s0 = [st_ref[p] for p in ps]
    x1 = [jnp.concatenate([-kk[p] * jnp.exp(cum[p] - lw[p]), r_ref[bi[p], :, sl[p]] * pw[p]], axis=0) for p in ps]
    x1h = {(p, h): jnp.where(lane_head == h, x1[p], 0.0) for p, h in ph}
    g1 = {q: _dot(x1h[q], bh[q[0]], _NT) for q in ph}
    g2 = {q: _dot(x1h[q], kh[q[0]], _NT) for q in ph}
    hm = [_dot(x1[p], s0[p], _NT) for p in ps]
    ab = {q: jnp.where(strict, g1[q][:chunk], 0.0) for q in ph}
    d1 = {q: jnp.where(blk8, ab[q], 0.0) for q in ph}
    d2 = {q: _dot3(d1[q], d1[q]) for q in ph}
    inv = {q: eye + d1[q] for q in ph}
    inv = {q: inv[q] + _dot3(inv[q], d2[q]) for q in ph}
    d4 = {q: _dot3(d2[q], d2[q]) for q in ph}
    inv = {q: inv[q] + _dot3(inv[q], d4[q]) for q in ph}
    for m in merge_masks:
        li = {q: _dot(jnp.where(m, ab[q], 0.0), inv[q]) for q in ph}
        inv = {q: inv[q] + _dot(inv[q], li[q]) for q in ph}
    akv = {q: _dot(jnp.where(strict, g2[q][:chunk], 0.0), v[q[0]]) for q in ph}
    rhs = [hm[p][:chunk] + jnp.where(head0, akv[p, 0], akv[p, 1]) for p in ps]
    eh = {q: _dot(inv[q], rhs[q[0]]) for q in ph}
    e = [jnp.where(head0, eh[p, 0], eh[p, 1]) for p in ps]
    yh = {q: _dot(jnp.where(incl, g1[q][chunk:], 0.0), e[q[0]])
          + _dot(jnp.where(incl, g2[q][chunk:], 0.0), v[q[0]]) for q in ph}
    for p in ps:
        y_ref[bi[p], :, sl[p]] = hm[p][chunk:] + jnp.where(head0, yh[p, 0], yh[p, 1])
    for p in ps:
        ev = jnp.concatenate([e[p], v[p]], axis=0)
        x2 = jnp.concatenate([bh[p] * p_end[p], kh[p] * p_end[p]], axis=0)
        st_ref[p] = s0[p] * p_end[p] + jnp.where(same_head, _dot(ev, x2, _TN), 0.0)


def _rwkv_pair_scan(r, lw, k, v, kk, a, *, ctx_len, reverse):
    nb, l, d = r.shape
    chunk = RW_CHUNK
    order = _scan_block_order(ctx_len // chunk, l // chunk, reverse)
    bb = RW_BATCH_BLOCK if nb % RW_BATCH_BLOCK == 0 else 1
    spec = pl.BlockSpec((bb, chunk, d), lambda b, c: (b, order(c), 0))
    return pl.pallas_call(
        functools.partial(_rwkv_pair_kernel, chunk=chunk, reverse=reverse),
        grid=(nb // bb, l // chunk),
        in_specs=[spec] * 6,
        out_specs=spec,
        out_shape=jax.ShapeDtypeStruct((nb, l, d), jnp.float32),
        scratch_shapes=[pltpu.VMEM((bb * d // _RW_PAIR, _RW_PAIR, _RW_PAIR), jnp.float32)],
        compiler_params=pltpu.CompilerParams(dimension_semantics=("parallel", "arbitrary")),
        name="rwkv_scan",
    )(r, lw, k, v, kk, a)


def _softplus(z):
    return jnp.maximum(z, 0.0) + jnp.log(1.0 + jnp.exp(-jnp.abs(z)))


def _sigmoid(z):
    return 1.0 / (1.0 + jnp.exp(-z))


def _rwkv_pre_kernel(prev_ref, cur_ref, next_ref, mu_ref, vec_ref, gup_ref, wup_ref, aup_ref, ones_ref,
                     r_ref, v_ref, kk_ref, g_ref, bonus_ref, lw0_ref, k0_ref, a0_ref, lw1_ref, k1_ref, a1_ref,
                     ext_ref, s_ref, *, ctx_len, seq_len):
    t_rows, w = RW_PRE_T, GRID_W
    ext_ref[0:w] = prev_ref[0]
    ext_ref[w:w + t_rows] = cur_ref[0]
    ext_ref[w + t_rows:w + t_rows + w] = next_ref[0]
    t = pl.program_id(1) * t_rows + lax.broadcasted_iota(jnp.int32, (t_rows, 128), 0)
    lane4 = lax.broadcasted_iota(jnp.int32, (t_rows, 128), 1) % 4
    is_ctx = t < ctx_len
    tx = t - ctx_len
    col = tx % w
    is_lat = jnp.logical_not(is_ctx)
    use_p1 = (is_ctx & (lane4 % 2 == 0) & (t > 0)) | (is_lat & (lane4 == 0) & (col != 0))
    use_n1 = (is_ctx & (lane4 % 2 == 1) & (t < ctx_len - 1)) | (is_lat & (lane4 == 1) & (col != w - 1))
    use_p64 = is_lat & (lane4 == 2) & (tx >= w)
    use_n64 = is_lat & (lane4 == 3) & (tx < seq_len - w)
    for lt in range(RW_COLS // 128):
        lanes = slice(lt * 128, (lt + 1) * 128)
        p = ext_ref[w:w + t_rows, lanes]
        shifted = jnp.where(use_p1, ext_ref[w - 1:w - 1 + t_rows, lanes],
                            jnp.where(use_n1, ext_ref[w + 1:w + 1 + t_rows, lanes],
                                      jnp.where(use_p64, ext_ref[0:t_rows, lanes],
                                                jnp.where(use_n64, ext_ref[2 * w:2 * w + t_rows, lanes], 0.0))))
        s_ref[:, lanes] = p + mu_ref[:, lanes] * (shifted - p)
    d = D_RW
    r = s_ref[:, 0:d]
    k = s_ref[:, d:2 * d]
    v = s_ref[:, 2 * d:3 * d]
    o = 3 * d
    g_lo = s_ref[:, o:o + RW_G_LORA]
    wl = s_ref[:, o + RW_G_LORA:o + RW_G_LORA + 2 * RW_W_LORA]
    al = s_ref[:, o + RW_G_LORA + 2 * RW_W_LORA:o + RW_G_LORA + 2 * RW_W_LORA + 2 * RW_A_LORA]
    k_k, k_a, r_k = vec_ref[0:1], vec_ref[1:2], vec_ref[2:3]
    ones = ones_ref[...]
    r_ref[0] = r
    v_ref[0] = v
    g_ref[0] = jnp.dot(_sigmoid(g_lo).astype(_BF), gup_ref[...], preferred_element_type=jnp.float32)
    kkr = k * k_k
    ss = jnp.dot((kkr * kkr).astype(_BF), ones, preferred_element_type=jnp.float32)
    kk_ref[0] = kkr / jnp.maximum(jnp.sqrt(ss), 1e-12)
    w_pre = jnp.dot(jnp.tanh(wl).astype(_BF), wup_ref[...], preferred_element_type=jnp.float32)
    a_pre = jnp.dot(al.astype(_BF), aup_ref[...], preferred_element_type=jnp.float32)
    k_sum = 0.0
    for dr, (lw_o, k_o, a_o) in enumerate(((lw0_ref, k0_ref, a0_ref), (lw1_ref, k1_ref, a1_ref))):
        w_log = -_softplus(-(vec_ref[3 + dr:4 + dr] + w_pre[:, dr * d:(dr + 1) * d])) - 0.5
        a = _sigmoid(vec_ref[5 + dr:6 + dr] + a_pre[:, dr * d:(dr + 1) * d])
        k_d = k * (1.0 + (a - 1.0) * k_a)
        lw_o[0] = -jnp.exp(w_log)
        k_o[0] = k_d
        a_o[0] = a
        k_sum = k_sum + k_d
    bonus_ref[0] = jnp.dot((r * k_sum * r_k).astype(_BF), ones, preferred_element_type=jnp.float32)


def _head_ones(width, head):
    i = jnp.arange(width) // head
    return (i[:, None] == i[None, :]).astype(_BF)


def _rwkv_pre(p, mu, k_k, k_a, r_k, w0, a0, g_up, w_up, a_up, *, ctx_len):
    nb, l, cols = p.shape
    t_rows, w, d = RW_PRE_T, GRID_W, D_RW
    nblk = l // w
    per = t_rows // w
    zeros = jnp.zeros((RW_W_LORA, d), jnp.float32)
    wup = jnp.concatenate([jnp.concatenate([w_up[0], zeros], axis=1),
                           jnp.concatenate([zeros, w_up[1]], axis=1)], axis=0).astype(_BF)
    aup = jnp.concatenate([jnp.concatenate([a_up[0], zeros], axis=1),
                           jnp.concatenate([zeros, a_up[1]], axis=1)], axis=0).astype(_BF)
    vec = jnp.stack([k_k, k_a, r_k.reshape(d), w0[0], w0[1], a0[0], a0[1], jnp.zeros((d,), jnp.float32)])
    out = jax.ShapeDtypeStruct((nb, l, d), jnp.float32)
    out_spec = pl.BlockSpec((1, t_rows, d), lambda b, j: (b, j, 0))
    full = lambda shape: pl.BlockSpec(shape, lambda b, j: (0,) * len(shape))
    return pl.pallas_call(
        functools.partial(_rwkv_pre_kernel, ctx_len=ctx_len, seq_len=l - ctx_len),
        grid=(nb, l // t_rows),
        in_specs=[pl.BlockSpec((1, w, cols), lambda b, j: (b, jnp.maximum(j * per - 1, 0), 0)),
                  pl.BlockSpec((1, t_rows, cols), lambda b, j: (b, j, 0)),
                  pl.BlockSpec((1, w, cols), lambda b, j: (b, jnp.minimum(j * per + per, nblk - 1), 0)),
                  full((1, cols)), full((8, d)), full((RW_G_LORA, d)),
                  full((2 * RW_W_LORA, 2 * d)), full((2 * RW_A_LORA, 2 * d)), full((d, d))],
        out_specs=[out_spec] * 11,
        out_shape=[out] * 11,
        scratch_shapes=[pltpu.VMEM((t_rows + 2 * w, cols), jnp.float32), pltpu.VMEM((t_rows, cols), jnp.float32)],
        compiler_params=pltpu.CompilerParams(dimension_semantics=("parallel", "parallel"),
                                             vmem_limit_bytes=48 * 1024 * 1024),
        name="rwkv_pre",
    )(p, p, p, mu.reshape(1, cols), vec, g_up.astype(_BF), wup, aup, _head_ones(d, RW_HEAD))


def _rwkv_post_kernel(y0_ref, y1_ref, v_ref, g_ref, bonus_ref, gn_ref, ones_ref, o_ref):
    y = y0_ref[0] + y1_ref[0]
    ones = ones_ref[...]
    inv_n = 1.0 / RW_HEAD
    mu_y = jnp.dot(y.astype(_BF), ones, preferred_element_type=jnp.float32) * inv_n
    yc = y - mu_y
    var_y = jnp.dot((yc * yc).astype(_BF), ones, preferred_element_type=jnp.float32) * inv_n
    yn = yc * lax.rsqrt(var_y + RW_GN_EPS) * gn_ref[0:1] + gn_ref[1:2]
    o_ref[0] = (yn + bonus_ref[0] * v_ref[0]) * g_ref[0]


def _rwkv_post(y0, y1, v, g, bonus, gn_g, gn_b):
    nb, l, d = y0.shape
    spec = pl.BlockSpec((1, RW_PRE_T, d), lambda b, j: (b, j, 0))
    return pl.pallas_call(
        _rwkv_post_kernel,
        grid=(nb, l // RW_PRE_T),
        in_specs=[spec] * 5 + [pl.BlockSpec((2, d), lambda b, j: (0, 0)), pl.BlockSpec((d, d), lambda b, j: (0, 0))],
        out_specs=spec,
        out_shape=jax.ShapeDtypeStruct((nb, l, d), jnp.float32),
        compiler_params=pltpu.CompilerParams(dimension_semantics=("parallel", "parallel")),
        name="rwkv_post",
    )(y0, y1, v, g, bonus, jnp.stack([gn_g, gn_b]), _head_ones(d, RW_HEAD))


def _rwkv7_mixer_fused(p, lc, mu, w0, w_up, a0, a_up, g_up, k_k, k_a, r_k, gn_g, gn_b):
    r, v, kk, g, bonus, lw0, k0, a0_, lw1, k1, a1_ = _rwkv_pre(
        p, _f32(mu), _f32(k_k), _f32(k_a), _f32(r_k), _f32(w0), _f32(a0), _f32(g_up), _f32(w_up), _f32(a_up),
        ctx_len=lc)
    y0 = _rwkv_pair_scan(r, lw0, k0, v, kk, a0_, ctx_len=lc, reverse=False)
    y1 = _rwkv_pair_scan(r, lw1, k1, v, kk, a1_, ctx_len=lc, reverse=True)
    return _rwkv_post(y0, y1, v, g, bonus, _f32(gn_g), _f32(gn_b))


def _f32(t):
    return t.astype(jnp.float32)


def _layer_norm(x, g, b):
    xf = _f32(x)
    mu = jnp.mean(xf, -1, keepdims=True)
    var = jnp.mean(jnp.square(xf - mu), -1, keepdims=True)
    return ((xf - mu) * lax.rsqrt(var + LN_EPS) * _f32(g) + _f32(b)).astype(x.dtype)


def _s5_discretise(lam_re, lam_im, log_step, b_re, b_im):
    dt = jnp.exp(log_step)[:, None]
    mag = jnp.exp(lam_re * dt)
    ang = lam_im * dt
    lb_re, lb_im = mag * jnp.cos(ang), mag * jnp.sin(ang)
    den = lam_re * lam_re + lam_im * lam_im
    nr, ni = lb_re - 1.0, lb_im
    co_re = (nr * lam_re + ni * lam_im) / den
    co_im = (ni * lam_re - nr * lam_im) / den
    bb_re = co_re[..., None] * b_re - co_im[..., None] * b_im
    bb_im = co_re[..., None] * b_im + co_im[..., None] * b_re
    return lb_re, lb_im, bb_re, bb_im


def _s5_mixer(u, lc, lam_re, lam_im, log_step, b_re, b_im, c_re, c_im, d_skip, glu_w, glu_b):
    disc = [_s5_discretise(_f32(lam_re[d]), _f32(lam_im[d]), _f32(log_step[d]),
                           _f32(b_re[d]), _f32(b_im[d])) for d in range(2)]
    wb, lam, wc = _s5_weights(disc, _f32(c_re), _f32(c_im))
    u = _f32(u)
    nb, l, _ = u.shape
    assert 2 * nb == _S5_ROWS
    u_t = u.transpose(1, 0, 2)
    u8 = jnp.concatenate([u_t, u_t], axis=1).reshape(l * _S5_ROWS, D_S5)
    y_f, y_b = _s5_dir_scan(u8, wb, lam, wc, ctx_len=lc)
    y = (y_f.reshape(l, _S5_ROWS, D_S5)[:, :nb] + y_b.reshape(l, _S5_ROWS, D_S5)[:, nb:]).transpose(1, 0, 2)
    y = jax.nn.gelu(y + _f32(d_skip) * u)
    return y * jax.nn.sigmoid(_mm3(y, glu_w) + _f32(glu_b))


def _hgrn2_mixer(p, lc, lb, norm_g, col0=0):
    lb = _f32(lb)
    o_f = _hgrn_dir_scan(p, lb, ctx_len=lc, reverse=False, col0=col0)
    o_b = _hgrn_dir_scan(p, lb, ctx_len=lc, reverse=True, col0=col0)
    return _hgrn_post(o_f, o_b, p, _f32(norm_g), col0=col0)


ROW_TILE = 256


def _mod_spec(n_batch, tiles_per_batch, ctx_tiles):
    def index(i):
        return (jnp.where(i % tiles_per_batch < ctx_tiles, n_batch, i // tiles_per_batch), 0, 0)
    return pl.BlockSpec((1, 1, 6 * D_MODEL), index)


def _inproj_kernel(x_ref, mod_ref, w_ref, pa_ref, pc_ref):
    d = D_MODEL
    m = mod_ref[0]
    xm = (x_ref[...] * (1.0 + m[:, d:2 * d]) + m[:, 0:d]).astype(_BF)
    split = pa_ref.shape[1]
    pa_ref[...] = jnp.dot(xm, w_ref[:, :split], preferred_element_type=jnp.float32)
    pc_ref[...] = jnp.dot(xm, w_ref[:, split:], preferred_element_type=jnp.float32)


def _inproj(xs, mod, w, *, n_batch, ctx_len):
    t, d = xs.shape
    n_a = D_S5 + 5 * D_HG
    n_c = w.shape[1] - n_a
    tiles = t // ROW_TILE
    return pl.pallas_call(
        _inproj_kernel,
        grid=(tiles,),
        in_specs=[pl.BlockSpec((ROW_TILE, d), lambda i: (i, 0)),
                  _mod_spec(n_batch, tiles // n_batch, ctx_len // ROW_TILE),
                  pl.BlockSpec((d, n_a + n_c), lambda i: (0, 0))],
        out_specs=[pl.BlockSpec((ROW_TILE, n_a), lambda i: (i, 0)), pl.BlockSpec((ROW_TILE, n_c), lambda i: (i, 0))],
        out_shape=[jax.ShapeDtypeStruct((t, n_a), jnp.float32), jax.ShapeDtypeStruct((t, n_c), jnp.float32)],
        compiler_params=pltpu.CompilerParams(dimension_semantics=("parallel",),
                                             vmem_limit_bytes=48 * 1024 * 1024),
        name="in_proj",
    )(xs, mod, w.astype(_BF))


def _ln_rows(h, g, b):
    mu = jnp.mean(h, axis=-1, keepdims=True)
    hc = h - mu
    var = jnp.mean(hc * hc, axis=-1, keepdims=True)
    return hc * lax.rsqrt(var + LN_EPS) * g + b


def _outproj_ln_kernel(ya_ref, yb_ref, yc_ref, w_ref, xs_ref, mod_ref, ln_ref, x1_ref, h2_ref):
    d = D_MODEL
    mix = (jnp.dot(ya_ref[...].astype(_BF), w_ref[0:D_S5], preferred_element_type=jnp.float32)
           + jnp.dot(yb_ref[...].astype(_BF), w_ref[D_S5:D_S5 + D_HG], preferred_element_type=jnp.float32)
           + jnp.dot(yc_ref[...].astype(_BF), w_ref[D_S5 + D_HG:D_MIX], preferred_element_type=jnp.float32))
    m = mod_ref[0]
    x1 = _ln_rows(DEEPNORM_ALPHA * xs_ref[...] + m[:, 2 * d:3 * d] * mix, ln_ref[0:1], ln_ref[1:2])
    x1_ref[...] = x1
    h2_ref[...] = x1 * (1.0 + m[:, 4 * d:5 * d]) + m[:, 3 * d:4 * d]


def _outproj_ln(ya, yb, yc, w, xs, mod, ln_g, ln_b, *, n_batch, ctx_len):
    t, d = xs.shape
    tiles = t // ROW_TILE
    row = lambda n: pl.BlockSpec((ROW_TILE, n), lambda i: (i, 0))
    out = jax.ShapeDtypeStruct((t, d), jnp.float32)
    return pl.pallas_call(
        _outproj_ln_kernel,
        grid=(tiles,),
        in_specs=[row(D_S5), row(D_HG), row(D_RW), pl.BlockSpec((D_MIX, d), lambda i: (0, 0)), row(d),
                  _mod_spec(n_batch, tiles // n_batch, ctx_len // ROW_TILE),
                  pl.BlockSpec((2, d), lambda i: (0, 0))],
        out_specs=[row(d), row(d)],
        out_shape=[out, out],
        compiler_params=pltpu.CompilerParams(dimension_semantics=("parallel",)),
        name="out_proj_ln",
    )(ya, yb, yc, w.astype(_BF), xs, mod, jnp.stack([ln_g, ln_b]))


def kernel(x, c, ctx, c_ctx, ada_w, ada_b, w_in, w_out, ln1_g, ln1_b, ln2_g, ln2_b,
           s5_lam_re, s5_lam_im, s5_log_step, s5_b_re, s5_b_im, s5_c_re, s5_c_im,
           s5_d, s5_glu_w, s5_glu_b, hgrn_lb_logits, hgrn_norm_g,
           rwkv_mu, rwkv_w0, rwkv_w_up, rwkv_a0, rwkv_a_up, rwkv_g_up, rwkv_k_k, rwkv_k_a,
           rwkv_r_k, rwkv_gn_g, rwkv_gn_b, peer_wq, peer_keys, peer_u, peer_v):
    nb, lx, d = x.shape
    lc = ctx.shape[1]
    l_all = lc + lx
    assert lx % GRID_W == 0 and lc % ROW_TILE == 0 and lx % ROW_TILE == 0 and (nb * l_all) % PEER_TN == 0
    lb_cum = jnp.cumsum(jax.nn.softmax(_f32(hgrn_lb_logits), axis=1), axis=1)
    lb_all = lb_cum - lb_cum[:, :1]
    xs = jnp.concatenate([ctx, x], axis=1).reshape(nb * l_all, d)
    seq = lambda t: t.reshape(nb, l_all, t.shape[-1])
    for l in range(DEPTH):
        last = l == DEPTH - 1
        mod = jnp.concatenate([jax.nn.silu(c), jax.nn.silu(c_ctx)[None]], axis=0) @ ada_w[l] + ada_b[l]
        g2 = mod[:, None, 5 * d:]
        mod = mod[:, None, :]
        pa, pc = _inproj(xs, mod, w_in[l], n_batch=nb, ctx_len=lc)
        pa = seq(pa)
        ya = _s5_mixer(pa[..., :D_S5], lc, s5_lam_re[l], s5_lam_im[l], s5_log_step[l],
                       s5_b_re[l], s5_b_im[l], s5_c_re[l], s5_c_im[l], s5_d[l], s5_glu_w[l], s5_glu_b[l])
        yb = _hgrn2_mixer(pa, lc, lb_all[:, l], hgrn_norm_g[l], col0=D_S5 // D_HG)
        yc = _rwkv7_mixer_fused(seq(pc), lc, rwkv_mu[l], rwkv_w0[l], rwkv_w_up[l], rwkv_a0[l], rwkv_a_up[l],
                                rwkv_g_up[l], rwkv_k_k[l], rwkv_k_a[l], rwkv_r_k[l], rwkv_gn_g[l], rwkv_gn_b[l])
        flat = lambda t: t.reshape(nb * l_all, t.shape[-1])
        xs, h2 = _outproj_ln(flat(ya), flat(yb), flat(yc), w_out[l], xs, mod, ln1_g[l], ln1_b[l],
                             n_batch=nb, ctx_len=lc)
        if last:
            xs, h2 = seq(xs)[:, lc:], seq(h2)[:, lc:].reshape(nb * lx, d)
            gate = g2[:nb]
        else:
            xs = seq(xs)
            gate = jnp.where((jnp.arange(l_all) < lc)[None, :, None], g2[nb:], g2[:nb])
        ffn = _peer_ffn_dense(h2, peer_wq[l], peer_keys[l], peer_u[l].astype(_BF), peer_v[l].T.astype(_BF))
        xs = _layer_norm(DEEPNORM_ALPHA * xs + gate * ffn.reshape(xs.shape), ln2_g[l], ln2_b[l])
        xs = xs.reshape(-1, d)
    return xs.reshape(nb, lx, d)
```

```python
import functools
import math

import jax
import jax.numpy as jnp
from jax import lax
from jax.experimental import pallas as pl
from jax.experimental.pallas import tpu as pltpu

D_MODEL = 1024
DEPTH = 2
GRID_W = 64
D_S5 = D_MODEL // 4
D_HG = D_MODEL // 4
D_RW = D_MODEL // 2
D_MIX = D_S5 + D_HG + D_RW
S5_H = 16
S5_G = D_S5 // S5_H
S5_P = 64
HG_HEAD = 64
HG_CHUNK = 16
RW_HEAD = 64
RW_W_LORA = 64
RW_A_LORA = 64
RW_G_LORA = 128
RW_COLS = 3 * D_RW + RW_G_LORA + 2 * RW_W_LORA + 2 * RW_A_LORA
RW_GN_EPS = 64e-5
PEER_HEADS = 8
PEER_NKEYS = 128
PEER_N = PEER_NKEYS * PEER_NKEYS
PEER_QDIM = 256
PEER_TOPK = 16
LN_EPS = 1e-5
RMS_EPS = 1e-6
DEEPNORM_ALPHA = (2.0 * DEPTH) ** 0.25


def _mm_kernel(x_ref, w_ref, o_ref):
    o_ref[...] = jnp.dot(x_ref[...].astype(jnp.bfloat16), w_ref[...],
                         preferred_element_type=jnp.float32)


def _pick_tile(n, cands):
    for c in cands:
        if n % c == 0:
            return c
    return n


def _matmul(x, w):
    m, k = x.shape
    n = w.shape[1]
    tm = _pick_tile(m, (512, 256, 128, 8))
    tn = _pick_tile(n, (1152, 1024, 512, 256, 128))
    return pl.pallas_call(
        _mm_kernel,
        grid=(m // tm, n // tn),
        in_specs=[pl.BlockSpec((tm, k), lambda i, j: (i, 0)),
                  pl.BlockSpec((k, tn), lambda i, j: (0, j))],
        out_specs=pl.BlockSpec((tm, tn), lambda i, j: (i, j)),
        out_shape=jax.ShapeDtypeStruct((m, n), jnp.float32),
        compiler_params=pltpu.CompilerParams(
            dimension_semantics=("parallel", "parallel"),
            vmem_limit_bytes=48 * 1024 * 1024),
    )(x, w.astype(jnp.bfloat16))


def _mm3(x, w):
    b, l, k = x.shape
    return _matmul(x.reshape(b * l, k), w).reshape(b, l, w.shape[1])


_BF = jnp.bfloat16
_NN = (((1,), (0,)), ((), ()))
_NT = (((1,), (1,)), ((), ()))
_TN = (((0,), (0,)), ((), ()))


def _dot(a, b, dims=_NN):
    return lax.dot_general(a.astype(_BF), b.astype(_BF), dims,
                           preferred_element_type=jnp.float32)


def _split_bf16(a):
    hi = a.astype(_BF)
    return hi, (a - hi.astype(jnp.float32)).astype(_BF)


def _dot3(a, b, dims=_NN):
    a_hi, a_lo = _split_bf16(a)
    b_hi, b_lo = _split_bf16(b)
    d = functools.partial(lax.dot_general, dimension_numbers=dims,
                          preferred_element_type=jnp.float32)
    return d(a_hi, b_hi) + (d(a_hi, b_lo) + d(a_lo, b_hi))


RW_CHUNK = 64


def _scan_block_order(n_ctx, n_all, reverse):
    if not reverse:
        return lambda c: c
    return lambda c: jnp.where(c < n_ctx, n_ctx - 1 - c, n_all - 1 - c + n_ctx)


PEER_TN = 512
PEER_EBLK = 2048
_PEER_SUB = 256
_PEER_OUT = 512
_NEG = -3.0e38


def _top_rows(s, n):
    vals = []
    for _ in range(n):
        m = jnp.max(s, axis=0, keepdims=True)
        vals.append(m)
        s = jnp.where(s == m, _NEG, s)
    return vals


def _peer_route_kernel(h_ref, wq_ref, keys_ref, xb_ref, pi_ref, c1_ref, p2_ref):
    hb = h_ref[...].astype(_BF)
    xb_ref[...] = hb
    q = jnp.dot(hb, wq_ref[...], preferred_element_type=jnp.float32)
    half = PEER_QDIM // 2
    for h in range(PEER_HEADS):
        s1 = _dot(keys_ref[2 * h], q[:, (2 * h) * half:(2 * h + 1) * half], _NT)
        s2 = _dot(keys_ref[2 * h + 1], q[:, (2 * h + 1) * half:(2 * h + 2) * half], _NT)
        t1 = _top_rows(s1, PEER_TOPK + 1)
        t2 = _top_rows(s2, PEER_TOPK + 1)
        cand = [t1[j1] + t2[j2] for j1 in range(PEER_TOPK + 1) for j2 in range(PEER_TOPK + 1)
                if (j1 + 1) * (j2 + 1) <= PEER_TOPK + 1]
        cand += [jnp.full_like(cand[0], _NEG)] * (-len(cand) % 8)
        top = _top_rows(jnp.concatenate(cand, axis=0), PEER_TOPK + 1)
        theta = 0.5 * (top[PEER_TOPK - 1] + top[PEER_TOPK])
        z = top[0] * 0.0
        for c in top[:PEER_TOPK]:
            z = z + jnp.exp(c - top[0])
        pi_ref[h] = jnp.exp(jnp.maximum(theta - s1 - t2[0], -80.0))
        c1_ref[h] = jnp.exp(s1 - t1[0]) / z
        p2_ref[h] = jnp.exp(s2 - t2[0])


def _peer_route(hf, wq, keys):
    t, d = hf.shape
    tn = PEER_TN
    nq = PEER_HEADS * PEER_QDIM
    aux = jax.ShapeDtypeStruct((PEER_HEADS, PEER_NKEYS, t), jnp.float32)
    aux_spec = pl.BlockSpec((PEER_HEADS, PEER_NKEYS, tn), lambda i: (0, 0, i))
    return pl.pallas_call(
        _peer_route_kernel,
        grid=(t // tn,),
        in_specs=[pl.BlockSpec((tn, d), lambda i: (i, 0)),
                  pl.BlockSpec((d, nq), lambda i: (0, 0)),
                  pl.BlockSpec((PEER_HEADS * 2, PEER_NKEYS, PEER_QDIM // 2), lambda i: (0, 0, 0))],
        out_specs=[pl.BlockSpec((tn, d), lambda i: (i, 0))] + [aux_spec] * 3,
        out_shape=[jax.ShapeDtypeStruct((t, d), _BF)] + [aux] * 3,
        compiler_params=pltpu.CompilerParams(dimension_semantics=("parallel",),
                                             vmem_limit_bytes=56 * 1024 * 1024),
        name="peer_route",
    )(hf, wq.astype(_BF), keys.reshape(PEER_HEADS * 2, PEER_NKEYS, PEER_QDIM // 2))


def _gelu_tanh(z):
    c = math.sqrt(2.0 / math.pi)
    hz = 0.5 * z
    return hz + hz * jnp.tanh(z * (c + (c * 0.044715) * (z * z)))


def _peer_expert_kernel(x_ref, u_ref, vt_ref, pi_ref, c1_ref, p2_ref, o_ref, acc_ref, at_ref):
    j = pl.program_id(1)

    @pl.when(j == 0)
    def _():
        acc_ref[...] = jnp.zeros_like(acc_ref)

    nk = PEER_NKEYS
    n_slab = PEER_EBLK // nk
    rows = pl.ds(pl.multiple_of(j * n_slab, n_slab), n_slab)
    sub = _PEER_SUB
    per = sub // nk

    def scores(sb):
        return lax.dot_general(u_ref[sb * sub:(sb + 1) * sub, :], x_ref[...], _NT,
                               preferred_element_type=jnp.float32)

    zt = scores(0)
    for sb in range(PEER_EBLK // sub):
        zt_next = scores(sb + 1) if (sb + 1) * sub < PEER_EBLK else None
        for k in range(per):
            s = sb * per + k
            for lt in range(PEER_TN // 128):
                lanes = pl.ds(lt * 128, 128)
                g = jnp.zeros((nk, 128), jnp.float32)
                for h in range(PEER_HEADS):
                    pi = pi_ref[h, rows, lanes][s:s + 1]
                    c1 = c1_ref[h, rows, lanes][s:s + 1]
                    p2 = p2_ref[h, :, lanes]
                    g = g + jnp.where(p2 >= pi, p2 * c1, 0.0)
                z = zt[k * nk:(k + 1) * nk, lt * 128:(lt + 1) * 128]
                at_ref[s * nk:(s + 1) * nk, lt * 128:(lt + 1) * 128] = (g * _gelu_tanh(z)).astype(_BF)
        zt = zt_next
        done = (sb + 1) * sub
        if done % _PEER_OUT == 0:
            part = slice(done - _PEER_OUT, done)
            acc_ref[...] += jnp.dot(vt_ref[:, part], at_ref[part, :], preferred_element_type=jnp.float32)

    @pl.when(j == pl.num_programs(1) - 1)
    def _():
        o_ref[...] = acc_ref[...].T


def _peer_expert(xb, u_b, vt_b, aux):
    t, d = xb.shape
    tn, eb = PEER_TN, PEER_EBLK
    aux_spec = pl.BlockSpec((PEER_HEADS, PEER_NKEYS, tn), lambda i, j: (0, 0, i))
    return pl.pallas_call(
        _peer_expert_kernel,
        grid=(t // tn, PEER_N // eb),
        in_specs=[pl.BlockSpec((tn, d), lambda i, j: (i, 0)),
                  pl.BlockSpec((eb, d), lambda i, j: (j, 0)),
                  pl.BlockSpec((d, eb), lambda i, j: (0, j))] + [aux_spec] * 3,
        out_specs=pl.BlockSpec((tn, d), lambda i, j: (i, 0)),
        out_shape=jax.ShapeDtypeStruct((t, d), jnp.float32),
        scratch_shapes=[pltpu.VMEM((d, tn), jnp.float32), pltpu.VMEM((eb, tn), _BF)],
        compiler_params=pltpu.CompilerParams(dimension_semantics=("parallel", "arbitrary"),
                                             vmem_limit_bytes=56 * 1024 * 1024),
        name="peer_expert",
    )(xb, u_b, vt_b, *aux)


def _peer_ffn_dense(hf, wq, keys, u_b, vt_b):
    xb, *aux = _peer_route(hf, wq, keys)
    return _peer_expert(xb, u_b, vt_b, aux)


S5_STEPS = 128
_S5_ROWS = 8
_S5_STATE = S5_G * S5_P


def _s5_weights(disc, c_re, c_im):
    eye = jnp.eye(S5_G, dtype=jnp.float32)

    def bdiag_in(bb):
        return jnp.einsum('gph,gk->ghkp', bb, eye).reshape(D_S5, _S5_STATE)

    def bdiag_out(cc):
        return jnp.einsum('ghp,gk->gpkh', cc, eye).reshape(_S5_STATE, D_S5)

    wb = jnp.stack([jnp.concatenate([bdiag_in(disc[d][2]), bdiag_in(disc[d][3])], axis=1) for d in range(2)])
    wc = jnp.stack([jnp.concatenate([bdiag_out(c_re[d]), -bdiag_out(c_im[d])], axis=0) for d in range(2)])
    half = _S5_ROWS // 2
    lam = jnp.stack([
        jnp.concatenate([jnp.broadcast_to(disc[d][i].reshape(1, _S5_STATE), (half, _S5_STATE))
                         for d in range(2)], axis=0) for i in range(2)])
    return wb.astype(_BF), lam, wc.astype(_BF)


def _s5_dir_kernel(uf_ref, ub_ref, wb_ref, lam_ref, wc_ref, yf_ref, yb_ref, buf_ref, bub_ref, xf_ref, xb_ref, st_ref):
    @pl.when(pl.program_id(0) == 0)
    def _():
        st_ref[...] = jnp.zeros_like(st_ref)

    n = _S5_STATE
    buf_ref[...] = jnp.dot(uf_ref[...].astype(_BF), wb_ref[0], preferred_element_type=jnp.float32)
    bub_ref[...] = jnp.dot(ub_ref[...].astype(_BF), wb_ref[1], preferred_element_type=jnp.float32)
    lr = lam_ref[0]
    li = lam_ref[1]
    fwd = lax.broadcasted_iota(jnp.int32, (_S5_ROWS, n), 0) < (_S5_ROWS // 2)

    def step(i, carry):
        xr, xi = carry
        rf = pl.ds(pl.multiple_of(i * _S5_ROWS, _S5_ROWS), _S5_ROWS)
        rb = pl.ds(pl.multiple_of((S5_STEPS - 1 - i) * _S5_ROWS, _S5_ROWS), _S5_ROWS)
        nr = lr * xr - li * xi + jnp.where(fwd, buf_ref[rf, :n], bub_ref[rb, :n])
        ni = lr * xi + li * xr + jnp.where(fwd, buf_ref[rf, n:], bub_ref[rb, n:])
        xf_ref[rf, :n] = nr
        xf_ref[rf, n:] = ni
        xb_ref[rb, :n] = nr
        xb_ref[rb, n:] = ni
        return nr, ni

    xr, xi = lax.fori_loop(0, S5_STEPS, step, (st_ref[0], st_ref[1]), unroll=4)
    st_ref[0] = xr
    st_ref[1] = xi
    yf_ref[...] = jnp.dot(xf_ref[...].astype(_BF), wc_ref[0], preferred_element_type=jnp.float32)
    yb_ref[...] = jnp.dot(xb_ref[...].astype(_BF), wc_ref[1], preferred_element_type=jnp.float32)


def _s5_dir_scan(u8, wb, lam, wc, *, ctx_len):
    rows = u8.shape[0]
    blk = S5_STEPS * _S5_ROWS
    n = _S5_STATE
    order = _scan_block_order(ctx_len // S5_STEPS, rows // blk, True)
    spec_f = pl.BlockSpec((blk, D_S5), lambda i: (i, 0))
    spec_b = pl.BlockSpec((blk, D_S5), lambda i: (order(i), 0))
    out = jax.ShapeDtypeStruct((rows, D_S5), jnp.float32)
    big = pltpu.VMEM((blk, 2 * n), jnp.float32)
    return pl.pallas_call(
        _s5_dir_kernel,
        grid=(rows // blk,),
        in_specs=[spec_f, spec_b,
                  pl.BlockSpec((2, D_S5, 2 * n), lambda i: (0, 0, 0)),
                  pl.BlockSpec((2, _S5_ROWS, n), lambda i: (0, 0, 0)),
                  pl.BlockSpec((2, 2 * n, D_S5), lambda i: (0, 0, 0))],
        out_specs=[spec_f, spec_b],
        out_shape=[out, out],
        scratch_shapes=[big, big, big, big, pltpu.VMEM((2, _S5_ROWS, n), jnp.float32)],
        compiler_params=pltpu.CompilerParams(dimension_semantics=("arbitrary",),
                                             vmem_limit_bytes=56 * 1024 * 1024),
        name="s5_scan",
    )(u8, u8, wb, lam, wc)


HG_BLOCK = 128
HG_BATCH_BLOCK = 2
_HG_PAIR = 2 * HG_HEAD


def _hgrn_dir_kernel(q_ref, v_ref, f_ref, lb_ref, o_ref, st_ref, *, reverse):
    @pl.when(pl.program_id(1) == 0)
    def _():
        st_ref[...] = jnp.zeros_like(st_ref)

    ch = HG_CHUNK
    n_ch = HG_BLOCK // ch
    r = lax.broadcasted_iota(jnp.int32, (HG_BLOCK, HG_BLOCK), 0)
    c = lax.broadcasted_iota(jnp.int32, (HG_BLOCK, HG_BLOCK), 1)
    tri = ((r // ch == c // ch) & ((r <= c) if reverse else (r >= c))).astype(jnp.float32)
    same_head = (lax.broadcasted_iota(jnp.int32, (_HG_PAIR, _HG_PAIR), 0) // HG_HEAD
                 == lax.broadcasted_iota(jnp.int32, (_HG_PAIR, _HG_PAIR), 1) // HG_HEAD)
    ones_blk = same_head.astype(_BF)
    tcol = lax.broadcasted_iota(jnp.int32, (ch, 1), 0)
    end = 0 if reverse else ch - 1
    n_pair = D_HG // _HG_PAIR
    chains = [(bi, p) for bi in range(q_ref.shape[0]) for p in range(n_pair)]
    pair_lanes = lambda p: slice(p * _HG_PAIR, (p + 1) * _HG_PAIR)
    intra, kv, dec, qd = {}, {}, {}, {}
    for bi in range(q_ref.shape[0]):
        q_raw = q_ref[bi]
        q_all = q_raw * _sigmoid(q_raw)
        forget = lb_ref[...] + (1.0 - lb_ref[...]) * _sigmoid(f_ref[bi])
        k_all = 1.0 - forget
        cum_all = _dot3(tri, jnp.log(forget))
        for p in range(n_pair):
            lanes = pair_lanes(p)
            for ci in range(n_ch):
                rows = slice(ci * ch, (ci + 1) * ch)
                q = q_all[rows, lanes]
                k = k_all[rows, lanes]
                v = v_ref[bi, rows, lanes]
                cum = cum_all[rows, lanes]
                last = cum[end:end + 1]
                w = [q * jnp.exp(jnp.minimum(cum - cum[s:s + 1], 0.0)) * k[s:s + 1] for s in range(ch)]
                rel = jnp.dot(jnp.concatenate(w, axis=0).astype(_BF), ones_blk,
                              preferred_element_type=jnp.float32)
                o = jnp.zeros((ch, _HG_PAIR), jnp.float32)
                for s in range(ch):
                    seen = (tcol <= s) if reverse else (tcol >= s)
                    o = o + jnp.where(seen, rel[s * ch:(s + 1) * ch], 0.0) * v[s:s + 1]
                intra[bi, p, ci] = o
                kv[bi, p, ci] = jnp.where(same_head, _dot(v, k * jnp.exp(last - cum), _TN), 0.0)
                dec[bi, p, ci] = jnp.exp(last)
                qd[bi, p, ci] = q * jnp.exp(cum)
    state = [st_ref[i] for i in range(len(chains))]
    for ci in (range(n_ch - 1, -1, -1) if reverse else range(n_ch)):
        for i, (bi, p) in enumerate(chains):
            o_ref[bi, ci * ch:(ci + 1) * ch, pair_lanes(p)] = intra[bi, p, ci] + _dot(qd[bi, p, ci], state[i], _NT)
            state[i] = state[i] * dec[bi, p, ci] + kv[bi, p, ci]
    for i in range(len(chains)):
        st_ref[i] = state[i]


def _hgrn_dir_scan(p, lb, *, ctx_len, reverse, col0=0):
    g, l, _ = p.shape
    d = D_HG
    order = _scan_block_order(ctx_len // HG_BLOCK, l // HG_BLOCK, reverse)
    bb = HG_BATCH_BLOCK if g % HG_BATCH_BLOCK == 0 else 1
    col = lambda c: pl.BlockSpec((bb, HG_BLOCK, d), lambda i, j: (i, order(j), col0 + c))
    dr = 1 if reverse else 0
    return pl.pallas_call(
        functools.partial(_hgrn_dir_kernel, reverse=reverse),
        grid=(g // bb, l // HG_BLOCK),
        in_specs=[col(0), col(1), col(2 + dr), pl.BlockSpec((1, d), lambda i, j: (0, 0))],
        out_specs=pl.BlockSpec((bb, HG_BLOCK, d), lambda i, j: (i, order(j), 0)),
        out_shape=jax.ShapeDtypeStruct((g, l, d), jnp.float32),
        scratch_shapes=[pltpu.VMEM((bb * d // _HG_PAIR, _HG_PAIR, _HG_PAIR), jnp.float32)],
        compiler_params=pltpu.CompilerParams(dimension_semantics=("parallel", "arbitrary")),
        name="hgrn_scan",
    )(p, p, p, lb[dr:dr + 1])


def _hgrn_post_kernel(of_ref, ob_ref, g_ref, ng_ref, ones_ref, y_ref):
    o = of_ref[0] + ob_ref[0]
    ms = jnp.dot((o * o).astype(_BF), ones_ref[...], preferred_element_type=jnp.float32) * (1.0 / HG_HEAD)
    g = g_ref[0]
    y_ref[0] = o * lax.rsqrt(ms + RMS_EPS) * ng_ref[...] * (g * _sigmoid(g))


def _hgrn_post(o_f, o_b, p, norm_g, col0=0):
    nb, l, d = o_f.shape
    spec = pl.BlockSpec((1, HG_BLOCK, d), lambda i, j: (i, j, 0))
    return pl.pallas_call(
        _hgrn_post_kernel,
        grid=(nb, l // HG_BLOCK),
        in_specs=[spec, spec, pl.BlockSpec((1, HG_BLOCK, d), lambda i, j: (i, j, col0 + 4)),
                  pl.BlockSpec((1, d), lambda i, j: (0, 0)), pl.BlockSpec((d, d), lambda i, j: (0, 0))],
        out_specs=spec,
        out_shape=jax.ShapeDtypeStruct((nb, l, d), jnp.float32),
        compiler_params=pltpu.CompilerParams(dimension_semantics=("parallel", "parallel")),
        name="hgrn_post",
    )(o_f, o_b, p, norm_g.reshape(1, d), _head_ones(d, HG_HEAD))


_RW_PAIR = 2 * RW_HEAD
RW_PRE_T = 128
RW_BATCH_BLOCK = 4


def _rwkv_pair_kernel(r_ref, lw_ref, k_ref, v_ref, kk_ref, a_ref, y_ref, st_ref, *, chunk, reverse):
    @pl.when(pl.program_id(1) == 0)
    def _():
        st_ref[...] = jnp.zeros_like(st_ref)

    row = lax.broadcasted_iota(jnp.int32, (chunk, chunk), 0)
    col = lax.broadcasted_iota(jnp.int32, (chunk, chunk), 1)
    strict = row < col if reverse else row > col
    incl = row <= col if reverse else row >= col
    end = 0 if reverse else chunk - 1
    tri = incl.astype(jnp.float32)
    eye = (row == col).astype(jnp.float32)
    blk8 = row // 8 == col // 8
    merge_masks = []
    size = 8
    while size < chunk:
        merge_masks.append((row // (2 * size) == col // (2 * size)) & (row // size != col // size))
        size *= 2
    lane_head = lax.broadcasted_iota(jnp.int32, (1, _RW_PAIR), 1) // RW_HEAD
    head0 = lane_head == 0
    same_head = (lax.broadcasted_iota(jnp.int32, (_RW_PAIR, _RW_PAIR), 0) // RW_HEAD
                 == lax.broadcasted_iota(jnp.int32, (_RW_PAIR, _RW_PAIR), 1) // RW_HEAD)
    n_pair = D_RW // _RW_PAIR
    ps = range(r_ref.shape[0] * n_pair)
    ph = [(p, h) for p in ps for h in range(2)]
    bi = [p // n_pair for p in ps]
    sl = [slice((p % n_pair) * _RW_PAIR, (p % n_pair + 1) * _RW_PAIR) for p in ps]
    lw = [lw_ref[bi[p], :, sl[p]] for p in ps]
    cum = [_dot3(tri, lw[p]) for p in ps]
    pw = [jnp.exp(cum[p]) for p in ps]
    p_inv = [jnp.exp(-cum[p]) for p in ps]
    p_end = [pw[p][end:end + 1, :] for p in ps]
    kk = [kk_ref[bi[p], :, sl[p]] for p in ps]
    bh = [kk[p] * a_ref[bi[p], :, sl[p]] * p_inv[p] for p in ps]
    kh = [k_ref[bi[p], :, sl[p]] * p_inv[p] for p in ps]
    v = [v_ref[bi[p], :, sl[p]] for p in ps]
    s0 = [st_ref[p] for p in ps]
    x1 = [jnp.concatenate([-kk[p] * jnp.exp(cum[p] - lw[p]), r_ref[bi[p], :, sl[p]] * pw[p]], axis=0) for p in ps]
    x1h = {(p, h): jnp.where(lane_head == h, x1[p], 0.0) for p, h in ph}
    g1 = {q: _dot(x1h[q], bh[q[0]], _NT) for q in ph}
    g2 = {q: _dot(x1h[q], kh[q[0]], _NT) for q in ph}
    hm = [_dot(x1[p], s0[p], _NT) for p in ps]
    ab = {q: jnp.where(strict, g1[q][:chunk], 0.0) for q in ph}
    d1 = {q: jnp.where(blk8, ab[q], 0.0) for q in ph}
    d2 = {q: _dot3(d1[q], d1[q]) for q in ph}
    inv = {q: eye + d1[q] for q in ph}
    inv = {q: inv[q] + _dot3(inv[q], d2[q]) for q in ph}
    d4 = {q: _dot3(d2[q], d2[q]) for q in ph}
    inv = {q: inv[q] + _dot3(inv[q], d4[q]) for q in ph}
    for m in merge_masks:
        li = {q: _dot(jnp.where(m, ab[q], 0.0), inv[q]) for q in ph}
        inv = {q: inv[q] + _dot(inv[q], li[q]) for q in ph}
    akv = {q: _dot(jnp.where(strict, g2[q][:chunk], 0.0), v[q[0]]) for q in ph}
    rhs = [hm[p][:chunk] + jnp.where(head0, akv[p, 0], akv[p, 1]) for p in ps]
    eh = {q: _dot(inv[q], rhs[q[0]]) for q in ph}
    e = [jnp.where(head0, eh[p, 0], eh[p, 1]) for p in ps]
    yh = {q: _dot(jnp.where(incl, g1[q][chunk:], 0.0), e[q[0]])
          + _dot(jnp.where(incl, g2[q][chunk:], 0.0), v[q[0]]) for q in ph}
    for p in ps:
        y_ref[bi[p], :, sl[p]] = hm[p][chunk:] + jnp.where(head0, yh[p, 0], yh[p, 1])
    for p in ps:
        ev = jnp.concatenate([e[p], v[p]], axis=0)
        x2 = jnp.concatenate([bh[p] * p_end[p], kh[p] * p_end[p]], axis=0)
        st_ref[p] = s0[p] * p_end[p] + jnp.where(same_head, _dot(ev, x2, _TN), 0.0)


def _rwkv_pair_scan(r, lw, k, v, kk, a, *, ctx_len, reverse):
    nb, l, d = r.shape
    chunk = RW_CHUNK
    order = _scan_block_order(ctx_len // chunk, l // chunk, reverse)
    bb = RW_BATCH_BLOCK if nb % RW_BATCH_BLOCK == 0 else 1
    spec = pl.BlockSpec((bb, chunk, d), lambda b, c: (b, order(c), 0))
    return pl.pallas_call(
        functools.partial(_rwkv_pair_kernel, chunk=chunk, reverse=reverse),
        grid=(nb // bb, l // chunk),
        in_specs=[spec] * 6,
        out_specs=spec,
        out_shape=jax.ShapeDtypeStruct((nb, l, d), jnp.float32),
        scratch_shapes=[pltpu.VMEM((bb * d // _RW_PAIR, _RW_PAIR, _RW_PAIR), jnp.float32)],
        compiler_params=pltpu.CompilerParams(dimension_semantics=("parallel", "arbitrary")),
        name="rwkv_scan",
    )(r, lw, k, v, kk, a)


def _softplus(z):
    return jnp.maximum(z, 0.0) + jnp.log(1.0 + jnp.exp(-jnp.abs(z)))


def _sigmoid(z):
    return 1.0 / (1.0 + jnp.exp(-z))


def _rwkv_pre_kernel(prev_ref, cur_ref, next_ref, mu_ref, vec_ref, gup_ref, wup_ref, aup_ref, ones_ref,
                     r_ref, v_ref, kk_ref, g_ref, bonus_ref, lw0_ref, k0_ref, a0_ref, lw1_ref, k1_ref, a1_ref,
                     ext_ref, s_ref, *, ctx_len, seq_len):
    t_rows, w = RW_PRE_T, GRID_W
    ext_ref[0:w] = prev_ref[0]
    ext_ref[w:w + t_rows] = cur_ref[0]
    ext_ref[w + t_rows:w + t_rows + w] = next_ref[0]
    t = pl.program_id(1) * t_rows + lax.broadcasted_iota(jnp.int32, (t_rows, 128), 0)
    lane4 = lax.broadcasted_iota(jnp.int32, (t_rows, 128), 1) % 4
    is_ctx = t < ctx_len
    tx = t - ctx_len
    col = tx % w
    is_lat = jnp.logical_not(is_ctx)
    use_p1 = (is_ctx & (lane4 % 2 == 0) & (t > 0)) | (is_lat & (lane4 == 0) & (col != 0))
    use_n1 = (is_ctx & (lane4 % 2 == 1) & (t < ctx_len - 1)) | (is_lat & (lane4 == 1) & (col != w - 1))
    use_p64 = is_lat & (lane4 == 2) & (tx >= w)
    use_n64 = is_lat & (lane4 == 3) & (tx < seq_len - w)
    for lt in range(RW_COLS // 128):
        lanes = slice(lt * 128, (lt + 1) * 128)
        p = ext_ref[w:w + t_rows, lanes]
        shifted = jnp.where(use_p1, ext_ref[w - 1:w - 1 + t_rows, lanes],
                            jnp.where(use_n1, ext_ref[w + 1:w + 1 + t_rows, lanes],
                                      jnp.where(use_p64, ext_ref[0:t_rows, lanes],
                                                jnp.where(use_n64, ext_ref[2 * w:2 * w + t_rows, lanes], 0.0))))
        s_ref[:, lanes] = p + mu_ref[:, lanes] * (shifted - p)
    d = D_RW
    r = s_ref[:, 0:d]
    k = s_ref[:, d:2 * d]
    v = s_ref[:, 2 * d:3 * d]
    o = 3 * d
    g_lo = s_ref[:, o:o + RW_G_LORA]
    wl = s_ref[:, o + RW_G_LORA:o + RW_G_LORA + 2 * RW_W_LORA]
    al = s_ref[:, o + RW_G_LORA + 2 * RW_W_LORA:o + RW_G_LORA + 2 * RW_W_LORA + 2 * RW_A_LORA]
    k_k, k_a, r_k = vec_ref[0:1], vec_ref[1:2], vec_ref[2:3]
    ones = ones_ref[...]
    r_ref[0] = r
    v_ref[0] = v
    g_ref[0] = jnp.dot(_sigmoid(g_lo).astype(_BF), gup_ref[...], preferred_element_type=jnp.float32)
    kkr = k * k_k
    ss = jnp.dot((kkr * kkr).astype(_BF), ones, preferred_element_type=jnp.float32)
    kk_ref[0] = kkr / jnp.maximum(jnp.sqrt(ss), 1e-12)
    w_pre = jnp.dot(jnp.tanh(wl).astype(_BF), wup_ref[...], preferred_element_type=jnp.float32)
    a_pre = jnp.dot(al.astype(_BF), aup_ref[...], preferred_element_type=jnp.float32)
    k_sum = 0.0
    for dr, (lw_o, k_o, a_o) in enumerate(((lw0_ref, k0_ref, a0_ref), (lw1_ref, k1_ref, a1_ref))):
        w_log = -_softplus(-(vec_ref[3 + dr:4 + dr] + w_pre[:, dr * d:(dr + 1) * d])) - 0.5
        a = _sigmoid(vec_ref[5 + dr:6 + dr] + a_pre[:, dr * d:(dr + 1) * d])
        k_d = k * (1.0 + (a - 1.0) * k_a)
        lw_o[0] = -jnp.exp(w_log)
        k_o[0] = k_d
        a_o[0] = a
        k_sum = k_sum + k_d
    bonus_ref[0] = jnp.dot((r * k_sum * r_k).astype(_BF), ones, preferred_element_type=jnp.float32)


def _head_ones(width, head):
    i = jnp.arange(width) // head
    return (i[:, None] == i[None, :]).astype(_BF)


def _rwkv_pre(p, mu, k_k, k_a, r_k, w0, a0, g_up, w_up, a_up, *, ctx_len):
    nb, l, cols = p.shape
    t_rows, w, d = RW_PRE_T, GRID_W, D_RW
    nblk = l // w
    per = t_rows // w
    zeros = jnp.zeros((RW_W_LORA, d), jnp.float32)
    wup = jnp.concatenate([jnp.concatenate([w_up[0], zeros], axis=1),
                           jnp.concatenate([zeros, w_up[1]], axis=1)], axis=0).astype(_BF)
    aup = jnp.concatenate([jnp.concatenate([a_up[0], zeros], axis=1),
                           jnp.concatenate([zeros, a_up[1]], axis=1)], axis=0).astype(_BF)
    vec = jnp.stack([k_k, k_a, r_k.reshape(d), w0[0], w0[1], a0[0], a0[1], jnp.zeros((d,), jnp.float32)])
    out = jax.ShapeDtypeStruct((nb, l, d), jnp.float32)
    out_spec = pl.BlockSpec((1, t_rows, d), lambda b, j: (b, j, 0))
    full = lambda shape: pl.BlockSpec(shape, lambda b, j: (0,) * len(shape))
    return pl.pallas_call(
        functools.partial(_rwkv_pre_kernel, ctx_len=ctx_len, seq_len=l - ctx_len),
        grid=(nb, l // t_rows),
        in_specs=[pl.BlockSpec((1, w, cols), lambda b, j: (b, jnp.maximum(j * per - 1, 0), 0)),
                  pl.BlockSpec((1, t_rows, cols), lambda b, j: (b, j, 0)),
                  pl.BlockSpec((1, w, cols), lambda b, j: (b, jnp.minimum(j * per + per, nblk - 1), 0)),
                  full((1, cols)), full((8, d)), full((RW_G_LORA, d)),
                  full((2 * RW_W_LORA, 2 * d)), full((2 * RW_A_LORA, 2 * d)), full((d, d))],
        out_specs=[out_spec] * 11,
        out_shape=[out] * 11,
        scratch_shapes=[pltpu.VMEM((t_rows + 2 * w, cols), jnp.float32), pltpu.VMEM((t_rows, cols), jnp.float32)],
        compiler_params=pltpu.CompilerParams(dimension_semantics=("parallel", "parallel"),
                                             vmem_limit_bytes=48 * 1024 * 1024),
        name="rwkv_pre",
    )(p, p, p, mu.reshape(1, cols), vec, g_up.astype(_BF), wup, aup, _head_ones(d, RW_HEAD))


def _rwkv_post_kernel(y0_ref, y1_ref, v_ref, g_ref, bonus_ref, gn_ref, ones_ref, o_ref):
    y = y0_ref[0] + y1_ref[0]
    ones = ones_ref[...]
    inv_n = 1.0 / RW_HEAD
    mu_y = jnp.dot(y.astype(_BF), ones, preferred_element_type=jnp.float32) * inv_n
    yc = y - mu_y
    var_y = jnp.dot((yc * yc).astype(_BF), ones, preferred_element_type=jnp.float32) * inv_n
    yn = yc * lax.rsqrt(var_y + RW_GN_EPS) * gn_ref[0:1] + gn_ref[1:2]
    o_ref[0] = (yn + bonus_ref[0] * v_ref[0]) * g_ref[0]


def _rwkv_post(y0, y1, v, g, bonus, gn_g, gn_b):
    nb, l, d = y0.shape
    spec = pl.BlockSpec((1, RW_PRE_T, d), lambda b, j: (b, j, 0))
    return pl.pallas_call(
        _rwkv_post_kernel,
        grid=(nb, l // RW_PRE_T),
        in_specs=[spec] * 5 + [pl.BlockSpec((2, d), lambda b, j: (0, 0)), pl.BlockSpec((d, d), lambda b, j: (0, 0))],
        out_specs=spec,
        out_shape=jax.ShapeDtypeStruct((nb, l, d), jnp.float32),
        compiler_params=pltpu.CompilerParams(dimension_semantics=("parallel", "parallel")),
        name="rwkv_post",
    )(y0, y1, v, g, bonus, jnp.stack([gn_g, gn_b]), _head_ones(d, RW_HEAD))


def _rwkv7_mixer_fused(p, lc, mu, w0, w_up, a0, a_up, g_up, k_k, k_a, r_k, gn_g, gn_b):
    r, v, kk, g, bonus, lw0, k0, a0_, lw1, k1, a1_ = _rwkv_pre(
        p, _f32(mu), _f32(k_k), _f32(k_a), _f32(r_k), _f32(w0), _f32(a0), _f32(g_up), _f32(w_up), _f32(a_up),
        ctx_len=lc)
    y0 = _rwkv_pair_scan(r, lw0, k0, v, kk, a0_, ctx_len=lc, reverse=False)
    y1 = _rwkv_pair_scan(r, lw1, k1, v, kk, a1_, ctx_len=lc, reverse=True)
    return _rwkv_post(y0, y1, v, g, bonus, _f32(gn_g), _f32(gn_b))


def _f32(t):
    return t.astype(jnp.float32)


def _layer_norm(x, g, b):
    xf = _f32(x)
    mu = jnp.mean(xf, -1, keepdims=True)
    var = jnp.mean(jnp.square(xf - mu), -1, keepdims=True)
    return ((xf - mu) * lax.rsqrt(var + LN_EPS) * _f32(g) + _f32(b)).astype(x.dtype)


def _s5_discretise(lam_re, lam_im, log_step, b_re, b_im):
    dt = jnp.exp(log_step)[:, None]
    mag = jnp.exp(lam_re * dt)
    ang = lam_im * dt
    lb_re, lb_im = mag * jnp.cos(ang), mag * jnp.sin(ang)
    den = lam_re * lam_re + lam_im * lam_im
    nr, ni = lb_re - 1.0, lb_im
    co_re = (nr * lam_re + ni * lam_im) / den
    co_im = (ni * lam_re - nr * lam_im) / den
    bb_re = co_re[..., None] * b_re - co_im[..., None] * b_im
    bb_im = co_re[..., None] * b_im + co_im[..., None] * b_re
    return lb_re, lb_im, bb_re, bb_im


def _s5_mixer(u, lc, lam_re, lam_im, log_step, b_re, b_im, c_re, c_im, d_skip, glu_w, glu_b):
    disc = [_s5_discretise(_f32(lam_re[d]), _f32(lam_im[d]), _f32(log_step[d]),
                           _f32(b_re[d]), _f32(b_im[d])) for d in range(2)]
    wb, lam, wc = _s5_weights(disc, _f32(c_re), _f32(c_im))
    u = _f32(u)
    nb, l, _ = u.shape
    assert 2 * nb == _S5_ROWS
    u_t = u.transpose(1, 0, 2)
    u8 = jnp.concatenate([u_t, u_t], axis=1).reshape(l * _S5_ROWS, D_S5)
    y_f, y_b = _s5_dir_scan(u8, wb, lam, wc, ctx_len=lc)
    y = (y_f.reshape(l, _S5_ROWS, D_S5)[:, :nb] + y_b.reshape(l, _S5_ROWS, D_S5)[:, nb:]).transpose(1, 0, 2)
    y = jax.nn.gelu(y + _f32(d_skip) * u)
    return y * jax.nn.sigmoid(_mm3(y, glu_w) + _f32(glu_b))


def _hgrn2_mixer(p, lc, lb, norm_g, col0=0):
    lb = _f32(lb)
    o_f = _hgrn_dir_scan(p, lb, ctx_len=lc, reverse=False, col0=col0)
    o_b = _hgrn_dir_scan(p, lb, ctx_len=lc, reverse=True, col0=col0)
    return _hgrn_post(o_f, o_b, p, _f32(norm_g), col0=col0)


ROW_TILE = 256


def _mod_spec(n_batch, tiles_per_batch, ctx_tiles):
    def index(i):
        return (jnp.where(i % tiles_per_batch < ctx_tiles, n_batch, i // tiles_per_batch), 0, 0)
    return pl.BlockSpec((1, 1, 6 * D_MODEL), index)


def _inproj_kernel(x_ref, mod_ref, w_ref, pa_ref, pc_ref):
    d = D_MODEL
    m = mod_ref[0]
    xm = (x_ref[...] * (1.0 + m[:, d:2 * d]) + m[:, 0:d]).astype(_BF)
    split = pa_ref.shape[1]
    pa_ref[...] = jnp.dot(xm, w_ref[:, :split], preferred_element_type=jnp.float32)
    pc_ref[...] = jnp.dot(xm, w_ref[:, split:], preferred_element_type=jnp.float32)


def _inproj(xs, mod, w, *, n_batch, ctx_len):
    t, d = xs.shape
    n_a = D_S5 + 5 * D_HG
    n_c = w.shape[1] - n_a
    tiles = t // ROW_TILE
    return pl.pallas_call(
        _inproj_kernel,
        grid=(tiles,),
        in_specs=[pl.BlockSpec((ROW_TILE, d), lambda i: (i, 0)),
                  _mod_spec(n_batch, tiles // n_batch, ctx_len // ROW_TILE),
                  pl.BlockSpec((d, n_a + n_c), lambda i: (0, 0))],
        out_specs=[pl.BlockSpec((ROW_TILE, n_a), lambda i: (i, 0)), pl.BlockSpec((ROW_TILE, n_c), lambda i: (i, 0))],
        out_shape=[jax.ShapeDtypeStruct((t, n_a), jnp.float32), jax.ShapeDtypeStruct((t, n_c), jnp.float32)],
        compiler_params=pltpu.CompilerParams(dimension_semantics=("parallel",),
                                             vmem_limit_bytes=48 * 1024 * 1024),
        name="in_proj",
    )(xs, mod, w.astype(_BF))


def _ln_rows(h, g, b):
    mu = jnp.mean(h, axis=-1, keepdims=True)
    hc = h - mu
    var = jnp.mean(hc * hc, axis=-1, keepdims=True)
    return hc * lax.rsqrt(var + LN_EPS) * g + b


def _outproj_ln_kernel(ya_ref, yb_ref, yc_ref, w_ref, xs_ref, mod_ref, ln_ref, x1_ref, h2_ref):
    d = D_MODEL
    mix = (jnp.dot(ya_ref[...].astype(_BF), w_ref[0:D_S5], preferred_element_type=jnp.float32)
           + jnp.dot(yb_ref[...].astype(_BF), w_ref[D_S5:D_S5 + D_HG], preferred_element_type=jnp.float32)
           + jnp.dot(yc_ref[...].astype(_BF), w_ref[D_S5 + D_HG:D_MIX], preferred_element_type=jnp.float32))
    m = mod_ref[0]
    x1 = _ln_rows(DEEPNORM_ALPHA * xs_ref[...] + m[:, 2 * d:3 * d] * mix, ln_ref[0:1], ln_ref[1:2])
    x1_ref[...] = x1
    h2_ref[...] = x1 * (1.0 + m[:, 4 * d:5 * d]) + m[:, 3 * d:4 * d]


def _outproj_ln(ya, yb, yc, w, xs, mod, ln_g, ln_b, *, n_batch, ctx_len):
    t, d = xs.shape
    tiles = t // ROW_TILE
    row = lambda n: pl.BlockSpec((ROW_TILE, n), lambda i: (i, 0))
    out = jax.ShapeDtypeStruct((t, d), jnp.float32)
    return pl.pallas_call(
        _outproj_ln_kernel,
        grid=(tiles,),
        in_specs=[row(D_S5), row(D_HG), row(D_RW), pl.BlockSpec((D_MIX, d), lambda i: (0, 0)), row(d),
                  _mod_spec(n_batch, tiles // n_batch, ctx_len // ROW_TILE),
                  pl.BlockSpec((2, d), lambda i: (0, 0))],
        out_specs=[row(d), row(d)],
        out_shape=[out, out],
        compiler_params=pltpu.CompilerParams(dimension_semantics=("parallel",)),
        name="out_proj_ln",
    )(ya, yb, yc, w.astype(_BF), xs, mod, jnp.stack([ln_g, ln_b]))


def kernel(x, c, ctx, c_ctx, ada_w, ada_b, w_in, w_out, ln1_g, ln1_b, ln2_g, ln2_b,
           s5_lam_re, s5_lam_im, s5_log_step, s5_b_re, s5_b_im, s5_c_re, s5_c_im,
           s5_d, s5_glu_w, s5_glu_b, hgrn_lb_logits, hgrn_norm_g,
           rwkv_mu, rwkv_w0, rwkv_w_up, rwkv_a0, rwkv_a_up, rwkv_g_up, rwkv_k_k, rwkv_k_a,
           rwkv_r_k, rwkv_gn_g, rwkv_gn_b, peer_wq, peer_keys, peer_u, peer_v):
    nb, lx, d = x.shape
    lc = ctx.shape[1]
    l_all = lc + lx
    assert lx % GRID_W == 0 and lc % ROW_TILE == 0 and lx % ROW_TILE == 0 and (nb * l_all) % PEER_TN == 0
    lb_cum = jnp.cumsum(jax.nn.softmax(_f32(hgrn_lb_logits), axis=1), axis=1)
    lb_all = lb_cum - lb_cum[:, :1]
    xs = jnp.concatenate([ctx, x], axis=1).reshape(nb * l_all, d)
    seq = lambda t: t.reshape(nb, l_all, t.shape[-1])
    for l in range(DEPTH):
        last = l == DEPTH - 1
        mod = jnp.concatenate([jax.nn.silu(c), jax.nn.silu(c_ctx)[None]], axis=0) @ ada_w[l] + ada_b[l]
        g2 = mod[:, None, 5 * d:]
        mod = mod[:, None, :]
        pa, pc = _inproj(xs, mod, w_in[l], n_batch=nb, ctx_len=lc)
        pa = seq(pa)
        ya = _s5_mixer(pa[..., :D_S5], lc, s5_lam_re[l], s5_lam_im[l], s5_log_step[l],
                       s5_b_re[l], s5_b_im[l], s5_c_re[l], s5_c_im[l], s5_d[l], s5_glu_w[l], s5_glu_b[l])
        yb = _hgrn2_mixer(pa, lc, lb_all[:, l], hgrn_norm_g[l], col0=D_S5 // D_HG)
        yc = _rwkv7_mixer_fused(seq(pc), lc, rwkv_mu[l], rwkv_w0[l], rwkv_w_up[l], rwkv_a0[l], rwkv_a_up[l],
                                rwkv_g_up[l], rwkv_k_k[l], rwkv_k_a[l], rwkv_r_k[l], rwkv_gn_g[l], rwkv_gn_b[l])
        flat = lambda t: t.reshape(nb * l_all, t.shape[-1])
        xs, h2 = _outproj_ln(flat(ya), flat(yb), flat(yc), w_out[l], xs, mod, ln1_g[l], ln1_b[l],
                             n_batch=nb, ctx_len=lc)
        if last:
            xs, h2 = seq(xs)[:, lc:], seq(h2)[:, lc:].reshape(nb * lx, d)
            gate = g2[:nb]
        else:
            xs = seq(xs)
            gate = jnp.where((jnp.arange(l_all) < lc)[None, :, None], g2[nb:], g2[:nb])
        ffn = _peer_ffn_dense(h2, peer_wq[l], peer_keys[l], peer_u[l].astype(_BF), peer_v[l].T.astype(_BF))
        xs = _layer_norm(DEEPNORM_ALPHA * xs + gate * ffn.reshape(xs.shape), ln2_g[l], ln2_b[l])
        xs = xs.reshape(-1, d)
    return xs.reshape(nb, lx, d)
```

```python
import functools
import math

import jax
import jax.numpy as jnp
from jax import lax
from jax.experimental import pallas as pl
from jax.experimental.pallas import tpu as pltpu

D_MODEL = 1024
DEPTH = 2
GRID_W = 64
D_S5 = D_MODEL // 4
D_HG = D_MODEL // 4
D_RW = D_MODEL // 2
D_MIX = D_S5 + D_HG + D_RW
S5_H = 16
S5_G = D_S5 // S5_H
S5_P = 64
HG_HEAD = 64
HG_CHUNK = 16
RW_HEAD = 64
RW_W_LORA = 64
RW_A_LORA = 64
RW_G_LORA = 128
RW_COLS = 3 * D_RW + RW_G_LORA + 2 * RW_W_LORA + 2 * RW_A_LORA
RW_GN_EPS = 64e-5
PEER_HEADS = 8
PEER_NKEYS = 128
PEER_N = PEER_NKEYS * PEER_NKEYS
PEER_QDIM = 256
PEER_TOPK = 16
LN_EPS = 1e-5
RMS_EPS = 1e-6
DEEPNORM_ALPHA = (2.0 * DEPTH) ** 0.25


def _mm_kernel(x_ref, w_ref, o_ref):
    o_ref[...] = jnp.dot(x_ref[...].astype(jnp.bfloat16), w_ref[...],
                         preferred_element_type=jnp.float32)


def _pick_tile(n, cands):
    for c in cands:
        if n % c == 0:
            return c
    return n


def _matmul(x, w):
    m, k = x.shape
    n = w.shape[1]
    tm = _pick_tile(m, (512, 256, 128, 8))
    tn = _pick_tile(n, (1152, 1024, 512, 256, 128))
    return pl.pallas_call(
        _mm_kernel,
        grid=(m // tm, n // tn),
        in_specs=[pl.BlockSpec((tm, k), lambda i, j: (i, 0)),
                  pl.BlockSpec((k, tn), lambda i, j: (0, j))],
        out_specs=pl.BlockSpec((tm, tn), lambda i, j: (i, j)),
        out_shape=jax.ShapeDtypeStruct((m, n), jnp.float32),
        compiler_params=pltpu.CompilerParams(
            dimension_semantics=("parallel", "parallel"),
            vmem_limit_bytes=48 * 1024 * 1024),
    )(x, w.astype(jnp.bfloat16))


def _mm3(x, w):
    b, l, k = x.shape
    return _matmul(x.reshape(b * l, k), w).reshape(b, l, w.shape[1])


_BF = jnp.bfloat16
_NN = (((1,), (0,)), ((), ()))
_NT = (((1,), (1,)), ((), ()))
_TN = (((0,), (0,)), ((), ()))


def _dot(a, b, dims=_NN):
    return lax.dot_general(a.astype(_BF), b.astype(_BF), dims,
                           preferred_element_type=jnp.float32)


def _split_bf16(a):
    hi = a.astype(_BF)
    return hi, (a - hi.astype(jnp.float32)).astype(_BF)


def _dot3(a, b, dims=_NN):
    a_hi, a_lo = _split_bf16(a)
    b_hi, b_lo = _split_bf16(b)
    d = functools.partial(lax.dot_general, dimension_numbers=dims,
                          preferred_element_type=jnp.float32)
    return d(a_hi, b_hi) + (d(a_hi, b_lo) + d(a_lo, b_hi))


RW_CHUNK = 64


def _scan_block_order(n_ctx, n_all, reverse):
    if not reverse:
        return lambda c: c
    return lambda c: jnp.where(c < n_ctx, n_ctx - 1 - c, n_all - 1 - c + n_ctx)


PEER_TN = 512
PEER_EBLK = 2048
_PEER_SUB = 256
_PEER_OUT = 512
_NEG = -3.0e38


def _top_rows(s, n):
    vals = []
    for _ in range(n):
        m = jnp.max(s, axis=0, keepdims=True)
        vals.append(m)
        s = jnp.where(s == m, _NEG, s)
    return vals


def _peer_route_kernel(h_ref, wq_ref, keys_ref, xb_ref, pi_ref, c1_ref, p2_ref):
    hb = h_ref[...].astype(_BF)
    xb_ref[...] = h_ref[...].T.astype(_BF)
    q = jnp.dot(hb, wq_ref[...], preferred_element_type=jnp.float32)
    half = PEER_QDIM // 2
    for h in range(PEER_HEADS):
        s1 = _dot(keys_ref[2 * h], q[:, (2 * h) * half:(2 * h + 1) * half], _NT)
        s2 = _dot(keys_ref[2 * h + 1], q[:, (2 * h + 1) * half:(2 * h + 2) * half], _NT)
        t1 = _top_rows(s1, PEER_TOPK + 1)
        t2 = _top_rows(s2, PEER_TOPK + 1)
        cand = [t1[j1] + t2[j2] for j1 in range(PEER_TOPK + 1) for j2 in range(PEER_TOPK + 1)
                if (j1 + 1) * (j2 + 1) <= PEER_TOPK + 1]
        cand += [jnp.full_like(cand[0], _NEG)] * (-len(cand) % 8)
        top = _top_rows(jnp.concatenate(cand, axis=0), PEER_TOPK + 1)
        theta = 0.5 * (top[PEER_TOPK - 1] + top[PEER_TOPK])
        z = top[0] * 0.0
        for c in top[:PEER_TOPK]:
            z = z + jnp.exp(c - top[0])
        pi_ref[h] = jnp.exp(jnp.maximum(theta - s1 - t2[0], -80.0))
        c1_ref[h] = jnp.exp(s1 - t1[0]) / z
        p2_ref[h] = jnp.exp(s2 - t2[0])


def _peer_route(hf, wq, keys):
    t, d = hf.shape
    tn = PEER_TN
    nq = PEER_HEADS * PEER_QDIM
    aux = jax.ShapeDtypeStruct((PEER_HEADS, PEER_NKEYS, t), jnp.float32)
    aux_spec = pl.BlockSpec((PEER_HEADS, PEER_NKEYS, tn), lambda i: (0, 0, i))
    return pl.pallas_call(
        _peer_route_kernel,
        grid=(t // tn,),
        in_specs=[pl.BlockSpec((tn, d), lambda i: (i, 0)),
                  pl.BlockSpec((d, nq), lambda i: (0, 0)),
                  pl.BlockSpec((PEER_HEADS * 2, PEER_NKEYS, PEER_QDIM // 2), lambda i: (0, 0, 0))],
        out_specs=[pl.BlockSpec((d, tn), lambda i: (0, i))] + [aux_spec] * 3,
        out_shape=[jax.ShapeDtypeStruct((d, t), _BF)] + [aux] * 3,
        compiler_params=pltpu.CompilerParams(dimension_semantics=("parallel",),
                                             vmem_limit_bytes=56 * 1024 * 1024),
        name="peer_route",
    )(hf, wq.astype(_BF), keys.reshape(PEER_HEADS * 2, PEER_NKEYS, PEER_QDIM // 2))


def _gelu_tanh(z):
    c = math.sqrt(2.0 / math.pi)
    hz = 0.5 * z
    return hz + hz * jnp.tanh(z * (c + (c * 0.044715) * (z * z)))


def _peer_expert_kernel(x_ref, u_ref, vt_ref, pi_ref, c1_ref, p2_ref, o_ref, acc_ref, at_ref):
    j = pl.program_id(1)

    @pl.when(j == 0)
    def _():
        acc_ref[...] = jnp.zeros_like(acc_ref)

    nk = PEER_NKEYS
    n_slab = PEER_EBLK // nk
    rows = pl.ds(pl.multiple_of(j * n_slab, n_slab), n_slab)
    sub = _PEER_SUB
    per = sub // nk

    def scores(sb):
        return jnp.dot(u_ref[sb * sub:(sb + 1) * sub, :], x_ref[...],
                       preferred_element_type=jnp.float32)

    zt = scores(0)
    for sb in range(PEER_EBLK // sub):
        zt_next = scores(sb + 1) if (sb + 1) * sub < PEER_EBLK else None
        for k in range(per):
            s = sb * per + k
            for lt in range(PEER_TN // 128):
                lanes = pl.ds(lt * 128, 128)
                g = jnp.zeros((nk, 128), jnp.float32)
                for h in range(PEER_HEADS):
                    pi = pi_ref[h, rows, lanes][s:s + 1]
                    c1 = c1_ref[h, rows, lanes][s:s + 1]
                    p2 = p2_ref[h, :, lanes]
                    g = g + jnp.where(p2 >= pi, p2 * c1, 0.0)
                z = zt[k * nk:(k + 1) * nk, lt * 128:(lt + 1) * 128]
                at_ref[s * nk:(s + 1) * nk, lt * 128:(lt + 1) * 128] = (g * _gelu_tanh(z)).astype(_BF)
        zt = zt_next
        done = (sb + 1) * sub
        if done % _PEER_OUT == 0:
            part = slice(done - _PEER_OUT, done)
            acc_ref[...] += jnp.dot(vt_ref[:, part], at_ref[part, :], preferred_element_type=jnp.float32)

    @pl.when(j == pl.num_programs(1) - 1)
    def _():
        o_ref[...] = acc_ref[...].T


def _peer_expert(xb, u_b, vt_b, aux):
    d, t = xb.shape
    tn, eb = PEER_TN, PEER_EBLK
    aux_spec = pl.BlockSpec((PEER_HEADS, PEER_NKEYS, tn), lambda i, j: (0, 0, i))
    return pl.pallas_call(
        _peer_expert_kernel,
        grid=(t // tn, PEER_N // eb),
        in_specs=[pl.BlockSpec((d, tn), lambda i, j: (0, i)),
                  pl.BlockSpec((eb, d), lambda i, j: (j, 0)),
                  pl.BlockSpec((d, eb), lambda i, j: (0, j))] + [aux_spec] * 3,
        out_specs=pl.BlockSpec((tn, d), lambda i, j: (i, 0)),
        out_shape=jax.ShapeDtypeStruct((t, d), jnp.float32),
        scratch_shapes=[pltpu.VMEM((d, tn), jnp.float32), pltpu.VMEM((eb, tn), _BF)],
        compiler_params=pltpu.CompilerParams(dimension_semantics=("parallel", "arbitrary"),
                                             vmem_limit_bytes=56 * 1024 * 1024),
        name="peer_expert",
    )(xb, u_b, vt_b, *aux)


def _peer_ffn_dense(hf, wq, keys, u_b, vt_b):
    xb, *aux = _peer_route(hf, wq, keys)
    return _peer_expert(xb, u_b, vt_b, aux)


S5_STEPS = 128
_S5_ROWS = 8
_S5_STATE = S5_G * S5_P


def _s5_weights(disc, c_re, c_im):
    eye = jnp.eye(S5_G, dtype=jnp.float32)

    def bdiag_in(bb):
        return jnp.einsum('gph,gk->ghkp', bb, eye).reshape(D_S5, _S5_STATE)

    def bdiag_out(cc):
        return jnp.einsum('ghp,gk->gpkh', cc, eye).reshape(_S5_STATE, D_S5)

    wb = jnp.stack([jnp.concatenate([bdiag_in(disc[d][2]), bdiag_in(disc[d][3])], axis=1) for d in range(2)])
    wc = jnp.stack([jnp.concatenate([bdiag_out(c_re[d]), -bdiag_out(c_im[d])], axis=0) for d in range(2)])
    half = _S5_ROWS // 2
    lam = jnp.stack([
        jnp.concatenate([jnp.broadcast_to(disc[d][i].reshape(1, _S5_STATE), (half, _S5_STATE))
                         for d in range(2)], axis=0) for i in range(2)])
    return wb.astype(_BF), lam, wc.astype(_BF)


def _s5_dir_kernel(uf_ref, ub_ref, wb_ref, lam_ref, wc_ref, yf_ref, yb_ref, buf_ref, bub_ref, xf_ref, xb_ref, st_ref):
    @pl.when(pl.program_id(0) == 0)
    def _():
        st_ref[...] = jnp.zeros_like(st_ref)

    n = _S5_STATE
    buf_ref[...] = jnp.dot(uf_ref[...].astype(_BF), wb_ref[0], preferred_element_type=jnp.float32)
    bub_ref[...] = jnp.dot(ub_ref[...].astype(_BF), wb_ref[1], preferred_element_type=jnp.float32)
    lr = lam_ref[0]
    li = lam_ref[1]
    fwd = lax.broadcasted_iota(jnp.int32, (_S5_ROWS, n), 0) < (_S5_ROWS // 2)

    def step(i, carry):
        xr, xi = carry
        rf = pl.ds(pl.multiple_of(i * _S5_ROWS, _S5_ROWS), _S5_ROWS)
        rb = pl.ds(pl.multiple_of((S5_STEPS - 1 - i) * _S5_ROWS, _S5_ROWS), _S5_ROWS)
        nr = lr * xr - li * xi + jnp.where(fwd, buf_ref[rf, :n], bub_ref[rb, :n])
        ni = lr * xi + li * xr + jnp.where(fwd, buf_ref[rf, n:], bub_ref[rb, n:])
        xf_ref[rf, :n] = nr
        xf_ref[rf, n:] = ni
        xb_ref[rb, :n] = nr
        xb_ref[rb, n:] = ni
        return nr, ni

    xr, xi = lax.fori_loop(0, S5_STEPS, step, (st_ref[0], st_ref[1]), unroll=4)
    st_ref[0] = xr
    st_ref[1] = xi
    yf_ref[...] = jnp.dot(xf_ref[...].astype(_BF), wc_ref[0], preferred_element_type=jnp.float32)
    yb_ref[...] = jnp.dot(xb_ref[...].astype(_BF), wc_ref[1], preferred_element_type=jnp.float32)


def _s5_dir_scan(u8, wb, lam, wc, *, ctx_len):
    rows = u8.shape[0]
    blk = S5_STEPS * _S5_ROWS
    n = _S5_STATE
    order = _scan_block_order(ctx_len // S5_STEPS, rows // blk, True)
    spec_f = pl.BlockSpec((blk, D_S5), lambda i: (i, 0))
    spec_b = pl.BlockSpec((blk, D_S5), lambda i: (order(i), 0))
    out = jax.ShapeDtypeStruct((rows, D_S5), jnp.float32)
    big = pltpu.VMEM((blk, 2 * n), jnp.float32)
    return pl.pallas_call(
        _s5_dir_kernel,
        grid=(rows // blk,),
        in_specs=[spec_f, spec_b,
                  pl.BlockSpec((2, D_S5, 2 * n), lambda i: (0, 0, 0)),
                  pl.BlockSpec((2, _S5_ROWS, n), lambda i: (0, 0, 0)),
                  pl.BlockSpec((2, 2 * n, D_S5), lambda i: (0, 0, 0))],
        out_specs=[spec_f, spec_b],
        out_shape=[out, out],
        scratch_shapes=[big, big, big, big, pltpu.VMEM((2, _S5_ROWS, n), jnp.float32)],
        compiler_params=pltpu.CompilerParams(dimension_semantics=("arbitrary",),
                                             vmem_limit_bytes=56 * 1024 * 1024),
        name="s5_scan",
    )(u8, u8, wb, lam, wc)


HG_BLOCK = 128
HG_BATCH_BLOCK = 2
_HG_PAIR = 2 * HG_HEAD


def _hgrn_dir_kernel(q_ref, v_ref, f_ref, lb_ref, o_ref, st_ref, *, reverse):
    @pl.when(pl.program_id(1) == 0)
    def _():
        st_ref[...] = jnp.zeros_like(st_ref)

    ch = HG_CHUNK
    n_ch = HG_BLOCK // ch
    r = lax.broadcasted_iota(jnp.int32, (HG_BLOCK, HG_BLOCK), 0)
    c = lax.broadcasted_iota(jnp.int32, (HG_BLOCK, HG_BLOCK), 1)
    tri = ((r // ch == c // ch) & ((r <= c) if reverse else (r >= c))).astype(jnp.float32)
    same_head = (lax.broadcasted_iota(jnp.int32, (_HG_PAIR, _HG_PAIR), 0) // HG_HEAD
                 == lax.broadcasted_iota(jnp.int32, (_HG_PAIR, _HG_PAIR), 1) // HG_HEAD)
    ones_blk = same_head.astype(_BF)
    tcol = lax.broadcasted_iota(jnp.int32, (ch, 1), 0)
    end = 0 if reverse else ch - 1
    n_pair = D_HG // _HG_PAIR
    chains = [(bi, p) for bi in range(q_ref.shape[0]) for p in range(n_pair)]
    pair_lanes = lambda p: slice(p * _HG_PAIR, (p + 1) * _HG_PAIR)
    intra, kv, dec, qd = {}, {}, {}, {}
    for bi in range(q_ref.shape[0]):
        q_raw = q_ref[bi]
        q_all = q_raw * _sigmoid(q_raw)
        forget = lb_ref[...] + (1.0 - lb_ref[...]) * _sigmoid(f_ref[bi])
        k_all = 1.0 - forget
        cum_all = _dot3(tri, jnp.log(forget))
        for p in range(n_pair):
            lanes = pair_lanes(p)
            for ci in range(n_ch):
                rows = slice(ci * ch, (ci + 1) * ch)
                q = q_all[rows, lanes]
                k = k_all[rows, lanes]
                v = v_ref[bi, rows, lanes]
                cum = cum_all[rows, lanes]
                last = cum[end:end + 1]
                w = [q * jnp.exp(jnp.minimum(cum - cum[s:s + 1], 0.0)) * k[s:s + 1] for s in range(ch)]
                rel = jnp.dot(jnp.concatenate(w, axis=0).astype(_BF), ones_blk,
                              preferred_element_type=jnp.float32)
                o = jnp.zeros((ch, _HG_PAIR), jnp.float32)
                for s in range(ch):
                    seen = (tcol <= s) if reverse else (tcol >= s)
                    o = o + jnp.where(seen, rel[s * ch:(s + 1) * ch], 0.0) * v[s:s + 1]
                intra[bi, p, ci] = o
                kv[bi, p, ci] = jnp.where(same_head, _dot(v, k * jnp.exp(last - cum), _TN), 0.0)
                dec[bi, p, ci] = jnp.exp(last)
                qd[bi, p, ci] = q * jnp.exp(cum)
    state = [st_ref[i] for i in range(len(chains))]
    for ci in (range(n_ch - 1, -1, -1) if reverse else range(n_ch)):
        for i, (bi, p) in enumerate(chains):
            o_ref[bi, ci * ch:(ci + 1) * ch, pair_lanes(p)] = intra[bi, p, ci] + _dot(qd[bi, p, ci], state[i], _NT)
            state[i] = state[i] * dec[bi, p, ci] + kv[bi, p, ci]
    for i in range(len(chains)):
        st_ref[i] = state[i]


def _hgrn_dir_scan(p, lb, *, ctx_len, reverse, col0=0):
    g, l, _ = p.shape
    d = D_HG
    order = _scan_block_order(ctx_len // HG_BLOCK, l // HG_BLOCK, reverse)
    bb = HG_BATCH_BLOCK if g % HG_BATCH_BLOCK == 0 else 1
    col = lambda c: pl.BlockSpec((bb, HG_BLOCK, d), lambda i, j: (i, order(j), col0 + c))
    dr = 1 if reverse else 0
    return pl.pallas_call(
        functools.partial(_hgrn_dir_kernel, reverse=reverse),
        grid=(g // bb, l // HG_BLOCK),
        in_specs=[col(0), col(1), col(2 + dr), pl.BlockSpec((1, d), lambda i, j: (0, 0))],
        out_specs=pl.BlockSpec((bb, HG_BLOCK, d), lambda i, j: (i, order(j), 0)),
        out_shape=jax.ShapeDtypeStruct((g, l, d), jnp.float32),
        scratch_shapes=[pltpu.VMEM((bb * d // _HG_PAIR, _HG_PAIR, _HG_PAIR), jnp.float32)],
        compiler_params=pltpu.CompilerParams(dimension_semantics=("parallel", "arbitrary")),
        name="hgrn_scan",
    )(p, p, p, lb[dr:dr + 1])


def _hgrn_post_kernel(of_ref, ob_ref, g_ref, ng_ref, ones_ref, y_ref):
    o = of_ref[0] + ob_ref[0]
    ms = jnp.dot((o * o).astype(_BF), ones_ref[...], preferred_element_type=jnp.float32) * (1.0 / HG_HEAD)
    g = g_ref[0]
    y_ref[0] = o * lax.rsqrt(ms + RMS_EPS) * ng_ref[...] * (g * _sigmoid(g))


def _hgrn_post(o_f, o_b, p, norm_g, col0=0):
    nb, l, d = o_f.shape
    spec = pl.BlockSpec((1, HG_BLOCK, d), lambda i, j: (i, j, 0))
    return pl.pallas_call(
        _hgrn_post_kernel,
        grid=(nb, l // HG_BLOCK),
        in_specs=[spec, spec, pl.BlockSpec((1, HG_BLOCK, d), lambda i, j: (i, j, col0 + 4)),
                  pl.BlockSpec((1, d), lambda i, j: (0, 0)), pl.BlockSpec((d, d), lambda i, j: (0, 0))],
        out_specs=spec,
        out_shape=jax.ShapeDtypeStruct((nb, l, d), jnp.float32),
        compiler_params=pltpu.CompilerParams(dimension_semantics=("parallel", "parallel")),
        name="hgrn_post",
    )(o_f, o_b, p, norm_g.reshape(1, d), _head_ones(d, HG_HEAD))


_RW_PAIR = 2 * RW_HEAD
RW_PRE_T = 128
RW_BATCH_BLOCK = 4


def _rwkv_pair_kernel(r_ref, lw_ref, k_ref, v_ref, kk_ref, a_ref, y_ref, st_ref, *, chunk, reverse):
    @pl.when(pl.program_id(1) == 0)
    def _():
        st_ref[...] = jnp.zeros_like(st_ref)

    row = lax.broadcasted_iota(jnp.int32, (chunk, chunk), 0)
    col = lax.broadcasted_iota(jnp.int32, (chunk, chunk), 1)
    strict = row < col if reverse else row > col
    incl = row <= col if reverse else row >= col
    end = 0 if reverse else chunk - 1
    tri = incl.astype(jnp.float32)
    eye = (row == col).astype(jnp.float32)
    blk8 = row // 8 == col // 8
    merge_masks = []
    size = 8
    while size < chunk:
        merge_masks.append((row // (2 * size) == col // (2 * size)) & (row // size != col // size))
        size *= 2
    lane_head = lax.broadcasted_iota(jnp.int32, (1, _RW_PAIR), 1) // RW_HEAD
    head0 = lane_head == 0
    same_head = (lax.broadcasted_iota(jnp.int32, (_RW_PAIR, _RW_PAIR), 0) // RW_HEAD
                 == lax.broadcasted_iota(jnp.int32, (_RW_PAIR, _RW_PAIR), 1) // RW_HEAD)
    n_pair = D_RW // _RW_PAIR
    ps = range(r_ref.shape[0] * n_pair)
    ph = [(p, h) for p in ps for h in range(2)]
    bi = [p // n_pair for p in ps]
    sl = [slice((p % n_pair) * _RW_PAIR, (p % n_pair + 1) * _RW_PAIR) for p in ps]
    lw = [lw_ref[bi[p], :, sl[p]] for p in ps]
    cum = [_dot3(tri, lw[p]) for p in ps]
    pw = [jnp.exp(cum[p]) for p in ps]
    p_inv = [jnp.exp(-cum[p]) for p in ps]
    p_end = [pw[p][end:end + 1, :] for p in ps]
    kk = [kk_ref[bi[p], :, sl[p]] for p in ps]
    bh = [kk[p] * a_ref[bi[p], :, sl[p]] * p_inv[p] for p in ps]
    kh = [k_ref[bi[p], :, sl[p]] * p_inv[p] for p in ps]
    v = [v_ref[bi[p], :, sl[p]] for p in ps]
    s0 = [st_ref[p] for p in ps]
    x1 = [jnp.concatenate([-kk[p] * jnp.exp(cum[p] - lw[p]), r_ref[bi[p], :, sl[p]] * pw[p]], axis=0) for p in ps]
    x1h = {(p, h): jnp.where(lane_head == h, x1[p], 0.0) for p, h in ph}
    g1 = {q: _dot(x1h[q], bh[q[0]], _NT) for q in ph}
    g2 = {q: _dot(x1h[q], kh[q[0]], _NT) for q in ph}
    hm = [_dot(x1[p], s0[p], _NT) for p in ps]
    ab = {q: jnp.where(strict, g1[q][:chunk], 0.0) for q in ph}
    d1 = {q: jnp.where(blk8, ab[q], 0.0) for q in ph}
    d2 = {q: _dot3(d1[q], d1[q]) for q in ph}
    inv = {q: eye + d1[q] for q in ph}
    inv = {q: inv[q] + _dot3(inv[q], d2[q]) for q in ph}
    d4 = {q: _dot3(d2[q], d2[q]) for q in ph}
    inv = {q: inv[q] + _dot3(inv[q], d4[q]) for q in ph}
    for m in merge_masks:
        li = {q: _dot(jnp.where(m, ab[q], 0.0), inv[q]) for q in ph}
        inv = {q: inv[q] + _dot(inv[q], li[q]) for q in ph}
    akv = {q: _dot(jnp.where(strict, g2[q][:chunk], 0.0), v[q[0]]) for q in ph}
    rhs = [hm[p][:chunk] + jnp.where(head0, akv[p, 0], akv[p, 1]) for p in ps]
    eh = {q: _dot(inv[q], rhs[q[0]]) for q in ph}
    e = [jnp.where(head0, eh[p, 0], eh[p, 1]) for p in ps]
    yh = {q: _dot(jnp.where(incl, g1[q][chunk:], 0.0), e[q[0]])
          + _dot(jnp.where(incl, g2[q][chunk:], 0.0), v[q[0]]) for q in ph}
    for p in ps:
        y_ref[bi[p], :, sl[p]] = hm[p][chunk:] + jnp.where(head0, yh[p, 0], yh[p, 1])
    for p in ps:
        ev = jnp.concatenate([e[p], v[p]], axis=0)
        x2 = jnp.concatenate([bh[p] * p_end[p], kh[p] * p_end[p]], axis=0)
        st_ref[p] = s0[p] * p_end[p] + jnp.where(same_head, _dot(ev, x2, _TN), 0.0)


def _rwkv_pair_scan(r, lw, k, v, kk, a, *, ctx_len, reverse):
    nb, l, d = r.shape
    chunk = RW_CHUNK
    order = _scan_block_order(ctx_len // chunk, l // chunk, reverse)
    bb = RW_BATCH_BLOCK if nb % RW_BATCH_BLOCK == 0 else 1
    spec = pl.BlockSpec((bb, chunk, d), lambda b, c: (b, order(c), 0))
    return pl.pallas_call(
        functools.partial(_rwkv_pair_kernel, chunk=chunk, reverse=reverse),
        grid=(nb // bb, l // chunk),
        in_specs=[spec] * 6,
        out_specs=spec,
        out_shape=jax.ShapeDtypeStruct((nb, l, d), jnp.float32),
        scratch_shapes=[pltpu.VMEM((bb * d // _RW_PAIR, _RW_PAIR, _RW_PAIR), jnp.float32)],
        compiler_params=pltpu.CompilerParams(dimension_semantics=("parallel", "arbitrary")),
        name="rwkv_scan",
    )(r, lw, k, v, kk, a)


def _softplus(z):
    return jnp.maximum(z, 0.0) + jnp.log(1.0 + jnp.exp(-jnp.abs(z)))


def _sigmoid(z):
    return 1.0 / (1.0 + jnp.exp(-z))


def _rwkv_pre_kernel(prev_ref, cur_ref, next_ref, mu_ref, vec_ref, gup_ref, wup_ref, aup_ref, ones_ref,
                     r_ref, v_ref, kk_ref, g_ref, bonus_ref, lw0_ref, k0_ref, a0_ref, lw1_ref, k1_ref, a1_ref,
                     ext_ref, s_ref, *, ctx_len, seq_len):
    t_rows, w = RW_PRE_T, GRID_W
    ext_ref[0:w] = prev_ref[0]
    ext_ref[w:w + t_rows] = cur_ref[0]
    ext_ref[w + t_rows:w + t_rows + w] = next_ref[0]
    t = pl.program_id(1) * t_rows + lax.broadcasted_iota(jnp.int32, (t_rows, 128), 0)
    lane4 = lax.broadcasted_iota(jnp.int32, (t_rows, 128), 1) % 4
    is_ctx = t < ctx_len
    tx = t - ctx_len
    col = tx % w
    is_lat = jnp.logical_not(is_ctx)
    use_p1 = (is_ctx & (lane4 % 2 == 0) & (t > 0)) | (is_lat & (lane4 == 0) & (col != 0))
    use_n1 = (is_ctx & (lane4 % 2 == 1) & (t < ctx_len - 1)) | (is_lat & (lane4 == 1) & (col != w - 1))
    use_p64 = is_lat & (lane4 == 2) & (tx >= w)
    use_n64 = is_lat & (lane4 == 3) & (tx < seq_len - w)
    for lt in range(RW_COLS // 128):
        lanes = slice(lt * 128, (lt + 1) * 128)
        p = ext_ref[w:w + t_rows, lanes]
        shifted = jnp.where(use_p1, ext_ref[w - 1:w - 1 + t_rows, lanes],
                            jnp.where(use_n1, ext_ref[w + 1:w + 1 + t_rows, lanes],
                                      jnp.where(use_p64, ext_ref[0:t_rows, lanes],
                                                jnp.where(use_n64, ext_ref[2 * w:2 * w + t_rows, lanes], 0.0))))
        s_ref[:, lanes] = p + mu_ref[:, lanes] * (shifted - p)
    d = D_RW
    r = s_ref[:, 0:d]
    k = s_ref[:, d:2 * d]
    v = s_ref[:, 2 * d:3 * d]
    o = 3 * d
    g_lo = s_ref[:, o:o + RW_G_LORA]
    wl = s_ref[:, o + RW_G_LORA:o + RW_G_LORA + 2 * RW_W_LORA]
    al = s_ref[:, o + RW_G_LORA + 2 * RW_W_LORA:o + RW_G_LORA + 2 * RW_W_LORA + 2 * RW_A_LORA]
    k_k, k_a, r_k = vec_ref[0:1], vec_ref[1:2], vec_ref[2:3]
    ones = ones_ref[...]
    r_ref[0] = r
    v_ref[0] = v
    g_ref[0] = jnp.dot(_sigmoid(g_lo).astype(_BF), gup_ref[...], preferred_element_type=jnp.float32)
    kkr = k * k_k
    ss = jnp.dot((kkr * kkr).astype(_BF), ones, preferred_element_type=jnp.float32)
    kk_ref[0] = kkr / jnp.maximum(jnp.sqrt(ss), 1e-12)
    w_pre = jnp.dot(jnp.tanh(wl).astype(_BF), wup_ref[...], preferred_element_type=jnp.float32)
    a_pre = jnp.dot(al.astype(_BF), aup_ref[...], preferred_element_type=jnp.float32)
    k_sum = 0.0
    for dr, (lw_o, k_o, a_o) in enumerate(((lw0_ref, k0_ref, a0_ref), (lw1_ref, k1_ref, a1_ref))):
        w_log = -_softplus(-(vec_ref[3 + dr:4 + dr] + w_pre[:, dr * d:(dr + 1) * d])) - 0.5
        a = _sigmoid(vec_ref[5 + dr:6 + dr] + a_pre[:, dr * d:(dr + 1) * d])
        k_d = k * (1.0 + (a - 1.0) * k_a)
        lw_o[0] = -jnp.exp(w_log)
        k_o[0] = k_d
        a_o[0] = a
        k_sum = k_sum + k_d
    bonus_ref[0] = jnp.dot((r * k_sum * r_k).astype(_BF), ones, preferred_element_type=jnp.float32)


def _head_ones(width, head):
    i = jnp.arange(width) // head
    return (i[:, None] == i[None, :]).astype(_BF)


def _rwkv_pre(p, mu, k_k, k_a, r_k, w0, a0, g_up, w_up, a_up, *, ctx_len):
    nb, l, cols = p.shape
    t_rows, w, d = RW_PRE_T, GRID_W, D_RW
    nblk = l // w
    per = t_rows // w
    zeros = jnp.zeros((RW_W_LORA, d), jnp.float32)
    wup = jnp.concatenate([jnp.concatenate([w_up[0], zeros], axis=1),
                           jnp.concatenate([zeros, w_up[1]], axis=1)], axis=0).astype(_BF)
    aup = jnp.concatenate([jnp.concatenate([a_up[0], zeros], axis=1),
                           jnp.concatenate([zeros, a_up[1]], axis=1)], axis=0).astype(_BF)
    vec = jnp.stack([k_k, k_a, r_k.reshape(d), w0[0], w0[1], a0[0], a0[1], jnp.zeros((d,), jnp.float32)])
    out = jax.ShapeDtypeStruct((nb, l, d), jnp.float32)
    out_spec = pl.BlockSpec((1, t_rows, d), lambda b, j: (b, j, 0))
    full = lambda shape: pl.BlockSpec(shape, lambda b, j: (0,) * len(shape))
    return pl.pallas_call(
        functools.partial(_rwkv_pre_kernel, ctx_len=ctx_len, seq_len=l - ctx_len),
        grid=(nb, l // t_rows),
        in_specs=[pl.BlockSpec((1, w, cols), lambda b, j: (b, jnp.maximum(j * per - 1, 0), 0)),
                  pl.BlockSpec((1, t_rows, cols), lambda b, j: (b, j, 0)),
                  pl.BlockSpec((1, w, cols), lambda b, j: (b, jnp.minimum(j * per + per, nblk - 1), 0)),
                  full((1, cols)), full((8, d)), full((RW_G_LORA, d)),
                  full((2 * RW_W_LORA, 2 * d)), full((2 * RW_A_LORA, 2 * d)), full((d, d))],
        out_specs=[out_spec] * 11,
        out_shape=[out] * 11,
        scratch_shapes=[pltpu.VMEM((t_rows + 2 * w, cols), jnp.float32), pltpu.VMEM((t_rows, cols), jnp.float32)],
        compiler_params=pltpu.CompilerParams(dimension_semantics=("parallel", "parallel"),
                                             vmem_limit_bytes=48 * 1024 * 1024),
        name="rwkv_pre",
    )(p, p, p, mu.reshape(1, cols), vec, g_up.astype(_BF), wup, aup, _head_ones(d, RW_HEAD))


def _rwkv_post_kernel(y0_ref, y1_ref, v_ref, g_ref, bonus_ref, gn_ref, ones_ref, o_ref):
    y = y0_ref[0] + y1_ref[0]
    ones = ones_ref[...]
    inv_n = 1.0 / RW_HEAD
    mu_y = jnp.dot(y.astype(_BF), ones, preferred_element_type=jnp.float32) * inv_n
    yc = y - mu_y
    var_y = jnp.dot((yc * yc).astype(_BF), ones, preferred_element_type=jnp.float32) * inv_n
    yn = yc * lax.rsqrt(var_y + RW_GN_EPS) * gn_ref[0:1] + gn_ref[1:2]
    o_ref[0] = (yn + bonus_ref[0] * v_ref[0]) * g_ref[0]


def _rwkv_post(y0, y1, v, g, bonus, gn_g, gn_b):
    nb, l, d = y0.shape
    spec = pl.BlockSpec((1, RW_PRE_T, d), lambda b, j: (b, j, 0))
    return pl.pallas_call(
        _rwkv_post_kernel,
        grid=(nb, l // RW_PRE_T),
        in_specs=[spec] * 5 + [pl.BlockSpec((2, d), lambda b, j: (0, 0)), pl.BlockSpec((d, d), lambda b, j: (0, 0))],
        out_specs=spec,
        out_shape=jax.ShapeDtypeStruct((nb, l, d), jnp.float32),
        compiler_params=pltpu.CompilerParams(dimension_semantics=("parallel", "parallel")),
        name="rwkv_post",
    )(y0, y1, v, g, bonus, jnp.stack([gn_g, gn_b]), _head_ones(d, RW_HEAD))


def _rwkv7_mixer_fused(p, lc, mu, w0, w_up, a0, a_up, g_up, k_k, k_a, r_k, gn_g, gn_b):
    r, v, kk, g, bonus, lw0, k0, a0_, lw1, k1, a1_ = _rwkv_pre(
        p, _f32(mu), _f32(k_k), _f32(k_a), _f32(r_k), _f32(w0), _f32(a0), _f32(g_up), _f32(w_up), _f32(a_up),
        ctx_len=lc)
    y0 = _rwkv_pair_scan(r, lw0, k0, v, kk, a0_, ctx_len=lc, reverse=False)
    y1 = _rwkv_pair_scan(r, lw1, k1, v, kk, a1_, ctx_len=lc, reverse=True)
    return _rwkv_post(y0, y1, v, g, bonus, _f32(gn_g), _f32(gn_b))


def _f32(t):
    return t.astype(jnp.float32)


def _layer_norm(x, g, b):
    xf = _f32(x)
    mu = jnp.mean(xf, -1, keepdims=True)
    var = jnp.mean(jnp.square(xf - mu), -1, keepdims=True)
    return ((xf - mu) * lax.rsqrt(var + LN_EPS) * _f32(g) + _f32(b)).astype(x.dtype)


def _s5_discretise(lam_re, lam_im, log_step, b_re, b_im):
    dt = jnp.exp(log_step)[:, None]
    mag = jnp.exp(lam_re * dt)
    ang = lam_im * dt
    lb_re, lb_im = mag * jnp.cos(ang), mag * jnp.sin(ang)
    den = lam_re * lam_re + lam_im * lam_im
    nr, ni = lb_re - 1.0, lb_im
    co_re = (nr * lam_re + ni * lam_im) / den
    co_im = (ni * lam_re - nr * lam_im) / den
    bb_re = co_re[..., None] * b_re - co_im[..., None] * b_im
    bb_im = co_re[..., None] * b_im + co_im[..., None] * b_re
    return lb_re, lb_im, bb_re, bb_im


def _s5_mixer(u, lc, lam_re, lam_im, log_step, b_re, b_im, c_re, c_im, d_skip, glu_w, glu_b):
    disc = [_s5_discretise(_f32(lam_re[d]), _f32(lam_im[d]), _f32(log_step[d]),
                           _f32(b_re[d]), _f32(b_im[d])) for d in range(2)]
    wb, lam, wc = _s5_weights(disc, _f32(c_re), _f32(c_im))
    u = _f32(u)
    nb, l, _ = u.shape
    assert 2 * nb == _S5_ROWS
    u_t = u.transpose(1, 0, 2)
    u8 = jnp.concatenate([u_t, u_t], axis=1).reshape(l * _S5_ROWS, D_S5)
    y_f, y_b = _s5_dir_scan(u8, wb, lam, wc, ctx_len=lc)
    y = (y_f.reshape(l, _S5_ROWS, D_S5)[:, :nb] + y_b.reshape(l, _S5_ROWS, D_S5)[:, nb:]).transpose(1, 0, 2)
    y = jax.nn.gelu(y + _f32(d_skip) * u)
    return y * jax.nn.sigmoid(_mm3(y, glu_w) + _f32(glu_b))


def _hgrn2_mixer(p, lc, lb, norm_g, col0=0):
    lb = _f32(lb)
    o_f = _hgrn_dir_scan(p, lb, ctx_len=lc, reverse=False, col0=col0)
    o_b = _hgrn_dir_scan(p, lb, ctx_len=lc, reverse=True, col0=col0)
    return _hgrn_post(o_f, o_b, p, _f32(norm_g), col0=col0)


ROW_TILE = 256


def _mod_spec(n_batch, tiles_per_batch, ctx_tiles):
    def index(i):
        return (jnp.where(i % tiles_per_batch < ctx_tiles, n_batch, i // tiles_per_batch), 0, 0)
    return pl.BlockSpec((1, 1, 6 * D_MODEL), index)


def _inproj_kernel(x_ref, mod_ref, w_ref, pa_ref, pc_ref):
    d = D_MODEL
    m = mod_ref[0]
    xm = (x_ref[...] * (1.0 + m[:, d:2 * d]) + m[:, 0:d]).astype(_BF)
    split = pa_ref.shape[1]
    pa_ref[...] = jnp.dot(xm, w_ref[:, :split], preferred_element_type=jnp.float32)
    pc_ref[...] = jnp.dot(xm, w_ref[:, split:], preferred_element_type=jnp.float32)


def _inproj(xs, mod, w, *, n_batch, ctx_len):
    t, d = xs.shape
    n_a = D_S5 + 5 * D_HG
    n_c = w.shape[1] - n_a
    tiles = t // ROW_TILE
    return pl.pallas_call(
        _inproj_kernel,
        grid=(tiles,),
        in_specs=[pl.BlockSpec((ROW_TILE, d), lambda i: (i, 0)),
                  _mod_spec(n_batch, tiles // n_batch, ctx_len // ROW_TILE),
                  pl.BlockSpec((d, n_a + n_c), lambda i: (0, 0))],
        out_specs=[pl.BlockSpec((ROW_TILE, n_a), lambda i: (i, 0)), pl.BlockSpec((ROW_TILE, n_c), lambda i: (i, 0))],
        out_shape=[jax.ShapeDtypeStruct((t, n_a), jnp.float32), jax.ShapeDtypeStruct((t, n_c), jnp.float32)],
        compiler_params=pltpu.CompilerParams(dimension_semantics=("parallel",),
                                             vmem_limit_bytes=48 * 1024 * 1024),
        name="in_proj",
    )(xs, mod, w.astype(_BF))


def _ln_rows(h, g, b):
    mu = jnp.mean(h, axis=-1, keepdims=True)
    hc = h - mu
    var = jnp.mean(hc * hc, axis=-1, keepdims=True)
    return hc * lax.rsqrt(var + LN_EPS) * g + b


def _outproj_ln_kernel(ya_ref, yb_ref, yc_ref, w_ref, xs_ref, mod_ref, ln_ref, x1_ref, h2_ref):
    d = D_MODEL
    mix = (jnp.dot(ya_ref[...].astype(_BF), w_ref[0:D_S5], preferred_element_type=jnp.float32)
           + jnp.dot(yb_ref[...].astype(_BF), w_ref[D_S5:D_S5 + D_HG], preferred_element_type=jnp.float32)
           + jnp.dot(yc_ref[...].astype(_BF), w_ref[D_S5 + D_HG:D_MIX], preferred_element_type=jnp.float32))
    m = mod_ref[0]
    x1 = _ln_rows(DEEPNORM_ALPHA * xs_ref[...] + m[:, 2 * d:3 * d] * mix, ln_ref[0:1], ln_ref[1:2])
    x1_ref[...] = x1
    h2_ref[...] = x1 * (1.0 + m[:, 4 * d:5 * d]) + m[:, 3 * d:4 * d]


def _outproj_ln(ya, yb, yc, w, xs, mod, ln_g, ln_b, *, n_batch, ctx_len):
    t, d = xs.shape
    tiles = t // ROW_TILE
    row = lambda n: pl.BlockSpec((ROW_TILE, n), lambda i: (i, 0))
    out = jax.ShapeDtypeStruct((t, d), jnp.float32)
    return pl.pallas_call(
        _outproj_ln_kernel,
        grid=(tiles,),
        in_specs=[row(D_S5), row(D_HG), row(D_RW), pl.BlockSpec((D_MIX, d), lambda i: (0, 0)), row(d),
                  _mod_spec(n_batch, tiles // n_batch, ctx_len // ROW_TILE),
                  pl.BlockSpec((2, d), lambda i: (0, 0))],
        out_specs=[row(d), row(d)],
        out_shape=[out, out],
        compiler_params=pltpu.CompilerParams(dimension_semantics=("parallel",)),
        name="out_proj_ln",
    )(ya, yb, yc, w.astype(_BF), xs, mod, jnp.stack([ln_g, ln_b]))


def kernel(x, c, ctx, c_ctx, ada_w, ada_b, w_in, w_out, ln1_g, ln1_b, ln2_g, ln2_b,
           s5_lam_re, s5_lam_im, s5_log_step, s5_b_re, s5_b_im, s5_c_re, s5_c_im,
           s5_d, s5_glu_w, s5_glu_b, hgrn_lb_logits, hgrn_norm_g,
           rwkv_mu, rwkv_w0, rwkv_w_up, rwkv_a0, rwkv_a_up, rwkv_g_up, rwkv_k_k, rwkv_k_a,
           rwkv_r_k, rwkv_gn_g, rwkv_gn_b, peer_wq, peer_keys, peer_u, peer_v):
    nb, lx, d = x.shape
    lc = ctx.shape[1]
    l_all = lc + lx
    assert lx % GRID_W == 0 and lc % ROW_TILE == 0 and lx % ROW_TILE == 0 and (nb * l_all) % PEER_TN == 0
    lb_cum = jnp.cumsum(jax.nn.softmax(_f32(hgrn_lb_logits), axis=1), axis=1)
    lb_all = lb_cum - lb_cum[:, :1]
    xs = jnp.concatenate([ctx, x], axis=1).reshape(nb * l_all, d)
    seq = lambda t: t.reshape(nb, l_all, t.shape[-1])
    for l in range(DEPTH):
        last = l == DEPTH - 1
        mod = jnp.concatenate([jax.nn.silu(c), jax.nn.silu(c_ctx)[None]], axis=0) @ ada_w[l] + ada_b[l]
        g2 = mod[:, None, 5 * d:]
        mod = mod[:, None, :]
        pa, pc = _inproj(xs, mod, w_in[l], n_batch=nb, ctx_len=lc)
        pa = seq(pa)
        ya = _s5_mixer(pa[..., :D_S5], lc, s5_lam_re[l], s5_lam_im[l], s5_log_step[l],
                       s5_b_re[l], s5_b_im[l], s5_c_re[l], s5_c_im[l], s5_d[l], s5_glu_w[l], s5_glu_b[l])
        yb = _hgrn2_mixer(pa, lc, lb_all[:, l], hgrn_norm_g[l], col0=D_S5 // D_HG)
        yc = _rwkv7_mixer_fused(seq(pc), lc, rwkv_mu[l], rwkv_w0[l], rwkv_w_up[l], rwkv_a0[l], rwkv_a_up[l],
                                rwkv_g_up[l], rwkv_k_k[l], rwkv_k_a[l], rwkv_r_k[l], rwkv_gn_g[l], rwkv_gn_b[l])
        flat = lambda t: t.reshape(nb * l_all, t.shape[-1])
        xs, h2 = _outproj_ln(flat(ya), flat(yb), flat(yc), w_out[l], xs, mod, ln1_g[l], ln1_b[l],
                             n_batch=nb, ctx_len=lc)
        if last:
            xs, h2 = seq(xs)[:, lc:], seq(h2)[:, lc:].reshape(nb * lx, d)
            gate = g2[:nb]
        else:
            xs = seq(xs)
            gate = jnp.where((jnp.arange(l_all) < lc)[None, :, None], g2[nb:], g2[:nb])
        ffn = _peer_ffn_dense(h2, peer_wq[l], peer_keys[l], peer_u[l].astype(_BF), peer_v[l].T.astype(_BF))
        xs = _layer_norm(DEEPNORM_ALPHA * xs + gate * ffn.reshape(xs.shape), ln2_g[l], ln2_b[l])
        xs = xs.reshape(-1, d)
    return xs.reshape(nb, lx, d)
```

```python
import functools
import math

import jax
import jax.numpy as jnp
from jax import lax
from jax.experimental import pallas as pl
from jax.experimental.pallas import tpu as pltpu

D_MODEL = 1024
DEPTH = 2
GRID_W = 64
D_S5 = D_MODEL // 4
D_HG = D_MODEL // 4
D_RW = D_MODEL // 2
D_MIX = D_S5 + D_HG + D_RW
S5_H = 16
S5_G = D_S5 // S5_H
S5_P = 64
HG_HEAD = 64
HG_CHUNK = 16
RW_HEAD = 64
RW_W_LORA = 64
RW_A_LORA = 64
RW_G_LORA = 128
RW_COLS = 3 * D_RW + RW_G_LORA + 2 * RW_W_LORA + 2 * RW_A_LORA
RW_GN_EPS = 64e-5
PEER_HEADS = 8
PEER_NKEYS = 128
PEER_N = PEER_NKEYS * PEER_NKEYS
PEER_QDIM = 256
PEER_TOPK = 16
LN_EPS = 1e-5
RMS_EPS = 1e-6
DEEPNORM_ALPHA = (2.0 * DEPTH) ** 0.25


def _mm_kernel(x_ref, w_ref, o_ref):
    o_ref[...] = jnp.dot(x_ref[...].astype(jnp.bfloat16), w_ref[...],
                         preferred_element_type=jnp.float32)


def _pick_tile(n, cands):
    for c in cands:
        if n % c == 0:
            return c
    return n


def _matmul(x, w):
    m, k = x.shape
    n = w.shape[1]
    tm = _pick_tile(m, (512, 256, 128, 8))
    tn = _pick_tile(n, (1152, 1024, 512, 256, 128))
    return pl.pallas_call(
        _mm_kernel,
        grid=(m // tm, n // tn),
        in_specs=[pl.BlockSpec((tm, k), lambda i, j: (i, 0)),
                  pl.BlockSpec((k, tn), lambda i, j: (0, j))],
        out_specs=pl.BlockSpec((tm, tn), lambda i, j: (i, j)),
        out_shape=jax.ShapeDtypeStruct((m, n), jnp.float32),
        compiler_params=pltpu.CompilerParams(
            dimension_semantics=("parallel", "parallel"),
            vmem_limit_bytes=48 * 1024 * 1024),
    )(x, w.astype(jnp.bfloat16))


def _mm3(x, w):
    b, l, k = x.shape
    return _matmul(x.reshape(b * l, k), w).reshape(b, l, w.shape[1])


_BF = jnp.bfloat16
_NN = (((1,), (0,)), ((), ()))
_NT = (((1,), (1,)), ((), ()))
_TN = (((0,), (0,)), ((), ()))


def _dot(a, b, dims=_NN):
    return lax.dot_general(a.astype(_BF), b.astype(_BF), dims,
                           preferred_element_type=jnp.float32)


def _split_bf16(a):
    hi = a.astype(_BF)
    return hi, (a - hi.astype(jnp.float32)).astype(_BF)


def _dot3(a, b, dims=_NN):
    a_hi, a_lo = _split_bf16(a)
    b_hi, b_lo = _split_bf16(b)
    d = functools.partial(lax.dot_general, dimension_numbers=dims,
                          preferred_element_type=jnp.float32)
    return d(a_hi, b_hi) + (d(a_hi, b_lo) + d(a_lo, b_hi))


RW_CHUNK = 64


def _scan_block_order(n_ctx, n_all, reverse):
    if not reverse:
        return lambda c: c
    return lambda c: jnp.where(c < n_ctx, n_ctx - 1 - c, n_all - 1 - c + n_ctx)


PEER_TN = 1024
PEER_EBLK = 1024
_PEER_SUB = 256
_PEER_OUT = 512
_NEG = -3.0e38


def _top_rows(s, n):
    vals = []
    for _ in range(n):
        m = jnp.max(s, axis=0, keepdims=True)
        vals.append(m)
        s = jnp.where(s == m, _NEG, s)
    return vals


def _peer_route_kernel(h_ref, wq_ref, keys_ref, xb_ref, pi_ref, c1_ref, p2_ref):
    hb = h_ref[...].astype(_BF)
    xb_ref[...] = h_ref[...].T.astype(_BF)
    q = jnp.dot(hb, wq_ref[...], preferred_element_type=jnp.float32)
    half = PEER_QDIM // 2
    for h in range(PEER_HEADS):
        s1 = _dot(keys_ref[2 * h], q[:, (2 * h) * half:(2 * h + 1) * half], _NT)
        s2 = _dot(keys_ref[2 * h + 1], q[:, (2 * h + 1) * half:(2 * h + 2) * half], _NT)
        t1 = _top_rows(s1, PEER_TOPK + 1)
        t2 = _top_rows(s2, PEER_TOPK + 1)
        cand = [t1[j1] + t2[j2] for j1 in range(PEER_TOPK + 1) for j2 in range(PEER_TOPK + 1)
                if (j1 + 1) * (j2 + 1) <= PEER_TOPK + 1]
        cand += [jnp.full_like(cand[0], _NEG)] * (-len(cand) % 8)
        top = _top_rows(jnp.concatenate(cand, axis=0), PEER_TOPK + 1)
        theta = 0.5 * (top[PEER_TOPK - 1] + top[PEER_TOPK])
        z = top[0] * 0.0
        for c in top[:PEER_TOPK]:
            z = z + jnp.exp(c - top[0])
        pi_ref[h] = jnp.exp(jnp.maximum(theta - s1 - t2[0], -80.0))
        c1_ref[h] = jnp.exp(s1 - t1[0]) / z
        p2_ref[h] = jnp.exp(s2 - t2[0])


def _peer_route(hf, wq, keys):
    t, d = hf.shape
    tn = PEER_TN
    nq = PEER_HEADS * PEER_QDIM
    aux = jax.ShapeDtypeStruct((PEER_HEADS, PEER_NKEYS, t), jnp.float32)
    aux_spec = pl.BlockSpec((PEER_HEADS, PEER_NKEYS, tn), lambda i: (0, 0, i))
    return pl.pallas_call(
        _peer_route_kernel,
        grid=(t // tn,),
        in_specs=[pl.BlockSpec((tn, d), lambda i: (i, 0)),
                  pl.BlockSpec((d, nq), lambda i: (0, 0)),
                  pl.BlockSpec((PEER_HEADS * 2, PEER_NKEYS, PEER_QDIM // 2), lambda i: (0, 0, 0))],
        out_specs=[pl.BlockSpec((d, tn), lambda i: (0, i))] + [aux_spec] * 3,
        out_shape=[jax.ShapeDtypeStruct((d, t), _BF)] + [aux] * 3,
        compiler_params=pltpu.CompilerParams(dimension_semantics=("parallel",),
                                             vmem_limit_bytes=56 * 1024 * 1024),
        name="peer_route",
    )(hf, wq.astype(_BF), keys.reshape(PEER_HEADS * 2, PEER_NKEYS, PEER_QDIM // 2))


def _gelu_tanh(z):
    c = math.sqrt(2.0 / math.pi)
    hz = 0.5 * z
    return hz + hz * jnp.tanh(z * (c + (c * 0.044715) * (z * z)))


def _peer_expert_kernel(x_ref, u_ref, vt_ref, pi_ref, c1_ref, p2_ref, o_ref, acc_ref, at_ref):
    j = pl.program_id(1)

    @pl.when(j == 0)
    def _():
        acc_ref[...] = jnp.zeros_like(acc_ref)

    nk = PEER_NKEYS
    n_slab = PEER_EBLK // nk
    rows = pl.ds(pl.multiple_of(j * n_slab, n_slab), n_slab)
    sub = _PEER_SUB
    per = sub // nk

    def scores(sb):
        return jnp.dot(u_ref[sb * sub:(sb + 1) * sub, :], x_ref[...],
                       preferred_element_type=jnp.float32)

    zt = scores(0)
    for sb in range(PEER_EBLK // sub):
        zt_next = scores(sb + 1) if (sb + 1) * sub < PEER_EBLK else None
        for k in range(per):
            s = sb * per + k
            for lt in range(PEER_TN // 128):
                lanes = pl.ds(lt * 128, 128)
                g = jnp.zeros((nk, 128), jnp.float32)
                for h in range(PEER_HEADS):
                    pi = pi_ref[h, rows, lanes][s:s + 1]
                    c1 = c1_ref[h, rows, lanes][s:s + 1]
                    p2 = p2_ref[h, :, lanes]
                    g = g + jnp.where(p2 >= pi, p2 * c1, 0.0)
                z = zt[k * nk:(k + 1) * nk, lt * 128:(lt + 1) * 128]
                at_ref[s * nk:(s + 1) * nk, lt * 128:(lt + 1) * 128] = (g * _gelu_tanh(z)).astype(_BF)
        zt = zt_next
        done = (sb + 1) * sub
        if done % _PEER_OUT == 0:
            part = slice(done - _PEER_OUT, done)
            acc_ref[...] += jnp.dot(vt_ref[:, part], at_ref[part, :], preferred_element_type=jnp.float32)

    @pl.when(j == pl.num_programs(1) - 1)
    def _():
        o_ref[...] = acc_ref[...].T


def _peer_expert(xb, u_b, vt_b, aux):
    d, t = xb.shape
    tn, eb = PEER_TN, PEER_EBLK
    aux_spec = pl.BlockSpec((PEER_HEADS, PEER_NKEYS, tn), lambda i, j: (0, 0, i))
    return pl.pallas_call(
        _peer_expert_kernel,
        grid=(t // tn, PEER_N // eb),
        in_specs=[pl.BlockSpec((d, tn), lambda i, j: (0, i)),
                  pl.BlockSpec((eb, d), lambda i, j: (j, 0)),
                  pl.BlockSpec((d, eb), lambda i, j: (0, j))] + [aux_spec] * 3,
        out_specs=pl.BlockSpec((tn, d), lambda i, j: (i, 0)),
        out_shape=jax.ShapeDtypeStruct((t, d), jnp.float32),
        scratch_shapes=[pltpu.VMEM((d, tn), jnp.float32), pltpu.VMEM((eb, tn), _BF)],
        compiler_params=pltpu.CompilerParams(dimension_semantics=("parallel", "arbitrary"),
                                             vmem_limit_bytes=56 * 1024 * 1024),
        name="peer_expert",
    )(xb, u_b, vt_b, *aux)


def _peer_ffn_dense(hf, wq, keys, u_b, vt_b):
    xb, *aux = _peer_route(hf, wq, keys)
    return _peer_expert(xb, u_b, vt_b, aux)


S5_STEPS = 128
_S5_ROWS = 8
_S5_STATE = S5_G * S5_P


def _s5_weights(disc, c_re, c_im):
    eye = jnp.eye(S5_G, dtype=jnp.float32)

    def bdiag_in(bb):
        return jnp.einsum('gph,gk->ghkp', bb, eye).reshape(D_S5, _S5_STATE)

    def bdiag_out(cc):
        return jnp.einsum('ghp,gk->gpkh', cc, eye).reshape(_S5_STATE, D_S5)

    wb = jnp.stack([jnp.concatenate([bdiag_in(disc[d][2]), bdiag_in(disc[d][3])], axis=1) for d in range(2)])
    wc = jnp.stack([jnp.concatenate([bdiag_out(c_re[d]), -bdiag_out(c_im[d])], axis=0) for d in range(2)])
    half = _S5_ROWS // 2
    lam = jnp.stack([
        jnp.concatenate([jnp.broadcast_to(disc[d][i].reshape(1, _S5_STATE), (half, _S5_STATE))
                         for d in range(2)], axis=0) for i in range(2)])
    return wb.astype(_BF), lam, wc.astype(_BF)


def _s5_dir_kernel(uf_ref, ub_ref, wb_ref, lam_ref, wc_ref, yf_ref, yb_ref, buf_ref, bub_ref, xf_ref, xb_ref, st_ref):
    @pl.when(pl.program_id(0) == 0)
    def _():
        st_ref[...] = jnp.zeros_like(st_ref)

    n = _S5_STATE
    buf_ref[...] = jnp.dot(uf_ref[...].astype(_BF), wb_ref[0], preferred_element_type=jnp.float32)
    bub_ref[...] = jnp.dot(ub_ref[...].astype(_BF), wb_ref[1], preferred_element_type=jnp.float32)
    lr = lam_ref[0]
    li = lam_ref[1]
    fwd = lax.broadcasted_iota(jnp.int32, (_S5_ROWS, n), 0) < (_S5_ROWS // 2)

    def step(i, carry):
        xr, xi = carry
        rf = pl.ds(pl.multiple_of(i * _S5_ROWS, _S5_ROWS), _S5_ROWS)
        rb = pl.ds(pl.multiple_of((S5_STEPS - 1 - i) * _S5_ROWS, _S5_ROWS), _S5_ROWS)
        nr = lr * xr - li * xi + jnp.where(fwd, buf_ref[rf, :n], bub_ref[rb, :n])
        ni = lr * xi + li * xr + jnp.where(fwd, buf_ref[rf, n:], bub_ref[rb, n:])
        xf_ref[rf, :n] = nr
        xf_ref[rf, n:] = ni
        xb_ref[rb, :n] = nr
        xb_ref[rb, n:] = ni
        return nr, ni

    xr, xi = lax.fori_loop(0, S5_STEPS, step, (st_ref[0], st_ref[1]), unroll=4)
    st_ref[0] = xr
    st_ref[1] = xi
    yf_ref[...] = jnp.dot(xf_ref[...].astype(_BF), wc_ref[0], preferred_element_type=jnp.float32)
    yb_ref[...] = jnp.dot(xb_ref[...].astype(_BF), wc_ref[1], preferred_element_type=jnp.float32)


def _s5_dir_scan(u8, wb, lam, wc, *, ctx_len):
    rows = u8.shape[0]
    blk = S5_STEPS * _S5_ROWS
    n = _S5_STATE
    order = _scan_block_order(ctx_len // S5_STEPS, rows // blk, True)
    spec_f = pl.BlockSpec((blk, D_S5), lambda i: (i, 0))
    spec_b = pl.BlockSpec((blk, D_S5), lambda i: (order(i), 0))
    out = jax.ShapeDtypeStruct((rows, D_S5), jnp.float32)
    big = pltpu.VMEM((blk, 2 * n), jnp.float32)
    return pl.pallas_call(
        _s5_dir_kernel,
        grid=(rows // blk,),
        in_specs=[spec_f, spec_b,
                  pl.BlockSpec((2, D_S5, 2 * n), lambda i: (0, 0, 0)),
                  pl.BlockSpec((2, _S5_ROWS, n), lambda i: (0, 0, 0)),
                  pl.BlockSpec((2, 2 * n, D_S5), lambda i: (0, 0, 0))],
        out_specs=[spec_f, spec_b],
        out_shape=[out, out],
        scratch_shapes=[big, big, big, big, pltpu.VMEM((2, _S5_ROWS, n), jnp.float32)],
        compiler_params=pltpu.CompilerParams(dimension_semantics=("arbitrary",),
                                             vmem_limit_bytes=56 * 1024 * 1024),
        name="s5_scan",
    )(u8, u8, wb, lam, wc)


HG_BLOCK = 128
HG_BATCH_BLOCK = 2
_HG_PAIR = 2 * HG_HEAD


def _hgrn_dir_kernel(q_ref, v_ref, f_ref, lb_ref, o_ref, st_ref, *, reverse):
    @pl.when(pl.program_id(1) == 0)
    def _():
        st_ref[...] = jnp.zeros_like(st_ref)

    ch = HG_CHUNK
    n_ch = HG_BLOCK // ch
    r = lax.broadcasted_iota(jnp.int32, (HG_BLOCK, HG_BLOCK), 0)
    c = lax.broadcasted_iota(jnp.int32, (HG_BLOCK, HG_BLOCK), 1)
    tri = ((r // ch == c // ch) & ((r <= c) if reverse else (r >= c))).astype(jnp.float32)
    same_head = (lax.broadcasted_iota(jnp.int32, (_HG_PAIR, _HG_PAIR), 0) // HG_HEAD
                 == lax.broadcasted_iota(jnp.int32, (_HG_PAIR, _HG_PAIR), 1) // HG_HEAD)
    ones_blk = same_head.astype(_BF)
    tcol = lax.broadcasted_iota(jnp.int32, (ch, 1), 0)
    end = 0 if reverse else ch - 1
    n_pair = D_HG // _HG_PAIR
    chains = [(bi, p) for bi in range(q_ref.shape[0]) for p in range(n_pair)]
    pair_lanes = lambda p: slice(p * _HG_PAIR, (p + 1) * _HG_PAIR)
    intra, kv, dec, qd = {}, {}, {}, {}
    for bi in range(q_ref.shape[0]):
        q_raw = q_ref[bi]
        q_all = q_raw * _sigmoid(q_raw)
        forget = lb_ref[...] + (1.0 - lb_ref[...]) * _sigmoid(f_ref[bi])
        k_all = 1.0 - forget
        cum_all = _dot3(tri, jnp.log(forget))
        for p in range(n_pair):
            lanes = pair_lanes(p)
            for ci in range(n_ch):
                rows = slice(ci * ch, (ci + 1) * ch)
                q = q_all[rows, lanes]
                k = k_all[rows, lanes]
                v = v_ref[bi, rows, lanes]
                cum = cum_all[rows, lanes]
                last = cum[end:end + 1]
                w = [q * jnp.exp(jnp.minimum(cum - cum[s:s + 1], 0.0)) * k[s:s + 1] for s in range(ch)]
                rel = jnp.dot(jnp.concatenate(w, axis=0).astype(_BF), ones_blk,
                              preferred_element_type=jnp.float32)
                o = jnp.zeros((ch, _HG_PAIR), jnp.float32)
                for s in range(ch):
                    seen = (tcol <= s) if reverse else (tcol >= s)
                    o = o + jnp.where(seen, rel[s * ch:(s + 1) * ch], 0.0) * v[s:s + 1]
                intra[bi, p, ci] = o
                kv[bi, p, ci] = jnp.where(same_head, _dot(v, k * jnp.exp(last - cum), _TN), 0.0)
                dec[bi, p, ci] = jnp.exp(last)
                qd[bi, p, ci] = q * jnp.exp(cum)
    state = [st_ref[i] for i in range(len(chains))]
    for ci in (range(n_ch - 1, -1, -1) if reverse else range(n_ch)):
        for i, (bi, p) in enumerate(chains):
            o_ref[bi, ci * ch:(ci + 1) * ch, pair_lanes(p)] = intra[bi, p, ci] + _dot(qd[bi, p, ci], state[i], _NT)
            state[i] = state[i] * dec[bi, p, ci] + kv[bi, p, ci]
    for i in range(len(chains)):
        st_ref[i] = state[i]


def _hgrn_dir_scan(p, lb, *, ctx_len, reverse, col0=0):
    g, l, _ = p.shape
    d = D_HG
    order = _scan_block_order(ctx_len // HG_BLOCK, l // HG_BLOCK, reverse)
    bb = HG_BATCH_BLOCK if g % HG_BATCH_BLOCK == 0 else 1
    col = lambda c: pl.BlockSpec((bb, HG_BLOCK, d), lambda i, j: (i, order(j), col0 + c))
    dr = 1 if reverse else 0
    return pl.pallas_call(
        functools.partial(_hgrn_dir_kernel, reverse=reverse),
        grid=(g // bb, l // HG_BLOCK),
        in_specs=[col(0), col(1), col(2 + dr), pl.BlockSpec((1, d), lambda i, j: (0, 0))],
        out_specs=pl.BlockSpec((bb, HG_BLOCK, d), lambda i, j: (i, order(j), 0)),
        out_shape=jax.ShapeDtypeStruct((g, l, d), jnp.float32),
        scratch_shapes=[pltpu.VMEM((bb * d // _HG_PAIR, _HG_PAIR, _HG_PAIR), jnp.float32)],
        compiler_params=pltpu.CompilerParams(dimension_semantics=("parallel", "arbitrary")),
        name="hgrn_scan",
    )(p, p, p, lb[dr:dr + 1])


def _hgrn_post_kernel(of_ref, ob_ref, g_ref, ng_ref, ones_ref, y_ref):
    o = of_ref[0] + ob_ref[0]
    ms = jnp.dot((o * o).astype(_BF), ones_ref[...], preferred_element_type=jnp.float32) * (1.0 / HG_HEAD)
    g = g_ref[0]
    y_ref[0] = o * lax.rsqrt(ms + RMS_EPS) * ng_ref[...] * (g * _sigmoid(g))


def _hgrn_post(o_f, o_b, p, norm_g, col0=0):
    nb, l, d = o_f.shape
    spec = pl.BlockSpec((1, HG_BLOCK, d), lambda i, j: (i, j, 0))
    return pl.pallas_call(
        _hgrn_post_kernel,
        grid=(nb, l // HG_BLOCK),
        in_specs=[spec, spec, pl.BlockSpec((1, HG_BLOCK, d), lambda i, j: (i, j, col0 + 4)),
                  pl.BlockSpec((1, d), lambda i, j: (0, 0)), pl.BlockSpec((d, d), lambda i, j: (0, 0))],
        out_specs=spec,
        out_shape=jax.ShapeDtypeStruct((nb, l, d), jnp.float32),
        compiler_params=pltpu.CompilerParams(dimension_semantics=("parallel", "parallel")),
        name="hgrn_post",
    )(o_f, o_b, p, norm_g.reshape(1, d), _head_ones(d, HG_HEAD))


_RW_PAIR = 2 * RW_HEAD
RW_PRE_T = 128
RW_BATCH_BLOCK = 4


def _rwkv_pair_kernel(r_ref, lw_ref, k_ref, v_ref, kk_ref, a_ref, y_ref, st_ref, *, chunk, reverse):
    @pl.when(pl.program_id(1) == 0)
    def _():
        st_ref[...] = jnp.zeros_like(st_ref)

    row = lax.broadcasted_iota(jnp.int32, (chunk, chunk), 0)
    col = lax.broadcasted_iota(jnp.int32, (chunk, chunk), 1)
    strict = row < col if reverse else row > col
    incl = row <= col if reverse else row >= col
    end = 0 if reverse else chunk - 1
    tri = incl.astype(jnp.float32)
    eye = (row == col).astype(jnp.float32)
    blk8 = row // 8 == col // 8
    merge_masks = []
    size = 8
    while size < chunk:
        merge_masks.append((row // (2 * size) == col // (2 * size)) & (row // size != col // size))
        size *= 2
    lane_head = lax.broadcasted_iota(jnp.int32, (1, _RW_PAIR), 1) // RW_HEAD
    head0 = lane_head == 0
    same_head = (lax.broadcasted_iota(jnp.int32, (_RW_PAIR, _RW_PAIR), 0) // RW_HEAD
                 == lax.broadcasted_iota(jnp.int32, (_RW_PAIR, _RW_PAIR), 1) // RW_HEAD)
    n_pair = D_RW // _RW_PAIR
    ps = range(r_ref.shape[0] * n_pair)
    ph = [(p, h) for p in ps for h in range(2)]
    bi = [p // n_pair for p in ps]
    sl = [slice((p % n_pair) * _RW_PAIR, (p % n_pair + 1) * _RW_PAIR) for p in ps]
    lw = [lw_ref[bi[p], :, sl[p]] for p in ps]
    cum = [_dot3(tri, lw[p]) for p in ps]
    pw = [jnp.exp(cum[p]) for p in ps]
    p_inv = [jnp.exp(-cum[p]) for p in ps]
    p_end = [pw[p][end:end + 1, :] for p in ps]
    kk = [kk_ref[bi[p], :, sl[p]] for p in ps]
    bh = [kk[p] * a_ref[bi[p], :, sl[p]] * p_inv[p] for p in ps]
    kh = [k_ref[bi[p], :, sl[p]] * p_inv[p] for p in ps]
    v = [v_ref[bi[p], :, sl[p]] for p in ps]
    s0 = [st_ref[p] for p in ps]
    x1 = [jnp.concatenate([-kk[p] * jnp.exp(cum[p] - lw[p]), r_ref[bi[p], :, sl[p]] * pw[p]], axis=0) for p in ps]
    x1h = {(p, h): jnp.where(lane_head == h, x1[p], 0.0) for p, h in ph}
    g1 = {q: _dot(x1h[q], bh[q[0]], _NT) for q in ph}
    g2 = {q: _dot(x1h[q], kh[q[0]], _NT) for q in ph}
    hm = [_dot(x1[p], s0[p], _NT) for p in ps]
    ab = {q: jnp.where(strict, g1[q][:chunk], 0.0) for q in ph}
    d1 = {q: jnp.where(blk8, ab[q], 0.0) for q in ph}
    d2 = {q: _dot3(d1[q], d1[q]) for q in ph}
    inv = {q: eye + d1[q] for q in ph}
    inv = {q: inv[q] + _dot3(inv[q], d2[q]) for q in ph}
    d4 = {q: _dot3(d2[q], d2[q]) for q in ph}
    inv = {q: inv[q] + _dot3(inv[q], d4[q]) for q in ph}
    for m in merge_masks:
        li = {q: _dot(jnp.where(m, ab[q], 0.0), inv[q]) for q in ph}
        inv = {q: inv[q] + _dot(inv[q], li[q]) for q in ph}
    akv = {q: _dot(jnp.where(strict, g2[q][:chunk], 0.0), v[q[0]]) for q in ph}
    rhs = [hm[p][:chunk] + jnp.where(head0, akv[p, 0], akv[p, 1]) for p in ps]
    eh = {q: _dot(inv[q], rhs[q[0]]) for q in ph}
    e = [jnp.where(head0, eh[p, 0], eh[p, 1]) for p in ps]
    yh = {q: _dot(jnp.where(incl, g1[q][chunk:], 0.0), e[q[0]])
          + _dot(jnp.where(incl, g2[q][chunk:], 0.0), v[q[0]]) for q in ph}
    for p in ps:
        y_ref[bi[p], :, sl[p]] = hm[p][chunk:] + jnp.where(head0, yh[p, 0], yh[p, 1])
    for p in ps:
        ev = jnp.concatenate([e[p], v[p]], axis=0)
        x2 = jnp.concatenate([bh[p] * p_end[p], kh[p] * p_end[p]], axis=0)
        st_ref[p] = s0[p] * p_end[p] + jnp.where(same_head, _dot(ev, x2, _TN), 0.0)


def _rwkv_pair_scan(r, lw, k, v, kk, a, *, ctx_len, reverse):
    nb, l, d = r.shape
    chunk = RW_CHUNK
    order = _scan_block_order(ctx_len // chunk, l // chunk, reverse)
    bb = RW_BATCH_BLOCK if nb % RW_BATCH_BLOCK == 0 else 1
    spec = pl.BlockSpec((bb, chunk, d), lambda b, c: (b, order(c), 0))
    return pl.pallas_call(
        functools.partial(_rwkv_pair_kernel, chunk=chunk, reverse=reverse),
        grid=(nb // bb, l // chunk),
        in_specs=[spec] * 6,
        out_specs=spec,
        out_shape=jax.ShapeDtypeStruct((nb, l, d), jnp.float32),
        scratch_shapes=[pltpu.VMEM((bb * d // _RW_PAIR, _RW_PAIR, _RW_PAIR), jnp.float32)],
        compiler_params=pltpu.CompilerParams(dimension_semantics=("parallel", "arbitrary")),
        name="rwkv_scan",
    )(r, lw, k, v, kk, a)


def _softplus(z):
    return jnp.maximum(z, 0.0) + jnp.log(1.0 + jnp.exp(-jnp.abs(z)))


def _sigmoid(z):
    return 1.0 / (1.0 + jnp.exp(-z))


def _rwkv_pre_kernel(prev_ref, cur_ref, next_ref, mu_ref, vec_ref, gup_ref, wup_ref, aup_ref, ones_ref,
                     r_ref, v_ref, kk_ref, g_ref, bonus_ref, lw0_ref, k0_ref, a0_ref, lw1_ref, k1_ref, a1_ref,
                     ext_ref, s_ref, *, ctx_len, seq_len):
    t_rows, w = RW_PRE_T, GRID_W
    ext_ref[0:w] = prev_ref[0]
    ext_ref[w:w + t_rows] = cur_ref[0]
    ext_ref[w + t_rows:w + t_rows + w] = next_ref[0]
    t = pl.program_id(1) * t_rows + lax.broadcasted_iota(jnp.int32, (t_rows, 128), 0)
    lane4 = lax.broadcasted_iota(jnp.int32, (t_rows, 128), 1) % 4
    is_ctx = t < ctx_len
    tx = t - ctx_len
    col = tx % w
    is_lat = jnp.logical_not(is_ctx)
    use_p1 = (is_ctx & (lane4 % 2 == 0) & (t > 0)) | (is_lat & (lane4 == 0) & (col != 0))
    use_n1 = (is_ctx & (lane4 % 2 == 1) & (t < ctx_len - 1)) | (is_lat & (lane4 == 1) & (col != w - 1))
    use_p64 = is_lat & (lane4 == 2) & (tx >= w)
    use_n64 = is_lat & (lane4 == 3) & (tx < seq_len - w)
    for lt in range(RW_COLS // 128):
        lanes = slice(lt * 128, (lt + 1) * 128)
        p = ext_ref[w:w + t_rows, lanes]
        shifted = jnp.where(use_p1, ext_ref[w - 1:w - 1 + t_rows, lanes],
                            jnp.where(use_n1, ext_ref[w + 1:w + 1 + t_rows, lanes],
                                      jnp.where(use_p64, ext_ref[0:t_rows, lanes],
                                                jnp.where(use_n64, ext_ref[2 * w:2 * w + t_rows, lanes], 0.0))))
        s_ref[:, lanes] = p + mu_ref[:, lanes] * (shifted - p)
    d = D_RW
    r = s_ref[:, 0:d]
    k = s_ref[:, d:2 * d]
    v = s_ref[:, 2 * d:3 * d]
    o = 3 * d
    g_lo = s_ref[:, o:o + RW_G_LORA]
    wl = s_ref[:, o + RW_G_LORA:o + RW_G_LORA + 2 * RW_W_LORA]
    al = s_ref[:, o + RW_G_LORA + 2 * RW_W_LORA:o + RW_G_LORA + 2 * RW_W_LORA + 2 * RW_A_LORA]
    k_k, k_a, r_k = vec_ref[0:1], vec_ref[1:2], vec_ref[2:3]
    ones = ones_ref[...]
    r_ref[0] = r
    v_ref[0] = v
    g_ref[0] = jnp.dot(_sigmoid(g_lo).astype(_BF), gup_ref[...], preferred_element_type=jnp.float32)
    kkr = k * k_k
    ss = jnp.dot((kkr * kkr).astype(_BF), ones, preferred_element_type=jnp.float32)
    kk_ref[0] = kkr / jnp.maximum(jnp.sqrt(ss), 1e-12)
    w_pre = jnp.dot(jnp.tanh(wl).astype(_BF), wup_ref[...], preferred_element_type=jnp.float32)
    a_pre = jnp.dot(al.astype(_BF), aup_ref[...], preferred_element_type=jnp.float32)
    k_sum = 0.0
    for dr, (lw_o, k_o, a_o) in enumerate(((lw0_ref, k0_ref, a0_ref), (lw1_ref, k1_ref, a1_ref))):
        w_log = -_softplus(-(vec_ref[3 + dr:4 + dr] + w_pre[:, dr * d:(dr + 1) * d])) - 0.5
        a = _sigmoid(vec_ref[5 + dr:6 + dr] + a_pre[:, dr * d:(dr + 1) * d])
        k_d = k * (1.0 + (a - 1.0) * k_a)
        lw_o[0] = -jnp.exp(w_log)
        k_o[0] = k_d
        a_o[0] = a
        k_sum = k_sum + k_d
    bonus_ref[0] = jnp.dot((r * k_sum * r_k).astype(_BF), ones, preferred_element_type=jnp.float32)


def _head_ones(width, head):
    i = jnp.arange(width) // head
    return (i[:, None] == i[None, :]).astype(_BF)


def _rwkv_pre(p, mu, k_k, k_a, r_k, w0, a0, g_up, w_up, a_up, *, ctx_len):
    nb, l, cols = p.shape
    t_rows, w, d = RW_PRE_T, GRID_W, D_RW
    nblk = l // w
    per = t_rows // w
    zeros = jnp.zeros((RW_W_LORA, d), jnp.float32)
    wup = jnp.concatenate([jnp.concatenate([w_up[0], zeros], axis=1),
                           jnp.concatenate([zeros, w_up[1]], axis=1)], axis=0).astype(_BF)
    aup = jnp.concatenate([jnp.concatenate([a_up[0], zeros], axis=1),
                           jnp.concatenate([zeros, a_up[1]], axis=1)], axis=0).astype(_BF)
    vec = jnp.stack([k_k, k_a, r_k.reshape(d), w0[0], w0[1], a0[0], a0[1], jnp.zeros((d,), jnp.float32)])
    out = jax.ShapeDtypeStruct((nb, l, d), jnp.float32)
    out_spec = pl.BlockSpec((1, t_rows, d), lambda b, j: (b, j, 0))
    full = lambda shape: pl.BlockSpec(shape, lambda b, j: (0,) * len(shape))
    return pl.pallas_call(
        functools.partial(_rwkv_pre_kernel, ctx_len=ctx_len, seq_len=l - ctx_len),
        grid=(nb, l // t_rows),
        in_specs=[pl.BlockSpec((1, w, cols), lambda b, j: (b, jnp.maximum(j * per - 1, 0), 0)),
                  pl.BlockSpec((1, t_rows, cols), lambda b, j: (b, j, 0)),
                  pl.BlockSpec((1, w, cols), lambda b, j: (b, jnp.minimum(j * per + per, nblk - 1), 0)),
                  full((1, cols)), full((8, d)), full((RW_G_LORA, d)),
                  full((2 * RW_W_LORA, 2 * d)), full((2 * RW_A_LORA, 2 * d)), full((d, d))],
        out_specs=[out_spec] * 11,
        out_shape=[out] * 11,
        scratch_shapes=[pltpu.VMEM((t_rows + 2 * w, cols), jnp.float32), pltpu.VMEM((t_rows, cols), jnp.float32)],
        compiler_params=pltpu.CompilerParams(dimension_semantics=("parallel", "parallel"),
                                             vmem_limit_bytes=48 * 1024 * 1024),
        name="rwkv_pre",
    )(p, p, p, mu.reshape(1, cols), vec, g_up.astype(_BF), wup, aup, _head_ones(d, RW_HEAD))


def _rwkv_post_kernel(y0_ref, y1_ref, v_ref, g_ref, bonus_ref, gn_ref, ones_ref, o_ref):
    y = y0_ref[0] + y1_ref[0]
    ones = ones_ref[...]
    inv_n = 1.0 / RW_HEAD
    mu_y = jnp.dot(y.astype(_BF), ones, preferred_element_type=jnp.float32) * inv_n
    yc = y - mu_y
    var_y = jnp.dot((yc * yc).astype(_BF), ones, preferred_element_type=jnp.float32) * inv_n
    yn = yc * lax.rsqrt(var_y + RW_GN_EPS) * gn_ref[0:1] + gn_ref[1:2]
    o_ref[0] = (yn + bonus_ref[0] * v_ref[0]) * g_ref[0]


def _rwkv_post(y0, y1, v, g, bonus, gn_g, gn_b):
    nb, l, d = y0.shape
    spec = pl.BlockSpec((1, RW_PRE_T, d), lambda b, j: (b, j, 0))
    return pl.pallas_call(
        _rwkv_post_kernel,
        grid=(nb, l // RW_PRE_T),
        in_specs=[spec] * 5 + [pl.BlockSpec((2, d), lambda b, j: (0, 0)), pl.BlockSpec((d, d), lambda b, j: (0, 0))],
        out_specs=spec,
        out_shape=jax.ShapeDtypeStruct((nb, l, d), jnp.float32),
        compiler_params=pltpu.CompilerParams(dimension_semantics=("parallel", "parallel")),
        name="rwkv_post",
    )(y0, y1, v, g, bonus, jnp.stack([gn_g, gn_b]), _head_ones(d, RW_HEAD))


def _rwkv7_mixer_fused(p, lc, mu, w0, w_up, a0, a_up, g_up, k_k, k_a, r_k, gn_g, gn_b):
    r, v, kk, g, bonus, lw0, k0, a0_, lw1, k1, a1_ = _rwkv_pre(
        p, _f32(mu), _f32(k_k), _f32(k_a), _f32(r_k), _f32(w0), _f32(a0), _f32(g_up), _f32(w_up), _f32(a_up),
        ctx_len=lc)
    y0 = _rwkv_pair_scan(r, lw0, k0, v, kk, a0_, ctx_len=lc, reverse=False)
    y1 = _rwkv_pair_scan(r, lw1, k1, v, kk, a1_, ctx_len=lc, reverse=True)
    return _rwkv_post(y0, y1, v, g, bonus, _f32(gn_g), _f32(gn_b))


def _f32(t):
    return t.astype(jnp.float32)


def _layer_norm(x, g, b):
    xf = _f32(x)
    mu = jnp.mean(xf, -1, keepdims=True)
    var = jnp.mean(jnp.square(xf - mu), -1, keepdims=True)
    return ((xf - mu) * lax.rsqrt(var + LN_EPS) * _f32(g) + _f32(b)).astype(x.dtype)


def _s5_discretise(lam_re, lam_im, log_step, b_re, b_im):
    dt = jnp.exp(log_step)[:, None]
    mag = jnp.exp(lam_re * dt)
    ang = lam_im * dt
    lb_re, lb_im = mag * jnp.cos(ang), mag * jnp.sin(ang)
    den = lam_re * lam_re + lam_im * lam_im
    nr, ni = lb_re - 1.0, lb_im
    co_re = (nr * lam_re + ni * lam_im) / den
    co_im = (ni * lam_re - nr * lam_im) / den
    bb_re = co_re[..., None] * b_re - co_im[..., None] * b_im
    bb_im = co_re[..., None] * b_im + co_im[..., None] * b_re
    return lb_re, lb_im, bb_re, bb_im


def _s5_mixer(u, lc, lam_re, lam_im, log_step, b_re, b_im, c_re, c_im, d_skip, glu_w, glu_b):
    disc = [_s5_discretise(_f32(lam_re[d]), _f32(lam_im[d]), _f32(log_step[d]),
                           _f32(b_re[d]), _f32(b_im[d])) for d in range(2)]
    wb, lam, wc = _s5_weights(disc, _f32(c_re), _f32(c_im))
    u = _f32(u)
    nb, l, _ = u.shape
    assert 2 * nb == _S5_ROWS
    u_t = u.transpose(1, 0, 2)
    u8 = jnp.concatenate([u_t, u_t], axis=1).reshape(l * _S5_ROWS, D_S5)
    y_f, y_b = _s5_dir_scan(u8, wb, lam, wc, ctx_len=lc)
    y = (y_f.reshape(l, _S5_ROWS, D_S5)[:, :nb] + y_b.reshape(l, _S5_ROWS, D_S5)[:, nb:]).transpose(1, 0, 2)
    y = jax.nn.gelu(y + _f32(d_skip) * u)
    return y * jax.nn.sigmoid(_mm3(y, glu_w) + _f32(glu_b))


def _hgrn2_mixer(p, lc, lb, norm_g, col0=0):
    lb = _f32(lb)
    o_f = _hgrn_dir_scan(p, lb, ctx_len=lc, reverse=False, col0=col0)
    o_b = _hgrn_dir_scan(p, lb, ctx_len=lc, reverse=True, col0=col0)
    return _hgrn_post(o_f, o_b, p, _f32(norm_g), col0=col0)


ROW_TILE = 256


def _mod_spec(n_batch, tiles_per_batch, ctx_tiles):
    def index(i):
        return (jnp.where(i % tiles_per_batch < ctx_tiles, n_batch, i // tiles_per_batch), 0, 0)
    return pl.BlockSpec((1, 1, 6 * D_MODEL), index)


def _inproj_kernel(x_ref, mod_ref, w_ref, pa_ref, pc_ref):
    d = D_MODEL
    m = mod_ref[0]
    xm = (x_ref[...] * (1.0 + m[:, d:2 * d]) + m[:, 0:d]).astype(_BF)
    split = pa_ref.shape[1]
    pa_ref[...] = jnp.dot(xm, w_ref[:, :split], preferred_element_type=jnp.float32)
    pc_ref[...] = jnp.dot(xm, w_ref[:, split:], preferred_element_type=jnp.float32)


def _inproj(xs, mod, w, *, n_batch, ctx_len):
    t, d = xs.shape
    n_a = D_S5 + 5 * D_HG
    n_c = w.shape[1] - n_a
    tiles = t // ROW_TILE
    return pl.pallas_call(
        _inproj_kernel,
        grid=(tiles,),
        in_specs=[pl.BlockSpec((ROW_TILE, d), lambda i: (i, 0)),
                  _mod_spec(n_batch, tiles // n_batch, ctx_len // ROW_TILE),
                  pl.BlockSpec((d, n_a + n_c), lambda i: (0, 0))],
        out_specs=[pl.BlockSpec((ROW_TILE, n_a), lambda i: (i, 0)), pl.BlockSpec((ROW_TILE, n_c), lambda i: (i, 0))],
        out_shape=[jax.ShapeDtypeStruct((t, n_a), jnp.float32), jax.ShapeDtypeStruct((t, n_c), jnp.float32)],
        compiler_params=pltpu.CompilerParams(dimension_semantics=("parallel",),
                                             vmem_limit_bytes=48 * 1024 * 1024),
        name="in_proj",
    )(xs, mod, w.astype(_BF))


def _ln_rows(h, g, b):
    mu = jnp.mean(h, axis=-1, keepdims=True)
    hc = h - mu
    var = jnp.mean(hc * hc, axis=-1, keepdims=True)
    return hc * lax.rsqrt(var + LN_EPS) * g + b


def _outproj_ln_kernel(ya_ref, yb_ref, yc_ref, w_ref, xs_ref, mod_ref, ln_ref, x1_ref, h2_ref):
    d = D_MODEL
    mix = (jnp.dot(ya_ref[...].astype(_BF), w_ref[0:D_S5], preferred_element_type=jnp.float32)
           + jnp.dot(yb_ref[...].astype(_BF), w_ref[D_S5:D_S5 + D_HG], preferred_element_type=jnp.float32)
           + jnp.dot(yc_ref[...].astype(_BF), w_ref[D_S5 + D_HG:D_MIX], preferred_element_type=jnp.float32))
    m = mod_ref[0]
    x1 = _ln_rows(DEEPNORM_ALPHA * xs_ref[...] + m[:, 2 * d:3 * d] * mix, ln_ref[0:1], ln_ref[1:2])
    x1_ref[...] = x1
    h2_ref[...] = x1 * (1.0 + m[:, 4 * d:5 * d]) + m[:, 3 * d:4 * d]


def _outproj_ln(ya, yb, yc, w, xs, mod, ln_g, ln_b, *, n_batch, ctx_len):
    t, d = xs.shape
    tiles = t // ROW_TILE
    row = lambda n: pl.BlockSpec((ROW_TILE, n), lambda i: (i, 0))
    out = jax.ShapeDtypeStruct((t, d), jnp.float32)
    return pl.pallas_call(
        _outproj_ln_kernel,
        grid=(tiles,),
        in_specs=[row(D_S5), row(D_HG), row(D_RW), pl.BlockSpec((D_MIX, d), lambda i: (0, 0)), row(d),
                  _mod_spec(n_batch, tiles // n_batch, ctx_len // ROW_TILE),
                  pl.BlockSpec((2, d), lambda i: (0, 0))],
        out_specs=[row(d), row(d)],
        out_shape=[out, out],
        compiler_params=pltpu.CompilerParams(dimension_semantics=("parallel",)),
        name="out_proj_ln",
    )(ya, yb, yc, w.astype(_BF), xs, mod, jnp.stack([ln_g, ln_b]))


def kernel(x, c, ctx, c_ctx, ada_w, ada_b, w_in, w_out, ln1_g, ln1_b, ln2_g, ln2_b,
           s5_lam_re, s5_lam_im, s5_log_step, s5_b_re, s5_b_im, s5_c_re, s5_c_im,
           s5_d, s5_glu_w, s5_glu_b, hgrn_lb_logits, hgrn_norm_g,
           rwkv_mu, rwkv_w0, rwkv_w_up, rwkv_a0, rwkv_a_up, rwkv_g_up, rwkv_k_k, rwkv_k_a,
           rwkv_r_k, rwkv_gn_g, rwkv_gn_b, peer_wq, peer_keys, peer_u, peer_v):
    nb, lx, d = x.shape
    lc = ctx.shape[1]
    l_all = lc + lx
    assert lx % GRID_W == 0 and lc % ROW_TILE == 0 and lx % ROW_TILE == 0 and (nb * l_all) % PEER_TN == 0
    lb_cum = jnp.cumsum(jax.nn.softmax(_f32(hgrn_lb_logits), axis=1), axis=1)
    lb_all = lb_cum - lb_cum[:, :1]
    xs = jnp.concatenate([ctx, x], axis=1).reshape(nb * l_all, d)
    seq = lambda t: t.reshape(nb, l_all, t.shape[-1])
    for l in range(DEPTH):
        last = l == DEPTH - 1
        mod = jnp.concatenate([jax.nn.silu(c), jax.nn.silu(c_ctx)[None]], axis=0) @ ada_w[l] + ada_b[l]
        g2 = mod[:, None, 5 * d:]
        mod = mod[:, None, :]
        pa, pc = _inproj(xs, mod, w_in[l], n_batch=nb, ctx_len=lc)
        pa = seq(pa)
        ya = _s5_mixer(pa[..., :D_S5], lc, s5_lam_re[l], s5_lam_im[l], s5_log_step[l],
                       s5_b_re[l], s5_b_im[l], s5_c_re[l], s5_c_im[l], s5_d[l], s5_glu_w[l], s5_glu_b[l])
        yb = _hgrn2_mixer(pa, lc, lb_all[:, l], hgrn_norm_g[l], col0=D_S5 // D_HG)
        yc = _rwkv7_mixer_fused(seq(pc), lc, rwkv_mu[l], rwkv_w0[l], rwkv_w_up[l], rwkv_a0[l], rwkv_a_up[l],
                                rwkv_g_up[l], rwkv_k_k[l], rwkv_k_a[l], rwkv_r_k[l], rwkv_gn_g[l], rwkv_gn_b[l])
        flat = lambda t: t.reshape(nb * l_all, t.shape[-1])
        xs, h2 = _outproj_ln(flat(ya), flat(yb), flat(yc), w_out[l], xs, mod, ln1_g[l], ln1_b[l],
                             n_batch=nb, ctx_len=lc)
        if last:
            xs, h2 = seq(xs)[:, lc:], seq(h2)[:, lc:].reshape(nb * lx, d)
            gate = g2[:nb]
        else:
            xs = seq(xs)
            gate = jnp.where((jnp.arange(l_all) < lc)[None, :, None], g2[nb:], g2[:nb])
        ffn = _peer_ffn_dense(h2, peer_wq[l], peer_keys[l], peer_u[l].astype(_BF), peer_v[l].T.astype(_BF))
        xs = _layer_norm(DEEPNORM_ALPHA * xs + gate * ffn.reshape(xs.shape), ln2_g[l], ln2_b[l])
        xs = xs.reshape(-1, d)
    return xs.reshape(nb, lx, d)
```

```python
import functools
import math

import jax
import jax.numpy as jnp
from jax import lax
from jax.experimental import pallas as pl
from jax.experimental.pallas import tpu as pltpu

D_MODEL = 1024
DEPTH = 2
GRID_W = 64
D_S5 = D_MODEL // 4
D_HG = D_MODEL // 4
D_RW = D_MODEL // 2
D_MIX = D_S5 + D_HG + D_RW
S5_H = 16
S5_G = D_S5 // S5_H
S5_P = 64
HG_HEAD = 64
HG_CHUNK = 16
RW_HEAD = 64
RW_W_LORA = 64
RW_A_LORA = 64
RW_G_LORA = 128
RW_COLS = 3 * D_RW + RW_G_LORA + 2 * RW_W_LORA + 2 * RW_A_LORA
RW_GN_EPS = 64e-5
PEER_HEADS = 8
PEER_NKEYS = 128
PEER_N = PEER_NKEYS * PEER_NKEYS
PEER_QDIM = 256
PEER_TOPK = 16
LN_EPS = 1e-5
RMS_EPS = 1e-6
DEEPNORM_ALPHA = (2.0 * DEPTH) ** 0.25


def _mm_kernel(x_ref, w_ref, o_ref):
    o_ref[...] = jnp.dot(x_ref[...].astype(jnp.bfloat16), w_ref[...],
                         preferred_element_type=jnp.float32)


def _pick_tile(n, cands):
    for c in cands:
        if n % c == 0:
            return c
    return n


def _matmul(x, w):
    m, k = x.shape
    n = w.shape[1]
    tm = _pick_tile(m, (512, 256, 128, 8))
    tn = _pick_tile(n, (1152, 1024, 512, 256, 128))
    return pl.pallas_call(
        _mm_kernel,
        grid=(m // tm, n // tn),
        in_specs=[pl.BlockSpec((tm, k), lambda i, j: (i, 0)),
                  pl.BlockSpec((k, tn), lambda i, j: (0, j))],
        out_specs=pl.BlockSpec((tm, tn), lambda i, j: (i, j)),
        out_shape=jax.ShapeDtypeStruct((m, n), jnp.float32),
        compiler_params=pltpu.CompilerParams(
            dimension_semantics=("parallel", "parallel"),
            vmem_limit_bytes=48 * 1024 * 1024),
    )(x, w.astype(jnp.bfloat16))


def _mm3(x, w):
    b, l, k = x.shape
    return _matmul(x.reshape(b * l, k), w).reshape(b, l, w.shape[1])


_BF = jnp.bfloat16
_NN = (((1,), (0,)), ((), ()))
_NT = (((1,), (1,)), ((), ()))
_TN = (((0,), (0,)), ((), ()))


def _dot(a, b, dims=_NN):
    return lax.dot_general(a.astype(_BF), b.astype(_BF), dims,
                           preferred_element_type=jnp.float32)


def _split_bf16(a):
    hi = a.astype(_BF)
    return hi, (a - hi.astype(jnp.float32)).astype(_BF)


def _dot3(a, b, dims=_NN):
    a_hi, a_lo = _split_bf16(a)
    b_hi, b_lo = _split_bf16(b)
    d = functools.partial(lax.dot_general, dimension_numbers=dims,
                          preferred_element_type=jnp.float32)
    return d(a_hi, b_hi) + (d(a_hi, b_lo) + d(a_lo, b_hi))


RW_CHUNK = 64


def _scan_block_order(n_ctx, n_all, reverse):
    if not reverse:
        return lambda c: c
    return lambda c: jnp.where(c < n_ctx, n_ctx - 1 - c, n_all - 1 - c + n_ctx)


PEER_TN = 512
PEER_EBLK = 2048
_PEER_SUB = 256
_PEER_OUT = 1024
_NEG = -3.0e38


def _top_rows(s, n):
    vals = []
    for _ in range(n):
        m = jnp.max(s, axis=0, keepdims=True)
        vals.append(m)
        s = jnp.where(s == m, _NEG, s)
    return vals


def _peer_route_kernel(h_ref, wq_ref, keys_ref, xb_ref, pi_ref, c1_ref, p2_ref):
    hb = h_ref[...].astype(_BF)
    xb_ref[...] = h_ref[...].T.astype(_BF)
    q = jnp.dot(hb, wq_ref[...], preferred_element_type=jnp.float32)
    half = PEER_QDIM // 2
    for h in range(PEER_HEADS):
        s1 = _dot(keys_ref[2 * h], q[:, (2 * h) * half:(2 * h + 1) * half], _NT)
        s2 = _dot(keys_ref[2 * h + 1], q[:, (2 * h + 1) * half:(2 * h + 2) * half], _NT)
        t1 = _top_rows(s1, PEER_TOPK + 1)
        t2 = _top_rows(s2, PEER_TOPK + 1)
        cand = [t1[j1] + t2[j2] for j1 in range(PEER_TOPK + 1) for j2 in range(PEER_TOPK + 1)
                if (j1 + 1) * (j2 + 1) <= PEER_TOPK + 1]
        cand += [jnp.full_like(cand[0], _NEG)] * (-len(cand) % 8)
        top = _top_rows(jnp.concatenate(cand, axis=0), PEER_TOPK + 1)
        theta = 0.5 * (top[PEER_TOPK - 1] + top[PEER_TOPK])
        z = top[0] * 0.0
        for c in top[:PEER_TOPK]:
            z = z + jnp.exp(c - top[0])
        pi_ref[h] = jnp.exp(jnp.maximum(theta - s1 - t2[0], -80.0))
        c1_ref[h] = jnp.exp(s1 - t1[0]) / z
        p2_ref[h] = jnp.exp(s2 - t2[0])


def _peer_route(hf, wq, keys):
    t, d = hf.shape
    tn = PEER_TN
    nq = PEER_HEADS * PEER_QDIM
    aux = jax.ShapeDtypeStruct((PEER_HEADS, PEER_NKEYS, t), jnp.float32)
    aux_spec = pl.BlockSpec((PEER_HEADS, PEER_NKEYS, tn), lambda i: (0, 0, i))
    return pl.pallas_call(
        _peer_route_kernel,
        grid=(t // tn,),
        in_specs=[pl.BlockSpec((tn, d), lambda i: (i, 0)),
                  pl.BlockSpec((d, nq), lambda i: (0, 0)),
                  pl.BlockSpec((PEER_HEADS * 2, PEER_NKEYS, PEER_QDIM // 2), lambda i: (0, 0, 0))],
        out_specs=[pl.BlockSpec((d, tn), lambda i: (0, i))] + [aux_spec] * 3,
        out_shape=[jax.ShapeDtypeStruct((d, t), _BF)] + [aux] * 3,
        compiler_params=pltpu.CompilerParams(dimension_semantics=("parallel",),
                                             vmem_limit_bytes=56 * 1024 * 1024),
        name="peer_route",
    )(hf, wq.astype(_BF), keys.reshape(PEER_HEADS * 2, PEER_NKEYS, PEER_QDIM // 2))


def _gelu_tanh(z):
    c = math.sqrt(2.0 / math.pi)
    hz = 0.5 * z
    return hz + hz * jnp.tanh(z * (c + (c * 0.044715) * (z * z)))


def _peer_expert_kernel(x_ref, u_ref, vt_ref, pi_ref, c1_ref, p2_ref, o_ref, acc_ref, at_ref):
    j = pl.program_id(1)

    @pl.when(j == 0)
    def _():
        acc_ref[...] = jnp.zeros_like(acc_ref)

    nk = PEER_NKEYS
    n_slab = PEER_EBLK // nk
    rows = pl.ds(pl.multiple_of(j * n_slab, n_slab), n_slab)
    sub = _PEER_SUB
    per = sub // nk

    def scores(sb):
        return jnp.dot(u_ref[sb * sub:(sb + 1) * sub, :], x_ref[...],
                       preferred_element_type=jnp.float32)

    zt = scores(0)
    for sb in range(PEER_EBLK // sub):
        zt_next = scores(sb + 1) if (sb + 1) * sub < PEER_EBLK else None
        for k in range(per):
            s = sb * per + k
            for lt in range(PEER_TN // 128):
                lanes = pl.ds(lt * 128, 128)
                g = jnp.zeros((nk, 128), jnp.float32)
                for h in range(PEER_HEADS):
                    pi = pi_ref[h, rows, lanes][s:s + 1]
                    c1 = c1_ref[h, rows, lanes][s:s + 1]
                    p2 = p2_ref[h, :, lanes]
                    g = g + jnp.where(p2 >= pi, p2 * c1, 0.0)
                z = zt[k * nk:(k + 1) * nk, lt * 128:(lt + 1) * 128]
                at_ref[s * nk:(s + 1) * nk, lt * 128:(lt + 1) * 128] = (g * _gelu_tanh(z)).astype(_BF)
        zt = zt_next
        done = (sb + 1) * sub
        if done % _PEER_OUT == 0:
            part = slice(done - _PEER_OUT, done)
            acc_ref[...] += jnp.dot(vt_ref[:, part], at_ref[part, :], preferred_element_type=jnp.float32)

    @pl.when(j == pl.num_programs(1) - 1)
    def _():
        o_ref[...] = acc_ref[...].T


def _peer_expert(xb, u_b, vt_b, aux):
    d, t = xb.shape
    tn, eb = PEER_TN, PEER_EBLK
    aux_spec = pl.BlockSpec((PEER_HEADS, PEER_NKEYS, tn), lambda i, j: (0, 0, i))
    return pl.pallas_call(
        _peer_expert_kernel,
        grid=(t // tn, PEER_N // eb),
        in_specs=[pl.BlockSpec((d, tn), lambda i, j: (0, i)),
                  pl.BlockSpec((eb, d), lambda i, j: (j, 0)),
                  pl.BlockSpec((d, eb), lambda i, j: (0, j))] + [aux_spec] * 3,
        out_specs=pl.BlockSpec((tn, d), lambda i, j: (i, 0)),
        out_shape=jax.ShapeDtypeStruct((t, d), jnp.float32),
        scratch_shapes=[pltpu.VMEM((d, tn), jnp.float32), pltpu.VMEM((eb, tn), _BF)],
        compiler_params=pltpu.CompilerParams(dimension_semantics=("parallel", "arbitrary"),
                                             vmem_limit_bytes=56 * 1024 * 1024),
        name="peer_expert",
    )(xb, u_b, vt_b, *aux)


def _peer_ffn_dense(hf, wq, keys, u_b, vt_b):
    xb, *aux = _peer_route(hf, wq, keys)
    return _peer_expert(xb, u_b, vt_b, aux)


S5_STEPS = 128
_S5_ROWS = 8
_S5_STATE = S5_G * S5_P


def _s5_weights(disc, c_re, c_im):
    eye = jnp.eye(S5_G, dtype=jnp.float32)

    def bdiag_in(bb):
        return jnp.einsum('gph,gk->ghkp', bb, eye).reshape(D_S5, _S5_STATE)

    def bdiag_out(cc):
        return jnp.einsum('ghp,gk->gpkh', cc, eye).reshape(_S5_STATE, D_S5)

    wb = jnp.stack([jnp.concatenate([bdiag_in(disc[d][2]), bdiag_in(disc[d][3])], axis=1) for d in range(2)])
    wc = jnp.stack([jnp.concatenate([bdiag_out(c_re[d]), -bdiag_out(c_im[d])], axis=0) for d in range(2)])
    half = _S5_ROWS // 2
    lam = jnp.stack([
        jnp.concatenate([jnp.broadcast_to(disc[d][i].reshape(1, _S5_STATE), (half, _S5_STATE))
                         for d in range(2)], axis=0) for i in range(2)])
    return wb.astype(_BF), lam, wc.astype(_BF)


def _s5_dir_kernel(uf_ref, ub_ref, wb_ref, lam_ref, wc_ref, yf_ref, yb_ref, buf_ref, bub_ref, xf_ref, xb_ref, st_ref):
    @pl.when(pl.program_id(0) == 0)
    def _():
        st_ref[...] = jnp.zeros_like(st_ref)

    n = _S5_STATE
    buf_ref[...] = jnp.dot(uf_ref[...].astype(_BF), wb_ref[0], preferred_element_type=jnp.float32)
    bub_ref[...] = jnp.dot(ub_ref[...].astype(_BF), wb_ref[1], preferred_element_type=jnp.float32)
    lr = lam_ref[0]
    li = lam_ref[1]
    fwd = lax.broadcasted_iota(jnp.int32, (_S5_ROWS, n), 0) < (_S5_ROWS // 2)

    def step(i, carry):
        xr, xi = carry
        rf = pl.ds(pl.multiple_of(i * _S5_ROWS, _S5_ROWS), _S5_ROWS)
        rb = pl.ds(pl.multiple_of((S5_STEPS - 1 - i) * _S5_ROWS, _S5_ROWS), _S5_ROWS)
        nr = lr * xr - li * xi + jnp.where(fwd, buf_ref[rf, :n], bub_ref[rb, :n])
        ni = lr * xi + li * xr + jnp.where(fwd, buf_ref[rf, n:], bub_ref[rb, n:])
        xf_ref[rf, :n] = nr
        xf_ref[rf, n:] = ni
        xb_ref[rb, :n] = nr
        xb_ref[rb, n:] = ni
        return nr, ni

    xr, xi = lax.fori_loop(0, S5_STEPS, step, (st_ref[0], st_ref[1]), unroll=4)
    st_ref[0] = xr
    st_ref[1] = xi
    yf_ref[...] = jnp.dot(xf_ref[...].astype(_BF), wc_ref[0], preferred_element_type=jnp.float32)
    yb_ref[...] = jnp.dot(xb_ref[...].astype(_BF), wc_ref[1], preferred_element_type=jnp.float32)


def _s5_dir_scan(u8, wb, lam, wc, *, ctx_len):
    rows = u8.shape[0]
    blk = S5_STEPS * _S5_ROWS
    n = _S5_STATE
    order = _scan_block_order(ctx_len // S5_STEPS, rows // blk, True)
    spec_f = pl.BlockSpec((blk, D_S5), lambda i: (i, 0))
    spec_b = pl.BlockSpec((blk, D_S5), lambda i: (order(i), 0))
    out = jax.ShapeDtypeStruct((rows, D_S5), jnp.float32)
    big = pltpu.VMEM((blk, 2 * n), jnp.float32)
    return pl.pallas_call(
        _s5_dir_kernel,
        grid=(rows // blk,),
        in_specs=[spec_f, spec_b,
                  pl.BlockSpec((2, D_S5, 2 * n), lambda i: (0, 0, 0)),
                  pl.BlockSpec((2, _S5_ROWS, n), lambda i: (0, 0, 0)),
                  pl.BlockSpec((2, 2 * n, D_S5), lambda i: (0, 0, 0))],
        out_specs=[spec_f, spec_b],
        out_shape=[out, out],
        scratch_shapes=[big, big, big, big, pltpu.VMEM((2, _S5_ROWS, n), jnp.float32)],
        compiler_params=pltpu.CompilerParams(dimension_semantics=("arbitrary",),
                                             vmem_limit_bytes=56 * 1024 * 1024),
        name="s5_scan",
    )(u8, u8, wb, lam, wc)


HG_BLOCK = 128
HG_BATCH_BLOCK = 4
_HG_PAIR = 2 * HG_HEAD


def _hgrn_dir_kernel(q_ref, v_ref, f_ref, lb_ref, o_ref, st_ref, *, reverse):
    @pl.when(pl.program_id(1) == 0)
    def _():
        st_ref[...] = jnp.zeros_like(st_ref)

    ch = HG_CHUNK
    n_ch = HG_BLOCK // ch
    r = lax.broadcasted_iota(jnp.int32, (HG_BLOCK, HG_BLOCK), 0)
    c = lax.broadcasted_iota(jnp.int32, (HG_BLOCK, HG_BLOCK), 1)
    tri = ((r // ch == c // ch) & ((r <= c) if reverse else (r >= c))).astype(jnp.float32)
    same_head = (lax.broadcasted_iota(jnp.int32, (_HG_PAIR, _HG_PAIR), 0) // HG_HEAD
                 == lax.broadcasted_iota(jnp.int32, (_HG_PAIR, _HG_PAIR), 1) // HG_HEAD)
    ones_blk = same_head.astype(_BF)
    tcol = lax.broadcasted_iota(jnp.int32, (ch, 1), 0)
    end = 0 if reverse else ch - 1
    n_pair = D_HG // _HG_PAIR
    chains = [(bi, p) for bi in range(q_ref.shape[0]) for p in range(n_pair)]
    pair_lanes = lambda p: slice(p * _HG_PAIR, (p + 1) * _HG_PAIR)
    intra, kv, dec, qd = {}, {}, {}, {}
    for bi in range(q_ref.shape[0]):
        q_raw = q_ref[bi]
        q_all = q_raw * _sigmoid(q_raw)
        forget = lb_ref[...] + (1.0 - lb_ref[...]) * _sigmoid(f_ref[bi])
        k_all = 1.0 - forget
        cum_all = _dot3(tri, jnp.log(forget))
        for p in range(n_pair):
            lanes = pair_lanes(p)
            for ci in range(n_ch):
                rows = slice(ci * ch, (ci + 1) * ch)
                q = q_all[rows, lanes]
                k = k_all[rows, lanes]
                v = v_ref[bi, rows, lanes]
                cum = cum_all[rows, lanes]
                last = cum[end:end + 1]
                w = [q * jnp.exp(jnp.minimum(cum - cum[s:s + 1], 0.0)) * k[s:s + 1] for s in range(ch)]
                rel = jnp.dot(jnp.concatenate(w, axis=0).astype(_BF), ones_blk,
                              preferred_element_type=jnp.float32)
                o = jnp.zeros((ch, _HG_PAIR), jnp.float32)
                for s in range(ch):
                    seen = (tcol <= s) if reverse else (tcol >= s)
                    o = o + jnp.where(seen, rel[s * ch:(s + 1) * ch], 0.0) * v[s:s + 1]
                intra[bi, p, ci] = o
                kv[bi, p, ci] = jnp.where(same_head, _dot(v, k * jnp.exp(last - cum), _TN), 0.0)
                dec[bi, p, ci] = jnp.exp(last)
                qd[bi, p, ci] = q * jnp.exp(cum)
    state = [st_ref[i] for i in range(len(chains))]
    for ci in (range(n_ch - 1, -1, -1) if reverse else range(n_ch)):
        for i, (bi, p) in enumerate(chains):
            o_ref[bi, ci * ch:(ci + 1) * ch, pair_lanes(p)] = intra[bi, p, ci] + _dot(qd[bi, p, ci], state[i], _NT)
            state[i] = state[i] * dec[bi, p, ci] + kv[bi, p, ci]
    for i in range(len(chains)):
        st_ref[i] = state[i]


def _hgrn_dir_scan(p, lb, *, ctx_len, reverse, col0=0):
    g, l, _ = p.shape
    d = D_HG
    order = _scan_block_order(ctx_len // HG_BLOCK, l // HG_BLOCK, reverse)
    bb = HG_BATCH_BLOCK if g % HG_BATCH_BLOCK == 0 else 1
    col = lambda c: pl.BlockSpec((bb, HG_BLOCK, d), lambda i, j: (i, order(j), col0 + c))
    dr = 1 if reverse else 0
    return pl.pallas_call(
        functools.partial(_hgrn_dir_kernel, reverse=reverse),
        grid=(g // bb, l // HG_BLOCK),
        in_specs=[col(0), col(1), col(2 + dr), pl.BlockSpec((1, d), lambda i, j: (0, 0))],
        out_specs=pl.BlockSpec((bb, HG_BLOCK, d), lambda i, j: (i, order(j), 0)),
        out_shape=jax.ShapeDtypeStruct((g, l, d), jnp.float32),
        scratch_shapes=[pltpu.VMEM((bb * d // _HG_PAIR, _HG_PAIR, _HG_PAIR), jnp.float32)],
        compiler_params=pltpu.CompilerParams(dimension_semantics=("parallel", "arbitrary")),
        name="hgrn_scan",
    )(p, p, p, lb[dr:dr + 1])


def _hgrn_post_kernel(of_ref, ob_ref, g_ref, ng_ref, ones_ref, y_ref):
    o = of_ref[0] + ob_ref[0]
    ms = jnp.dot((o * o).astype(_BF), ones_ref[...], preferred_element_type=jnp.float32) * (1.0 / HG_HEAD)
    g = g_ref[0]
    y_ref[0] = o * lax.rsqrt(ms + RMS_EPS) * ng_ref[...] * (g * _sigmoid(g))


def _hgrn_post(o_f, o_b, p, norm_g, col0=0):
    nb, l, d = o_f.shape
    spec = pl.BlockSpec((1, HG_BLOCK, d), lambda i, j: (i, j, 0))
    return pl.pallas_call(
        _hgrn_post_kernel,
        grid=(nb, l // HG_BLOCK),
        in_specs=[spec, spec, pl.BlockSpec((1, HG_BLOCK, d), lambda i, j: (i, j, col0 + 4)),
                  pl.BlockSpec((1, d), lambda i, j: (0, 0)), pl.BlockSpec((d, d), lambda i, j: (0, 0))],
        out_specs=spec,
        out_shape=jax.ShapeDtypeStruct((nb, l, d), jnp.float32),
        compiler_params=pltpu.CompilerParams(dimension_semantics=("parallel", "parallel")),
        name="hgrn_post",
    )(o_f, o_b, p, norm_g.reshape(1, d), _head_ones(d, HG_HEAD))


_RW_PAIR = 2 * RW_HEAD
RW_PRE_T = 128
RW_BATCH_BLOCK = 4


def _rwkv_pair_kernel(r_ref, lw_ref, k_ref, v_ref, kk_ref, a_ref, y_ref, st_ref, *, chunk, reverse):
    @pl.when(pl.program_id(1) == 0)
    def _():
        st_ref[...] = jnp.zeros_like(st_ref)

    row = lax.broadcasted_iota(jnp.int32, (chunk, chunk), 0)
    col = lax.broadcasted_iota(jnp.int32, (chunk, chunk), 1)
    strict = row < col if reverse else row > col
    incl = row <= col if reverse else row >= col
    end = 0 if reverse else chunk - 1
    tri = incl.astype(jnp.float32)
    eye = (row == col).astype(jnp.float32)
    blk8 = row // 8 == col // 8
    merge_masks = []
    size = 8
    while size < chunk:
        merge_masks.append((row // (2 * size) == col // (2 * size)) & (row // size != col // size))
        size *= 2
    lane_head = lax.broadcasted_iota(jnp.int32, (1, _RW_PAIR), 1) // RW_HEAD
    head0 = lane_head == 0
    same_head = (lax.broadcasted_iota(jnp.int32, (_RW_PAIR, _RW_PAIR), 0) // RW_HEAD
                 == lax.broadcasted_iota(jnp.int32, (_RW_PAIR, _RW_PAIR), 1) // RW_HEAD)
    n_pair = D_RW // _RW_PAIR
    ps = range(r_ref.shape[0] * n_pair)
    ph = [(p, h) for p in ps for h in range(2)]
    bi = [p // n_pair for p in ps]
    sl = [slice((p % n_pair) * _RW_PAIR, (p % n_pair + 1) * _RW_PAIR) for p in ps]
    lw = [lw_ref[bi[p], :, sl[p]] for p in ps]
    cum = [_dot3(tri, lw[p]) for p in ps]
    pw = [jnp.exp(cum[p]) for p in ps]
    p_inv = [jnp.exp(-cum[p]) for p in ps]
    p_end = [pw[p][end:end + 1, :] for p in ps]
    kk = [kk_ref[bi[p], :, sl[p]] for p in ps]
    bh = [kk[p] * a_ref[bi[p], :, sl[p]] * p_inv[p] for p in ps]
    kh = [k_ref[bi[p], :, sl[p]] * p_inv[p] for p in ps]
    v = [v_ref[bi[p], :, sl[p]] for p in ps]
    s0 = [st_ref[p] for p in ps]
    x1 = [jnp.concatenate([-kk[p] * jnp.exp(cum[p] - lw[p]), r_ref[bi[p], :, sl[p]] * pw[p]], axis=0) for p in ps]
    x1h = {(p, h): jnp.where(lane_head == h, x1[p], 0.0) for p, h in ph}
    g1 = {q: _dot(x1h[q], bh[q[0]], _NT) for q in ph}
    g2 = {q: _dot(x1h[q], kh[q[0]], _NT) for q in ph}
    hm = [_dot(x1[p], s0[p], _NT) for p in ps]
    ab = {q: jnp.where(strict, g1[q][:chunk], 0.0) for q in ph}
    d1 = {q: jnp.where(blk8, ab[q], 0.0) for q in ph}
    d2 = {q: _dot3(d1[q], d1[q]) for q in ph}
    inv = {q: eye + d1[q] for q in ph}
    inv = {q: inv[q] + _dot3(inv[q], d2[q]) for q in ph}
    d4 = {q: _dot3(d2[q], d2[q]) for q in ph}
    inv = {q: inv[q] + _dot3(inv[q], d4[q]) for q in ph}
    for m in merge_masks:
        li = {q: _dot(jnp.where(m, ab[q], 0.0), inv[q]) for q in ph}
        inv = {q: inv[q] + _dot(inv[q], li[q]) for q in ph}
    akv = {q: _dot(jnp.where(strict, g2[q][:chunk], 0.0), v[q[0]]) for q in ph}
    rhs = [hm[p][:chunk] + jnp.where(head0, akv[p, 0], akv[p, 1]) for p in ps]
    eh = {q: _dot(inv[q], rhs[q[0]]) for q in ph}
    e = [jnp.where(head0, eh[p, 0], eh[p, 1]) for p in ps]
    yh = {q: _dot(jnp.where(incl, g1[q][chunk:], 0.0), e[q[0]])
          + _dot(jnp.where(incl, g2[q][chunk:], 0.0), v[q[0]]) for q in ph}
    for p in ps:
        y_ref[bi[p], :, sl[p]] = hm[p][chunk:] + jnp.where(head0, yh[p, 0], yh[p, 1])
    for p in ps:
        ev = jnp.concatenate([e[p], v[p]], axis=0)
        x2 = jnp.concatenate([bh[p] * p_end[p], kh[p] * p_end[p]], axis=0)
        st_ref[p] = s0[p] * p_end[p] + jnp.where(same_head, _dot(ev, x2, _TN), 0.0)


def _rwkv_pair_scan(r, lw, k, v, kk, a, *, ctx_len, reverse):
    nb, l, d = r.shape
    chunk = RW_CHUNK
    order = _scan_block_order(ctx_len // chunk, l // chunk, reverse)
    bb = RW_BATCH_BLOCK if nb % RW_BATCH_BLOCK == 0 else 1
    spec = pl.BlockSpec((bb, chunk, d), lambda b, c: (b, order(c), 0))
    return pl.pallas_call(
        functools.partial(_rwkv_pair_kernel, chunk=chunk, reverse=reverse),
        grid=(nb // bb, l // chunk),
        in_specs=[spec] * 6,
        out_specs=spec,
        out_shape=jax.ShapeDtypeStruct((nb, l, d), jnp.float32),
        scratch_shapes=[pltpu.VMEM((bb * d // _RW_PAIR, _RW_PAIR, _RW_PAIR), jnp.float32)],
        compiler_params=pltpu.CompilerParams(dimension_semantics=("parallel", "arbitrary")),
        name="rwkv_scan",
    )(r, lw, k, v, kk, a)


def _softplus(z):
    return jnp.maximum(z, 0.0) + jnp.log(1.0 + jnp.exp(-jnp.abs(z)))


def _sigmoid(z):
    return 1.0 / (1.0 + jnp.exp(-z))


def _rwkv_pre_kernel(prev_ref, cur_ref, next_ref, mu_ref, vec_ref, gup_ref, wup_ref, aup_ref, ones_ref,
                     r_ref, v_ref, kk_ref, g_ref, bonus_ref, lw0_ref, k0_ref, a0_ref, lw1_ref, k1_ref, a1_ref,
                     ext_ref, s_ref, *, ctx_len, seq_len):
    t_rows, w = RW_PRE_T, GRID_W
    ext_ref[0:w] = prev_ref[0]
    ext_ref[w:w + t_rows] = cur_ref[0]
    ext_ref[w + t_rows:w + t_rows + w] = next_ref[0]
    t = pl.program_id(1) * t_rows + lax.broadcasted_iota(jnp.int32, (t_rows, 128), 0)
    lane4 = lax.broadcasted_iota(jnp.int32, (t_rows, 128), 1) % 4
    is_ctx = t < ctx_len
    tx = t - ctx_len
    col = tx % w
    is_lat = jnp.logical_not(is_ctx)
    use_p1 = (is_ctx & (lane4 % 2 == 0) & (t > 0)) | (is_lat & (lane4 == 0) & (col != 0))
    use_n1 = (is_ctx & (lane4 % 2 == 1) & (t < ctx_len - 1)) | (is_lat & (lane4 == 1) & (col != w - 1))
    use_p64 = is_lat & (lane4 == 2) & (tx >= w)
    use_n64 = is_lat & (lane4 == 3) & (tx < seq_len - w)
    for lt in range(RW_COLS // 128):
        lanes = slice(lt * 128, (lt + 1) * 128)
        p = ext_ref[w:w + t_rows, lanes]
        shifted = jnp.where(use_p1, ext_ref[w - 1:w - 1 + t_rows, lanes],
                            jnp.where(use_n1, ext_ref[w + 1:w + 1 + t_rows, lanes],
                                      jnp.where(use_p64, ext_ref[0:t_rows, lanes],
                                                jnp.where(use_n64, ext_ref[2 * w:2 * w + t_rows, lanes], 0.0))))
        s_ref[:, lanes] = p + mu_ref[:, lanes] * (shifted - p)
    d = D_RW
    r = s_ref[:, 0:d]
    k = s_ref[:, d:2 * d]
    v = s_ref[:, 2 * d:3 * d]
    o = 3 * d
    g_lo = s_ref[:, o:o + RW_G_LORA]
    wl = s_ref[:, o + RW_G_LORA:o + RW_G_LORA + 2 * RW_W_LORA]
    al = s_ref[:, o + RW_G_LORA + 2 * RW_W_LORA:o + RW_G_LORA + 2 * RW_W_LORA + 2 * RW_A_LORA]
    k_k, k_a, r_k = vec_ref[0:1], vec_ref[1:2], vec_ref[2:3]
    ones = ones_ref[...]
    r_ref[0] = r
    v_ref[0] = v
    g_ref[0] = jnp.dot(_sigmoid(g_lo).astype(_BF), gup_ref[...], preferred_element_type=jnp.float32)
    kkr = k * k_k
    ss = jnp.dot((kkr * kkr).astype(_BF), ones, preferred_element_type=jnp.float32)
    kk_ref[0] = kkr / jnp.maximum(jnp.sqrt(ss), 1e-12)
    w_pre = jnp.dot(jnp.tanh(wl).astype(_BF), wup_ref[...], preferred_element_type=jnp.float32)
    a_pre = jnp.dot(al.astype(_BF), aup_ref[...], preferred_element_type=jnp.float32)
    k_sum = 0.0
    for dr, (lw_o, k_o, a_o) in enumerate(((lw0_ref, k0_ref, a0_ref), (lw1_ref, k1_ref, a1_ref))):
        w_log = -_softplus(-(vec_ref[3 + dr:4 + dr] + w_pre[:, dr * d:(dr + 1) * d])) - 0.5
        a = _sigmoid(vec_ref[5 + dr:6 + dr] + a_pre[:, dr * d:(dr + 1) * d])
        k_d = k * (1.0 + (a - 1.0) * k_a)
        lw_o[0] = -jnp.exp(w_log)
        k_o[0] = k_d
        a_o[0] = a
        k_sum = k_sum + k_d
    bonus_ref[0] = jnp.dot((r * k_sum * r_k).astype(_BF), ones, preferred_element_type=jnp.float32)


def _head_ones(width, head):
    i = jnp.arange(width) // head
    return (i[:, None] == i[None, :]).astype(_BF)


def _rwkv_pre(p, mu, k_k, k_a, r_k, w0, a0, g_up, w_up, a_up, *, ctx_len):
    nb, l, cols = p.shape
    t_rows, w, d = RW_PRE_T, GRID_W, D_RW
    nblk = l // w
    per = t_rows // w
    zeros = jnp.zeros((RW_W_LORA, d), jnp.float32)
    wup = jnp.concatenate([jnp.concatenate([w_up[0], zeros], axis=1),
                           jnp.concatenate([zeros, w_up[1]], axis=1)], axis=0).astype(_BF)
    aup = jnp.concatenate([jnp.concatenate([a_up[0], zeros], axis=1),
                           jnp.concatenate([zeros, a_up[1]], axis=1)], axis=0).astype(_BF)
    vec = jnp.stack([k_k, k_a, r_k.reshape(d), w0[0], w0[1], a0[0], a0[1], jnp.zeros((d,), jnp.float32)])
    out = jax.ShapeDtypeStruct((nb, l, d), jnp.float32)
    out_spec = pl.BlockSpec((1, t_rows, d), lambda b, j: (b, j, 0))
    full = lambda shape: pl.BlockSpec(shape, lambda b, j: (0,) * len(shape))
    return pl.pallas_call(
        functools.partial(_rwkv_pre_kernel, ctx_len=ctx_len, seq_len=l - ctx_len),
        grid=(nb, l // t_rows),
        in_specs=[pl.BlockSpec((1, w, cols), lambda b, j: (b, jnp.maximum(j * per - 1, 0), 0)),
                  pl.BlockSpec((1, t_rows, cols), lambda b, j: (b, j, 0)),
                  pl.BlockSpec((1, w, cols), lambda b, j: (b, jnp.minimum(j * per + per, nblk - 1), 0)),
                  full((1, cols)), full((8, d)), full((RW_G_LORA, d)),
                  full((2 * RW_W_LORA, 2 * d)), full((2 * RW_A_LORA, 2 * d)), full((d, d))],
        out_specs=[out_spec] * 11,
        out_shape=[out] * 11,
        scratch_shapes=[pltpu.VMEM((t_rows + 2 * w, cols), jnp.float32), pltpu.VMEM((t_rows, cols), jnp.float32)],
        compiler_params=pltpu.CompilerParams(dimension_semantics=("parallel", "parallel"),
                                             vmem_limit_bytes=48 * 1024 * 1024),
        name="rwkv_pre",
    )(p, p, p, mu.reshape(1, cols), vec, g_up.astype(_BF), wup, aup, _head_ones(d, RW_HEAD))


def _rwkv_post_kernel(y0_ref, y1_ref, v_ref, g_ref, bonus_ref, gn_ref, ones_ref, o_ref):
    y = y0_ref[0] + y1_ref[0]
    ones = ones_ref[...]
    inv_n = 1.0 / RW_HEAD
    mu_y = jnp.dot(y.astype(_BF), ones, preferred_element_type=jnp.float32) * inv_n
    yc = y - mu_y
    var_y = jnp.dot((yc * yc).astype(_BF), ones, preferred_element_type=jnp.float32) * inv_n
    yn = yc * lax.rsqrt(var_y + RW_GN_EPS) * gn_ref[0:1] + gn_ref[1:2]
    o_ref[0] = (yn + bonus_ref[0] * v_ref[0]) * g_ref[0]


def _rwkv_post(y0, y1, v, g, bonus, gn_g, gn_b):
    nb, l, d = y0.shape
    spec = pl.BlockSpec((1, RW_PRE_T, d), lambda b, j: (b, j, 0))
    return pl.pallas_call(
        _rwkv_post_kernel,
        grid=(nb, l // RW_PRE_T),
        in_specs=[spec] * 5 + [pl.BlockSpec((2, d), lambda b, j: (0, 0)), pl.BlockSpec((d, d), lambda b, j: (0, 0))],
        out_specs=spec,
        out_shape=jax.ShapeDtypeStruct((nb, l, d), jnp.float32),
        compiler_params=pltpu.CompilerParams(dimension_semantics=("parallel", "parallel")),
        name="rwkv_post",
    )(y0, y1, v, g, bonus, jnp.stack([gn_g, gn_b]), _head_ones(d, RW_HEAD))


def _rwkv7_mixer_fused(p, lc, mu, w0, w_up, a0, a_up, g_up, k_k, k_a, r_k, gn_g, gn_b):
    r, v, kk, g, bonus, lw0, k0, a0_, lw1, k1, a1_ = _rwkv_pre(
        p, _f32(mu), _f32(k_k), _f32(k_a), _f32(r_k), _f32(w0), _f32(a0), _f32(g_up), _f32(w_up), _f32(a_up),
        ctx_len=lc)
    y0 = _rwkv_pair_scan(r, lw0, k0, v, kk, a0_, ctx_len=lc, reverse=False)
    y1 = _rwkv_pair_scan(r, lw1, k1, v, kk, a1_, ctx_len=lc, reverse=True)
    return _rwkv_post(y0, y1, v, g, bonus, _f32(gn_g), _f32(gn_b))


def _f32(t):
    return t.astype(jnp.float32)


def _layer_norm(x, g, b):
    xf = _f32(x)
    mu = jnp.mean(xf, -1, keepdims=True)
    var = jnp.mean(jnp.square(xf - mu), -1, keepdims=True)
    return ((xf - mu) * lax.rsqrt(var + LN_EPS) * _f32(g) + _f32(b)).astype(x.dtype)


def _s5_discretise(lam_re, lam_im, log_step, b_re, b_im):
    dt = jnp.exp(log_step)[:, None]
    mag = jnp.exp(lam_re * dt)
    ang = lam_im * dt
    lb_re, lb_im = mag * jnp.cos(ang), mag * jnp.sin(ang)
    den = lam_re * lam_re + lam_im * lam_im
    nr, ni = lb_re - 1.0, lb_im
    co_re = (nr * lam_re + ni * lam_im) / den
    co_im = (ni * lam_re - nr * lam_im) / den
    bb_re = co_re[..., None] * b_re - co_im[..., None] * b_im
    bb_im = co_re[..., None] * b_im + co_im[..., None] * b_re
    return lb_re, lb_im, bb_re, bb_im


def _s5_mixer(u, lc, lam_re, lam_im, log_step, b_re, b_im, c_re, c_im, d_skip, glu_w, glu_b):
    disc = [_s5_discretise(_f32(lam_re[d]), _f32(lam_im[d]), _f32(log_step[d]),
                           _f32(b_re[d]), _f32(b_im[d])) for d in range(2)]
    wb, lam, wc = _s5_weights(disc, _f32(c_re), _f32(c_im))
    u = _f32(u)
    nb, l, _ = u.shape
    assert 2 * nb == _S5_ROWS
    u_t = u.transpose(1, 0, 2)
    u8 = jnp.concatenate([u_t, u_t], axis=1).reshape(l * _S5_ROWS, D_S5)
    y_f, y_b = _s5_dir_scan(u8, wb, lam, wc, ctx_len=lc)
    y = (y_f.reshape(l, _S5_ROWS, D_S5)[:, :nb] + y_b.reshape(l, _S5_ROWS, D_S5)[:, nb:]).transpose(1, 0, 2)
    y = jax.nn.gelu(y + _f32(d_skip) * u)
    return y * jax.nn.sigmoid(_mm3(y, glu_w) + _f32(glu_b))


def _hgrn2_mixer(p, lc, lb, norm_g, col0=0):
    lb = _f32(lb)
    o_f = _hgrn_dir_scan(p, lb, ctx_len=lc, reverse=False, col0=col0)
    o_b = _hgrn_dir_scan(p, lb, ctx_len=lc, reverse=True, col0=col0)
    return _hgrn_post(o_f, o_b, p, _f32(norm_g), col0=col0)


ROW_TILE = 256


def _mod_spec(n_batch, tiles_per_batch, ctx_tiles):
    def index(i):
        return (jnp.where(i % tiles_per_batch < ctx_tiles, n_batch, i // tiles_per_batch), 0, 0)
    return pl.BlockSpec((1, 1, 6 * D_MODEL), index)


def _inproj_kernel(x_ref, mod_ref, w_ref, pa_ref, pc_ref):
    d = D_MODEL
    m = mod_ref[0]
    xm = (x_ref[...] * (1.0 + m[:, d:2 * d]) + m[:, 0:d]).astype(_BF)
    split = pa_ref.shape[1]
    pa_ref[...] = jnp.dot(xm, w_ref[:, :split], preferred_element_type=jnp.float32)
    pc_ref[...] = jnp.dot(xm, w_ref[:, split:], preferred_element_type=jnp.float32)


def _inproj(xs, mod, w, *, n_batch, ctx_len):
    t, d = xs.shape
    n_a = D_S5 + 5 * D_HG
    n_c = w.shape[1] - n_a
    tiles = t // ROW_TILE
    return pl.pallas_call(
        _inproj_kernel,
        grid=(tiles,),
        in_specs=[pl.BlockSpec((ROW_TILE, d), lambda i: (i, 0)),
                  _mod_spec(n_batch, tiles // n_batch, ctx_len // ROW_TILE),
                  pl.BlockSpec((d, n_a + n_c), lambda i: (0, 0))],
        out_specs=[pl.BlockSpec((ROW_TILE, n_a), lambda i: (i, 0)), pl.BlockSpec((ROW_TILE, n_c), lambda i: (i, 0))],
        out_shape=[jax.ShapeDtypeStruct((t, n_a), jnp.float32), jax.ShapeDtypeStruct((t, n_c), jnp.float32)],
        compiler_params=pltpu.CompilerParams(dimension_semantics=("parallel",),
                                             vmem_limit_bytes=48 * 1024 * 1024),
        name="in_proj",
    )(xs, mod, w.astype(_BF))


def _ln_rows(h, g, b):
    mu = jnp.mean(h, axis=-1, keepdims=True)
    hc = h - mu
    var = jnp.mean(hc * hc, axis=-1, keepdims=True)
    return hc * lax.rsqrt(var + LN_EPS) * g + b


def _outproj_ln_kernel(ya_ref, yb_ref, yc_ref, w_ref, xs_ref, mod_ref, ln_ref, x1_ref, h2_ref):
    d = D_MODEL
    mix = (jnp.dot(ya_ref[...].astype(_BF), w_ref[0:D_S5], preferred_element_type=jnp.float32)
           + jnp.dot(yb_ref[...].astype(_BF), w_ref[D_S5:D_S5 + D_HG], preferred_element_type=jnp.float32)
           + jnp.dot(yc_ref[...].astype(_BF), w_ref[D_S5 + D_HG:D_MIX], preferred_element_type=jnp.float32))
    m = mod_ref[0]
    x1 = _ln_rows(DEEPNORM_ALPHA * xs_ref[...] + m[:, 2 * d:3 * d] * mix, ln_ref[0:1], ln_ref[1:2])
    x1_ref[...] = x1
    h2_ref[...] = x1 * (1.0 + m[:, 4 * d:5 * d]) + m[:, 3 * d:4 * d]


def _outproj_ln(ya, yb, yc, w, xs, mod, ln_g, ln_b, *, n_batch, ctx_len):
    t, d = xs.shape
    tiles = t // ROW_TILE
    row = lambda n: pl.BlockSpec((ROW_TILE, n), lambda i: (i, 0))
    out = jax.ShapeDtypeStruct((t, d), jnp.float32)
    return pl.pallas_call(
        _outproj_ln_kernel,
        grid=(tiles,),
        in_specs=[row(D_S5), row(D_HG), row(D_RW), pl.BlockSpec((D_MIX, d), lambda i: (0, 0)), row(d),
                  _mod_spec(n_batch, tiles // n_batch, ctx_len // ROW_TILE),
                  pl.BlockSpec((2, d), lambda i: (0, 0))],
        out_specs=[row(d), row(d)],
        out_shape=[out, out],
        compiler_params=pltpu.CompilerParams(dimension_semantics=("parallel",)),
        name="out_proj_ln",
    )(ya, yb, yc, w.astype(_BF), xs, mod, jnp.stack([ln_g, ln_b]))


def kernel(x, c, ctx, c_ctx, ada_w, ada_b, w_in, w_out, ln1_g, ln1_b, ln2_g, ln2_b,
           s5_lam_re, s5_lam_im, s5_log_step, s5_b_re, s5_b_im, s5_c_re, s5_c_im,
           s5_d, s5_glu_w, s5_glu_b, hgrn_lb_logits, hgrn_norm_g,
           rwkv_mu, rwkv_w0, rwkv_w_up, rwkv_a0, rwkv_a_up, rwkv_g_up, rwkv_k_k, rwkv_k_a,
           rwkv_r_k, rwkv_gn_g, rwkv_gn_b, peer_wq, peer_keys, peer_u, peer_v):
    nb, lx, d = x.shape
    lc = ctx.shape[1]
    l_all = lc + lx
    assert lx % GRID_W == 0 and lc % ROW_TILE == 0 and lx % ROW_TILE == 0 and (nb * l_all) % PEER_TN == 0
    lb_cum = jnp.cumsum(jax.nn.softmax(_f32(hgrn_lb_logits), axis=1), axis=1)
    lb_all = lb_cum - lb_cum[:, :1]
    xs = jnp.concatenate([ctx, x], axis=1).reshape(nb * l_all, d)
    seq = lambda t: t.reshape(nb, l_all, t.shape[-1])
    for l in range(DEPTH):
        last = l == DEPTH - 1
        mod = jnp.concatenate([jax.nn.silu(c), jax.nn.silu(c_ctx)[None]], axis=0) @ ada_w[l] + ada_b[l]
        g2 = mod[:, None, 5 * d:]
        mod = mod[:, None, :]
        pa, pc = _inproj(xs, mod, w_in[l], n_batch=nb, ctx_len=lc)
        pa = seq(pa)
        ya = _s5_mixer(pa[..., :D_S5], lc, s5_lam_re[l], s5_lam_im[l], s5_log_step[l],
                       s5_b_re[l], s5_b_im[l], s5_c_re[l], s5_c_im[l], s5_d[l], s5_glu_w[l], s5_glu_b[l])
        yb = _hgrn2_mixer(pa, lc, lb_all[:, l], hgrn_norm_g[l], col0=D_S5 // D_HG)
        yc = _rwkv7_mixer_fused(seq(pc), lc, rwkv_mu[l], rwkv_w0[l], rwkv_w_up[l], rwkv_a0[l], rwkv_a_up[l],
                                rwkv_g_up[l], rwkv_k_k[l], rwkv_k_a[l], rwkv_r_k[l], rwkv_gn_g[l], rwkv_gn_b[l])
        flat = lambda t: t.reshape(nb * l_all, t.shape[-1])
        xs, h2 = _outproj_ln(flat(ya), flat(yb), flat(yc), w_out[l], xs, mod, ln1_g[l], ln1_b[l],
                             n_batch=nb, ctx_len=lc)
        if last:
            xs, h2 = seq(xs)[:, lc:], seq(h2)[:, lc:].reshape(nb * lx, d)
            gate = g2[:nb]
        else:
            xs = seq(xs)
            gate = jnp.where((jnp.arange(l_all) < lc)[None, :, None], g2[nb:], g2[:nb])
        ffn = _peer_ffn_dense(h2, peer_wq[l], peer_keys[l], peer_u[l].astype(_BF), peer_v[l].T.astype(_BF))
        xs = _layer_norm(DEEPNORM_ALPHA * xs + gate * ffn.reshape(xs.shape), ln2_g[l], ln2_b[l])
        xs = xs.reshape(-1, d)
    return xs.reshape(nb, lx, d)
```

```python
import functools
import math

import jax
import jax.numpy as jnp
from jax import lax
from jax.experimental import pallas as pl
from jax.experimental.pallas import tpu as pltpu

D_MODEL = 1024
DEPTH = 2
GRID_W = 64
D_S5 = D_MODEL // 4
D_HG = D_MODEL // 4
D_RW = D_MODEL // 2
D_MIX = D_S5 + D_HG + D_RW
S5_H = 16
S5_G = D_S5 // S5_H
S5_P = 64
HG_HEAD = 64
HG_CHUNK = 16
RW_HEAD = 64
RW_W_LORA = 64
RW_A_LORA = 64
RW_G_LORA = 128
RW_COLS = 3 * D_RW + RW_G_LORA + 2 * RW_W_LORA + 2 * RW_A_LORA
RW_GN_EPS = 64e-5
PEER_HEADS = 8
PEER_NKEYS = 128
PEER_N = PEER_NKEYS * PEER_NKEYS
PEER_QDIM = 256
PEER_TOPK = 16
LN_EPS = 1e-5
RMS_EPS = 1e-6
DEEPNORM_ALPHA = (2.0 * DEPTH) ** 0.25


def _mm_kernel(x_ref, w_ref, o_ref):
    o_ref[...] = jnp.dot(x_ref[...].astype(jnp.bfloat16), w_ref[...],
                         preferred_element_type=jnp.float32)


def _pick_tile(n, cands):
    for c in cands:
        if n % c == 0:
            return c
    return n


def _matmul(x, w):
    m, k = x.shape
    n = w.shape[1]
    tm = _pick_tile(m, (512, 256, 128, 8))
    tn = _pick_tile(n, (1152, 1024, 512, 256, 128))
    return pl.pallas_call(
        _mm_kernel,
        grid=(m // tm, n // tn),
        in_specs=[pl.BlockSpec((tm, k), lambda i, j: (i, 0)),
                  pl.BlockSpec((k, tn), lambda i, j: (0, j))],
        out_specs=pl.BlockSpec((tm, tn), lambda i, j: (i, j)),
        out_shape=jax.ShapeDtypeStruct((m, n), jnp.float32),
        compiler_params=pltpu.CompilerParams(
            dimension_semantics=("parallel", "parallel"),
            vmem_limit_bytes=48 * 1024 * 1024),
    )(x, w.astype(jnp.bfloat16))


def _mm3(x, w):
    b, l, k = x.shape
    return _matmul(x.reshape(b * l, k), w).reshape(b, l, w.shape[1])


_BF = jnp.bfloat16
_NN = (((1,), (0,)), ((), ()))
_NT = (((1,), (1,)), ((), ()))
_TN = (((0,), (0,)), ((), ()))


def _dot(a, b, dims=_NN):
    return lax.dot_general(a.astype(_BF), b.astype(_BF), dims,
                           preferred_element_type=jnp.float32)


def _split_bf16(a):
    hi = a.astype(_BF)
    return hi, (a - hi.astype(jnp.float32)).astype(_BF)


def _dot3(a, b, dims=_NN):
    a_hi, a_lo = _split_bf16(a)
    b_hi, b_lo = _split_bf16(b)
    d = functools.partial(lax.dot_general, dimension_numbers=dims,
                          preferred_element_type=jnp.float32)
    return d(a_hi, b_hi) + (d(a_hi, b_lo) + d(a_lo, b_hi))


RW_CHUNK = 64


def _scan_block_order(n_ctx, n_all, reverse):
    if not reverse:
        return lambda c: c
    return lambda c: jnp.where(c < n_ctx, n_ctx - 1 - c, n_all - 1 - c + n_ctx)


PEER_TN = 512
PEER_EBLK = 2048
_PEER_SUB = 512
_PEER_OUT = 1024
_NEG = -3.0e38


def _top_rows(s, n):
    vals = []
    for _ in range(n):
        m = jnp.max(s, axis=0, keepdims=True)
        vals.append(m)
        s = jnp.where(s == m, _NEG, s)
    return vals


def _peer_route_kernel(h_ref, wq_ref, keys_ref, xb_ref, pi_ref, c1_ref, p2_ref):
    hb = h_ref[...].astype(_BF)
    xb_ref[...] = h_ref[...].T.astype(_BF)
    q = jnp.dot(hb, wq_ref[...], preferred_element_type=jnp.float32)
    half = PEER_QDIM // 2
    for h in range(PEER_HEADS):
        s1 = _dot(keys_ref[2 * h], q[:, (2 * h) * half:(2 * h + 1) * half], _NT)
        s2 = _dot(keys_ref[2 * h + 1], q[:, (2 * h + 1) * half:(2 * h + 2) * half], _NT)
        t1 = _top_rows(s1, PEER_TOPK + 1)
        t2 = _top_rows(s2, PEER_TOPK + 1)
        cand = [t1[j1] + t2[j2] for j1 in range(PEER_TOPK + 1) for j2 in range(PEER_TOPK + 1)
                if (j1 + 1) * (j2 + 1) <= PEER_TOPK + 1]
        cand += [jnp.full_like(cand[0], _NEG)] * (-len(cand) % 8)
        top = _top_rows(jnp.concatenate(cand, axis=0), PEER_TOPK + 1)
        theta = 0.5 * (top[PEER_TOPK - 1] + top[PEER_TOPK])
        z = top[0] * 0.0
        for c in top[:PEER_TOPK]:
            z = z + jnp.exp(c - top[0])
        pi_ref[h] = jnp.exp(jnp.maximum(theta - s1 - t2[0], -80.0))
        c1_ref[h] = jnp.exp(s1 - t1[0]) / z
        p2_ref[h] = jnp.exp(s2 - t2[0])


def _peer_route(hf, wq, keys):
    t, d = hf.shape
    tn = PEER_TN
    nq = PEER_HEADS * PEER_QDIM
    aux = jax.ShapeDtypeStruct((PEER_HEADS, PEER_NKEYS, t), jnp.float32)
    aux_spec = pl.BlockSpec((PEER_HEADS, PEER_NKEYS, tn), lambda i: (0, 0, i))
    return pl.pallas_call(
        _peer_route_kernel,
        grid=(t // tn,),
        in_specs=[pl.BlockSpec((tn, d), lambda i: (i, 0)),
                  pl.BlockSpec((d, nq), lambda i: (0, 0)),
                  pl.BlockSpec((PEER_HEADS * 2, PEER_NKEYS, PEER_QDIM // 2), lambda i: (0, 0, 0))],
        out_specs=[pl.BlockSpec((d, tn), lambda i: (0, i))] + [aux_spec] * 3,
        out_shape=[jax.ShapeDtypeStruct((d, t), _BF)] + [aux] * 3,
        compiler_params=pltpu.CompilerParams(dimension_semantics=("parallel",),
                                             vmem_limit_bytes=56 * 1024 * 1024),
        name="peer_route",
    )(hf, wq.astype(_BF), keys.reshape(PEER_HEADS * 2, PEER_NKEYS, PEER_QDIM // 2))


def _gelu_tanh(z):
    c = math.sqrt(2.0 / math.pi)
    hz = 0.5 * z
    return hz + hz * jnp.tanh(z * (c + (c * 0.044715) * (z * z)))


def _peer_expert_kernel(x_ref, u_ref, vt_ref, pi_ref, c1_ref, p2_ref, o_ref, acc_ref, at_ref):
    j = pl.program_id(1)

    @pl.when(j == 0)
    def _():
        acc_ref[...] = jnp.zeros_like(acc_ref)

    nk = PEER_NKEYS
    n_slab = PEER_EBLK // nk
    rows = pl.ds(pl.multiple_of(j * n_slab, n_slab), n_slab)
    sub = _PEER_SUB
    per = sub // nk

    def scores(sb):
        return jnp.dot(u_ref[sb * sub:(sb + 1) * sub, :], x_ref[...],
                       preferred_element_type=jnp.float32)

    zt = scores(0)
    for sb in range(PEER_EBLK // sub):
        zt_next = scores(sb + 1) if (sb + 1) * sub < PEER_EBLK else None
        for k in range(per):
            s = sb * per + k
            for lt in range(PEER_TN // 128):
                lanes = pl.ds(lt * 128, 128)
                g = jnp.zeros((nk, 128), jnp.float32)
                for h in range(PEER_HEADS):
                    pi = pi_ref[h, rows, lanes][s:s + 1]
                    c1 = c1_ref[h, rows, lanes][s:s + 1]
                    p2 = p2_ref[h, :, lanes]
                    g = g + jnp.where(p2 >= pi, p2 * c1, 0.0)
                z = zt[k * nk:(k + 1) * nk, lt * 128:(lt + 1) * 128]
                at_ref[s * nk:(s + 1) * nk, lt * 128:(lt + 1) * 128] = (g * _gelu_tanh(z)).astype(_BF)
        zt = zt_next
        done = (sb + 1) * sub
        if done % _PEER_OUT == 0:
            part = slice(done - _PEER_OUT, done)
            acc_ref[...] += jnp.dot(vt_ref[:, part], at_ref[part, :], preferred_element_type=jnp.float32)

    @pl.when(j == pl.num_programs(1) - 1)
    def _():
        o_ref[...] = acc_ref[...].T


def _peer_expert(xb, u_b, vt_b, aux):
    d, t = xb.shape
    tn, eb = PEER_TN, PEER_EBLK
    aux_spec = pl.BlockSpec((PEER_HEADS, PEER_NKEYS, tn), lambda i, j: (0, 0, i))
    return pl.pallas_call(
        _peer_expert_kernel,
        grid=(t // tn, PEER_N // eb),
        in_specs=[pl.BlockSpec((d, tn), lambda i, j: (0, i)),
                  pl.BlockSpec((eb, d), lambda i, j: (j, 0)),
                  pl.BlockSpec((d, eb), lambda i, j: (0, j))] + [aux_spec] * 3,
        out_specs=pl.BlockSpec((tn, d), lambda i, j: (i, 0)),
        out_shape=jax.ShapeDtypeStruct((t, d), jnp.float32),
        scratch_shapes=[pltpu.VMEM((d, tn), jnp.float32), pltpu.VMEM((eb, tn), _BF)],
        compiler_params=pltpu.CompilerParams(dimension_semantics=("parallel", "arbitrary"),
                                             vmem_limit_bytes=56 * 1024 * 1024),
        name="peer_expert",
    )(xb, u_b, vt_b, *aux)


def _peer_ffn_dense(hf, wq, keys, u_b, vt_b):
    xb, *aux = _peer_route(hf, wq, keys)
    return _peer_expert(xb, u_b, vt_b, aux)


S5_STEPS = 128
_S5_ROWS = 8
_S5_STATE = S5_G * S5_P


def _s5_weights(disc, c_re, c_im):
    eye = jnp.eye(S5_G, dtype=jnp.float32)

    def bdiag_in(bb):
        return jnp.einsum('gph,gk->ghkp', bb, eye).reshape(D_S5, _S5_STATE)

    def bdiag_out(cc):
        return jnp.einsum('ghp,gk->gpkh', cc, eye).reshape(_S5_STATE, D_S5)

    wb = jnp.stack([jnp.concatenate([bdiag_in(disc[d][2]), bdiag_in(disc[d][3])], axis=1) for d in range(2)])
    wc = jnp.stack([jnp.concatenate([bdiag_out(c_re[d]), -bdiag_out(c_im[d])], axis=0) for d in range(2)])
    half = _S5_ROWS // 2
    lam = jnp.stack([
        jnp.concatenate([jnp.broadcast_to(disc[d][i].reshape(1, _S5_STATE), (half, _S5_STATE))
                         for d in range(2)], axis=0) for i in range(2)])
    return wb.astype(_BF), lam, wc.astype(_BF)


def _s5_dir_kernel(uf_ref, ub_ref, wb_ref, lam_ref, wc_ref, yf_ref, yb_ref, buf_ref, bub_ref, xf_ref, xb_ref, st_ref):
    @pl.when(pl.program_id(0) == 0)
    def _():
        st_ref[...] = jnp.zeros_like(st_ref)

    n = _S5_STATE
    buf_ref[...] = jnp.dot(uf_ref[...].astype(_BF), wb_ref[0], preferred_element_type=jnp.float32)
    bub_ref[...] = jnp.dot(ub_ref[...].astype(_BF), wb_ref[1], preferred_element_type=jnp.float32)
    lr = lam_ref[0]
    li = lam_ref[1]
    fwd = lax.broadcasted_iota(jnp.int32, (_S5_ROWS, n), 0) < (_S5_ROWS // 2)

    def step(i, carry):
        xr, xi = carry
        rf = pl.ds(pl.multiple_of(i * _S5_ROWS, _S5_ROWS), _S5_ROWS)
        rb = pl.ds(pl.multiple_of((S5_STEPS - 1 - i) * _S5_ROWS, _S5_ROWS), _S5_ROWS)
        nr = lr * xr - li * xi + jnp.where(fwd, buf_ref[rf, :n], bub_ref[rb, :n])
        ni = lr * xi + li * xr + jnp.where(fwd, buf_ref[rf, n:], bub_ref[rb, n:])
        xf_ref[rf, :n] = nr
        xf_ref[rf, n:] = ni
        xb_ref[rb, :n] = nr
        xb_ref[rb, n:] = ni
        return nr, ni

    xr, xi = lax.fori_loop(0, S5_STEPS, step, (st_ref[0], st_ref[1]), unroll=4)
    st_ref[0] = xr
    st_ref[1] = xi
    yf_ref[...] = jnp.dot(xf_ref[...].astype(_BF), wc_ref[0], preferred_element_type=jnp.float32)
    yb_ref[...] = jnp.dot(xb_ref[...].astype(_BF), wc_ref[1], preferred_element_type=jnp.float32)


def _s5_dir_scan(u8, wb, lam, wc, *, ctx_len):
    rows = u8.shape[0]
    blk = S5_STEPS * _S5_ROWS
    n = _S5_STATE
    order = _scan_block_order(ctx_len // S5_STEPS, rows // blk, True)
    spec_f = pl.BlockSpec((blk, D_S5), lambda i: (i, 0))
    spec_b = pl.BlockSpec((blk, D_S5), lambda i: (order(i), 0))
    out = jax.ShapeDtypeStruct((rows, D_S5), jnp.float32)
    big = pltpu.VMEM((blk, 2 * n), jnp.float32)
    return pl.pallas_call(
        _s5_dir_kernel,
        grid=(rows // blk,),
        in_specs=[spec_f, spec_b,
                  pl.BlockSpec((2, D_S5, 2 * n), lambda i: (0, 0, 0)),
                  pl.BlockSpec((2, _S5_ROWS, n), lambda i: (0, 0, 0)),
                  pl.BlockSpec((2, 2 * n, D_S5), lambda i: (0, 0, 0))],
        out_specs=[spec_f, spec_b],
        out_shape=[out, out],
        scratch_shapes=[big, big, big, big, pltpu.VMEM((2, _S5_ROWS, n), jnp.float32)],
        compiler_params=pltpu.CompilerParams(dimension_semantics=("arbitrary",),
                                             vmem_limit_bytes=56 * 1024 * 1024),
        name="s5_scan",
    )(u8, u8, wb, lam, wc)


HG_BLOCK = 128
HG_BATCH_BLOCK = 4
_HG_PAIR = 2 * HG_HEAD


def _hgrn_dir_kernel(q_ref, v_ref, f_ref, lb_ref, o_ref, st_ref, *, reverse):
    @pl.when(pl.program_id(1) == 0)
    def _():
        st_ref[...] = jnp.zeros_like(st_ref)

    ch = HG_CHUNK
    n_ch = HG_BLOCK // ch
    r = lax.broadcasted_iota(jnp.int32, (HG_BLOCK, HG_BLOCK), 0)
    c = lax.broadcasted_iota(jnp.int32, (HG_BLOCK, HG_BLOCK), 1)
    tri = ((r // ch == c // ch) & ((r <= c) if reverse else (r >= c))).astype(jnp.float32)
    same_head = (lax.broadcasted_iota(jnp.int32, (_HG_PAIR, _HG_PAIR), 0) // HG_HEAD
                 == lax.broadcasted_iota(jnp.int32, (_HG_PAIR, _HG_PAIR), 1) // HG_HEAD)
    ones_blk = same_head.astype(_BF)
    tcol = lax.broadcasted_iota(jnp.int32, (ch, 1), 0)
    end = 0 if reverse else ch - 1
    n_pair = D_HG // _HG_PAIR
    chains = [(bi, p) for bi in range(q_ref.shape[0]) for p in range(n_pair)]
    pair_lanes = lambda p: slice(p * _HG_PAIR, (p + 1) * _HG_PAIR)
    intra, kv, dec, qd = {}, {}, {}, {}
    for bi in range(q_ref.shape[0]):
        q_raw = q_ref[bi]
        q_all = q_raw * _sigmoid(q_raw)
        forget = lb_ref[...] + (1.0 - lb_ref[...]) * _sigmoid(f_ref[bi])
        k_all = 1.0 - forget
        cum_all = _dot3(tri, jnp.log(forget))
        for p in range(n_pair):
            lanes = pair_lanes(p)
            for ci in range(n_ch):
                rows = slice(ci * ch, (ci + 1) * ch)
                q = q_all[rows, lanes]
                k = k_all[rows, lanes]
                v = v_ref[bi, rows, lanes]
                cum = cum_all[rows, lanes]
                last = cum[end:end + 1]
                w = [q * jnp.exp(jnp.minimum(cum - cum[s:s + 1], 0.0)) * k[s:s + 1] for s in range(ch)]
                rel = jnp.dot(jnp.concatenate(w, axis=0).astype(_BF), ones_blk,
                              preferred_element_type=jnp.float32)
                o = jnp.zeros((ch, _HG_PAIR), jnp.float32)
                for s in range(ch):
                    seen = (tcol <= s) if reverse else (tcol >= s)
                    o = o + jnp.where(seen, rel[s * ch:(s + 1) * ch], 0.0) * v[s:s + 1]
                intra[bi, p, ci] = o
                kv[bi, p, ci] = jnp.where(same_head, _dot(v, k * jnp.exp(last - cum), _TN), 0.0)
                dec[bi, p, ci] = jnp.exp(last)
                qd[bi, p, ci] = q * jnp.exp(cum)
    state = [st_ref[i] for i in range(len(chains))]
    for ci in (range(n_ch - 1, -1, -1) if reverse else range(n_ch)):
        for i, (bi, p) in enumerate(chains):
            o_ref[bi, ci * ch:(ci + 1) * ch, pair_lanes(p)] = intra[bi, p, ci] + _dot(qd[bi, p, ci], state[i], _NT)
            state[i] = state[i] * dec[bi, p, ci] + kv[bi, p, ci]
    for i in range(len(chains)):
        st_ref[i] = state[i]


def _hgrn_dir_scan(p, lb, *, ctx_len, reverse, col0=0):
    g, l, _ = p.shape
    d = D_HG
    order = _scan_block_order(ctx_len // HG_BLOCK, l // HG_BLOCK, reverse)
    bb = HG_BATCH_BLOCK if g % HG_BATCH_BLOCK == 0 else 1
    col = lambda c: pl.BlockSpec((bb, HG_BLOCK, d), lambda i, j: (i, order(j), col0 + c))
    dr = 1 if reverse else 0
    return pl.pallas_call(
        functools.partial(_hgrn_dir_kernel, reverse=reverse),
        grid=(g // bb, l // HG_BLOCK),
        in_specs=[col(0), col(1), col(2 + dr), pl.BlockSpec((1, d), lambda i, j: (0, 0))],
        out_specs=pl.BlockSpec((bb, HG_BLOCK, d), lambda i, j: (i, order(j), 0)),
        out_shape=jax.ShapeDtypeStruct((g, l, d), jnp.float32),
        scratch_shapes=[pltpu.VMEM((bb * d // _HG_PAIR, _HG_PAIR, _HG_PAIR), jnp.float32)],
        compiler_params=pltpu.CompilerParams(dimension_semantics=("parallel", "arbitrary")),
        name="hgrn_scan",
    )(p, p, p, lb[dr:dr + 1])


def _hgrn_post_kernel(of_ref, ob_ref, g_ref, ng_ref, ones_ref, y_ref):
    o = of_ref[0] + ob_ref[0]
    ms = jnp.dot((o * o).astype(_BF), ones_ref[...], preferred_element_type=jnp.float32) * (1.0 / HG_HEAD)
    g = g_ref[0]
    y_ref[0] = o * lax.rsqrt(ms + RMS_EPS) * ng_ref[...] * (g * _sigmoid(g))


def _hgrn_post(o_f, o_b, p, norm_g, col0=0):
    nb, l, d = o_f.shape
    spec = pl.BlockSpec((1, HG_BLOCK, d), lambda i, j: (i, j, 0))
    return pl.pallas_call(
        _hgrn_post_kernel,
        grid=(nb, l // HG_BLOCK),
        in_specs=[spec, spec, pl.BlockSpec((1, HG_BLOCK, d), lambda i, j: (i, j, col0 + 4)),
                  pl.BlockSpec((1, d), lambda i, j: (0, 0)), pl.BlockSpec((d, d), lambda i, j: (0, 0))],
        out_specs=spec,
        out_shape=jax.ShapeDtypeStruct((nb, l, d), jnp.float32),
        compiler_params=pltpu.CompilerParams(dimension_semantics=("parallel", "parallel")),
        name="hgrn_post",
    )(o_f, o_b, p, norm_g.reshape(1, d), _head_ones(d, HG_HEAD))


_RW_PAIR = 2 * RW_HEAD
RW_PRE_T = 128
RW_BATCH_BLOCK = 4


def _rwkv_pair_kernel(r_ref, lw_ref, k_ref, v_ref, kk_ref, a_ref, y_ref, st_ref, *, chunk, reverse):
    @pl.when(pl.program_id(1) == 0)
    def _():
        st_ref[...] = jnp.zeros_like(st_ref)

    row = lax.broadcasted_iota(jnp.int32, (chunk, chunk), 0)
    col = lax.broadcasted_iota(jnp.int32, (chunk, chunk), 1)
    strict = row < col if reverse else row > col
    incl = row <= col if reverse else row >= col
    end = 0 if reverse else chunk - 1
    tri = incl.astype(jnp.float32)
    eye = (row == col).astype(jnp.float32)
    blk8 = row // 8 == col // 8
    merge_masks = []
    size = 8
    while size < chunk:
        merge_masks.append((row // (2 * size) == col // (2 * size)) & (row // size != col // size))
        size *= 2
    lane_head = lax.broadcasted_iota(jnp.int32, (1, _RW_PAIR), 1) // RW_HEAD
    head0 = lane_head == 0
    same_head = (lax.broadcasted_iota(jnp.int32, (_RW_PAIR, _RW_PAIR), 0) // RW_HEAD
                 == lax.broadcasted_iota(jnp.int32, (_RW_PAIR, _RW_PAIR), 1) // RW_HEAD)
    n_pair = D_RW // _RW_PAIR
    ps = range(r_ref.shape[0] * n_pair)
    ph = [(p, h) for p in ps for h in range(2)]
    bi = [p // n_pair for p in ps]
    sl = [slice((p % n_pair) * _RW_PAIR, (p % n_pair + 1) * _RW_PAIR) for p in ps]
    lw = [lw_ref[bi[p], :, sl[p]] for p in ps]
    cum = [_dot3(tri, lw[p]) for p in ps]
    pw = [jnp.exp(cum[p]) for p in ps]
    p_inv = [jnp.exp(-cum[p]) for p in ps]
    p_end = [pw[p][end:end + 1, :] for p in ps]
    kk = [kk_ref[bi[p], :, sl[p]] for p in ps]
    bh = [kk[p] * a_ref[bi[p], :, sl[p]] * p_inv[p] for p in ps]
    kh = [k_ref[bi[p], :, sl[p]] * p_inv[p] for p in ps]
    v = [v_ref[bi[p], :, sl[p]] for p in ps]
    s0 = [st_ref[p] for p in ps]
    x1 = [jnp.concatenate([-kk[p] * jnp.exp(cum[p] - lw[p]), r_ref[bi[p], :, sl[p]] * pw[p]], axis=0) for p in ps]
    x1h = {(p, h): jnp.where(lane_head == h, x1[p], 0.0) for p, h in ph}
    g1 = {q: _dot(x1h[q], bh[q[0]], _NT) for q in ph}
    g2 = {q: _dot(x1h[q], kh[q[0]], _NT) for q in ph}
    hm = [_dot(x1[p], s0[p], _NT) for p in ps]
    ab = {q: jnp.where(strict, g1[q][:chunk], 0.0) for q in ph}
    d1 = {q: jnp.where(blk8, ab[q], 0.0) for q in ph}
    d2 = {q: _dot3(d1[q], d1[q]) for q in ph}
    inv = {q: eye + d1[q] for q in ph}
    inv = {q: inv[q] + _dot3(inv[q], d2[q]) for q in ph}
    d4 = {q: _dot3(d2[q], d2[q]) for q in ph}
    inv = {q: inv[q] + _dot3(inv[q], d4[q]) for q in ph}
    for m in merge_masks:
        li = {q: _dot(jnp.where(m, ab[q], 0.0), inv[q]) for q in ph}
        inv = {q: inv[q] + _dot(inv[q], li[q]) for q in ph}
    akv = {q: _dot(jnp.where(strict, g2[q][:chunk], 0.0), v[q[0]]) for q in ph}
    rhs = [hm[p][:chunk] + jnp.where(head0, akv[p, 0], akv[p, 1]) for p in ps]
    eh = {q: _dot(inv[q], rhs[q[0]]) for q in ph}
    e = [jnp.where(head0, eh[p, 0], eh[p, 1]) for p in ps]
    yh = {q: _dot(jnp.where(incl, g1[q][chunk:], 0.0), e[q[0]])
          + _dot(jnp.where(incl, g2[q][chunk:], 0.0), v[q[0]]) for q in ph}
    for p in ps:
        y_ref[bi[p], :, sl[p]] = hm[p][chunk:] + jnp.where(head0, yh[p, 0], yh[p, 1])
    for p in ps:
        ev = jnp.concatenate([e[p], v[p]], axis=0)
        x2 = jnp.concatenate([bh[p] * p_end[p], kh[p] * p_end[p]], axis=0)
        st_ref[p] = s0[p] * p_end[p] + jnp.where(same_head, _dot(ev, x2, _TN), 0.0)


def _rwkv_pair_scan(r, lw, k, v, kk, a, *, ctx_len, reverse):
    nb, l, d = r.shape
    chunk = RW_CHUNK
    order = _scan_block_order(ctx_len // chunk, l // chunk, reverse)
    bb = RW_BATCH_BLOCK if nb % RW_BATCH_BLOCK == 0 else 1
    spec = pl.BlockSpec((bb, chunk, d), lambda b, c: (b, order(c), 0))
    return pl.pallas_call(
        functools.partial(_rwkv_pair_kernel, chunk=chunk, reverse=reverse),
        grid=(nb // bb, l // chunk),
        in_specs=[spec] * 6,
        out_specs=spec,
        out_shape=jax.ShapeDtypeStruct((nb, l, d), jnp.float32),
        scratch_shapes=[pltpu.VMEM((bb * d // _RW_PAIR, _RW_PAIR, _RW_PAIR), jnp.float32)],
        compiler_params=pltpu.CompilerParams(dimension_semantics=("parallel", "arbitrary")),
        name="rwkv_scan",
    )(r, lw, k, v, kk, a)


def _softplus(z):
    return jnp.maximum(z, 0.0) + jnp.log(1.0 + jnp.exp(-jnp.abs(z)))


def _sigmoid(z):
    return 1.0 / (1.0 + jnp.exp(-z))


def _rwkv_pre_kernel(prev_ref, cur_ref, next_ref, mu_ref, vec_ref, gup_ref, wup_ref, aup_ref, ones_ref,
                     r_ref, v_ref, kk_ref, g_ref, bonus_ref, lw0_ref, k0_ref, a0_ref, lw1_ref, k1_ref, a1_ref,
                     ext_ref, s_ref, *, ctx_len, seq_len):
    t_rows, w = RW_PRE_T, GRID_W
    ext_ref[0:w] = prev_ref[0]
    ext_ref[w:w + t_rows] = cur_ref[0]
    ext_ref[w + t_rows:w + t_rows + w] = next_ref[0]
    t = pl.program_id(1) * t_rows + lax.broadcasted_iota(jnp.int32, (t_rows, 128), 0)
    lane4 = lax.broadcasted_iota(jnp.int32, (t_rows, 128), 1) % 4
    is_ctx = t < ctx_len
    tx = t - ctx_len
    col = tx % w
    is_lat = jnp.logical_not(is_ctx)
    use_p1 = (is_ctx & (lane4 % 2 == 0) & (t > 0)) | (is_lat & (lane4 == 0) & (col != 0))
    use_n1 = (is_ctx & (lane4 % 2 == 1) & (t < ctx_len - 1)) | (is_lat & (lane4 == 1) & (col != w - 1))
    use_p64 = is_lat & (lane4 == 2) & (tx >= w)
    use_n64 = is_lat & (lane4 == 3) & (tx < seq_len - w)
    for lt in range(RW_COLS // 128):
        lanes = slice(lt * 128, (lt + 1) * 128)
        p = ext_ref[w:w + t_rows, lanes]
        shifted = jnp.where(use_p1, ext_ref[w - 1:w - 1 + t_rows, lanes],
                            jnp.where(use_n1, ext_ref[w + 1:w + 1 + t_rows, lanes],
                                      jnp.where(use_p64, ext_ref[0:t_rows, lanes],
                                                jnp.where(use_n64, ext_ref[2 * w:2 * w + t_rows, lanes], 0.0))))
        s_ref[:, lanes] = p + mu_ref[:, lanes] * (shifted - p)
    d = D_RW
    r = s_ref[:, 0:d]
    k = s_ref[:, d:2 * d]
    v = s_ref[:, 2 * d:3 * d]
    o = 3 * d
    g_lo = s_ref[:, o:o + RW_G_LORA]
    wl = s_ref[:, o + RW_G_LORA:o + RW_G_LORA + 2 * RW_W_LORA]
    al = s_ref[:, o + RW_G_LORA + 2 * RW_W_LORA:o + RW_G_LORA + 2 * RW_W_LORA + 2 * RW_A_LORA]
    k_k, k_a, r_k = vec_ref[0:1], vec_ref[1:2], vec_ref[2:3]
    ones = ones_ref[...]
    r_ref[0] = r
    v_ref[0] = v
    g_ref[0] = jnp.dot(_sigmoid(g_lo).astype(_BF), gup_ref[...], preferred_element_type=jnp.float32)
    kkr = k * k_k
    ss = jnp.dot((kkr * kkr).astype(_BF), ones, preferred_element_type=jnp.float32)
    kk_ref[0] = kkr / jnp.maximum(jnp.sqrt(ss), 1e-12)
    w_pre = jnp.dot(jnp.tanh(wl).astype(_BF), wup_ref[...], preferred_element_type=jnp.float32)
    a_pre = jnp.dot(al.astype(_BF), aup_ref[...], preferred_element_type=jnp.float32)
    k_sum = 0.0
    for dr, (lw_o, k_o, a_o) in enumerate(((lw0_ref, k0_ref, a0_ref), (lw1_ref, k1_ref, a1_ref))):
        w_log = -_softplus(-(vec_ref[3 + dr:4 + dr] + w_pre[:, dr * d:(dr + 1) * d])) - 0.5
        a = _sigmoid(vec_ref[5 + dr:6 + dr] + a_pre[:, dr * d:(dr + 1) * d])
        k_d = k * (1.0 + (a - 1.0) * k_a)
        lw_o[0] = -jnp.exp(w_log)
        k_o[0] = k_d
        a_o[0] = a
        k_sum = k_sum + k_d
    bonus_ref[0] = jnp.dot((r * k_sum * r_k).astype(_BF), ones, preferred_element_type=jnp.float32)


def _head_ones(width, head):
    i = jnp.arange(width) // head
    return (i[:, None] == i[None, :]).astype(_BF)


def _rwkv_pre(p, mu, k_k, k_a, r_k, w0, a0, g_up, w_up, a_up, *, ctx_len):
    nb, l, cols = p.shape
    t_rows, w, d = RW_PRE_T, GRID_W, D_RW
    nblk = l // w
    per = t_rows // w
    zeros = jnp.zeros((RW_W_LORA, d), jnp.float32)
    wup = jnp.concatenate([jnp.concatenate([w_up[0], zeros], axis=1),
                           jnp.concatenate([zeros, w_up[1]], axis=1)], axis=0).astype(_BF)
    aup = jnp.concatenate([jnp.concatenate([a_up[0], zeros], axis=1),
                           jnp.concatenate([zeros, a_up[1]], axis=1)], axis=0).astype(_BF)
    vec = jnp.stack([k_k, k_a, r_k.reshape(d), w0[0], w0[1], a0[0], a0[1], jnp.zeros((d,), jnp.float32)])
    out = jax.ShapeDtypeStruct((nb, l, d), jnp.float32)
    out_spec = pl.BlockSpec((1, t_rows, d), lambda b, j: (b, j, 0))
    full = lambda shape: pl.BlockSpec(shape, lambda b, j: (0,) * len(shape))
    return pl.pallas_call(
        functools.partial(_rwkv_pre_kernel, ctx_len=ctx_len, seq_len=l - ctx_len),
        grid=(nb, l // t_rows),
        in_specs=[pl.BlockSpec((1, w, cols), lambda b, j: (b, jnp.maximum(j * per - 1, 0), 0)),
                  pl.BlockSpec((1, t_rows, cols), lambda b, j: (b, j, 0)),
                  pl.BlockSpec((1, w, cols), lambda b, j: (b, jnp.minimum(j * per + per, nblk - 1), 0)),
                  full((1, cols)), full((8, d)), full((RW_G_LORA, d)),
                  full((2 * RW_W_LORA, 2 * d)), full((2 * RW_A_LORA, 2 * d)), full((d, d))],
        out_specs=[out_spec] * 11,
        out_shape=[out] * 11,
        scratch_shapes=[pltpu.VMEM((t_rows + 2 * w, cols), jnp.float32), pltpu.VMEM((t_rows, cols), jnp.float32)],
        compiler_params=pltpu.CompilerParams(dimension_semantics=("parallel", "parallel"),
                                             vmem_limit_bytes=48 * 1024 * 1024),
        name="rwkv_pre",
    )(p, p, p, mu.reshape(1, cols), vec, g_up.astype(_BF), wup, aup, _head_ones(d, RW_HEAD))


def _rwkv_post_kernel(y0_ref, y1_ref, v_ref, g_ref, bonus_ref, gn_ref, ones_ref, o_ref):
    y = y0_ref[0] + y1_ref[0]
    ones = ones_ref[...]
    inv_n = 1.0 / RW_HEAD
    mu_y = jnp.dot(y.astype(_BF), ones, preferred_element_type=jnp.float32) * inv_n
    yc = y - mu_y
    var_y = jnp.dot((yc * yc).astype(_BF), ones, preferred_element_type=jnp.float32) * inv_n
    yn = yc * lax.rsqrt(var_y + RW_GN_EPS) * gn_ref[0:1] + gn_ref[1:2]
    o_ref[0] = (yn + bonus_ref[0] * v_ref[0]) * g_ref[0]


def _rwkv_post(y0, y1, v, g, bonus, gn_g, gn_b):
    nb, l, d = y0.shape
    spec = pl.BlockSpec((1, RW_PRE_T, d), lambda b, j: (b, j, 0))
    return pl.pallas_call(
        _rwkv_post_kernel,
        grid=(nb, l // RW_PRE_T),
        in_specs=[spec] * 5 + [pl.BlockSpec((2, d), lambda b, j: (0, 0)), pl.BlockSpec((d, d), lambda b, j: (0, 0))],
        out_specs=spec,
        out_shape=jax.ShapeDtypeStruct((nb, l, d), jnp.float32),
        compiler_params=pltpu.CompilerParams(dimension_semantics=("parallel", "parallel")),
        name="rwkv_post",
    )(y0, y1, v, g, bonus, jnp.stack([gn_g, gn_b]), _head_ones(d, RW_HEAD))


def _rwkv7_mixer_fused(p, lc, mu, w0, w_up, a0, a_up, g_up, k_k, k_a, r_k, gn_g, gn_b):
    r, v, kk, g, bonus, lw0, k0, a0_, lw1, k1, a1_ = _rwkv_pre(
        p, _f32(mu), _f32(k_k), _f32(k_a), _f32(r_k), _f32(w0), _f32(a0), _f32(g_up), _f32(w_up), _f32(a_up),
        ctx_len=lc)
    y0 = _rwkv_pair_scan(r, lw0, k0, v, kk, a0_, ctx_len=lc, reverse=False)
    y1 = _rwkv_pair_scan(r, lw1, k1, v, kk, a1_, ctx_len=lc, reverse=True)
    return _rwkv_post(y0, y1, v, g, bonus, _f32(gn_g), _f32(gn_b))


def _f32(t):
    return t.astype(jnp.float32)


def _layer_norm(x, g, b):
    xf = _f32(x)
    mu = jnp.mean(xf, -1, keepdims=True)
    var = jnp.mean(jnp.square(xf - mu), -1, keepdims=True)
    return ((xf - mu) * lax.rsqrt(var + LN_EPS) * _f32(g) + _f32(b)).astype(x.dtype)


def _s5_discretise(lam_re, lam_im, log_step, b_re, b_im):
    dt = jnp.exp(log_step)[:, None]
    mag = jnp.exp(lam_re * dt)
    ang = lam_im * dt
    lb_re, lb_im = mag * jnp.cos(ang), mag * jnp.sin(ang)
    den = lam_re * lam_re + lam_im * lam_im
    nr, ni = lb_re - 1.0, lb_im
    co_re = (nr * lam_re + ni * lam_im) / den
    co_im = (ni * lam_re - nr * lam_im) / den
    bb_re = co_re[..., None] * b_re - co_im[..., None] * b_im
    bb_im = co_re[..., None] * b_im + co_im[..., None] * b_re
    return lb_re, lb_im, bb_re, bb_im


def _s5_mixer(u, lc, lam_re, lam_im, log_step, b_re, b_im, c_re, c_im, d_skip, glu_w, glu_b):
    disc = [_s5_discretise(_f32(lam_re[d]), _f32(lam_im[d]), _f32(log_step[d]),
                           _f32(b_re[d]), _f32(b_im[d])) for d in range(2)]
    wb, lam, wc = _s5_weights(disc, _f32(c_re), _f32(c_im))
    u = _f32(u)
    nb, l, _ = u.shape
    assert 2 * nb == _S5_ROWS
    u_t = u.transpose(1, 0, 2)
    u8 = jnp.concatenate([u_t, u_t], axis=1).reshape(l * _S5_ROWS, D_S5)
    y_f, y_b = _s5_dir_scan(u8, wb, lam, wc, ctx_len=lc)
    y = (y_f.reshape(l, _S5_ROWS, D_S5)[:, :nb] + y_b.reshape(l, _S5_ROWS, D_S5)[:, nb:]).transpose(1, 0, 2)
    y = jax.nn.gelu(y + _f32(d_skip) * u)
    return y * jax.nn.sigmoid(_mm3(y, glu_w) + _f32(glu_b))


def _hgrn2_mixer(p, lc, lb, norm_g, col0=0):
    lb = _f32(lb)
    o_f = _hgrn_dir_scan(p, lb, ctx_len=lc, reverse=False, col0=col0)
    o_b = _hgrn_dir_scan(p, lb, ctx_len=lc, reverse=True, col0=col0)
    return _hgrn_post(o_f, o_b, p, _f32(norm_g), col0=col0)


ROW_TILE = 256


def _mod_spec(n_batch, tiles_per_batch, ctx_tiles):
    def index(i):
        return (jnp.where(i % tiles_per_batch < ctx_tiles, n_batch, i // tiles_per_batch), 0, 0)
    return pl.BlockSpec((1, 1, 6 * D_MODEL), index)


def _inproj_kernel(x_ref, mod_ref, w_ref, pa_ref, pc_ref):
    d = D_MODEL
    m = mod_ref[0]
    xm = (x_ref[...] * (1.0 + m[:, d:2 * d]) + m[:, 0:d]).astype(_BF)
    split = pa_ref.shape[1]
    pa_ref[...] = jnp.dot(xm, w_ref[:, :split], preferred_element_type=jnp.float32)
    pc_ref[...] = jnp.dot(xm, w_ref[:, split:], preferred_element_type=jnp.float32)


def _inproj(xs, mod, w, *, n_batch, ctx_len):
    t, d = xs.shape
    n_a = D_S5 + 5 * D_HG
    n_c = w.shape[1] - n_a
    tiles = t // ROW_TILE
    return pl.pallas_call(
        _inproj_kernel,
        grid=(tiles,),
        in_specs=[pl.BlockSpec((ROW_TILE, d), lambda i: (i, 0)),
                  _mod_spec(n_batch, tiles // n_batch, ctx_len // ROW_TILE),
                  pl.BlockSpec((d, n_a + n_c), lambda i: (0, 0))],
        out_specs=[pl.BlockSpec((ROW_TILE, n_a), lambda i: (i, 0)), pl.BlockSpec((ROW_TILE, n_c), lambda i: (i, 0))],
        out_shape=[jax.ShapeDtypeStruct((t, n_a), jnp.float32), jax.ShapeDtypeStruct((t, n_c), jnp.float32)],
        compiler_params=pltpu.CompilerParams(dimension_semantics=("parallel",),
                                             vmem_limit_bytes=48 * 1024 * 1024),
        name="in_proj",
    )(xs, mod, w.astype(_BF))


def _ln_rows(h, g, b):
    mu = jnp.mean(h, axis=-1, keepdims=True)
    hc = h - mu
    var = jnp.mean(hc * hc, axis=-1, keepdims=True)
    return hc * lax.rsqrt(var + LN_EPS) * g + b


def _outproj_ln_kernel(ya_ref, yb_ref, yc_ref, w_ref, xs_ref, mod_ref, ln_ref, x1_ref, h2_ref):
    d = D_MODEL
    mix = (jnp.dot(ya_ref[...].astype(_BF), w_ref[0:D_S5], preferred_element_type=jnp.float32)
           + jnp.dot(yb_ref[...].astype(_BF), w_ref[D_S5:D_S5 + D_HG], preferred_element_type=jnp.float32)
           + jnp.dot(yc_ref[...].astype(_BF), w_ref[D_S5 + D_HG:D_MIX], preferred_element_type=jnp.float32))
    m = mod_ref[0]
    x1 = _ln_rows(DEEPNORM_ALPHA * xs_ref[...] + m[:, 2 * d:3 * d] * mix, ln_ref[0:1], ln_ref[1:2])
    x1_ref[...] = x1
    h2_ref[...] = x1 * (1.0 + m[:, 4 * d:5 * d]) + m[:, 3 * d:4 * d]


def _outproj_ln(ya, yb, yc, w, xs, mod, ln_g, ln_b, *, n_batch, ctx_len):
    t, d = xs.shape
    tiles = t // ROW_TILE
    row = lambda n: pl.BlockSpec((ROW_TILE, n), lambda i: (i, 0))
    out = jax.ShapeDtypeStruct((t, d), jnp.float32)
    return pl.pallas_call(
        _outproj_ln_kernel,
        grid=(tiles,),
        in_specs=[row(D_S5), row(D_HG), row(D_RW), pl.BlockSpec((D_MIX, d), lambda i: (0, 0)), row(d),
                  _mod_spec(n_batch, tiles // n_batch, ctx_len // ROW_TILE),
                  pl.BlockSpec((2, d), lambda i: (0, 0))],
        out_specs=[row(d), row(d)],
        out_shape=[out, out],
        compiler_params=pltpu.CompilerParams(dimension_semantics=("parallel",)),
        name="out_proj_ln",
    )(ya, yb, yc, w.astype(_BF), xs, mod, jnp.stack([ln_g, ln_b]))


def kernel(x, c, ctx, c_ctx, ada_w, ada_b, w_in, w_out, ln1_g, ln1_b, ln2_g, ln2_b,
           s5_lam_re, s5_lam_im, s5_log_step, s5_b_re, s5_b_im, s5_c_re, s5_c_im,
           s5_d, s5_glu_w, s5_glu_b, hgrn_lb_logits, hgrn_norm_g,
           rwkv_mu, rwkv_w0, rwkv_w_up, rwkv_a0, rwkv_a_up, rwkv_g_up, rwkv_k_k, rwkv_k_a,
           rwkv_r_k, rwkv_gn_g, rwkv_gn_b, peer_wq, peer_keys, peer_u, peer_v):
    nb, lx, d = x.shape
    lc = ctx.shape[1]
    l_all = lc + lx
    assert lx % GRID_W == 0 and lc % ROW_TILE == 0 and lx % ROW_TILE == 0 and (nb * l_all) % PEER_TN == 0
    lb_cum = jnp.cumsum(jax.nn.softmax(_f32(hgrn_lb_logits), axis=1), axis=1)
    lb_all = lb_cum - lb_cum[:, :1]
    xs = jnp.concatenate([ctx, x], axis=1).reshape(nb * l_all, d)
    seq = lambda t: t.reshape(nb, l_all, t.shape[-1])
    for l in range(DEPTH):
        last = l == DEPTH - 1
        mod = jnp.concatenate([jax.nn.silu(c), jax.nn.silu(c_ctx)[None]], axis=0) @ ada_w[l] + ada_b[l]
        g2 = mod[:, None, 5 * d:]
        mod = mod[:, None, :]
        pa, pc = _inproj(xs, mod, w_in[l], n_batch=nb, ctx_len=lc)
        pa = seq(pa)
        ya = _s5_mixer(pa[..., :D_S5], lc, s5_lam_re[l], s5_lam_im[l], s5_log_step[l],
                       s5_b_re[l], s5_b_im[l], s5_c_re[l], s5_c_im[l], s5_d[l], s5_glu_w[l], s5_glu_b[l])
        yb = _hgrn2_mixer(pa, lc, lb_all[:, l], hgrn_norm_g[l], col0=D_S5 // D_HG)
        yc = _rwkv7_mixer_fused(seq(pc), lc, rwkv_mu[l], rwkv_w0[l], rwkv_w_up[l], rwkv_a0[l], rwkv_a_up[l],
                                rwkv_g_up[l], rwkv_k_k[l], rwkv_k_a[l], rwkv_r_k[l], rwkv_gn_g[l], rwkv_gn_b[l])
        flat = lambda t: t.reshape(nb * l_all, t.shape[-1])
        xs, h2 = _outproj_ln(flat(ya), flat(yb), flat(yc), w_out[l], xs, mod, ln1_g[l], ln1_b[l],
                             n_batch=nb, ctx_len=lc)
        if last:
            xs, h2 = seq(xs)[:, lc:], seq(h2)[:, lc:].reshape(nb * lx, d)
            gate = g2[:nb]
        else:
            xs = seq(xs)
            gate = jnp.where((jnp.arange(l_all) < lc)[None, :, None], g2[nb:], g2[:nb])
        ffn = _peer_ffn_dense(h2, peer_wq[l], peer_keys[l], peer_u[l].astype(_BF), peer_v[l].T.astype(_BF))
        xs = _layer_norm(DEEPNORM_ALPHA * xs + gate * ffn.reshape(xs.shape), ln2_g[l], ln2_b[l])
        xs = xs.reshape(-1, d)
    return xs.reshape(nb, lx, d)
```

```python
import functools
import math

import jax
import jax.numpy as jnp
from jax import lax
from jax.experimental import pallas as pl
from jax.experimental.pallas import tpu as pltpu

D_MODEL = 1024
DEPTH = 2
GRID_W = 64
D_S5 = D_MODEL // 4
D_HG = D_MODEL // 4
D_RW = D_MODEL // 2
D_MIX = D_S5 + D_HG + D_RW
S5_H = 16
S5_G = D_S5 // S5_H
S5_P = 64
HG_HEAD = 64
HG_CHUNK = 16
RW_HEAD = 64
RW_W_LORA = 64
RW_A_LORA = 64
RW_G_LORA = 128
RW_COLS = 3 * D_RW + RW_G_LORA + 2 * RW_W_LORA + 2 * RW_A_LORA
RW_GN_EPS = 64e-5
PEER_HEADS = 8
PEER_NKEYS = 128
PEER_N = PEER_NKEYS * PEER_NKEYS
PEER_QDIM = 256
PEER_TOPK = 16
LN_EPS = 1e-5
RMS_EPS = 1e-6
DEEPNORM_ALPHA = (2.0 * DEPTH) ** 0.25


def _mm_kernel(x_ref, w_ref, o_ref):
    o_ref[...] = jnp.dot(x_ref[...].astype(jnp.bfloat16), w_ref[...],
                         preferred_element_type=jnp.float32)


def _pick_tile(n, cands):
    for c in cands:
        if n % c == 0:
            return c
    return n


def _matmul(x, w):
    m, k = x.shape
    n = w.shape[1]
    tm = _pick_tile(m, (512, 256, 128, 8))
    tn = _pick_tile(n, (1152, 1024, 512, 256, 128))
    return pl.pallas_call(
        _mm_kernel,
        grid=(m // tm, n // tn),
        in_specs=[pl.BlockSpec((tm, k), lambda i, j: (i, 0)),
                  pl.BlockSpec((k, tn), lambda i, j: (0, j))],
        out_specs=pl.BlockSpec((tm, tn), lambda i, j: (i, j)),
        out_shape=jax.ShapeDtypeStruct((m, n), jnp.float32),
        compiler_params=pltpu.CompilerParams(
            dimension_semantics=("parallel", "parallel"),
            vmem_limit_bytes=48 * 1024 * 1024),
    )(x, w.astype(jnp.bfloat16))


def _mm3(x, w):
    b, l, k = x.shape
    return _matmul(x.reshape(b * l, k), w).reshape(b, l, w.shape[1])


_BF = jnp.bfloat16
_NN = (((1,), (0,)), ((), ()))
_NT = (((1,), (1,)), ((), ()))
_TN = (((0,), (0,)), ((), ()))


def _dot(a, b, dims=_NN):
    return lax.dot_general(a.astype(_BF), b.astype(_BF), dims,
                           preferred_element_type=jnp.float32)


def _split_bf16(a):
    hi = a.astype(_BF)
    return hi, (a - hi.astype(jnp.float32)).astype(_BF)


def _dot3(a, b, dims=_NN):
    a_hi, a_lo = _split_bf16(a)
    b_hi, b_lo = _split_bf16(b)
    d = functools.partial(lax.dot_general, dimension_numbers=dims,
                          preferred_element_type=jnp.float32)
    return d(a_hi, b_hi) + (d(a_hi, b_lo) + d(a_lo, b_hi))


RW_CHUNK = 64


def _scan_block_order(n_ctx, n_all, reverse):
    if not reverse:
        return lambda c: c
    return lambda c: jnp.where(c < n_ctx, n_ctx - 1 - c, n_all - 1 - c + n_ctx)


PEER_TN = 512
PEER_EBLK = 2048
_PEER_SUB = 1024
_PEER_OUT = 1024
_NEG = -3.0e38


def _top_rows(s, n):
    vals = []
    for _ in range(n):
        m = jnp.max(s, axis=0, keepdims=True)
        vals.append(m)
        s = jnp.where(s == m, _NEG, s)
    return vals


def _peer_route_kernel(h_ref, wq_ref, keys_ref, xb_ref, pi_ref, c1_ref, p2_ref):
    hb = h_ref[...].astype(_BF)
    xb_ref[...] = h_ref[...].T.astype(_BF)
    q = jnp.dot(hb, wq_ref[...], preferred_element_type=jnp.float32)
    half = PEER_QDIM // 2
    for h in range(PEER_HEADS):
        s1 = _dot(keys_ref[2 * h], q[:, (2 * h) * half:(2 * h + 1) * half], _NT)
        s2 = _dot(keys_ref[2 * h + 1], q[:, (2 * h + 1) * half:(2 * h + 2) * half], _NT)
        t1 = _top_rows(s1, PEER_TOPK + 1)
        t2 = _top_rows(s2, PEER_TOPK + 1)
        cand = [t1[j1] + t2[j2] for j1 in range(PEER_TOPK + 1) for j2 in range(PEER_TOPK + 1)
                if (j1 + 1) * (j2 + 1) <= PEER_TOPK + 1]
        cand += [jnp.full_like(cand[0], _NEG)] * (-len(cand) % 8)
        top = _top_rows(jnp.concatenate(cand, axis=0), PEER_TOPK + 1)
        theta = 0.5 * (top[PEER_TOPK - 1] + top[PEER_TOPK])
        z = top[0] * 0.0
        for c in top[:PEER_TOPK]:
            z = z + jnp.exp(c - top[0])
        pi_ref[h] = jnp.exp(jnp.maximum(theta - s1 - t2[0], -80.0))
        c1_ref[h] = jnp.exp(s1 - t1[0]) / z
        p2_ref[h] = jnp.exp(s2 - t2[0])


def _peer_route(hf, wq, keys):
    t, d = hf.shape
    tn = PEER_TN
    nq = PEER_HEADS * PEER_QDIM
    aux = jax.ShapeDtypeStruct((PEER_HEADS, PEER_NKEYS, t), jnp.float32)
    aux_spec = pl.BlockSpec((PEER_HEADS, PEER_NKEYS, tn), lambda i: (0, 0, i))
    return pl.pallas_call(
        _peer_route_kernel,
        grid=(t // tn,),
        in_specs=[pl.BlockSpec((tn, d), lambda i: (i, 0)),
                  pl.BlockSpec((d, nq), lambda i: (0, 0)),
                  pl.BlockSpec((PEER_HEADS * 2, PEER_NKEYS, PEER_QDIM // 2), lambda i: (0, 0, 0))],
        out_specs=[pl.BlockSpec((d, tn), lambda i: (0, i))] + [aux_spec] * 3,
        out_shape=[jax.ShapeDtypeStruct((d, t), _BF)] + [aux] * 3,
        compiler_params=pltpu.CompilerParams(dimension_semantics=("parallel",),
                                             vmem_limit_bytes=56 * 1024 * 1024),
        name="peer_route",
    )(hf, wq.astype(_BF), keys.reshape(PEER_HEADS * 2, PEER_NKEYS, PEER_QDIM // 2))


def _gelu_tanh(z):
    c = math.sqrt(2.0 / math.pi)
    hz = 0.5 * z
    return hz + hz * jnp.tanh(z * (c + (c * 0.044715) * (z * z)))


def _peer_expert_kernel(x_ref, u_ref, vt_ref, pi_ref, c1_ref, p2_ref, o_ref, acc_ref, at_ref):
    j = pl.program_id(1)

    @pl.when(j == 0)
    def _():
        acc_ref[...] = jnp.zeros_like(acc_ref)

    nk = PEER_NKEYS
    n_slab = PEER_EBLK // nk
    rows = pl.ds(pl.multiple_of(j * n_slab, n_slab), n_slab)
    sub = _PEER_SUB
    per = sub // nk

    def scores(sb):
        return jnp.dot(u_ref[sb * sub:(sb + 1) * sub, :], x_ref[...],
                       preferred_element_type=jnp.float32)

    zt = scores(0)
    for sb in range(PEER_EBLK // sub):
        zt_next = scores(sb + 1) if (sb + 1) * sub < PEER_EBLK else None
        for k in range(per):
            s = sb * per + k
            for lt in range(PEER_TN // 128):
                lanes = pl.ds(lt * 128, 128)
                g = jnp.zeros((nk, 128), jnp.float32)
                for h in range(PEER_HEADS):
                    pi = pi_ref[h, rows, lanes][s:s + 1]
                    c1 = c1_ref[h, rows, lanes][s:s + 1]
                    p2 = p2_ref[h, :, lanes]
                    g = g + jnp.where(p2 >= pi, p2 * c1, 0.0)
                z = zt[k * nk:(k + 1) * nk, lt * 128:(lt + 1) * 128]
                at_ref[s * nk:(s + 1) * nk, lt * 128:(lt + 1) * 128] = (g * _gelu_tanh(z)).astype(_BF)
        zt = zt_next
        done = (sb + 1) * sub
        if done % _PEER_OUT == 0:
            part = slice(done - _PEER_OUT, done)
            acc_ref[...] += jnp.dot(vt_ref[:, part], at_ref[part, :], preferred_element_type=jnp.float32)

    @pl.when(j == pl.num_programs(1) - 1)
    def _():
        o_ref[...] = acc_ref[...].T


def _peer_expert(xb, u_b, vt_b, aux):
    d, t = xb.shape
    tn, eb = PEER_TN, PEER_EBLK
    aux_spec = pl.BlockSpec((PEER_HEADS, PEER_NKEYS, tn), lambda i, j: (0, 0, i))
    return pl.pallas_call(
        _peer_expert_kernel,
        grid=(t // tn, PEER_N // eb),
        in_specs=[pl.BlockSpec((d, tn), lambda i, j: (0, i)),
                  pl.BlockSpec((eb, d), lambda i, j: (j, 0)),
                  pl.BlockSpec((d, eb), lambda i, j: (0, j))] + [aux_spec] * 3,
        out_specs=pl.BlockSpec((tn, d), lambda i, j: (i, 0)),
        out_shape=jax.ShapeDtypeStruct((t, d), jnp.float32),
        scratch_shapes=[pltpu.VMEM((d, tn), jnp.float32), pltpu.VMEM((eb, tn), _BF)],
        compiler_params=pltpu.CompilerParams(dimension_semantics=("parallel", "arbitrary"),
                                             vmem_limit_bytes=56 * 1024 * 1024),
        name="peer_expert",
    )(xb, u_b, vt_b, *aux)


def _peer_ffn_dense(hf, wq, keys, u_b, vt_b):
    xb, *aux = _peer_route(hf, wq, keys)
    return _peer_expert(xb, u_b, vt_b, aux)


S5_STEPS = 128
_S5_ROWS = 8
_S5_STATE = S5_G * S5_P


def _s5_weights(disc, c_re, c_im):
    eye = jnp.eye(S5_G, dtype=jnp.float32)

    def bdiag_in(bb):
        return jnp.einsum('gph,gk->ghkp', bb, eye).reshape(D_S5, _S5_STATE)

    def bdiag_out(cc):
        return jnp.einsum('ghp,gk->gpkh', cc, eye).reshape(_S5_STATE, D_S5)

    wb = jnp.stack([jnp.concatenate([bdiag_in(disc[d][2]), bdiag_in(disc[d][3])], axis=1) for d in range(2)])
    wc = jnp.stack([jnp.concatenate([bdiag_out(c_re[d]), -bdiag_out(c_im[d])], axis=0) for d in range(2)])
    half = _S5_ROWS // 2
    lam = jnp.stack([
        jnp.concatenate([jnp.broadcast_to(disc[d][i].reshape(1, _S5_STATE), (half, _S5_STATE))
                         for d in range(2)], axis=0) for i in range(2)])
    return wb.astype(_BF), lam, wc.astype(_BF)


def _s5_dir_kernel(uf_ref, ub_ref, wb_ref, lam_ref, wc_ref, yf_ref, yb_ref, buf_ref, bub_ref, xf_ref, xb_ref, st_ref):
    @pl.when(pl.program_id(0) == 0)
    def _():
        st_ref[...] = jnp.zeros_like(st_ref)

    n = _S5_STATE
    buf_ref[...] = jnp.dot(uf_ref[...].astype(_BF), wb_ref[0], preferred_element_type=jnp.float32)
    bub_ref[...] = jnp.dot(ub_ref[...].astype(_BF), wb_ref[1], preferred_element_type=jnp.float32)
    lr = lam_ref[0]
    li = lam_ref[1]
    fwd = lax.broadcasted_iota(jnp.int32, (_S5_ROWS, n), 0) < (_S5_ROWS // 2)

    def step(i, carry):
        xr, xi = carry
        rf = pl.ds(pl.multiple_of(i * _S5_ROWS, _S5_ROWS), _S5_ROWS)
        rb = pl.ds(pl.multiple_of((S5_STEPS - 1 - i) * _S5_ROWS, _S5_ROWS), _S5_ROWS)
        nr = lr * xr - li * xi + jnp.where(fwd, buf_ref[rf, :n], bub_ref[rb, :n])
        ni = lr * xi + li * xr + jnp.where(fwd, buf_ref[rf, n:], bub_ref[rb, n:])
        xf_ref[rf, :n] = nr
        xf_ref[rf, n:] = ni
        xb_ref[rb, :n] = nr
        xb_ref[rb, n:] = ni
        return nr, ni

    xr, xi = lax.fori_loop(0, S5_STEPS, step, (st_ref[0], st_ref[1]), unroll=4)
    st_ref[0] = xr
    st_ref[1] = xi
    yf_ref[...] = jnp.dot(xf_ref[...].astype(_BF), wc_ref[0], preferred_element_type=jnp.float32)
    yb_ref[...] = jnp.dot(xb_ref[...].astype(_BF), wc_ref[1], preferred_element_type=jnp.float32)


def _s5_dir_scan(u8, wb, lam, wc, *, ctx_len):
    rows = u8.shape[0]
    blk = S5_STEPS * _S5_ROWS
    n = _S5_STATE
    order = _scan_block_order(ctx_len // S5_STEPS, rows // blk, True)
    spec_f = pl.BlockSpec((blk, D_S5), lambda i: (i, 0))
    spec_b = pl.BlockSpec((blk, D_S5), lambda i: (order(i), 0))
    out = jax.ShapeDtypeStruct((rows, D_S5), jnp.float32)
    big = pltpu.VMEM((blk, 2 * n), jnp.float32)
    return pl.pallas_call(
        _s5_dir_kernel,
        grid=(rows // blk,),
        in_specs=[spec_f, spec_b,
                  pl.BlockSpec((2, D_S5, 2 * n), lambda i: (0, 0, 0)),
                  pl.BlockSpec((2, _S5_ROWS, n), lambda i: (0, 0, 0)),
                  pl.BlockSpec((2, 2 * n, D_S5), lambda i: (0, 0, 0))],
        out_specs=[spec_f, spec_b],
        out_shape=[out, out],
        scratch_shapes=[big, big, big, big, pltpu.VMEM((2, _S5_ROWS, n), jnp.float32)],
        compiler_params=pltpu.CompilerParams(dimension_semantics=("arbitrary",),
                                             vmem_limit_bytes=56 * 1024 * 1024),
        name="s5_scan",
    )(u8, u8, wb, lam, wc)


HG_BLOCK = 128
HG_BATCH_BLOCK = 4
_HG_PAIR = 2 * HG_HEAD


def _hgrn_dir_kernel(q_ref, v_ref, f_ref, lb_ref, o_ref, st_ref, *, reverse):
    @pl.when(pl.program_id(1) == 0)
    def _():
        st_ref[...] = jnp.zeros_like(st_ref)

    ch = HG_CHUNK
    n_ch = HG_BLOCK // ch
    r = lax.broadcasted_iota(jnp.int32, (HG_BLOCK, HG_BLOCK), 0)
    c = lax.broadcasted_iota(jnp.int32, (HG_BLOCK, HG_BLOCK), 1)
    tri = ((r // ch == c // ch) & ((r <= c) if reverse else (r >= c))).astype(jnp.float32)
    same_head = (lax.broadcasted_iota(jnp.int32, (_HG_PAIR, _HG_PAIR), 0) // HG_HEAD
                 == lax.broadcasted_iota(jnp.int32, (_HG_PAIR, _HG_PAIR), 1) // HG_HEAD)
    ones_blk = same_head.astype(_BF)
    tcol = lax.broadcasted_iota(jnp.int32, (ch, 1), 0)
    end = 0 if reverse else ch - 1
    n_pair = D_HG // _HG_PAIR
    chains = [(bi, p) for bi in range(q_ref.shape[0]) for p in range(n_pair)]
    pair_lanes = lambda p: slice(p * _HG_PAIR, (p + 1) * _HG_PAIR)
    intra, kv, dec, qd = {}, {}, {}, {}
    for bi in range(q_ref.shape[0]):
        q_raw = q_ref[bi]
        q_all = q_raw * _sigmoid(q_raw)
        forget = lb_ref[...] + (1.0 - lb_ref[...]) * _sigmoid(f_ref[bi])
        k_all = 1.0 - forget
        cum_all = _dot3(tri, jnp.log(forget))
        for p in range(n_pair):
            lanes = pair_lanes(p)
            for ci in range(n_ch):
                rows = slice(ci * ch, (ci + 1) * ch)
                q = q_all[rows, lanes]
                k = k_all[rows, lanes]
                v = v_ref[bi, rows, lanes]
                cum = cum_all[rows, lanes]
                last = cum[end:end + 1]
                w = [q * jnp.exp(jnp.minimum(cum - cum[s:s + 1], 0.0)) * k[s:s + 1] for s in range(ch)]
                rel = jnp.dot(jnp.concatenate(w, axis=0).astype(_BF), ones_blk,
                              preferred_element_type=jnp.float32)
                o = jnp.zeros((ch, _HG_PAIR), jnp.float32)
                for s in range(ch):
                    seen = (tcol <= s) if reverse else (tcol >= s)
                    o = o + jnp.where(seen, rel[s * ch:(s + 1) * ch], 0.0) * v[s:s + 1]
                intra[bi, p, ci] = o
                kv[bi, p, ci] = jnp.where(same_head, _dot(v, k * jnp.exp(last - cum), _TN), 0.0)
                dec[bi, p, ci] = jnp.exp(last)
                qd[bi, p, ci] = q * jnp.exp(cum)
    state = [st_ref[i] for i in range(len(chains))]
    for ci in (range(n_ch - 1, -1, -1) if reverse else range(n_ch)):
        for i, (bi, p) in enumerate(chains):
            o_ref[bi, ci * ch:(ci + 1) * ch, pair_lanes(p)] = intra[bi, p, ci] + _dot(qd[bi, p, ci], state[i], _NT)
            state[i] = state[i] * dec[bi, p, ci] + kv[bi, p, ci]
    for i in range(len(chains)):
        st_ref[i] = state[i]


def _hgrn_dir_scan(p, lb, *, ctx_len, reverse, col0=0):
    g, l, _ = p.shape
    d = D_HG
    order = _scan_block_order(ctx_len // HG_BLOCK, l // HG_BLOCK, reverse)
    bb = HG_BATCH_BLOCK if g % HG_BATCH_BLOCK == 0 else 1
    col = lambda c: pl.BlockSpec((bb, HG_BLOCK, d), lambda i, j: (i, order(j), col0 + c))
    dr = 1 if reverse else 0
    return pl.pallas_call(
        functools.partial(_hgrn_dir_kernel, reverse=reverse),
        grid=(g // bb, l // HG_BLOCK),
        in_specs=[col(0), col(1), col(2 + dr), pl.BlockSpec((1, d), lambda i, j: (0, 0))],
        out_specs=pl.BlockSpec((bb, HG_BLOCK, d), lambda i, j: (i, order(j), 0)),
        out_shape=jax.ShapeDtypeStruct((g, l, d), jnp.float32),
        scratch_shapes=[pltpu.VMEM((bb * d // _HG_PAIR, _HG_PAIR, _HG_PAIR), jnp.float32)],
        compiler_params=pltpu.CompilerParams(dimension_semantics=("parallel", "arbitrary")),
        name="hgrn_scan",
    )(p, p, p, lb[dr:dr + 1])


def _hgrn_post_kernel(of_ref, ob_ref, g_ref, ng_ref, ones_ref, y_ref):
    o = of_ref[0] + ob_ref[0]
    ms = jnp.dot((o * o).astype(_BF), ones_ref[...], preferred_element_type=jnp.float32) * (1.0 / HG_HEAD)
    g = g_ref[0]
    y_ref[0] = o * lax.rsqrt(ms + RMS_EPS) * ng_ref[...] * (g * _sigmoid(g))


def _hgrn_post(o_f, o_b, p, norm_g, col0=0):
    nb, l, d = o_f.shape
    spec = pl.BlockSpec((1, HG_BLOCK, d), lambda i, j: (i, j, 0))
    return pl.pallas_call(
        _hgrn_post_kernel,
        grid=(nb, l // HG_BLOCK),
        in_specs=[spec, spec, pl.BlockSpec((1, HG_BLOCK, d), lambda i, j: (i, j, col0 + 4)),
                  pl.BlockSpec((1, d), lambda i, j: (0, 0)), pl.BlockSpec((d, d), lambda i, j: (0, 0))],
        out_specs=spec,
        out_shape=jax.ShapeDtypeStruct((nb, l, d), jnp.float32),
        compiler_params=pltpu.CompilerParams(dimension_semantics=("parallel", "parallel")),
        name="hgrn_post",
    )(o_f, o_b, p, norm_g.reshape(1, d), _head_ones(d, HG_HEAD))


_RW_PAIR = 2 * RW_HEAD
RW_PRE_T = 128
RW_BATCH_BLOCK = 4


def _rwkv_pair_kernel(r_ref, lw_ref, k_ref, v_ref, kk_ref, a_ref, y_ref, st_ref, *, chunk, reverse):
    @pl.when(pl.program_id(1) == 0)
    def _():
        st_ref[...] = jnp.zeros_like(st_ref)

    row = lax.broadcasted_iota(jnp.int32, (chunk, chunk), 0)
    col = lax.broadcasted_iota(jnp.int32, (chunk, chunk), 1)
    strict = row < col if reverse else row > col
    incl = row <= col if reverse else row >= col
    end = 0 if reverse else chunk - 1
    tri = incl.astype(jnp.float32)
    eye = (row == col).astype(jnp.float32)
    blk8 = row // 8 == col // 8
    merge_masks = []
    size = 8
    while size < chunk:
        merge_masks.append((row // (2 * size) == col // (2 * size)) & (row // size != col // size))
        size *= 2
    lane_head = lax.broadcasted_iota(jnp.int32, (1, _RW_PAIR), 1) // RW_HEAD
    head0 = lane_head == 0
    same_head = (lax.broadcasted_iota(jnp.int32, (_RW_PAIR, _RW_PAIR), 0) // RW_HEAD
                 == lax.broadcasted_iota(jnp.int32, (_RW_PAIR, _RW_PAIR), 1) // RW_HEAD)
    n_pair = D_RW // _RW_PAIR
    ps = range(r_ref.shape[0] * n_pair)
    ph = [(p, h) for p in ps for h in range(2)]
    bi = [p // n_pair for p in ps]
    sl = [slice((p % n_pair) * _RW_PAIR, (p % n_pair + 1) * _RW_PAIR) for p in ps]
    lw = [lw_ref[bi[p], :, sl[p]] for p in ps]
    cum = [_dot3(tri, lw[p]) for p in ps]
    pw = [jnp.exp(cum[p]) for p in ps]
    p_inv = [jnp.exp(-cum[p]) for p in ps]
    p_end = [pw[p][end:end + 1, :] for p in ps]
    kk = [kk_ref[bi[p], :, sl[p]] for p in ps]
    bh = [kk[p] * a_ref[bi[p], :, sl[p]] * p_inv[p] for p in ps]
    kh = [k_ref[bi[p], :, sl[p]] * p_inv[p] for p in ps]
    v = [v_ref[bi[p], :, sl[p]] for p in ps]
    s0 = [st_ref[p] for p in ps]
    x1 = [jnp.concatenate([-kk[p] * jnp.exp(cum[p] - lw[p]), r_ref[bi[p], :, sl[p]] * pw[p]], axis=0) for p in ps]
    x1h = {(p, h): jnp.where(lane_head == h, x1[p], 0.0) for p, h in ph}
    g1 = {q: _dot(x1h[q], bh[q[0]], _NT) for q in ph}
    g2 = {q: _dot(x1h[q], kh[q[0]], _NT) for q in ph}
    hm = [_dot(x1[p], s0[p], _NT) for p in ps]
    ab = {q: jnp.where(strict, g1[q][:chunk], 0.0) for q in ph}
    d1 = {q: jnp.where(blk8, ab[q], 0.0) for q in ph}
    d2 = {q: _dot3(d1[q], d1[q]) for q in ph}
    inv = {q: eye + d1[q] for q in ph}
    inv = {q: inv[q] + _dot3(inv[q], d2[q]) for q in ph}
    d4 = {q: _dot3(d2[q], d2[q]) for q in ph}
    inv = {q: inv[q] + _dot3(inv[q], d4[q]) for q in ph}
    for m in merge_masks:
        li = {q: _dot(jnp.where(m, ab[q], 0.0), inv[q]) for q in ph}
        inv = {q: inv[q] + _dot(inv[q], li[q]) for q in ph}
    akv = {q: _dot(jnp.where(strict, g2[q][:chunk], 0.0), v[q[0]]) for q in ph}
    rhs = [hm[p][:chunk] + jnp.where(head0, akv[p, 0], akv[p, 1]) for p in ps]
    eh = {q: _dot(inv[q], rhs[q[0]]) for q in ph}
    e = [jnp.where(head0, eh[p, 0], eh[p, 1]) for p in ps]
    yh = {q: _dot(jnp.where(incl, g1[q][chunk:], 0.0), e[q[0]])
          + _dot(jnp.where(incl, g2[q][chunk:], 0.0), v[q[0]]) for q in ph}
    for p in ps:
        y_ref[bi[p], :, sl[p]] = hm[p][chunk:] + jnp.where(head0, yh[p, 0], yh[p, 1])
    for p in ps:
        ev = jnp.concatenate([e[p], v[p]], axis=0)
        x2 = jnp.concatenate([bh[p] * p_end[p], kh[p] * p_end[p]], axis=0)
        st_ref[p] = s0[p] * p_end[p] + jnp.where(same_head, _dot(ev, x2, _TN), 0.0)


def _rwkv_pair_scan(r, lw, k, v, kk, a, *, ctx_len, reverse):
    nb, l, d = r.shape
    chunk = RW_CHUNK
    order = _scan_block_order(ctx_len // chunk, l // chunk, reverse)
    bb = RW_BATCH_BLOCK if nb % RW_BATCH_BLOCK == 0 else 1
    spec = pl.BlockSpec((bb, chunk, d), lambda b, c: (b, order(c), 0))
    return pl.pallas_call(
        functools.partial(_rwkv_pair_kernel, chunk=chunk, reverse=reverse),
        grid=(nb // bb, l // chunk),
        in_specs=[spec] * 6,
        out_specs=spec,
        out_shape=jax.ShapeDtypeStruct((nb, l, d), jnp.float32),
        scratch_shapes=[pltpu.VMEM((bb * d // _RW_PAIR, _RW_PAIR, _RW_PAIR), jnp.float32)],
        compiler_params=pltpu.CompilerParams(dimension_semantics=("parallel", "arbitrary")),
        name="rwkv_scan",
    )(r, lw, k, v, kk, a)


def _softplus(z):
    return jnp.maximum(z, 0.0) + jnp.log(1.0 + jnp.exp(-jnp.abs(z)))


def _sigmoid(z):
    return 1.0 / (1.0 + jnp.exp(-z))


def _rwkv_pre_kernel(prev_ref, cur_ref, next_ref, mu_ref, vec_ref, gup_ref, wup_ref, aup_ref, ones_ref,
                     r_ref, v_ref, kk_ref, g_ref, bonus_ref, lw0_ref, k0_ref, a0_ref, lw1_ref, k1_ref, a1_ref,
                     ext_ref, s_ref, *, ctx_len, seq_len):
    t_rows, w = RW_PRE_T, GRID_W
    ext_ref[0:w] = prev_ref[0]
    ext_ref[w:w + t_rows] = cur_ref[0]
    ext_ref[w + t_rows:w + t_rows + w] = next_ref[0]
    t = pl.program_id(1) * t_rows + lax.broadcasted_iota(jnp.int32, (t_rows, 128), 0)
    lane4 = lax.broadcasted_iota(jnp.int32, (t_rows, 128), 1) % 4
    is_ctx = t < ctx_len
    tx = t - ctx_len
    col = tx % w
    is_lat = jnp.logical_not(is_ctx)
    use_p1 = (is_ctx & (lane4 % 2 == 0) & (t > 0)) | (is_lat & (lane4 == 0) & (col != 0))
    use_n1 = (is_ctx & (lane4 % 2 == 1) & (t < ctx_len - 1)) | (is_lat & (lane4 == 1) & (col != w - 1))
    use_p64 = is_lat & (lane4 == 2) & (tx >= w)
    use_n64 = is_lat & (lane4 == 3) & (tx < seq_len - w)
    for lt in range(RW_COLS // 128):
        lanes = slice(lt * 128, (lt + 1) * 128)
        p = ext_ref[w:w + t_rows, lanes]
        shifted = jnp.where(use_p1, ext_ref[w - 1:w - 1 + t_rows, lanes],
                            jnp.where(use_n1, ext_ref[w + 1:w + 1 + t_rows, lanes],
                                      jnp.where(use_p64, ext_ref[0:t_rows, lanes],
                                                jnp.where(use_n64, ext_ref[2 * w:2 * w + t_rows, lanes], 0.0))))
        s_ref[:, lanes] = p + mu_ref[:, lanes] * (shifted - p)
    d = D_RW
    r = s_ref[:, 0:d]
    k = s_ref[:, d:2 * d]
    v = s_ref[:, 2 * d:3 * d]
    o = 3 * d
    g_lo = s_ref[:, o:o + RW_G_LORA]
    wl = s_ref[:, o + RW_G_LORA:o + RW_G_LORA + 2 * RW_W_LORA]
    al = s_ref[:, o + RW_G_LORA + 2 * RW_W_LORA:o + RW_G_LORA + 2 * RW_W_LORA + 2 * RW_A_LORA]
    k_k, k_a, r_k = vec_ref[0:1], vec_ref[1:2], vec_ref[2:3]
    ones = ones_ref[...]
    r_ref[0] = r
    v_ref[0] = v
    g_ref[0] = jnp.dot(_sigmoid(g_lo).astype(_BF), gup_ref[...], preferred_element_type=jnp.float32)
    kkr = k * k_k
    ss = jnp.dot((kkr * kkr).astype(_BF), ones, preferred_element_type=jnp.float32)
    kk_ref[0] = kkr / jnp.maximum(jnp.sqrt(ss), 1e-12)
    w_pre = jnp.dot(jnp.tanh(wl).astype(_BF), wup_ref[...], preferred_element_type=jnp.float32)
    a_pre = jnp.dot(al.astype(_BF), aup_ref[...], preferred_element_type=jnp.float32)
    k_sum = 0.0
    for dr, (lw_o, k_o, a_o) in enumerate(((lw0_ref, k0_ref, a0_ref), (lw1_ref, k1_ref, a1_ref))):
        w_log = -_softplus(-(vec_ref[3 + dr:4 + dr] + w_pre[:, dr * d:(dr + 1) * d])) - 0.5
        a = _sigmoid(vec_ref[5 + dr:6 + dr] + a_pre[:, dr * d:(dr + 1) * d])
        k_d = k * (1.0 + (a - 1.0) * k_a)
        lw_o[0] = -jnp.exp(w_log)
        k_o[0] = k_d
        a_o[0] = a
        k_sum = k_sum + k_d
    bonus_ref[0] = jnp.dot((r * k_sum * r_k).astype(_BF), ones, preferred_element_type=jnp.float32)


def _head_ones(width, head):
    i = jnp.arange(width) // head
    return (i[:, None] == i[None, :]).astype(_BF)


def _rwkv_pre(p, mu, k_k, k_a, r_k, w0, a0, g_up, w_up, a_up, *, ctx_len):
    nb, l, cols = p.shape
    t_rows, w, d = RW_PRE_T, GRID_W, D_RW
    nblk = l // w
    per = t_rows // w
    zeros = jnp.zeros((RW_W_LORA, d), jnp.float32)
    wup = jnp.concatenate([jnp.concatenate([w_up[0], zeros], axis=1),
                           jnp.concatenate([zeros, w_up[1]], axis=1)], axis=0).astype(_BF)
    aup = jnp.concatenate([jnp.concatenate([a_up[0], zeros], axis=1),
                           jnp.concatenate([zeros, a_up[1]], axis=1)], axis=0).astype(_BF)
    vec = jnp.stack([k_k, k_a, r_k.reshape(d), w0[0], w0[1], a0[0], a0[1], jnp.zeros((d,), jnp.float32)])
    out = jax.ShapeDtypeStruct((nb, l, d), jnp.float32)
    out_spec = pl.BlockSpec((1, t_rows, d), lambda b, j: (b, j, 0))
    full = lambda shape: pl.BlockSpec(shape, lambda b, j: (0,) * len(shape))
    return pl.pallas_call(
        functools.partial(_rwkv_pre_kernel, ctx_len=ctx_len, seq_len=l - ctx_len),
        grid=(nb, l // t_rows),
        in_specs=[pl.BlockSpec((1, w, cols), lambda b, j: (b, jnp.maximum(j * per - 1, 0), 0)),
                  pl.BlockSpec((1, t_rows, cols), lambda b, j: (b, j, 0)),
                  pl.BlockSpec((1, w, cols), lambda b, j: (b, jnp.minimum(j * per + per, nblk - 1), 0)),
                  full((1, cols)), full((8, d)), full((RW_G_LORA, d)),
                  full((2 * RW_W_LORA, 2 * d)), full((2 * RW_A_LORA, 2 * d)), full((d, d))],
        out_specs=[out_spec] * 11,
        out_shape=[out] * 11,
        scratch_shapes=[pltpu.VMEM((t_rows + 2 * w, cols), jnp.float32), pltpu.VMEM((t_rows, cols), jnp.float32)],
        compiler_params=pltpu.CompilerParams(dimension_semantics=("parallel", "parallel"),
                                             vmem_limit_bytes=48 * 1024 * 1024),
        name="rwkv_pre",
    )(p, p, p, mu.reshape(1, cols), vec, g_up.astype(_BF), wup, aup, _head_ones(d, RW_HEAD))


def _rwkv_post_kernel(y0_ref, y1_ref, v_ref, g_ref, bonus_ref, gn_ref, ones_ref, o_ref):
    y = y0_ref[0] + y1_ref[0]
    ones = ones_ref[...]
    inv_n = 1.0 / RW_HEAD
    mu_y = jnp.dot(y.astype(_BF), ones, preferred_element_type=jnp.float32) * inv_n
    yc = y - mu_y
    var_y = jnp.dot((yc * yc).astype(_BF), ones, preferred_element_type=jnp.float32) * inv_n
    yn = yc * lax.rsqrt(var_y + RW_GN_EPS) * gn_ref[0:1] + gn_ref[1:2]
    o_ref[0] = (yn + bonus_ref[0] * v_ref[0]) * g_ref[0]


def _rwkv_post(y0, y1, v, g, bonus, gn_g, gn_b):
    nb, l, d = y0.shape
    spec = pl.BlockSpec((1, RW_PRE_T, d), lambda b, j: (b, j, 0))
    return pl.pallas_call(
        _rwkv_post_kernel,
        grid=(nb, l // RW_PRE_T),
        in_specs=[spec] * 5 + [pl.BlockSpec((2, d), lambda b, j: (0, 0)), pl.BlockSpec((d, d), lambda b, j: (0, 0))],
        out_specs=spec,
        out_shape=jax.ShapeDtypeStruct((nb, l, d), jnp.float32),
        compiler_params=pltpu.CompilerParams(dimension_semantics=("parallel", "parallel")),
        name="rwkv_post",
    )(y0, y1, v, g, bonus, jnp.stack([gn_g, gn_b]), _head_ones(d, RW_HEAD))


def _rwkv7_mixer_fused(p, lc, mu, w0, w_up, a0, a_up, g_up, k_k, k_a, r_k, gn_g, gn_b):
    r, v, kk, g, bonus, lw0, k0, a0_, lw1, k1, a1_ = _rwkv_pre(
        p, _f32(mu), _f32(k_k), _f32(k_a), _f32(r_k), _f32(w0), _f32(a0), _f32(g_up), _f32(w_up), _f32(a_up),
        ctx_len=lc)
    y0 = _rwkv_pair_scan(r, lw0, k0, v, kk, a0_, ctx_len=lc, reverse=False)
    y1 = _rwkv_pair_scan(r, lw1, k1, v, kk, a1_, ctx_len=lc, reverse=True)
    return _rwkv_post(y0, y1, v, g, bonus, _f32(gn_g), _f32(gn_b))


def _f32(t):
    return t.astype(jnp.float32)


def _layer_norm(x, g, b):
    xf = _f32(x)
    mu = jnp.mean(xf, -1, keepdims=True)
    var = jnp.mean(jnp.square(xf - mu), -1, keepdims=True)
    return ((xf - mu) * lax.rsqrt(var + LN_EPS) * _f32(g) + _f32(b)).astype(x.dtype)


def _s5_discretise(lam_re, lam_im, log_step, b_re, b_im):
    dt = jnp.exp(log_step)[:, None]
    mag = jnp.exp(lam_re * dt)
    ang = lam_im * dt
    lb_re, lb_im = mag * jnp.cos(ang), mag * jnp.sin(ang)
    den = lam_re * lam_re + lam_im * lam_im
    nr, ni = lb_re - 1.0, lb_im
    co_re = (nr * lam_re + ni * lam_im) / den
    co_im = (ni * lam_re - nr * lam_im) / den
    bb_re = co_re[..., None] * b_re - co_im[..., None] * b_im
    bb_im = co_re[..., None] * b_im + co_im[..., None] * b_re
    return lb_re, lb_im, bb_re, bb_im


def _s5_mixer(u, lc, lam_re, lam_im, log_step, b_re, b_im, c_re, c_im, d_skip, glu_w, glu_b):
    disc = [_s5_discretise(_f32(lam_re[d]), _f32(lam_im[d]), _f32(log_step[d]),
                           _f32(b_re[d]), _f32(b_im[d])) for d in range(2)]
    wb, lam, wc = _s5_weights(disc, _f32(c_re), _f32(c_im))
    u = _f32(u)
    nb, l, _ = u.shape
    assert 2 * nb == _S5_ROWS
    u_t = u.transpose(1, 0, 2)
    u8 = jnp.concatenate([u_t, u_t], axis=1).reshape(l * _S5_ROWS, D_S5)
    y_f, y_b = _s5_dir_scan(u8, wb, lam, wc, ctx_len=lc)
    y = (y_f.reshape(l, _S5_ROWS, D_S5)[:, :nb] + y_b.reshape(l, _S5_ROWS, D_S5)[:, nb:]).transpose(1, 0, 2)
    y = jax.nn.gelu(y + _f32(d_skip) * u)
    return y * jax.nn.sigmoid(_mm3(y, glu_w) + _f32(glu_b))


def _hgrn2_mixer(p, lc, lb, norm_g, col0=0):
    lb = _f32(lb)
    o_f = _hgrn_dir_scan(p, lb, ctx_len=lc, reverse=False, col0=col0)
    o_b = _hgrn_dir_scan(p, lb, ctx_len=lc, reverse=True, col0=col0)
    return _hgrn_post(o_f, o_b, p, _f32(norm_g), col0=col0)


ROW_TILE = 256


def _mod_spec(n_batch, tiles_per_batch, ctx_tiles):
    def index(i):
        return (jnp.where(i % tiles_per_batch < ctx_tiles, n_batch, i // tiles_per_batch), 0, 0)
    return pl.BlockSpec((1, 1, 6 * D_MODEL), index)


def _inproj_kernel(x_ref, mod_ref, w_ref, pa_ref, pc_ref):
    d = D_MODEL
    m = mod_ref[0]
    xm = (x_ref[...] * (1.0 + m[:, d:2 * d]) + m[:, 0:d]).astype(_BF)
    split = pa_ref.shape[1]
    pa_ref[...] = jnp.dot(xm, w_ref[:, :split], preferred_element_type=jnp.float32)
    pc_ref[...] = jnp.dot(xm, w_ref[:, split:], preferred_element_type=jnp.float32)


def _inproj(xs, mod, w, *, n_batch, ctx_len):
    t, d = xs.shape
    n_a = D_S5 + 5 * D_HG
    n_c = w.shape[1] - n_a
    tiles = t // ROW_TILE
    return pl.pallas_call(
        _inproj_kernel,
        grid=(tiles,),
        in_specs=[pl.BlockSpec((ROW_TILE, d), lambda i: (i, 0)),
                  _mod_spec(n_batch, tiles // n_batch, ctx_len // ROW_TILE),
                  pl.BlockSpec((d, n_a + n_c), lambda i: (0, 0))],
        out_specs=[pl.BlockSpec((ROW_TILE, n_a), lambda i: (i, 0)), pl.BlockSpec((ROW_TILE, n_c), lambda i: (i, 0))],
        out_shape=[jax.ShapeDtypeStruct((t, n_a), jnp.float32), jax.ShapeDtypeStruct((t, n_c), jnp.float32)],
        compiler_params=pltpu.CompilerParams(dimension_semantics=("parallel",),
                                             vmem_limit_bytes=48 * 1024 * 1024),
        name="in_proj",
    )(xs, mod, w.astype(_BF))


def _ln_rows(h, g, b):
    mu = jnp.mean(h, axis=-1, keepdims=True)
    hc = h - mu
    var = jnp.mean(hc * hc, axis=-1, keepdims=True)
    return hc * lax.rsqrt(var + LN_EPS) * g + b


def _outproj_ln_kernel(ya_ref, yb_ref, yc_ref, w_ref, xs_ref, mod_ref, ln_ref, x1_ref, h2_ref):
    d = D_MODEL
    mix = (jnp.dot(ya_ref[...].astype(_BF), w_ref[0:D_S5], preferred_element_type=jnp.float32)
           + jnp.dot(yb_ref[...].astype(_BF), w_ref[D_S5:D_S5 + D_HG], preferred_element_type=jnp.float32)
           + jnp.dot(yc_ref[...].astype(_BF), w_ref[D_S5 + D_HG:D_MIX], preferred_element_type=jnp.float32))
    m = mod_ref[0]
    x1 = _ln_rows(DEEPNORM_ALPHA * xs_ref[...] + m[:, 2 * d:3 * d] * mix, ln_ref[0:1], ln_ref[1:2])
    x1_ref[...] = x1
    h2_ref[...] = x1 * (1.0 + m[:, 4 * d:5 * d]) + m[:, 3 * d:4 * d]


def _outproj_ln(ya, yb, yc, w, xs, mod, ln_g, ln_b, *, n_batch, ctx_len):
    t, d = xs.shape
    tiles = t // ROW_TILE
    row = lambda n: pl.BlockSpec((ROW_TILE, n), lambda i: (i, 0))
    out = jax.ShapeDtypeStruct((t, d), jnp.float32)
    return pl.pallas_call(
        _outproj_ln_kernel,
        grid=(tiles,),
        in_specs=[row(D_S5), row(D_HG), row(D_RW), pl.BlockSpec((D_MIX, d), lambda i: (0, 0)), row(d),
                  _mod_spec(n_batch, tiles // n_batch, ctx_len // ROW_TILE),
                  pl.BlockSpec((2, d), lambda i: (0, 0))],
        out_specs=[row(d), row(d)],
        out_shape=[out, out],
        compiler_params=pltpu.CompilerParams(dimension_semantics=("parallel",)),
        name="out_proj_ln",
    )(ya, yb, yc, w.astype(_BF), xs, mod, jnp.stack([ln_g, ln_b]))


def kernel(x, c, ctx, c_ctx, ada_w, ada_b, w_in, w_out, ln1_g, ln1_b, ln2_g, ln2_b,
           s5_lam_re, s5_lam_im, s5_log_step, s5_b_re, s5_b_im, s5_c_re, s5_c_im,
           s5_d, s5_glu_w, s5_glu_b, hgrn_lb_logits, hgrn_norm_g,
           rwkv_mu, rwkv_w0, rwkv_w_up, rwkv_a0, rwkv_a_up, rwkv_g_up, rwkv_k_k, rwkv_k_a,
           rwkv_r_k, rwkv_gn_g, rwkv_gn_b, peer_wq, peer_keys, peer_u, peer_v):
    nb, lx, d = x.shape
    lc = ctx.shape[1]
    l_all = lc + lx
    assert lx % GRID_W == 0 and lc % ROW_TILE == 0 and lx % ROW_TILE == 0 and (nb * l_all) % PEER_TN == 0
    lb_cum = jnp.cumsum(jax.nn.softmax(_f32(hgrn_lb_logits), axis=1), axis=1)
    lb_all = lb_cum - lb_cum[:, :1]
    xs = jnp.concatenate([ctx, x], axis=1).reshape(nb * l_all, d)
    seq = lambda t: t.reshape(nb, l_all, t.shape[-1])
    for l in range(DEPTH):
        last = l == DEPTH - 1
        mod = jnp.concatenate([jax.nn.silu(c), jax.nn.silu(c_ctx)[None]], axis=0) @ ada_w[l] + ada_b[l]
        g2 = mod[:, None, 5 * d:]
        mod = mod[:, None, :]
        pa, pc = _inproj(xs, mod, w_in[l], n_batch=nb, ctx_len=lc)
        pa = seq(pa)
        ya = _s5_mixer(pa[..., :D_S5], lc, s5_lam_re[l], s5_lam_im[l], s5_log_step[l],
                       s5_b_re[l], s5_b_im[l], s5_c_re[l], s5_c_im[l], s5_d[l], s5_glu_w[l], s5_glu_b[l])
        yb = _hgrn2_mixer(pa, lc, lb_all[:, l], hgrn_norm_g[l], col0=D_S5 // D_HG)
        yc = _rwkv7_mixer_fused(seq(pc), lc, rwkv_mu[l], rwkv_w0[l], rwkv_w_up[l], rwkv_a0[l], rwkv_a_up[l],
                                rwkv_g_up[l], rwkv_k_k[l], rwkv_k_a[l], rwkv_r_k[l], rwkv_gn_g[l], rwkv_gn_b[l])
        flat = lambda t: t.reshape(nb * l_all, t.shape[-1])
        xs, h2 = _outproj_ln(flat(ya), flat(yb), flat(yc), w_out[l], xs, mod, ln1_g[l], ln1_b[l],
                             n_batch=nb, ctx_len=lc)
        if last:
            xs, h2 = seq(xs)[:, lc:], seq(h2)[:, lc:].reshape(nb * lx, d)
            gate = g2[:nb]
        else:
            xs = seq(xs)
            gate = jnp.where((jnp.arange(l_all) < lc)[None, :, None], g2[nb:], g2[:nb])
        ffn = _peer_ffn_dense(h2, peer_wq[l], peer_keys[l], peer_u[l].astype(_BF), peer_v[l].T.astype(_BF))
        xs = _layer_norm(DEEPNORM_ALPHA * xs + gate * ffn.reshape(xs.shape), ln2_g[l], ln2_b[l])
        xs = xs.reshape(-1, d)
    return xs.reshape(nb, lx, d)
```

```python
import functools
import math

import jax
import jax.numpy as jnp
from jax import lax
from jax.experimental import pallas as pl
from jax.experimental.pallas import tpu as pltpu

D_MODEL = 1024
DEPTH = 2
GRID_W = 64
D_S5 = D_MODEL // 4
D_HG = D_MODEL // 4
D_RW = D_MODEL // 2
D_MIX = D_S5 + D_HG + D_RW
S5_H = 16
S5_G = D_S5 // S5_H
S5_P = 64
HG_HEAD = 64
HG_CHUNK = 16
RW_HEAD = 64
RW_W_LORA = 64
RW_A_LORA = 64
RW_G_LORA = 128
RW_COLS = 3 * D_RW + RW_G_LORA + 2 * RW_W_LORA + 2 * RW_A_LORA
RW_GN_EPS = 64e-5
PEER_HEADS = 8
PEER_NKEYS = 128
PEER_N = PEER_NKEYS * PEER_NKEYS
PEER_QDIM = 256
PEER_TOPK = 16
LN_EPS = 1e-5
RMS_EPS = 1e-6
DEEPNORM_ALPHA = (2.0 * DEPTH) ** 0.25


def _mm_kernel(x_ref, w_ref, o_ref):
    o_ref[...] = jnp.dot(x_ref[...].astype(jnp.bfloat16), w_ref[...],
                         preferred_element_type=jnp.float32)


def _pick_tile(n, cands):
    for c in cands:
        if n % c == 0:
            return c
    return n


def _matmul(x, w):
    m, k = x.shape
    n = w.shape[1]
    tm = _pick_tile(m, (512, 256, 128, 8))
    tn = _pick_tile(n, (1152, 1024, 512, 256, 128))
    return pl.pallas_call(
        _mm_kernel,
        grid=(m // tm, n // tn),
        in_specs=[pl.BlockSpec((tm, k), lambda i, j: (i, 0)),
                  pl.BlockSpec((k, tn), lambda i, j: (0, j))],
        out_specs=pl.BlockSpec((tm, tn), lambda i, j: (i, j)),
        out_shape=jax.ShapeDtypeStruct((m, n), jnp.float32),
        compiler_params=pltpu.CompilerParams(
            dimension_semantics=("parallel", "parallel"),
            vmem_limit_bytes=48 * 1024 * 1024),
    )(x, w.astype(jnp.bfloat16))


def _mm3(x, w):
    b, l, k = x.shape
    return _matmul(x.reshape(b * l, k), w).reshape(b, l, w.shape[1])


_BF = jnp.bfloat16
_NN = (((1,), (0,)), ((), ()))
_NT = (((1,), (1,)), ((), ()))
_TN = (((0,), (0,)), ((), ()))


def _dot(a, b, dims=_NN):
    return lax.dot_general(a.astype(_BF), b.astype(_BF), dims,
                           preferred_element_type=jnp.float32)


def _split_bf16(a):
    hi = a.astype(_BF)
    return hi, (a - hi.astype(jnp.float32)).astype(_BF)


def _dot3(a, b, dims=_NN):
    a_hi, a_lo = _split_bf16(a)
    b_hi, b_lo = _split_bf16(b)
    d = functools.partial(lax.dot_general, dimension_numbers=dims,
                          preferred_element_type=jnp.float32)
    return d(a_hi, b_hi) + (d(a_hi, b_lo) + d(a_lo, b_hi))


RW_CHUNK = 64


def _scan_block_order(n_ctx, n_all, reverse):
    if not reverse:
        return lambda c: c
    return lambda c: jnp.where(c < n_ctx, n_ctx - 1 - c, n_all - 1 - c + n_ctx)


PEER_TN = 512
PEER_EBLK = 2048
_PEER_SUB = 2048
_PEER_OUT = 2048
assert _PEER_OUT % _PEER_SUB == 0 and PEER_EBLK % _PEER_OUT == 0
_NEG = -3.0e38


def _top_rows(s, n):
    vals = []
    for _ in range(n):
        m = jnp.max(s, axis=0, keepdims=True)
        vals.append(m)
        s = jnp.where(s == m, _NEG, s)
    return vals


def _peer_route_kernel(h_ref, wq_ref, keys_ref, xb_ref, pi_ref, c1_ref, p2_ref):
    hb = h_ref[...].astype(_BF)
    xb_ref[...] = h_ref[...].T.astype(_BF)
    q = jnp.dot(hb, wq_ref[...], preferred_element_type=jnp.float32)
    half = PEER_QDIM // 2
    for h in range(PEER_HEADS):
        s1 = _dot(keys_ref[2 * h], q[:, (2 * h) * half:(2 * h + 1) * half], _NT)
        s2 = _dot(keys_ref[2 * h + 1], q[:, (2 * h + 1) * half:(2 * h + 2) * half], _NT)
        t1 = _top_rows(s1, PEER_TOPK + 1)
        t2 = _top_rows(s2, PEER_TOPK + 1)
        cand = [t1[j1] + t2[j2] for j1 in range(PEER_TOPK + 1) for j2 in range(PEER_TOPK + 1)
                if (j1 + 1) * (j2 + 1) <= PEER_TOPK + 1]
        cand += [jnp.full_like(cand[0], _NEG)] * (-len(cand) % 8)
        top = _top_rows(jnp.concatenate(cand, axis=0), PEER_TOPK + 1)
        theta = 0.5 * (top[PEER_TOPK - 1] + top[PEER_TOPK])
        z = top[0] * 0.0
        for c in top[:PEER_TOPK]:
            z = z + jnp.exp(c - top[0])
        pi_ref[h] = jnp.exp(jnp.maximum(theta - s1 - t2[0], -80.0))
        c1_ref[h] = jnp.exp(s1 - t1[0]) / z
        p2_ref[h] = jnp.exp(s2 - t2[0])


def _peer_route(hf, wq, keys):
    t, d = hf.shape
    tn = PEER_TN
    nq = PEER_HEADS * PEER_QDIM
    aux = jax.ShapeDtypeStruct((PEER_HEADS, PEER_NKEYS, t), jnp.float32)
    aux_spec = pl.BlockSpec((PEER_HEADS, PEER_NKEYS, tn), lambda i: (0, 0, i))
    return pl.pallas_call(
        _peer_route_kernel,
        grid=(t // tn,),
        in_specs=[pl.BlockSpec((tn, d), lambda i: (i, 0)),
                  pl.BlockSpec((d, nq), lambda i: (0, 0)),
                  pl.BlockSpec((PEER_HEADS * 2, PEER_NKEYS, PEER_QDIM // 2), lambda i: (0, 0, 0))],
        out_specs=[pl.BlockSpec((d, tn), lambda i: (0, i))] + [aux_spec] * 3,
        out_shape=[jax.ShapeDtypeStruct((d, t), _BF)] + [aux] * 3,
        compiler_params=pltpu.CompilerParams(dimension_semantics=("parallel",),
                                             vmem_limit_bytes=56 * 1024 * 1024),
        name="peer_route",
    )(hf, wq.astype(_BF), keys.reshape(PEER_HEADS * 2, PEER_NKEYS, PEER_QDIM // 2))


def _gelu_tanh(z):
    c = math.sqrt(2.0 / math.pi)
    hz = 0.5 * z
    return hz + hz * jnp.tanh(z * (c + (c * 0.044715) * (z * z)))


def _peer_expert_kernel(x_ref, u_ref, vt_ref, pi_ref, c1_ref, p2_ref, o_ref, acc_ref, at_ref):
    j = pl.program_id(1)

    @pl.when(j == 0)
    def _():
        acc_ref[...] = jnp.zeros_like(acc_ref)

    nk = PEER_NKEYS
    n_slab = PEER_EBLK // nk
    rows = pl.ds(pl.multiple_of(j * n_slab, n_slab), n_slab)
    sub = _PEER_SUB
    per = sub // nk

    def scores(sb):
        return jnp.dot(u_ref[sb * sub:(sb + 1) * sub, :], x_ref[...],
                       preferred_element_type=jnp.float32)

    zt = scores(0)
    for sb in range(PEER_EBLK // sub):
        zt_next = scores(sb + 1) if (sb + 1) * sub < PEER_EBLK else None
        for k in range(per):
            s = sb * per + k
            for lt in range(PEER_TN // 128):
                lanes = pl.ds(lt * 128, 128)
                g = jnp.zeros((nk, 128), jnp.float32)
                for h in range(PEER_HEADS):
                    pi = pi_ref[h, rows, lanes][s:s + 1]
                    c1 = c1_ref[h, rows, lanes][s:s + 1]
                    p2 = p2_ref[h, :, lanes]
                    g = g + jnp.where(p2 >= pi, p2 * c1, 0.0)
                z = zt[k * nk:(k + 1) * nk, lt * 128:(lt + 1) * 128]
                at_ref[s * nk:(s + 1) * nk, lt * 128:(lt + 1) * 128] = (g * _gelu_tanh(z)).astype(_BF)
        zt = zt_next
        done = (sb + 1) * sub
        if done % _PEER_OUT == 0:
            part = slice(done - _PEER_OUT, done)
            acc_ref[...] += jnp.dot(vt_ref[:, part], at_ref[part, :], preferred_element_type=jnp.float32)

    @pl.when(j == pl.num_programs(1) - 1)
    def _():
        o_ref[...] = acc_ref[...].T


def _peer_expert(xb, u_b, vt_b, aux):
    d, t = xb.shape
    tn, eb = PEER_TN, PEER_EBLK
    aux_spec = pl.BlockSpec((PEER_HEADS, PEER_NKEYS, tn), lambda i, j: (0, 0, i))
    return pl.pallas_call(
        _peer_expert_kernel,
        grid=(t // tn, PEER_N // eb),
        in_specs=[pl.BlockSpec((d, tn), lambda i, j: (0, i)),
                  pl.BlockSpec((eb, d), lambda i, j: (j, 0)),
                  pl.BlockSpec((d, eb), lambda i, j: (0, j))] + [aux_spec] * 3,
        out_specs=pl.BlockSpec((tn, d), lambda i, j: (i, 0)),
        out_shape=jax.ShapeDtypeStruct((t, d), jnp.float32),
        scratch_shapes=[pltpu.VMEM((d, tn), jnp.float32), pltpu.VMEM((eb, tn), _BF)],
        compiler_params=pltpu.CompilerParams(dimension_semantics=("parallel", "arbitrary"),
                                             vmem_limit_bytes=56 * 1024 * 1024),
        name="peer_expert",
    )(xb, u_b, vt_b, *aux)


def _peer_ffn_dense(hf, wq, keys, u_b, vt_b):
    xb, *aux = _peer_route(hf, wq, keys)
    return _peer_expert(xb, u_b, vt_b, aux)


S5_STEPS = 128
_S5_ROWS = 8
_S5_STATE = S5_G * S5_P


def _s5_weights(disc, c_re, c_im):
    eye = jnp.eye(S5_G, dtype=jnp.float32)

    def bdiag_in(bb):
        return jnp.einsum('gph,gk->ghkp', bb, eye).reshape(D_S5, _S5_STATE)

    def bdiag_out(cc):
        return jnp.einsum('ghp,gk->gpkh', cc, eye).reshape(_S5_STATE, D_S5)

    wb = jnp.stack([jnp.concatenate([bdiag_in(disc[d][2]), bdiag_in(disc[d][3])], axis=1) for d in range(2)])
    wc = jnp.stack([jnp.concatenate([bdiag_out(c_re[d]), -bdiag_out(c_im[d])], axis=0) for d in range(2)])
    half = _S5_ROWS // 2
    lam = jnp.stack([
        jnp.concatenate([jnp.broadcast_to(disc[d][i].reshape(1, _S5_STATE), (half, _S5_STATE))
                         for d in range(2)], axis=0) for i in range(2)])
    return wb.astype(_BF), lam, wc.astype(_BF)


def _s5_dir_kernel(uf_ref, ub_ref, wb_ref, lam_ref, wc_ref, yf_ref, yb_ref, buf_ref, bub_ref, xf_ref, xb_ref, st_ref):
    @pl.when(pl.program_id(0) == 0)
    def _():
        st_ref[...] = jnp.zeros_like(st_ref)

    n = _S5_STATE
    buf_ref[...] = jnp.dot(uf_ref[...].astype(_BF), wb_ref[0], preferred_element_type=jnp.float32)
    bub_ref[...] = jnp.dot(ub_ref[...].astype(_BF), wb_ref[1], preferred_element_type=jnp.float32)
    lr = lam_ref[0]
    li = lam_ref[1]
    fwd = lax.broadcasted_iota(jnp.int32, (_S5_ROWS, n), 0) < (_S5_ROWS // 2)

    def step(i, carry):
        xr, xi = carry
        rf = pl.ds(pl.multiple_of(i * _S5_ROWS, _S5_ROWS), _S5_ROWS)
        rb = pl.ds(pl.multiple_of((S5_STEPS - 1 - i) * _S5_ROWS, _S5_ROWS), _S5_ROWS)
        nr = lr * xr - li * xi + jnp.where(fwd, buf_ref[rf, :n], bub_ref[rb, :n])
        ni = lr * xi + li * xr + jnp.where(fwd, buf_ref[rf, n:], bub_ref[rb, n:])
        xf_ref[rf, :n] = nr
        xf_ref[rf, n:] = ni
        xb_ref[rb, :n] = nr
        xb_ref[rb, n:] = ni
        return nr, ni

    xr, xi = lax.fori_loop(0, S5_STEPS, step, (st_ref[0], st_ref[1]), unroll=4)
    st_ref[0] = xr
    st_ref[1] = xi
    yf_ref[...] = jnp.dot(xf_ref[...].astype(_BF), wc_ref[0], preferred_element_type=jnp.float32)
    yb_ref[...] = jnp.dot(xb_ref[...].astype(_BF), wc_ref[1], preferred_element_type=jnp.float32)


def _s5_dir_scan(u8, wb, lam, wc, *, ctx_len):
    rows = u8.shape[0]
    blk = S5_STEPS * _S5_ROWS
    n = _S5_STATE
    order = _scan_block_order(ctx_len // S5_STEPS, rows // blk, True)
    spec_f = pl.BlockSpec((blk, D_S5), lambda i: (i, 0))
    spec_b = pl.BlockSpec((blk, D_S5), lambda i: (order(i), 0))
    out = jax.ShapeDtypeStruct((rows, D_S5), jnp.float32)
    big = pltpu.VMEM((blk, 2 * n), jnp.float32)
    return pl.pallas_call(
        _s5_dir_kernel,
        grid=(rows // blk,),
        in_specs=[spec_f, spec_b,
                  pl.BlockSpec((2, D_S5, 2 * n), lambda i: (0, 0, 0)),
                  pl.BlockSpec((2, _S5_ROWS, n), lambda i: (0, 0, 0)),
                  pl.BlockSpec((2, 2 * n, D_S5), lambda i: (0, 0, 0))],
        out_specs=[spec_f, spec_b],
        out_shape=[out, out],
        scratch_shapes=[big, big, big, big, pltpu.VMEM((2, _S5_ROWS, n), jnp.float32)],
        compiler_params=pltpu.CompilerParams(dimension_semantics=("arbitrary",),
                                             vmem_limit_bytes=56 * 1024 * 1024),
        name="s5_scan",
    )(u8, u8, wb, lam, wc)


HG_BLOCK = 128
HG_BATCH_BLOCK = 4
_HG_PAIR = 2 * HG_HEAD


def _hgrn_dir_kernel(q_ref, v_ref, f_ref, lb_ref, o_ref, st_ref, *, reverse):
    @pl.when(pl.program_id(1) == 0)
    def _():
        st_ref[...] = jnp.zeros_like(st_ref)

    ch = HG_CHUNK
    n_ch = HG_BLOCK // ch
    r = lax.broadcasted_iota(jnp.int32, (HG_BLOCK, HG_BLOCK), 0)
    c = lax.broadcasted_iota(jnp.int32, (HG_BLOCK, HG_BLOCK), 1)
    tri = ((r // ch == c // ch) & ((r <= c) if reverse else (r >= c))).astype(jnp.float32)
    same_head = (lax.broadcasted_iota(jnp.int32, (_HG_PAIR, _HG_PAIR), 0) // HG_HEAD
                 == lax.broadcasted_iota(jnp.int32, (_HG_PAIR, _HG_PAIR), 1) // HG_HEAD)
    ones_blk = same_head.astype(_BF)
    tcol = lax.broadcasted_iota(jnp.int32, (ch, 1), 0)
    end = 0 if reverse else ch - 1
    n_pair = D_HG // _HG_PAIR
    chains = [(bi, p) for bi in range(q_ref.shape[0]) for p in range(n_pair)]
    pair_lanes = lambda p: slice(p * _HG_PAIR, (p + 1) * _HG_PAIR)
    intra, kv, dec, qd = {}, {}, {}, {}
    for bi in range(q_ref.shape[0]):
        q_raw = q_ref[bi]
        q_all = q_raw * _sigmoid(q_raw)
        forget = lb_ref[...] + (1.0 - lb_ref[...]) * _sigmoid(f_ref[bi])
        k_all = 1.0 - forget
        cum_all = _dot3(tri, jnp.log(forget))
        for p in range(n_pair):
            lanes = pair_lanes(p)
            for ci in range(n_ch):
                rows = slice(ci * ch, (ci + 1) * ch)
                q = q_all[rows, lanes]
                k = k_all[rows, lanes]
                v = v_ref[bi, rows, lanes]
                cum = cum_all[rows, lanes]
                last = cum[end:end + 1]
                w = [q * jnp.exp(jnp.minimum(cum - cum[s:s + 1], 0.0)) * k[s:s + 1] for s in range(ch)]
                rel = jnp.dot(jnp.concatenate(w, axis=0).astype(_BF), ones_blk,
                              preferred_element_type=jnp.float32)
                o = jnp.zeros((ch, _HG_PAIR), jnp.float32)
                for s in range(ch):
                    seen = (tcol <= s) if reverse else (tcol >= s)
                    o = o + jnp.where(seen, rel[s * ch:(s + 1) * ch], 0.0) * v[s:s + 1]
                intra[bi, p, ci] = o
                kv[bi, p, ci] = jnp.where(same_head, _dot(v, k * jnp.exp(last - cum), _TN), 0.0)
                dec[bi, p, ci] = jnp.exp(last)
                qd[bi, p, ci] = q * jnp.exp(cum)
    state = [st_ref[i] for i in range(len(chains))]
    for ci in (range(n_ch - 1, -1, -1) if reverse else range(n_ch)):
        for i, (bi, p) in enumerate(chains):
            o_ref[bi, ci * ch:(ci + 1) * ch, pair_lanes(p)] = intra[bi, p, ci] + _dot(qd[bi, p, ci], state[i], _NT)
            state[i] = state[i] * dec[bi, p, ci] + kv[bi, p, ci]
    for i in range(len(chains)):
        st_ref[i] = state[i]


def _hgrn_dir_scan(p, lb, *, ctx_len, reverse, col0=0):
    g, l, _ = p.shape
    d = D_HG
    order = _scan_block_order(ctx_len // HG_BLOCK, l // HG_BLOCK, reverse)
    bb = HG_BATCH_BLOCK if g % HG_BATCH_BLOCK == 0 else 1
    col = lambda c: pl.BlockSpec((bb, HG_BLOCK, d), lambda i, j: (i, order(j), col0 + c))
    dr = 1 if reverse else 0
    return pl.pallas_call(
        functools.partial(_hgrn_dir_kernel, reverse=reverse),
        grid=(g // bb, l // HG_BLOCK),
        in_specs=[col(0), col(1), col(2 + dr), pl.BlockSpec((1, d), lambda i, j: (0, 0))],
        out_specs=pl.BlockSpec((bb, HG_BLOCK, d), lambda i, j: (i, order(j), 0)),
        out_shape=jax.ShapeDtypeStruct((g, l, d), jnp.float32),
        scratch_shapes=[pltpu.VMEM((bb * d // _HG_PAIR, _HG_PAIR, _HG_PAIR), jnp.float32)],
        compiler_params=pltpu.CompilerParams(dimension_semantics=("parallel", "arbitrary")),
        name="hgrn_scan",
    )(p, p, p, lb[dr:dr + 1])


def _hgrn_post_kernel(of_ref, ob_ref, g_ref, ng_ref, ones_ref, y_ref):
    o = of_ref[0] + ob_ref[0]
    ms = jnp.dot((o * o).astype(_BF), ones_ref[...], preferred_element_type=jnp.float32) * (1.0 / HG_HEAD)
    g = g_ref[0]
    y_ref[0] = o * lax.rsqrt(ms + RMS_EPS) * ng_ref[...] * (g * _sigmoid(g))


def _hgrn_post(o_f, o_b, p, norm_g, col0=0):
    nb, l, d = o_f.shape
    spec = pl.BlockSpec((1, HG_BLOCK, d), lambda i, j: (i, j, 0))
    return pl.pallas_call(
        _hgrn_post_kernel,
        grid=(nb, l // HG_BLOCK),
        in_specs=[spec, spec, pl.BlockSpec((1, HG_BLOCK, d), lambda i, j: (i, j, col0 + 4)),
                  pl.BlockSpec((1, d), lambda i, j: (0, 0)), pl.BlockSpec((d, d), lambda i, j: (0, 0))],
        out_specs=spec,
        out_shape=jax.ShapeDtypeStruct((nb, l, d), jnp.float32),
        compiler_params=pltpu.CompilerParams(dimension_semantics=("parallel", "parallel")),
        name="hgrn_post",
    )(o_f, o_b, p, norm_g.reshape(1, d), _head_ones(d, HG_HEAD))


_RW_PAIR = 2 * RW_HEAD
RW_PRE_T = 128
RW_BATCH_BLOCK = 4


def _rwkv_pair_kernel(r_ref, lw_ref, k_ref, v_ref, kk_ref, a_ref, y_ref, st_ref, *, chunk, reverse):
    @pl.when(pl.program_id(1) == 0)
    def _():
        st_ref[...] = jnp.zeros_like(st_ref)

    row = lax.broadcasted_iota(jnp.int32, (chunk, chunk), 0)
    col = lax.broadcasted_iota(jnp.int32, (chunk, chunk), 1)
    strict = row < col if reverse else row > col
    incl = row <= col if reverse else row >= col
    end = 0 if reverse else chunk - 1
    tri = incl.astype(jnp.float32)
    eye = (row == col).astype(jnp.float32)
    blk8 = row // 8 == col // 8
    merge_masks = []
    size = 8
    while size < chunk:
        merge_masks.append((row // (2 * size) == col // (2 * size)) & (row // size != col // size))
        size *= 2
    lane_head = lax.broadcasted_iota(jnp.int32, (1, _RW_PAIR), 1) // RW_HEAD
    head0 = lane_head == 0
    same_head = (lax.broadcasted_iota(jnp.int32, (_RW_PAIR, _RW_PAIR), 0) // RW_HEAD
                 == lax.broadcasted_iota(jnp.int32, (_RW_PAIR, _RW_PAIR), 1) // RW_HEAD)
    n_pair = D_RW // _RW_PAIR
    ps = range(r_ref.shape[0] * n_pair)
    ph = [(p, h) for p in ps for h in range(2)]
    bi = [p // n_pair for p in ps]
    sl = [slice((p % n_pair) * _RW_PAIR, (p % n_pair + 1) * _RW_PAIR) for p in ps]
    lw = [lw_ref[bi[p], :, sl[p]] for p in ps]
    cum = [_dot3(tri, lw[p]) for p in ps]
    pw = [jnp.exp(cum[p]) for p in ps]
    p_inv = [jnp.exp(-cum[p]) for p in ps]
    p_end = [pw[p][end:end + 1, :] for p in ps]
    kk = [kk_ref[bi[p], :, sl[p]] for p in ps]
    bh = [kk[p] * a_ref[bi[p], :, sl[p]] * p_inv[p] for p in ps]
    kh = [k_ref[bi[p], :, sl[p]] * p_inv[p] for p in ps]
    v = [v_ref[bi[p], :, sl[p]] for p in ps]
    s0 = [st_ref[p] for p in ps]
    x1 = [jnp.concatenate([-kk[p] * jnp.exp(cum[p] - lw[p]), r_ref[bi[p], :, sl[p]] * pw[p]], axis=0) for p in ps]
    x1h = {(p, h): jnp.where(lane_head == h, x1[p], 0.0) for p, h in ph}
    g1 = {q: _dot(x1h[q], bh[q[0]], _NT) for q in ph}
    g2 = {q: _dot(x1h[q], kh[q[0]], _NT) for q in ph}
    hm = [_dot(x1[p], s0[p], _NT) for p in ps]
    ab = {q: jnp.where(strict, g1[q][:chunk], 0.0) for q in ph}
    d1 = {q: jnp.where(blk8, ab[q], 0.0) for q in ph}
    d2 = {q: _dot3(d1[q], d1[q]) for q in ph}
    inv = {q: eye + d1[q] for q in ph}
    inv = {q: inv[q] + _dot3(inv[q], d2[q]) for q in ph}
    d4 = {q: _dot3(d2[q], d2[q]) for q in ph}
    inv = {q: inv[q] + _dot3(inv[q], d4[q]) for q in ph}
    for m in merge_masks:
        li = {q: _dot(jnp.where(m, ab[q], 0.0), inv[q]) for q in ph}
        inv = {q: inv[q] + _dot(inv[q], li[q]) for q in ph}
    akv = {q: _dot(jnp.where(strict, g2[q][:chunk], 0.0), v[q[0]]) for q in ph}
    rhs = [hm[p][:chunk] + jnp.where(head0, akv[p, 0], akv[p, 1]) for p in ps]
    eh = {q: _dot(inv[q], rhs[q[0]]) for q in ph}
    e = [jnp.where(head0, eh[p, 0], eh[p, 1]) for p in ps]
    yh = {q: _dot(jnp.where(incl, g1[q][chunk:], 0.0), e[q[0]])
          + _dot(jnp.where(incl, g2[q][chunk:], 0.0), v[q[0]]) for q in ph}
    for p in ps:
        y_ref[bi[p], :, sl[p]] = hm[p][chunk:] + jnp.where(head0, yh[p, 0], yh[p, 1])
    for p in ps:
        ev = jnp.concatenate([e[p], v[p]], axis=0)
        x2 = jnp.concatenate([bh[p] * p_end[p], kh[p] * p_end[p]], axis=0)
        st_ref[p] = s0[p] * p_end[p] + jnp.where(same_head, _dot(ev, x2, _TN), 0.0)


def _rwkv_pair_scan(r, lw, k, v, kk, a, *, ctx_len, reverse):
    nb, l, d = r.shape
    chunk = RW_CHUNK
    order = _scan_block_order(ctx_len // chunk, l // chunk, reverse)
    bb = RW_BATCH_BLOCK if nb % RW_BATCH_BLOCK == 0 else 1
    spec = pl.BlockSpec((bb, chunk, d), lambda b, c: (b, order(c), 0))
    return pl.pallas_call(
        functools.partial(_rwkv_pair_kernel, chunk=chunk, reverse=reverse),
        grid=(nb // bb, l // chunk),
        in_specs=[spec] * 6,
        out_specs=spec,
        out_shape=jax.ShapeDtypeStruct((nb, l, d), jnp.float32),
        scratch_shapes=[pltpu.VMEM((bb * d // _RW_PAIR, _RW_PAIR, _RW_PAIR), jnp.float32)],
        compiler_params=pltpu.CompilerParams(dimension_semantics=("parallel", "arbitrary")),
        name="rwkv_scan",
    )(r, lw, k, v, kk, a)


def _softplus(z):
    return jnp.maximum(z, 0.0) + jnp.log(1.0 + jnp.exp(-jnp.abs(z)))


def _sigmoid(z):
    return 1.0 / (1.0 + jnp.exp(-z))


def _rwkv_pre_kernel(prev_ref, cur_ref, next_ref, mu_ref, vec_ref, gup_ref, wup_ref, aup_ref, ones_ref,
                     r_ref, v_ref, kk_ref, g_ref, bonus_ref, lw0_ref, k0_ref, a0_ref, lw1_ref, k1_ref, a1_ref,
                     ext_ref, s_ref, *, ctx_len, seq_len):
    t_rows, w = RW_PRE_T, GRID_W
    ext_ref[0:w] = prev_ref[0]
    ext_ref[w:w + t_rows] = cur_ref[0]
    ext_ref[w + t_rows:w + t_rows + w] = next_ref[0]
    t = pl.program_id(1) * t_rows + lax.broadcasted_iota(jnp.int32, (t_rows, 128), 0)
    lane4 = lax.broadcasted_iota(jnp.int32, (t_rows, 128), 1) % 4
    is_ctx = t < ctx_len
    tx = t - ctx_len
    col = tx % w
    is_lat = jnp.logical_not(is_ctx)
    use_p1 = (is_ctx & (lane4 % 2 == 0) & (t > 0)) | (is_lat & (lane4 == 0) & (col != 0))
    use_n1 = (is_ctx & (lane4 % 2 == 1) & (t < ctx_len - 1)) | (is_lat & (lane4 == 1) & (col != w - 1))
    use_p64 = is_lat & (lane4 == 2) & (tx >= w)
    use_n64 = is_lat & (lane4 == 3) & (tx < seq_len - w)
    for lt in range(RW_COLS // 128):
        lanes = slice(lt * 128, (lt + 1) * 128)
        p = ext_ref[w:w + t_rows, lanes]
        shifted = jnp.where(use_p1, ext_ref[w - 1:w - 1 + t_rows, lanes],
                            jnp.where(use_n1, ext_ref[w + 1:w + 1 + t_rows, lanes],
                                      jnp.where(use_p64, ext_ref[0:t_rows, lanes],
                                                jnp.where(use_n64, ext_ref[2 * w:2 * w + t_rows, lanes], 0.0))))
        s_ref[:, lanes] = p + mu_ref[:, lanes] * (shifted - p)
    d = D_RW
    r = s_ref[:, 0:d]
    k = s_ref[:, d:2 * d]
    v = s_ref[:, 2 * d:3 * d]
    o = 3 * d
    g_lo = s_ref[:, o:o + RW_G_LORA]
    wl = s_ref[:, o + RW_G_LORA:o + RW_G_LORA + 2 * RW_W_LORA]
    al = s_ref[:, o + RW_G_LORA + 2 * RW_W_LORA:o + RW_G_LORA + 2 * RW_W_LORA + 2 * RW_A_LORA]
    k_k, k_a, r_k = vec_ref[0:1], vec_ref[1:2], vec_ref[2:3]
    ones = ones_ref[...]
    r_ref[0] = r
    v_ref[0] = v
    g_ref[0] = jnp.dot(_sigmoid(g_lo).astype(_BF), gup_ref[...], preferred_element_type=jnp.float32)
    kkr = k * k_k
    ss = jnp.dot((kkr * kkr).astype(_BF), ones, preferred_element_type=jnp.float32)
    kk_ref[0] = kkr / jnp.maximum(jnp.sqrt(ss), 1e-12)
    w_pre = jnp.dot(jnp.tanh(wl).astype(_BF), wup_ref[...], preferred_element_type=jnp.float32)
    a_pre = jnp.dot(al.astype(_BF), aup_ref[...], preferred_element_type=jnp.float32)
    k_sum = 0.0
    for dr, (lw_o, k_o, a_o) in enumerate(((lw0_ref, k0_ref, a0_ref), (lw1_ref, k1_ref, a1_ref))):
        w_log = -_softplus(-(vec_ref[3 + dr:4 + dr] + w_pre[:, dr * d:(dr + 1) * d])) - 0.5
        a = _sigmoid(vec_ref[5 + dr:6 + dr] + a_pre[:, dr * d:(dr + 1) * d])
        k_d = k * (1.0 + (a - 1.0) * k_a)
        lw_o[0] = -jnp.exp(w_log)
        k_o[0] = k_d
        a_o[0] = a
        k_sum = k_sum + k_d
    bonus_ref[0] = jnp.dot((r * k_sum * r_k).astype(_BF), ones, preferred_element_type=jnp.float32)


def _head_ones(width, head):
    i = jnp.arange(width) // head
    return (i[:, None] == i[None, :]).astype(_BF)


def _rwkv_pre(p, mu, k_k, k_a, r_k, w0, a0, g_up, w_up, a_up, *, ctx_len):
    nb, l, cols = p.shape
    t_rows, w, d = RW_PRE_T, GRID_W, D_RW
    nblk = l // w
    per = t_rows // w
    zeros = jnp.zeros((RW_W_LORA, d), jnp.float32)
    wup = jnp.concatenate([jnp.concatenate([w_up[0], zeros], axis=1),
                           jnp.concatenate([zeros, w_up[1]], axis=1)], axis=0).astype(_BF)
    aup = jnp.concatenate([jnp.concatenate([a_up[0], zeros], axis=1),
                           jnp.concatenate([zeros, a_up[1]], axis=1)], axis=0).astype(_BF)
    vec = jnp.stack([k_k, k_a, r_k.reshape(d), w0[0], w0[1], a0[0], a0[1], jnp.zeros((d,), jnp.float32)])
    out = jax.ShapeDtypeStruct((nb, l, d), jnp.float32)
    out_spec = pl.BlockSpec((1, t_rows, d), lambda b, j: (b, j, 0))
    full = lambda shape: pl.BlockSpec(shape, lambda b, j: (0,) * len(shape))
    return pl.pallas_call(
        functools.partial(_rwkv_pre_kernel, ctx_len=ctx_len, seq_len=l - ctx_len),
        grid=(nb, l // t_rows),
        in_specs=[pl.BlockSpec((1, w, cols), lambda b, j: (b, jnp.maximum(j * per - 1, 0), 0)),
                  pl.BlockSpec((1, t_rows, cols), lambda b, j: (b, j, 0)),
                  pl.BlockSpec((1, w, cols), lambda b, j: (b, jnp.minimum(j * per + per, nblk - 1), 0)),
                  full((1, cols)), full((8, d)), full((RW_G_LORA, d)),
                  full((2 * RW_W_LORA, 2 * d)), full((2 * RW_A_LORA, 2 * d)), full((d, d))],
        out_specs=[out_spec] * 11,
        out_shape=[out] * 11,
        scratch_shapes=[pltpu.VMEM((t_rows + 2 * w, cols), jnp.float32), pltpu.VMEM((t_rows, cols), jnp.float32)],
        compiler_params=pltpu.CompilerParams(dimension_semantics=("parallel", "parallel"),
                                             vmem_limit_bytes=48 * 1024 * 1024),
        name="rwkv_pre",
    )(p, p, p, mu.reshape(1, cols), vec, g_up.astype(_BF), wup, aup, _head_ones(d, RW_HEAD))


def _rwkv_post_kernel(y0_ref, y1_ref, v_ref, g_ref, bonus_ref, gn_ref, ones_ref, o_ref):
    y = y0_ref[0] + y1_ref[0]
    ones = ones_ref[...]
    inv_n = 1.0 / RW_HEAD
    mu_y = jnp.dot(y.astype(_BF), ones, preferred_element_type=jnp.float32) * inv_n
    yc = y - mu_y
    var_y = jnp.dot((yc * yc).astype(_BF), ones, preferred_element_type=jnp.float32) * inv_n
    yn = yc * lax.rsqrt(var_y + RW_GN_EPS) * gn_ref[0:1] + gn_ref[1:2]
    o_ref[0] = (yn + bonus_ref[0] * v_ref[0]) * g_ref[0]


def _rwkv_post(y0, y1, v, g, bonus, gn_g, gn_b):
    nb, l, d = y0.shape
    spec = pl.BlockSpec((1, RW_PRE_T, d), lambda b, j: (b, j, 0))
    return pl.pallas_call(
        _rwkv_post_kernel,
        grid=(nb, l // RW_PRE_T),
        in_specs=[spec] * 5 + [pl.BlockSpec((2, d), lambda b, j: (0, 0)), pl.BlockSpec((d, d), lambda b, j: (0, 0))],
        out_specs=spec,
        out_shape=jax.ShapeDtypeStruct((nb, l, d), jnp.float32),
        compiler_params=pltpu.CompilerParams(dimension_semantics=("parallel", "parallel")),
        name="rwkv_post",
    )(y0, y1, v, g, bonus, jnp.stack([gn_g, gn_b]), _head_ones(d, RW_HEAD))


def _rwkv7_mixer_fused(p, lc, mu, w0, w_up, a0, a_up, g_up, k_k, k_a, r_k, gn_g, gn_b):
    r, v, kk, g, bonus, lw0, k0, a0_, lw1, k1, a1_ = _rwkv_pre(
        p, _f32(mu), _f32(k_k), _f32(k_a), _f32(r_k), _f32(w0), _f32(a0), _f32(g_up), _f32(w_up), _f32(a_up),
        ctx_len=lc)
    y0 = _rwkv_pair_scan(r, lw0, k0, v, kk, a0_, ctx_len=lc, reverse=False)
    y1 = _rwkv_pair_scan(r, lw1, k1, v, kk, a1_, ctx_len=lc, reverse=True)
    return _rwkv_post(y0, y1, v, g, bonus, _f32(gn_g), _f32(gn_b))


def _f32(t):
    return t.astype(jnp.float32)


def _layer_norm(x, g, b):
    xf = _f32(x)
    mu = jnp.mean(xf, -1, keepdims=True)
    var = jnp.mean(jnp.square(xf - mu), -1, keepdims=True)
    return ((xf - mu) * lax.rsqrt(var + LN_EPS) * _f32(g) + _f32(b)).astype(x.dtype)


def _s5_discretise(lam_re, lam_im, log_step, b_re, b_im):
    dt = jnp.exp(log_step)[:, None]
    mag = jnp.exp(lam_re * dt)
    ang = lam_im * dt
    lb_re, lb_im = mag * jnp.cos(ang), mag * jnp.sin(ang)
    den = lam_re * lam_re + lam_im * lam_im
    nr, ni = lb_re - 1.0, lb_im
    co_re = (nr * lam_re + ni * lam_im) / den
    co_im = (ni * lam_re - nr * lam_im) / den
    bb_re = co_re[..., None] * b_re - co_im[..., None] * b_im
    bb_im = co_re[..., None] * b_im + co_im[..., None] * b_re
    return lb_re, lb_im, bb_re, bb_im


def _s5_mixer(u, lc, lam_re, lam_im, log_step, b_re, b_im, c_re, c_im, d_skip, glu_w, glu_b):
    disc = [_s5_discretise(_f32(lam_re[d]), _f32(lam_im[d]), _f32(log_step[d]),
                           _f32(b_re[d]), _f32(b_im[d])) for d in range(2)]
    wb, lam, wc = _s5_weights(disc, _f32(c_re), _f32(c_im))
    u = _f32(u)
    nb, l, _ = u.shape
    assert 2 * nb == _S5_ROWS
    u_t = u.transpose(1, 0, 2)
    u8 = jnp.concatenate([u_t, u_t], axis=1).reshape(l * _S5_ROWS, D_S5)
    y_f, y_b = _s5_dir_scan(u8, wb, lam, wc, ctx_len=lc)
    y = (y_f.reshape(l, _S5_ROWS, D_S5)[:, :nb] + y_b.reshape(l, _S5_ROWS, D_S5)[:, nb:]).transpose(1, 0, 2)
    y = jax.nn.gelu(y + _f32(d_skip) * u)
    return y * jax.nn.sigmoid(_mm3(y, glu_w) + _f32(glu_b))


def _hgrn2_mixer(p, lc, lb, norm_g, col0=0):
    lb = _f32(lb)
    o_f = _hgrn_dir_scan(p, lb, ctx_len=lc, reverse=False, col0=col0)
    o_b = _hgrn_dir_scan(p, lb, ctx_len=lc, reverse=True, col0=col0)
    return _hgrn_post(o_f, o_b, p, _f32(norm_g), col0=col0)


ROW_TILE = 256


def _mod_spec(n_batch, tiles_per_batch, ctx_tiles):
    def index(i):
        return (jnp.where(i % tiles_per_batch < ctx_tiles, n_batch, i // tiles_per_batch), 0, 0)
    return pl.BlockSpec((1, 1, 6 * D_MODEL), index)


def _inproj_kernel(x_ref, mod_ref, w_ref, pa_ref, pc_ref):
    d = D_MODEL
    m = mod_ref[0]
    xm = (x_ref[...] * (1.0 + m[:, d:2 * d]) + m[:, 0:d]).astype(_BF)
    split = pa_ref.shape[1]
    pa_ref[...] = jnp.dot(xm, w_ref[:, :split], preferred_element_type=jnp.float32)
    pc_ref[...] = jnp.dot(xm, w_ref[:, split:], preferred_element_type=jnp.float32)


def _inproj(xs, mod, w, *, n_batch, ctx_len):
    t, d = xs.shape
    n_a = D_S5 + 5 * D_HG
    n_c = w.shape[1] - n_a
    tiles = t // ROW_TILE
    return pl.pallas_call(
        _inproj_kernel,
        grid=(tiles,),
        in_specs=[pl.BlockSpec((ROW_TILE, d), lambda i: (i, 0)),
                  _mod_spec(n_batch, tiles // n_batch, ctx_len // ROW_TILE),
                  pl.BlockSpec((d, n_a + n_c), lambda i: (0, 0))],
        out_specs=[pl.BlockSpec((ROW_TILE, n_a), lambda i: (i, 0)), pl.BlockSpec((ROW_TILE, n_c), lambda i: (i, 0))],
        out_shape=[jax.ShapeDtypeStruct((t, n_a), jnp.float32), jax.ShapeDtypeStruct((t, n_c), jnp.float32)],
        compiler_params=pltpu.CompilerParams(dimension_semantics=("parallel",),
                                             vmem_limit_bytes=48 * 1024 * 1024),
        name="in_proj",
    )(xs, mod, w.astype(_BF))


def _ln_rows(h, g, b):
    mu = jnp.mean(h, axis=-1, keepdims=True)
    hc = h - mu
    var = jnp.mean(hc * hc, axis=-1, keepdims=True)
    return hc * lax.rsqrt(var + LN_EPS) * g + b


def _outproj_ln_kernel(ya_ref, yb_ref, yc_ref, w_ref, xs_ref, mod_ref, ln_ref, x1_ref, h2_ref):
    d = D_MODEL
    mix = (jnp.dot(ya_ref[...].astype(_BF), w_ref[0:D_S5], preferred_element_type=jnp.float32)
           + jnp.dot(yb_ref[...].astype(_BF), w_ref[D_S5:D_S5 + D_HG], preferred_element_type=jnp.float32)
           + jnp.dot(yc_ref[...].astype(_BF), w_ref[D_S5 + D_HG:D_MIX], preferred_element_type=jnp.float32))
    m = mod_ref[0]
    x1 = _ln_rows(DEEPNORM_ALPHA * xs_ref[...] + m[:, 2 * d:3 * d] * mix, ln_ref[0:1], ln_ref[1:2])
    x1_ref[...] = x1
    h2_ref[...] = x1 * (1.0 + m[:, 4 * d:5 * d]) + m[:, 3 * d:4 * d]


def _outproj_ln(ya, yb, yc, w, xs, mod, ln_g, ln_b, *, n_batch, ctx_len):
    t, d = xs.shape
    tiles = t // ROW_TILE
    row = lambda n: pl.BlockSpec((ROW_TILE, n), lambda i: (i, 0))
    out = jax.ShapeDtypeStruct((t, d), jnp.float32)
    return pl.pallas_call(
        _outproj_ln_kernel,
        grid=(tiles,),
        in_specs=[row(D_S5), row(D_HG), row(D_RW), pl.BlockSpec((D_MIX, d), lambda i: (0, 0)), row(d),
                  _mod_spec(n_batch, tiles // n_batch, ctx_len // ROW_TILE),
                  pl.BlockSpec((2, d), lambda i: (0, 0))],
        out_specs=[row(d), row(d)],
        out_shape=[out, out],
        compiler_params=pltpu.CompilerParams(dimension_semantics=("parallel",)),
        name="out_proj_ln",
    )(ya, yb, yc, w.astype(_BF), xs, mod, jnp.stack([ln_g, ln_b]))


def kernel(x, c, ctx, c_ctx, ada_w, ada_b, w_in, w_out, ln1_g, ln1_b, ln2_g, ln2_b,
           s5_lam_re, s5_lam_im, s5_log_step, s5_b_re, s5_b_im, s5_c_re, s5_c_im,
           s5_d, s5_glu_w, s5_glu_b, hgrn_lb_logits, hgrn_norm_g,
           rwkv_mu, rwkv_w0, rwkv_w_up, rwkv_a0, rwkv_a_up, rwkv_g_up, rwkv_k_k, rwkv_k_a,
           rwkv_r_k, rwkv_gn_g, rwkv_gn_b, peer_wq, peer_keys, peer_u, peer_v):
    nb, lx, d = x.shape
    lc = ctx.shape[1]
    l_all = lc + lx
    assert lx % GRID_W == 0 and lc % ROW_TILE == 0 and lx % ROW_TILE == 0 and (nb * l_all) % PEER_TN == 0
    lb_cum = jnp.cumsum(jax.nn.softmax(_f32(hgrn_lb_logits), axis=1), axis=1)
    lb_all = lb_cum - lb_cum[:, :1]
    xs = jnp.concatenate([ctx, x], axis=1).reshape(nb * l_all, d)
    seq = lambda t: t.reshape(nb, l_all, t.shape[-1])
    for l in range(DEPTH):
        last = l == DEPTH - 1
        mod = jnp.concatenate([jax.nn.silu(c), jax.nn.silu(c_ctx)[None]], axis=0) @ ada_w[l] + ada_b[l]
        g2 = mod[:, None, 5 * d:]
        mod = mod[:, None, :]
        pa, pc = _inproj(xs, mod, w_in[l], n_batch=nb, ctx_len=lc)
        pa = seq(pa)
        ya = _s5_mixer(pa[..., :D_S5], lc, s5_lam_re[l], s5_lam_im[l], s5_log_step[l],
                       s5_b_re[l], s5_b_im[l], s5_c_re[l], s5_c_im[l], s5_d[l], s5_glu_w[l], s5_glu_b[l])
        yb = _hgrn2_mixer(pa, lc, lb_all[:, l], hgrn_norm_g[l], col0=D_S5 // D_HG)
        yc = _rwkv7_mixer_fused(seq(pc), lc, rwkv_mu[l], rwkv_w0[l], rwkv_w_up[l], rwkv_a0[l], rwkv_a_up[l],
                                rwkv_g_up[l], rwkv_k_k[l], rwkv_k_a[l], rwkv_r_k[l], rwkv_gn_g[l], rwkv_gn_b[l])
        flat = lambda t: t.reshape(nb * l_all, t.shape[-1])
        xs, h2 = _outproj_ln(flat(ya), flat(yb), flat(yc), w_out[l], xs, mod, ln1_g[l], ln1_b[l],
                             n_batch=nb, ctx_len=lc)
        if last:
            xs, h2 = seq(xs)[:, lc:], seq(h2)[:, lc:].reshape(nb * lx, d)
            gate = g2[:nb]
        else:
            xs = seq(xs)
            gate = jnp.where((jnp.arange(l_all) < lc)[None, :, None], g2[nb:], g2[:nb])
        ffn = _peer_ffn_dense(h2, peer_wq[l], peer_keys[l], peer_u[l].astype(_BF), peer_v[l].T.astype(_BF))
        xs = _layer_norm(DEEPNORM_ALPHA * xs + gate * ffn.reshape(xs.shape), ln2_g[l], ln2_b[l])
        xs = xs.reshape(-1, d)
    return xs.reshape(nb, lx, d)
```
